```python
import math
import jax, jax.numpy as jnp
from jax import lax
import numpy as np

D_MODEL = 2048
BATCH = 2
SEQ = 4096
DEPTH = 1

GRID_W = 64
HEAD_DIM = 128
N_Q_HEADS = 8
N_KV_HEADS = 2
Q_PER_KV = N_Q_HEADS // N_KV_HEADS
ATTN_W = N_Q_HEADS * HEAD_DIM
KV_W = N_KV_HEADS * HEAD_DIM
ROPE_THETA = 10000.0
Q_BLOCK = 128
HYENA_W = D_MODEL - ATTN_W
HYENA_GROUPS = 8
HYENA_ORDER = 2
SHORT_TAPS = 3
FILTER_EMB = 33
FILTER_HIDDEN = 64
DECAY_TARGET = 1e-2
FAST_DECAY_PCT = 0.3
SLOW_DECAY_PCT = 1.5
IN_W = (HYENA_ORDER + 1) * HYENA_W + ATTN_W + 2 * KV_W + 2 * D_MODEL
D_FF = -(-8 * D_MODEL // (3 * 256)) * 256
EPS = 1e-6

kernel_name = "hybrid_hyena_gqa_gated_encoder"


def rms_norm(x, g):
    xf = x.astype(jnp.float32)
    y = xf * lax.rsqrt(jnp.mean(xf * xf, axis=-1, keepdims=True) + EPS)
    return (y * g.astype(jnp.float32)).astype(x.dtype)


def centred_short_conv(u, w, b):
    L = u.shape[1]
    pad = SHORT_TAPS // 2
    up = jnp.pad(u, ((0, 0), (pad, pad), (0, 0)))
    out = b
    for j in range(SHORT_TAPS):
        out = out + up[:, j:j + L] * w[j]
    return out


def implicit_filter(L, w1, b1, w2, b2, w3, b3, w4, freq):
    f32 = jnp.float32
    pos = jnp.arange(L, dtype=f32)
    t = pos / max(L - 1, 1)
    bands = (FILTER_EMB - 1) // 2
    fb = jnp.linspace(1e-4, bands - 1, bands, dtype=f32)
    ang = (2.0 * math.pi * pos / L)[:, None] * fb[None, :]
    z = jnp.concatenate([t[:, None], jnp.cos(ang), -jnp.sin(ang)], axis=-1)
    fr = freq.astype(f32)
    h = jnp.sin(fr * (z @ w1.astype(f32) + b1.astype(f32)))
    h = jnp.sin(fr * (h @ w2.astype(f32) + b2.astype(f32)))
    h = jnp.sin(fr * (h @ w3.astype(f32) + b3.astype(f32)))
    h = h @ w4.astype(f32)
    max_decay = math.log(DECAY_TARGET) / FAST_DECAY_PCT
    min_decay = math.log(DECAY_TARGET) / SLOW_DECAY_PCT
    deltas = jnp.abs(jnp.linspace(min_decay, max_decay, HYENA_W, dtype=f32))
    decay = jnp.exp(-t[:, None] * deltas[None, :])
    h_fwd = h[:, :HYENA_W] * decay
    h_bwd = h[:, HYENA_W:] * decay
    k = jnp.concatenate([h_fwd, jnp.zeros((1, HYENA_W), f32), h_bwd[:0:-1]], axis=0)
    return k / jnp.sum(jnp.abs(k), axis=0, keepdims=True)


def bidir_fft_conv(u, k):
    L = u.shape[1]
    uf = jnp.fft.rfft(u.astype(jnp.float32), n=2 * L, axis=1)
    kf = jnp.fft.rfft(k, n=2 * L, axis=0)
    return jnp.fft.irfft(uf * kf[None], n=2 * L, axis=1)[:, :L]


def axial_rope(x, row, col):
    f32 = jnp.float32
    half = HEAD_DIM // 2
    inv = ROPE_THETA ** (-jnp.arange(0, half, 2, dtype=f32) / half)

    def rot(xs, p):
        ang = p.astype(f32)[:, None] * inv[None, :]
        c = jnp.cos(ang)[None, :, None, :]
        s = jnp.sin(ang)[None, :, None, :]
        a, b = xs[..., :half // 2], xs[..., half // 2:]
        return jnp.concatenate([a * c - b * s, a * s + b * c], axis=-1)

    xf = x.astype(f32)
    return jnp.concatenate([rot(xf[..., :half], row), rot(xf[..., half:], col)], axis=-1).astype(x.dtype)


def block_gqa_attention(q, k, v):
    B, S = q.shape[0], q.shape[1]
    nb = S // Q_BLOCK
    qb = q.reshape(B, nb, Q_BLOCK, N_KV_HEADS, Q_PER_KV, HEAD_DIM).transpose(1, 0, 2, 3, 4, 5)
    scale = HEAD_DIM ** -0.5

    def one_block(qblk):
        s = jnp.einsum('bqkgd,bskd->bkgqs', qblk, k).astype(jnp.float32) * scale
        p = jax.nn.softmax(s, axis=-1).astype(v.dtype)
        return jnp.einsum('bkgqs,bskd->bqkgd', p, v)

    o = lax.map(one_block, qb)
    return o.transpose(1, 0, 2, 3, 4, 5).reshape(B, S, ATTN_W)


def hybrid_layer(x, mix_norm_g, w_in, b_gate, hy_conv_w, hy_conv_b,
                 flt_w1, flt_b1, flt_w2, flt_b2, flt_w3, flt_b3, flt_w4, flt_freq, hy_bias,
                 q_norm_g, k_norm_g, w_br_hyena, w_br_attn, w_out,
                 ffn_norm_g, w_ffn_gate, w_ffn_up, w_ffn_down):
    B, S, _ = x.shape
    rows = S // GRID_W
    h = rms_norm(x, mix_norm_g)
    proj = h @ w_in
    c0 = (HYENA_ORDER + 1) * HYENA_W
    c1 = c0 + ATTN_W
    c2 = c1 + KV_W
    c3 = c2 + KV_W
    u_h, q, k, v, g = jnp.split(proj, [c0, c1, c2, c3], axis=-1)

    u_h = centred_short_conv(u_h, hy_conv_w, hy_conv_b)
    x0, x1, hv = jnp.split(u_h, HYENA_ORDER + 1, axis=-1)
    filt = implicit_filter(S, flt_w1, flt_b1, flt_w2, flt_b2, flt_w3, flt_b3, flt_w4, flt_freq)
    z = hv * x1
    zf = z.astype(jnp.float32)
    z = (bidir_fft_conv(z, filt) + hy_bias.astype(jnp.float32) * zf).astype(x.dtype)
    y_h = z * x0

    q = rms_norm(q.reshape(B, S, N_Q_HEADS, HEAD_DIM), q_norm_g)
    k = rms_norm(k.reshape(B, S, N_KV_HEADS, HEAD_DIM), k_norm_g)
    v = v.reshape(B, S, N_KV_HEADS, HEAD_DIM)
    row = jnp.repeat(jnp.arange(rows, dtype=jnp.int32), GRID_W)
    col = jnp.tile(jnp.arange(GRID_W, dtype=jnp.int32), rows)
    q = axial_rope(q, row, col)
    k = axial_rope(k, row, col)
    y_a = block_gqa_attention(q, k, v)

    g_h, g_a = jnp.split(jax.nn.sigmoid(g + b_gate), 2, axis=-1)
    merged = g_h * (y_h @ w_br_hyena) + g_a * (y_a @ w_br_attn)
    x = x + merged @ w_out

    h2 = rms_norm(x, ffn_norm_g)
    return x + (jax.nn.silu(h2 @ w_ffn_gate) * (h2 @ w_ffn_up)) @ w_ffn_down


def setup_inputs(seed: int = 0) -> dict:
    key = jax.random.key(seed)
    ks = jax.random.split(key, 32)
    f32 = jnp.float32

    def dense(k, shape, fan_in):
        return jax.random.normal(k, (DEPTH,) + shape, f32) * (fan_in ** -0.5)

    def gain(k, n):
        return 1.0 + 0.05 * jax.random.normal(k, (DEPTH, n), f32)

    def small(k, shape):
        return 0.01 * jax.random.normal(k, (DEPTH,) + shape, f32)

    return {
        "x": jax.random.normal(ks[0], (BATCH, SEQ, D_MODEL), f32),
        "mix_norm_g": gain(ks[1], D_MODEL),
        "w_in": dense(ks[2], (D_MODEL, IN_W), D_MODEL),
        "b_gate": small(ks[3], (2 * D_MODEL,)),
        "hy_conv_w": dense(ks[4], (SHORT_TAPS, (HYENA_ORDER + 1) * HYENA_W), SHORT_TAPS),
        "hy_conv_b": small(ks[5], ((HYENA_ORDER + 1) * HYENA_W,)),
        "flt_w1": dense(ks[6], (FILTER_EMB, FILTER_HIDDEN), FILTER_EMB),
        "flt_b1": small(ks[7], (FILTER_HIDDEN,)),
        "flt_w2": dense(ks[8], (FILTER_HIDDEN, FILTER_HIDDEN), FILTER_HIDDEN),
        "flt_b2": small(ks[9], (FILTER_HIDDEN,)),
        "flt_w3": dense(ks[10], (FILTER_HIDDEN, FILTER_HIDDEN), FILTER_HIDDEN),
        "flt_b3": small(ks[11], (FILTER_HIDDEN,)),
        "flt_w4": dense(ks[12], (FILTER_HIDDEN, 2 * HYENA_W), FILTER_HIDDEN),
        "flt_freq": gain(ks[13], FILTER_HIDDEN),
        "hy_bias": jax.random.normal(ks[14], (DEPTH, HYENA_W), f32),
        "q_norm_g": gain(ks[15], HEAD_DIM),
        "k_norm_g": gain(ks[16], HEAD_DIM),
        "w_br_hyena": dense(ks[17], (HYENA_W, D_MODEL), HYENA_W),
        "w_br_attn": dense(ks[18], (ATTN_W, D_MODEL), ATTN_W),
        "w_out": dense(ks[19], (D_MODEL, D_MODEL), D_MODEL),
        "ffn_norm_g": gain(ks[20], D_MODEL),
        "w_ffn_gate": dense(ks[21], (D_MODEL, D_FF), D_MODEL),
        "w_ffn_up": dense(ks[22], (D_MODEL, D_FF), D_MODEL),
        "w_ffn_down": dense(ks[23], (D_FF, D_MODEL), D_FF),
    }


def reference(x, mix_norm_g, w_in, b_gate, hy_conv_w, hy_conv_b,
              flt_w1, flt_b1, flt_w2, flt_b2, flt_w3, flt_b3, flt_w4, flt_freq, hy_bias,
              q_norm_g, k_norm_g, w_br_hyena, w_br_attn, w_out,
              ffn_norm_g, w_ffn_gate, w_ffn_up, w_ffn_down):
    for l in range(DEPTH):
        x = hybrid_layer(x, mix_norm_g[l], w_in[l], b_gate[l], hy_conv_w[l], hy_conv_b[l],
                         flt_w1[l], flt_b1[l], flt_w2[l], flt_b2[l], flt_w3[l], flt_b3[l],
                         flt_w4[l], flt_freq[l], hy_bias[l],
                         q_norm_g[l], k_norm_g[l], w_br_hyena[l], w_br_attn[l], w_out[l],
                         ffn_norm_g[l], w_ffn_gate[l], w_ffn_up[l], w_ffn_down[l])
    return x
```

```python
import functools
import math

import numpy as np
import jax
import jax.numpy as jnp
from jax import lax
from jax.experimental import pallas as pl
from jax.experimental.pallas import tpu as pltpu

F32 = jnp.float32
BF16 = jnp.bfloat16

D_MODEL = 2048
SEQ = 4096
GRID_W = 64
HEAD_DIM = 128
N_Q_HEADS = 8
N_KV_HEADS = 2
Q_PER_KV = N_Q_HEADS // N_KV_HEADS
ATTN_W = N_Q_HEADS * HEAD_DIM
KV_W = N_KV_HEADS * HEAD_DIM
ROPE_THETA = 10000.0
HYENA_W = D_MODEL - ATTN_W
SHORT_TAPS = 3
FILTER_EMB = 33
FILTER_HIDDEN = 64
DECAY_TARGET = 1e-2
FAST_DECAY_PCT = 0.3
SLOW_DECAY_PCT = 1.5
IN_W = 3 * HYENA_W + ATTN_W + 2 * KV_W + 2 * D_MODEL
D_FF = 5632
EPS = 1e-6

COL_Q = 3 * HYENA_W
COL_K = COL_Q + ATTN_W
COL_V = COL_K + KV_W
COL_G = COL_V + KV_W

FFT_N = 2 * SEQ
FFT_N1 = 64
FFT_N2 = 128

MIB = 1024 * 1024


def _cparams(n_axes, vmem_mib):
    return pltpu.CompilerParams(
        dimension_semantics=("arbitrary",) * n_axes,
        vmem_limit_bytes=vmem_mib * MIB,
    )


@functools.lru_cache(maxsize=None)
def _dft_tables():
    n, n1, n2 = FFT_N, FFT_N1, FFT_N2
    f1 = np.arange(n1)
    ang1 = 2.0 * np.pi * ((f1[:, None] * f1[None, :]) % n1) / n1
    s1_full = np.concatenate([np.cos(ang1), -np.sin(ang1)], axis=0)
    s1_half = s1_full[:, : n1 // 2]
    s2 = np.arange(n2)
    f = f1[:, None, None] + n1 * s2[None, :, None]
    th = 2.0 * np.pi * ((f * s2[None, None, :]) % n) / n
    c, s = np.cos(th), np.sin(th)
    fwd2 = np.concatenate(
        [np.concatenate([c, s], axis=2), np.concatenate([-s, c], axis=2)], axis=1)
    ct, st = np.transpose(c, (0, 2, 1)), np.transpose(s, (0, 2, 1))
    inv2 = np.concatenate(
        [np.concatenate([ct, -st], axis=2), np.concatenate([st, ct], axis=2)], axis=1)
    t1 = np.arange(n1 // 2)
    ph = 2.0 * np.pi * ((t1[:, None] * f1[None, :]) % n1) / n1
    inv1 = np.concatenate([np.cos(ph), -np.sin(ph)], axis=1) / n
    as_bf16 = lambda a: jnp.asarray(a.astype(np.float32)).astype(BF16)
    return dict(s1_half=as_bf16(s1_half), s1_full=as_bf16(s1_full),
                fwd2=as_bf16(fwd2), inv2=as_bf16(inv2), inv1=as_bf16(inv1))


def _rope_tables():
    half = HEAD_DIM // 2
    inv = ROPE_THETA ** (-jnp.arange(0, half, 2, dtype=F32) / half)
    pos = jnp.arange(SEQ, dtype=jnp.int32)
    row = (pos // GRID_W).astype(F32)
    col = (pos % GRID_W).astype(F32)
    ang_r = row[:, None] * inv[None, :]
    ang_c = col[:, None] * inv[None, :]
    cos = jnp.concatenate([jnp.cos(ang_r)] * 2 + [jnp.cos(ang_c)] * 2, axis=-1)
    sin = jnp.concatenate([-jnp.sin(ang_r), jnp.sin(ang_r), -jnp.sin(ang_c), jnp.sin(ang_c)], axis=-1)
    return cos, sin


def _filter_tables():
    L = SEQ
    bands = (FILTER_EMB - 1) // 2
    pos = jnp.concatenate([jnp.arange(L, dtype=F32), L - jnp.arange(L, dtype=F32)])
    t = pos / max(L - 1, 1)
    fb = jnp.linspace(1e-4, bands - 1, bands, dtype=F32)
    ang = (2.0 * math.pi * pos / L)[:, None] * fb[None, :]
    emb = jnp.concatenate([t[:, None], jnp.cos(ang), -jnp.sin(ang),
                           jnp.zeros((2 * L, FILTER_HIDDEN - FILTER_EMB), F32)], axis=-1)
    max_decay = math.log(DECAY_TARGET) / FAST_DECAY_PCT
    min_decay = math.log(DECAY_TARGET) / SLOW_DECAY_PCT
    deltas = jnp.abs(jnp.linspace(min_decay, max_decay, HYENA_W, dtype=F32))
    return emb, deltas[None, :]


def _rmsnorm_kernel(x_ref, g_ref, o_ref):
    x = x_ref[...]
    ms = jnp.mean(x * x, axis=-1, keepdims=True)
    o_ref[...] = (x * lax.rsqrt(ms + EPS) * g_ref[...]).astype(o_ref.dtype)


def _rmsnorm(x, g, tm=512):
    m, d = x.shape
    return pl.pallas_call(
        _rmsnorm_kernel,
        grid=(m // tm,),
        in_specs=[pl.BlockSpec((tm, d), lambda i: (i, 0)),
                  pl.BlockSpec((1, d), lambda i: (0, 0))],
        out_specs=pl.BlockSpec((tm, d), lambda i: (i, 0)),
        out_shape=jax.ShapeDtypeStruct((m, d), BF16),
        compiler_params=_cparams(1, 32),
        name="rmsnorm",
    )(x, g)


IN_TN = 512
IN_JQ = COL_Q // IN_TN
IN_JK = COL_K // IN_TN
IN_JG = COL_G // IN_TN


def _head_norm_rope(x, g, cos, sin, scale):
    ms = jnp.mean(x * x, axis=-1, keepdims=True)
    y = x * lax.rsqrt(ms + EPS) * g
    lane = lax.broadcasted_iota(jnp.int32, y.shape, 1)
    swapped = jnp.where((lane % 64) < 32, pltpu.roll(y, 96, 1), pltpu.roll(y, 32, 1))
    out = y * cos + swapped * sin
    return out * scale if scale != 1.0 else out


def _inproj_kernel(h_ref, w_ref, bg_ref, qg_ref, kg_ref, cos_ref, sin_ref, o_ref, wb_ref):
    j = pl.program_id(0)
    i = pl.program_id(1)

    @pl.when(i == 0)
    def _():
        wb_ref[...] = w_ref[...].astype(BF16)

    acc = jnp.dot(h_ref[...], wb_ref[...], preferred_element_type=F32)

    @pl.when(j < IN_JQ)
    def _():
        o_ref[...] = acc.astype(BF16)

    @pl.when((j >= IN_JQ) & (j < IN_JK))
    def _():
        cos, sin = cos_ref[...], sin_ref[...]
        for hh in range(IN_TN // HEAD_DIM):
            sl = slice(hh * HEAD_DIM, (hh + 1) * HEAD_DIM)
            o_ref[:, sl] = _head_norm_rope(acc[:, sl], qg_ref[...], cos, sin, HEAD_DIM ** -0.5).astype(BF16)

    @pl.when(j == IN_JK)
    def _():
        cos, sin = cos_ref[...], sin_ref[...]
        for hh in range(N_KV_HEADS):
            sl = slice(hh * HEAD_DIM, (hh + 1) * HEAD_DIM)
            o_ref[:, sl] = _head_norm_rope(acc[:, sl], kg_ref[...], cos, sin, 1.0).astype(BF16)
        o_ref[:, KV_W:] = acc[:, KV_W:].astype(BF16)

    @pl.when(j >= IN_JG)
    def _():
        o_ref[...] = jax.nn.sigmoid(acc + bg_ref[...]).astype(BF16)


def _inproj(h, w_in, b_gate, q_g, k_g, cos, sin, tm=1024):
    m, d = h.shape
    tn = IN_TN
    s_blocks = SEQ // tm
    return pl.pallas_call(
        _inproj_kernel,
        grid=(IN_W // tn, m // tm),
        in_specs=[
            pl.BlockSpec((tm, d), lambda j, i: (i, 0)),
            pl.BlockSpec((d, tn), lambda j, i: (0, j)),
            pl.BlockSpec((1, tn), lambda j, i: (0, jnp.maximum(j - IN_JG, 0))),
            pl.BlockSpec((1, HEAD_DIM), lambda j, i: (0, 0)),
            pl.BlockSpec((1, HEAD_DIM), lambda j, i: (0, 0)),
            pl.BlockSpec((tm, HEAD_DIM), lambda j, i: (i % s_blocks, 0)),
            pl.BlockSpec((tm, HEAD_DIM), lambda j, i: (i % s_blocks, 0)),
        ],
        out_specs=pl.BlockSpec((tm, tn), lambda j, i: (i, j)),
        out_shape=jax.ShapeDtypeStruct((m, IN_W), BF16),
        scratch_shapes=[pltpu.VMEM((d, tn), BF16)],
        compiler_params=_cparams(2, 48),
        name="inproj",
    )(h, w_in, b_gate, q_g, k_g, cos, sin)


HY_CT = 256
HY_RC = 256


def _hy_pre_kernel(x0_ref, x1_ref, v_ref, w0_ref, w1_ref, wv_ref, b0_ref, b1_ref, bv_ref,
                   z_ref, x0c_ref, pad_ref):
    L = x0_ref.shape[0]
    ct = x0_ref.shape[1]
    nchunk = L // HY_RC
    zeros8 = jnp.zeros((8, ct), F32)
    for a, src in enumerate((x0_ref, x1_ref, v_ref)):
        pad_ref[a, 0:8, :] = zeros8
        pad_ref[a, L + 8:L + 16, :] = zeros8

        def fill(r, carry, a=a, src=src):
            base = pl.multiple_of(r * HY_RC, HY_RC)
            pad_ref[a, pl.ds(base + 8, HY_RC), :] = src[pl.ds(base, HY_RC), :].astype(F32)
            return carry

        lax.fori_loop(0, nchunk, fill, 0)

    def conv(a, w_ref, b_ref, base):
        e = pad_ref[a, pl.ds(base, HY_RC + 16), :]
        up = pltpu.roll(e, 1, 0)[8:8 + HY_RC]
        mid = e[8:8 + HY_RC]
        dn = pltpu.roll(e, HY_RC + 15, 0)[8:8 + HY_RC]
        return b_ref[...] + up * w_ref[0:1, :] + mid * w_ref[1:2, :] + dn * w_ref[2:3, :]

    def body(r, carry):
        base = pl.multiple_of(r * HY_RC, HY_RC)
        x0c = conv(0, w0_ref, b0_ref, base)
        x1c = conv(1, w1_ref, b1_ref, base)
        vc = conv(2, wv_ref, bv_ref, base)
        z_ref[pl.ds(base, HY_RC), :] = (vc * x1c).astype(BF16)
        x0c_ref[pl.ds(base, HY_RC), :] = x0c.astype(BF16)
        return carry

    lax.fori_loop(0, nchunk, body, 0)


def _hy_pre(proj3, conv_w, conv_b):
    b, L, _ = proj3.shape
    ct = HY_CT
    nct = HYENA_W // ct
    in_specs = []
    for grp in range(3):
        in_specs.append(pl.BlockSpec((None, L, ct), lambda bi, c, grp=grp: (bi, 0, c + grp * nct)))
    for grp in range(3):
        in_specs.append(pl.BlockSpec((SHORT_TAPS, ct), lambda bi, c, grp=grp: (0, c + grp * nct)))
    for grp in range(3):
        in_specs.append(pl.BlockSpec((1, ct), lambda bi, c, grp=grp: (0, c + grp * nct)))
    out_spec = pl.BlockSpec((None, L, ct), lambda bi, c: (bi, 0, c))
    return pl.pallas_call(
        _hy_pre_kernel,
        grid=(b, nct),
        in_specs=in_specs,
        out_specs=[out_spec, out_spec],
        out_shape=[jax.ShapeDtypeStruct((b, L, HYENA_W), BF16)] * 2,
        scratch_shapes=[pltpu.VMEM((3, L + 16, ct), F32)],
        compiler_params=_cparams(2, 48),
        name="hyena_pre",
    )(proj3, proj3, proj3, conv_w, conv_w, conv_w, conv_b, conv_b, conv_b)


DFT_TL = 16384


def _dft_outer_kernel(m_ref, x_ref, o_ref):
    o_ref[...] = jnp.dot(m_ref[...], x_ref[...], preferred_element_type=F32).astype(o_ref.dtype)


def _dft_outer(mat, x):
    b, k, w = x.shape
    r = mat.shape[0]
    tl = DFT_TL
    return pl.pallas_call(
        _dft_outer_kernel,
        grid=(b, w // tl),
        in_specs=[pl.BlockSpec((r, k), lambda bi, l: (0, 0)),
                  pl.BlockSpec((None, k, tl), lambda bi, l: (bi, 0, l))],
        out_specs=pl.BlockSpec((None, r, tl), lambda bi, l: (bi, 0, l)),
        out_shape=jax.ShapeDtypeStruct((b, r, w), BF16),
        compiler_params=_cparams(2, 32),
        name="dft_outer",
    )(mat, x)


SP_F1B = 16
SP_CT = 512


def _filt_spec_kernel(a_ref, m_ref, l1_ref, o_ref):
    inv_l1 = 1.0 / l1_ref[...]

    def body(f, carry):
        x = a_ref[:, f].reshape(2 * FFT_N2, a_ref.shape[-1])
        y = jnp.dot(m_ref[f], x, preferred_element_type=F32) * inv_l1
        o_ref[:, f] = y.reshape(2, FFT_N2, y.shape[-1])
        return carry

    lax.fori_loop(0, SP_F1B, body, 0)


def _filt_spec(a5, fwd2, l1):
    c = a5.shape[-1]
    blk = pl.BlockSpec((2, SP_F1B, FFT_N2, SP_CT), lambda fb, cb: (0, fb, 0, cb))
    return pl.pallas_call(
        _filt_spec_kernel,
        grid=(FFT_N1 // SP_F1B, c // SP_CT),
        in_specs=[blk,
                  pl.BlockSpec((SP_F1B, 2 * FFT_N2, 2 * FFT_N2), lambda fb, cb: (fb, 0, 0)),
                  pl.BlockSpec((1, SP_CT), lambda fb, cb: (0, cb))],
        out_specs=blk,
        out_shape=jax.ShapeDtypeStruct(a5.shape, F32),
        compiler_params=_cparams(2, 48),
        name="filter_spectrum",
    )(a5, fwd2, l1)


def _spec_core_kernel(a_ref, fw_ref, iv_ref, k_ref, o_ref):
    ct = a_ref.shape[-1]

    def body(f, carry):
        x = a_ref[:, f].reshape(2 * FFT_N2, ct)
        u = jnp.dot(fw_ref[f], x, preferred_element_type=F32)
        ure, uim = u[:FFT_N2], u[FFT_N2:]
        kre, kim = k_ref[0, f], k_ref[1, f]
        p = jnp.concatenate([ure * kre - uim * kim, ure * kim + uim * kre], axis=0).astype(BF16)
        y = jnp.dot(iv_ref[f], p, preferred_element_type=F32)
        o_ref[:, f] = y.reshape(2, FFT_N2, ct).astype(o_ref.dtype)
        return carry

    lax.fori_loop(0, SP_F1B, body, 0)


def _spec_core(a6, fwd2, inv2, kf):
    b = a6.shape[0]
    c = a6.shape[-1]
    ablk = pl.BlockSpec((None, 2, SP_F1B, FFT_N2, SP_CT), lambda fb, cb, bi: (bi, 0, fb, 0, cb))
    mblk = pl.BlockSpec((SP_F1B, 2 * FFT_N2, 2 * FFT_N2), lambda fb, cb, bi: (fb, 0, 0))
    return pl.pallas_call(
        _spec_core_kernel,
        grid=(FFT_N1 // SP_F1B, c // SP_CT, b),
        in_specs=[ablk, mblk, mblk,
                  pl.BlockSpec((2, SP_F1B, FFT_N2, SP_CT), lambda fb, cb, bi: (0, fb, 0, cb))],
        out_specs=ablk,
        out_shape=jax.ShapeDtypeStruct(a6.shape, BF16),
        compiler_params=_cparams(3, 56),
        name="spectral_core",
    )(a6, fwd2, inv2, kf)


def _hy_post_kernel(m_ref, b_ref, z_ref, x0_ref, bias_ref, o_ref):
    y = jnp.dot(m_ref[...], b_ref[...], preferred_element_type=F32)
    z = z_ref[...].astype(F32)
    o_ref[...] = ((y + bias_ref[...] * z) * x0_ref[...].astype(F32)).astype(o_ref.dtype)


def _hy_post(inv1, bv, zv, x0v, biasv):
    b, r, w = bv.shape
    t = inv1.shape[0]
    tl = DFT_TL
    vblk = pl.BlockSpec((None, t, tl), lambda bi, l: (bi, 0, l))
    return pl.pallas_call(
        _hy_post_kernel,
        grid=(b, w // tl),
        in_specs=[pl.BlockSpec((t, r), lambda bi, l: (0, 0)),
                  pl.BlockSpec((None, r, tl), lambda bi, l: (bi, 0, l)),
                  vblk, vblk,
                  pl.BlockSpec((1, tl), lambda bi, l: (0, l))],
        out_specs=vblk,
        out_shape=jax.ShapeDtypeStruct((b, t, w), BF16),
        compiler_params=_cparams(2, 32),
        name="hyena_post",
    )(inv1, bv, zv, x0v, biasv)


FLT_TR = 512


def _filter_kernel(emb_ref, w1_ref, b1_ref, w2_ref, b2_ref, w3_ref, b3_ref, w4_ref, fr_ref, dl_ref,
                   k_ref, l1_ref):
    r = pl.program_id(0)
    hp = lax.Precision.HIGHEST
    e = emb_ref[...]
    fr = fr_ref[...]
    h = jnp.sin(fr * (jnp.dot(e, w1_ref[...], precision=hp, preferred_element_type=F32) + b1_ref[...]))
    h = jnp.sin(fr * (jnp.dot(h, w2_ref[...], precision=hp, preferred_element_type=F32) + b2_ref[...]))
    h = jnp.sin(fr * (jnp.dot(h, w3_ref[...], precision=hp, preferred_element_type=F32) + b3_ref[...]))
    taps = jnp.dot(h, w4_ref[...], precision=hp, preferred_element_type=F32)
    decay = jnp.exp(-e[:, 0:1] * dl_ref[...])
    rows = r * FLT_TR + lax.broadcasted_iota(jnp.int32, taps.shape, 0)
    taps = jnp.where(rows == SEQ, 0.0, taps * decay)
    k_ref[...] = taps.astype(k_ref.dtype)

    @pl.when(r == 0)
    def _():
        l1_ref[...] = jnp.zeros_like(l1_ref)

    l1_ref[...] += jnp.sum(jnp.abs(taps), axis=0, keepdims=True)


def _filter_taps(emb, w1p, b1, w2, b2, w3, b3, w4, freq, deltas):
    n = emb.shape[0]
    fh = FILTER_HIDDEN
    small = lambda shape: pl.BlockSpec(shape, lambda r: (0, 0))
    fwd_tiles = SEQ // FLT_TR
    return pl.pallas_call(
        _filter_kernel,
        grid=(n // FLT_TR,),
        in_specs=[pl.BlockSpec((FLT_TR, fh), lambda r: (r, 0)),
                  small((fh, fh)), small((1, fh)), small((fh, fh)), small((1, fh)),
                  small((fh, fh)), small((1, fh)),
                  pl.BlockSpec((fh, HYENA_W), lambda r: (0, r // fwd_tiles)),
                  small((1, fh)), small((1, HYENA_W))],
        out_specs=[pl.BlockSpec((FLT_TR, HYENA_W), lambda r: (r, 0)),
                   pl.BlockSpec((1, HYENA_W), lambda r: (0, 0))],
        out_shape=[jax.ShapeDtypeStruct((n, HYENA_W), BF16),
                   jax.ShapeDtypeStruct((1, HYENA_W), F32)],
        compiler_params=_cparams(1, 32),
        name="filter_taps",
    )(emb, w1p, b1, w2, b2, w3, b3, w4, freq, deltas)


AT_TQ = 256


def _attn_kernel(q_ref, k_ref, v_ref, o_ref):
    k = k_ref[...]
    v = v_ref[...]
    for g in range(Q_PER_KV):
        sl = slice(g * HEAD_DIM, (g + 1) * HEAD_DIM)
        s = lax.dot_general(q_ref[:, sl], k, (((1,), (1,)), ((), ())), preferred_element_type=F32)
        m = jnp.max(s, axis=-1, keepdims=True)
        p = jnp.exp(s - m)
        l = jnp.sum(p, axis=-1, keepdims=True)
        o = jnp.dot(p.astype(BF16), v, preferred_element_type=F32)
        o_ref[:, sl] = (o / l).astype(o_ref.dtype)


def _attention(proj3):
    b, s, _ = proj3.shape
    gw = Q_PER_KV * HEAD_DIM
    return pl.pallas_call(
        _attn_kernel,
        grid=(b, N_KV_HEADS, s // AT_TQ),
        in_specs=[pl.BlockSpec((None, AT_TQ, gw), lambda bi, kv, qi: (bi, qi, COL_Q // gw + kv)),
                  pl.BlockSpec((None, s, HEAD_DIM), lambda bi, kv, qi: (bi, 0, COL_K // HEAD_DIM + kv)),
                  pl.BlockSpec((None, s, HEAD_DIM), lambda bi, kv, qi: (bi, 0, COL_V // HEAD_DIM + kv))],
        out_specs=pl.BlockSpec((None, AT_TQ, gw), lambda bi, kv, qi: (bi, qi, kv)),
        out_shape=jax.ShapeDtypeStruct((b, s, ATTN_W), BF16),
        compiler_params=_cparams(3, 48),
        name="attention",
    )(proj3, proj3, proj3)


MM_TN = 512


def _merge_kernel(yh_ref, ya_ref, gh_ref, ga_ref, wh_ref, wa_ref, o_ref, whb_ref, wab_ref):
    @pl.when(pl.program_id(1) == 0)
    def _():
        whb_ref[...] = wh_ref[...].astype(BF16)
        wab_ref[...] = wa_ref[...].astype(BF16)

    ph = jnp.dot(yh_ref[...], whb_ref[...], preferred_element_type=F32)
    pa = jnp.dot(ya_ref[...], wab_ref[...], preferred_element_type=F32)
    o_ref[...] = (gh_ref[...].astype(F32) * ph + ga_ref[...].astype(F32) * pa).astype(o_ref.dtype)


def _merge(yh, ya, proj, w_h, w_a, tm=1024):
    m = yh.shape[0]
    tn = MM_TN
    nj = D_MODEL // tn
    return pl.pallas_call(
        _merge_kernel,
        grid=(nj, m // tm),
        in_specs=[pl.BlockSpec((tm, HYENA_W), lambda j, i: (i, 0)),
                  pl.BlockSpec((tm, ATTN_W), lambda j, i: (i, 0)),
                  pl.BlockSpec((tm, tn), lambda j, i: (i, COL_G // tn + j)),
                  pl.BlockSpec((tm, tn), lambda j, i: (i, COL_G // tn + nj + j)),
                  pl.BlockSpec((HYENA_W, tn), lambda j, i: (0, j)),
                  pl.BlockSpec((ATTN_W, tn), lambda j, i: (0, j))],
        out_specs=pl.BlockSpec((tm, tn), lambda j, i: (i, j)),
        out_shape=jax.ShapeDtypeStruct((m, D_MODEL), BF16),
        scratch_shapes=[pltpu.VMEM((HYENA_W, tn), BF16), pltpu.VMEM((ATTN_W, tn), BF16)],
        compiler_params=_cparams(2, 48),
        name="merge",
    )(yh, ya, proj, proj, w_h, w_a)


def _resmm_kernel(a_ref, x_ref, w_ref, o_ref, wb_ref):
    @pl.when(pl.program_id(1) == 0)
    def _():
        wb_ref[...] = w_ref[...].astype(BF16)

    o_ref[...] = x_ref[...] + jnp.dot(a_ref[...], wb_ref[...], preferred_element_type=F32)


def _resmm(a, x, w, tm, vmem_mib):
    m, k = a.shape
    n = w.shape[1]
    tn = MM_TN
    return pl.pallas_call(
        _resmm_kernel,
        grid=(n // tn, m // tm),
        in_specs=[pl.BlockSpec((tm, k), lambda j, i: (i, 0)),
                  pl.BlockSpec((tm, tn), lambda j, i: (i, j)),
                  pl.BlockSpec((k, tn), lambda j, i: (0, j))],
        out_specs=pl.BlockSpec((tm, tn), lambda j, i: (i, j)),
        out_shape=jax.ShapeDtypeStruct((m, n), F32),
        scratch_shapes=[pltpu.VMEM((k, tn), BF16)],
        compiler_params=_cparams(2, vmem_mib),
        name="residual_matmul",
    )(a, x, w)


def _swiglu_kernel(h_ref, wg_ref, wu_ref, o_ref, wgb_ref, wub_ref):
    @pl.when(pl.program_id(1) == 0)
    def _():
        wgb_ref[...] = wg_ref[...].astype(BF16)
        wub_ref[...] = wu_ref[...].astype(BF16)

    h = h_ref[...]
    g = jnp.dot(h, wgb_ref[...], preferred_element_type=F32)
    u = jnp.dot(h, wub_ref[...], preferred_element_type=F32)
    o_ref[...] = (jax.nn.silu(g) * u).astype(o_ref.dtype)


def _swiglu(h, w_g, w_u, tm=1024):
    m, d = h.shape
    n = w_g.shape[1]
    tn = MM_TN
    wspec = pl.BlockSpec((d, tn), lambda j, i: (0, j))
    return pl.pallas_call(
        _swiglu_kernel,
        grid=(n // tn, m // tm),
        in_specs=[pl.BlockSpec((tm, d), lambda j, i: (i, 0)), wspec, wspec],
        out_specs=pl.BlockSpec((tm, tn), lambda j, i: (i, j)),
        out_shape=jax.ShapeDtypeStruct((m, n), BF16),
        scratch_shapes=[pltpu.VMEM((d, tn), BF16), pltpu.VMEM((d, tn), BF16)],
        compiler_params=_cparams(2, 48),
        name="swiglu",
    )(h, w_g, w_u)


def _layer(x, mix_norm_g, w_in, b_gate, hy_conv_w, hy_conv_b,
           flt_w1, flt_b1, flt_w2, flt_b2, flt_w3, flt_b3, flt_w4, flt_freq, hy_bias,
           q_norm_g, k_norm_g, w_br_hyena, w_br_attn, w_out,
           ffn_norm_g, w_ffn_gate, w_ffn_up, w_ffn_down):
    b, s, d = x.shape
    m = b * s
    row = lambda a: a.reshape(1, -1)
    tabs = _dft_tables()
    cos, sin = _rope_tables()
    xm = x.reshape(m, d)

    h = _rmsnorm(xm, row(mix_norm_g))
    proj = _inproj(h, w_in, row(b_gate), row(q_norm_g), row(k_norm_g), cos, sin)
    proj3 = proj.reshape(b, s, IN_W)

    emb, deltas = _filter_tables()
    w1p = jnp.concatenate([flt_w1, jnp.zeros((FILTER_HIDDEN - FILTER_EMB, FILTER_HIDDEN), F32)], axis=0)
    taps, l1 = _filter_taps(emb, w1p, row(flt_b1), flt_w2, row(flt_b2), flt_w3, row(flt_b3),
                            flt_w4, row(flt_freq), deltas)
    wlanes = FFT_N2 * HYENA_W
    ka = _dft_outer(tabs["s1_full"], taps.reshape(1, FFT_N1, wlanes))
    kf = _filt_spec(ka.reshape(2, FFT_N1, FFT_N2, HYENA_W), tabs["fwd2"], l1)

    z, x0c = _hy_pre(proj3, hy_conv_w, row(hy_conv_b))
    za = _dft_outer(tabs["s1_half"], z.reshape(b, FFT_N1 // 2, wlanes))
    zb = _spec_core(za.reshape(b, 2, FFT_N1, FFT_N2, HYENA_W), tabs["fwd2"], tabs["inv2"], kf)
    bias_v = jnp.tile(row(hy_bias), (1, FFT_N2))
    y_h = _hy_post(tabs["inv1"], zb.reshape(b, 2 * FFT_N1, wlanes),
                   z.reshape(b, FFT_N1 // 2, wlanes), x0c.reshape(b, FFT_N1 // 2, wlanes), bias_v)
    y_h = y_h.reshape(m, HYENA_W)

    y_a = _attention(proj3).reshape(m, ATTN_W)

    merged = _merge(y_h, y_a, proj, w_br_hyena, w_br_attn)
    x1 = _resmm(merged, xm, w_out, tm=1024, vmem_mib=48)

    h2 = _rmsnorm(x1, row(ffn_norm_g))
    act = _swiglu(h2, w_ffn_gate, w_ffn_up)
    out = _resmm(act, x1, w_ffn_down, tm=512, vmem_mib=56)
    return out.reshape(b, s, d)


def kernel(x, mix_norm_g, w_in, b_gate, hy_conv_w, hy_conv_b, flt_w1, flt_b1, flt_w2, flt_b2, flt_w3, flt_b3, flt_w4, flt_freq, hy_bias, q_norm_g, k_norm_g, w_br_hyena, w_br_attn, w_out, ffn_norm_g, w_ffn_gate, w_ffn_up, w_ffn_down):
    params = (mix_norm_g, w_in, b_gate, hy_conv_w, hy_conv_b, flt_w1, flt_b1, flt_w2, flt_b2,
              flt_w3, flt_b3, flt_w4, flt_freq, hy_bias, q_norm_g, k_norm_g, w_br_hyena, w_br_attn,
              w_out, ffn_norm_g, w_ffn_gate, w_ffn_up, w_ffn_down)
    for l in range(mix_norm_g.shape[0]):
        x = _layer(x, *(p[l] for p in params))
    return x
```

```python
import functools
import math

import numpy as np
import jax
import jax.numpy as jnp
from jax import lax
from jax.experimental import pallas as pl
from jax.experimental.pallas import tpu as pltpu

F32 = jnp.float32
BF16 = jnp.bfloat16

D_MODEL = 2048
SEQ = 4096
GRID_W = 64
HEAD_DIM = 128
N_Q_HEADS = 8
N_KV_HEADS = 2
Q_PER_KV = N_Q_HEADS // N_KV_HEADS
ATTN_W = N_Q_HEADS * HEAD_DIM
KV_W = N_KV_HEADS * HEAD_DIM
ROPE_THETA = 10000.0
HYENA_W = D_MODEL - ATTN_W
SHORT_TAPS = 3
FILTER_EMB = 33
FILTER_HIDDEN = 64
DECAY_TARGET = 1e-2
FAST_DECAY_PCT = 0.3
SLOW_DECAY_PCT = 1.5
IN_W = 3 * HYENA_W + ATTN_W + 2 * KV_W + 2 * D_MODEL
D_FF = 5632
EPS = 1e-6

COL_Q = 3 * HYENA_W
COL_K = COL_Q + ATTN_W
COL_V = COL_K + KV_W
COL_G = COL_V + KV_W

FFT_N = 2 * SEQ
FFT_N1 = 64
FFT_N2 = 128

MIB = 1024 * 1024


def _cparams(n_axes, vmem_mib):
    return pltpu.CompilerParams(
        dimension_semantics=("arbitrary",) * n_axes,
        vmem_limit_bytes=vmem_mib * MIB,
    )


DFT_R = 8


@functools.lru_cache(maxsize=None)
def _dft_tables_np():
    n, n1, n2, r = FFT_N, FFT_N1, FFT_N2, DFT_R
    eye = np.eye(r)
    f1 = np.arange(n1)
    ang1 = 2.0 * np.pi * ((f1[:, None] * f1[None, :]) % n1) / n1
    c1, s1 = np.cos(ang1), np.sin(ang1)
    h = n1 // 2
    m_fwd = np.block([[c1[:, :h], s1[:, :h]], [-s1[:, :h], c1[:, :h]]])
    m_flt = np.concatenate([c1, -s1], axis=0)
    ct, st = c1[:h, :], s1[:h, :]
    m_inv = np.block([[ct, -st], [st, ct]]) / n
    s2 = np.arange(n2)
    f = f1[:, None, None] + n1 * s2[None, :, None]
    th = 2.0 * np.pi * ((f * s2[None, None, :]) % n) / n
    c, s = np.cos(th), np.sin(th)
    fwd2 = np.concatenate(
        [np.concatenate([c, s], axis=2), np.concatenate([-s, c], axis=2)], axis=1)
    c_t, s_t = np.transpose(c, (0, 2, 1)), np.transpose(s, (0, 2, 1))
    inv2 = np.concatenate(
        [np.concatenate([c_t, -s_t], axis=2), np.concatenate([s_t, c_t], axis=2)], axis=1)
    f32 = lambda a: np.ascontiguousarray(a, dtype=np.float32)
    return dict(k_fwd=f32(np.kron(m_fwd, eye)), k_flt=f32(np.kron(m_flt, eye)),
                k_inv=f32(np.kron(m_inv, eye)), fwd2=f32(fwd2), inv2=f32(inv2))


def _dft_tables():
    return {k: jnp.asarray(v).astype(BF16) for k, v in _dft_tables_np().items()}


@functools.lru_cache(maxsize=None)
def _rope_tables_np():
    half = HEAD_DIM // 2
    inv = ROPE_THETA ** (-np.arange(0, half, 2, dtype=np.float64) / half)
    pos = np.arange(SEQ)
    ang_r = (pos // GRID_W)[:, None] * inv[None, :]
    ang_c = (pos % GRID_W)[:, None] * inv[None, :]
    cos = np.concatenate([np.cos(ang_r)] * 2 + [np.cos(ang_c)] * 2, axis=-1)
    sin = np.concatenate([-np.sin(ang_r), np.sin(ang_r), -np.sin(ang_c), np.sin(ang_c)], axis=-1)
    return cos.astype(np.float32), sin.astype(np.float32)


@functools.lru_cache(maxsize=None)
def _filter_tables_np():
    L = SEQ
    bands = (FILTER_EMB - 1) // 2
    pos = np.concatenate([np.arange(L, dtype=np.float64), L - np.arange(L, dtype=np.float64)])
    t = pos / max(L - 1, 1)
    fb = np.linspace(1e-4, bands - 1, bands)
    ang = (2.0 * math.pi * pos / L)[:, None] * fb[None, :]
    emb = np.concatenate([t[:, None], np.cos(ang), -np.sin(ang),
                          np.zeros((2 * L, FILTER_HIDDEN - FILTER_EMB))], axis=-1)
    max_decay = math.log(DECAY_TARGET) / FAST_DECAY_PCT
    min_decay = math.log(DECAY_TARGET) / SLOW_DECAY_PCT
    deltas = np.abs(np.linspace(min_decay, max_decay, HYENA_W))
    return emb.astype(np.float32), deltas[None, :].astype(np.float32)


def _rmsnorm_kernel(x_ref, g_ref, o_ref):
    x = x_ref[...]
    ms = jnp.mean(x * x, axis=-1, keepdims=True)
    o_ref[...] = (x * lax.rsqrt(ms + EPS) * g_ref[...]).astype(o_ref.dtype)


def _rmsnorm(x, g, tm=512):
    m, d = x.shape
    return pl.pallas_call(
        _rmsnorm_kernel,
        grid=(m // tm,),
        in_specs=[pl.BlockSpec((tm, d), lambda i: (i, 0)),
                  pl.BlockSpec((1, d), lambda i: (0, 0))],
        out_specs=pl.BlockSpec((tm, d), lambda i: (i, 0)),
        out_shape=jax.ShapeDtypeStruct((m, d), BF16),
        compiler_params=_cparams(1, 32),
        name="rmsnorm",
    )(x, g)


IN_TN = 512
IN_JQ = COL_Q // IN_TN
IN_JK = COL_K // IN_TN
IN_JG = COL_G // IN_TN


def _head_norm_rope(x, g, cos, sin, scale):
    ms = jnp.mean(x * x, axis=-1, keepdims=True)
    y = x * lax.rsqrt(ms + EPS) * g
    lane = lax.broadcasted_iota(jnp.int32, y.shape, 1)
    swapped = jnp.where((lane % 64) < 32, pltpu.roll(y, 96, 1), pltpu.roll(y, 32, 1))
    out = y * cos + swapped * sin
    return out * scale if scale != 1.0 else out


def _inproj_kernel(h_ref, w_ref, bg_ref, qg_ref, kg_ref, cos_ref, sin_ref, o_ref, wb_ref):
    j = pl.program_id(0)
    i = pl.program_id(1)

    @pl.when(i == 0)
    def _():
        wb_ref[...] = w_ref[...].astype(BF16)

    acc = jnp.dot(h_ref[...], wb_ref[...], preferred_element_type=F32)

    @pl.when(j < IN_JQ)
    def _():
        o_ref[...] = acc.astype(BF16)

    @pl.when((j >= IN_JQ) & (j < IN_JK))
    def _():
        cos, sin = cos_ref[...], sin_ref[...]
        for hh in range(IN_TN // HEAD_DIM):
            sl = slice(hh * HEAD_DIM, (hh + 1) * HEAD_DIM)
            o_ref[:, sl] = _head_norm_rope(acc[:, sl], qg_ref[...], cos, sin, HEAD_DIM ** -0.5).astype(BF16)

    @pl.when(j == IN_JK)
    def _():
        cos, sin = cos_ref[...], sin_ref[...]
        for hh in range(N_KV_HEADS):
            sl = slice(hh * HEAD_DIM, (hh + 1) * HEAD_DIM)
            o_ref[:, sl] = _head_norm_rope(acc[:, sl], kg_ref[...], cos, sin, 1.0).astype(BF16)
        o_ref[:, KV_W:] = acc[:, KV_W:].astype(BF16)

    @pl.when(j >= IN_JG)
    def _():
        o_ref[...] = jax.nn.sigmoid(acc + bg_ref[...]).astype(BF16)


def _inproj(h, w_in, b_gate, q_g, k_g, cos, sin, tm=1024):
    m, d = h.shape
    tn = IN_TN
    s_blocks = SEQ // tm
    return pl.pallas_call(
        _inproj_kernel,
        grid=(IN_W // tn, m // tm),
        in_specs=[
            pl.BlockSpec((tm, d), lambda j, i: (i, 0)),
            pl.BlockSpec((d, tn), lambda j, i: (0, j)),
            pl.BlockSpec((1, tn), lambda j, i: (0, jnp.maximum(j - IN_JG, 0))),
            pl.BlockSpec((1, HEAD_DIM), lambda j, i: (0, 0)),
            pl.BlockSpec((1, HEAD_DIM), lambda j, i: (0, 0)),
            pl.BlockSpec((tm, HEAD_DIM), lambda j, i: (i % s_blocks, 0)),
            pl.BlockSpec((tm, HEAD_DIM), lambda j, i: (i % s_blocks, 0)),
        ],
        out_specs=pl.BlockSpec((tm, tn), lambda j, i: (i, j)),
        out_shape=jax.ShapeDtypeStruct((m, IN_W), BF16),
        scratch_shapes=[pltpu.VMEM((d, tn), BF16)],
        compiler_params=_cparams(2, 48),
        name="inproj",
    )(h, w_in, b_gate, q_g, k_g, cos, sin)


HY_CT = 256
HY_RC = 256


def _hy_pre_kernel(x0_ref, x1_ref, v_ref, w0_ref, w1_ref, wv_ref, b0_ref, b1_ref, bv_ref,
                   z_ref, x0c_ref, pad_ref):
    L = x0_ref.shape[0]
    ct = x0_ref.shape[1]
    nchunk = L // HY_RC
    zeros8 = jnp.zeros((8, ct), F32)
    for a, src in enumerate((x0_ref, x1_ref, v_ref)):
        pad_ref[a, 0:8, :] = zeros8
        pad_ref[a, L + 8:L + 16, :] = zeros8

        def fill(r, carry, a=a, src=src):
            base = pl.multiple_of(r * HY_RC, HY_RC)
            pad_ref[a, pl.ds(base + 8, HY_RC), :] = src[pl.ds(base, HY_RC), :].astype(F32)
            return carry

        lax.fori_loop(0, nchunk, fill, 0)

    def conv(a, w_ref, b_ref, base):
        e = pad_ref[a, pl.ds(base, HY_RC + 16), :]
        up = pltpu.roll(e, 1, 0)[8:8 + HY_RC]
        mid = e[8:8 + HY_RC]
        dn = pltpu.roll(e, HY_RC + 15, 0)[8:8 + HY_RC]
        return b_ref[...] + up * w_ref[0:1, :] + mid * w_ref[1:2, :] + dn * w_ref[2:3, :]

    def body(r, carry):
        base = pl.multiple_of(r * HY_RC, HY_RC)
        x0c = conv(0, w0_ref, b0_ref, base)
        x1c = conv(1, w1_ref, b1_ref, base)
        vc = conv(2, wv_ref, bv_ref, base)
        z_ref[pl.ds(base, HY_RC), :] = vc * x1c
        x0c_ref[pl.ds(base, HY_RC), :] = x0c.astype(BF16)
        return carry

    lax.fori_loop(0, nchunk, body, 0)


def _hy_pre(proj3, conv_w, conv_b):
    b, L, _ = proj3.shape
    ct = HY_CT
    nct = HYENA_W // ct
    in_specs = []
    for grp in range(3):
        in_specs.append(pl.BlockSpec((None, L, ct), lambda bi, c, grp=grp: (bi, 0, c + grp * nct)))
    for grp in range(3):
        in_specs.append(pl.BlockSpec((SHORT_TAPS, ct), lambda bi, c, grp=grp: (0, c + grp * nct)))
    for grp in range(3):
        in_specs.append(pl.BlockSpec((1, ct), lambda bi, c, grp=grp: (0, c + grp * nct)))
    out_spec = pl.BlockSpec((None, L, ct), lambda bi, c: (bi, 0, c))
    return pl.pallas_call(
        _hy_pre_kernel,
        grid=(b, nct),
        in_specs=in_specs,
        out_specs=[out_spec, out_spec],
        out_shape=[jax.ShapeDtypeStruct((b, L, HYENA_W), F32),
                   jax.ShapeDtypeStruct((b, L, HYENA_W), BF16)],
        scratch_shapes=[pltpu.VMEM((3, L + 16, ct), F32)],
        compiler_params=_cparams(2, 48),
        name="hyena_pre",
    )(proj3, proj3, proj3, conv_w, conv_w, conv_w, conv_b, conv_b, conv_b)


CV_CT = 256
CV_F1B = 8
CV_NF = FFT_N1 // CV_F1B
CV_SLAB = 2 * DFT_R
CV_NSLAB = FFT_N2 // CV_SLAB
CV_HALF = FFT_N1 // 2


def _outer_fwd_slab(src_ref, k_ref, a_ref, j, rows_in):
    ct = a_ref.shape[-1]
    halves = []
    for h in range(2):
        r = src_ref[:, :, 2 * j + h] if src_ref.ndim == 5 else src_ref[:, 2 * j + h]
        r = r.reshape(rows_in, ct).astype(BF16)
        o = jnp.dot(k_ref[...], r, preferred_element_type=F32)
        halves.append(o.reshape(2 * FFT_N1, DFT_R, ct))
    slab = jnp.concatenate(halves, axis=1).astype(BF16)
    a_ref[:, :, pl.ds(pl.multiple_of(j * CV_SLAB, CV_SLAB), CV_SLAB), :] = slab.reshape(
        2, FFT_N1, CV_SLAB, ct)


def _hy_conv_kernel(z_ref, x0_ref, bias_ref, kf_ref, ki_ref, fw_ref, iv_ref, spec_ref, o_ref, a_ref):
    s = pl.program_id(2)
    ct = a_ref.shape[-1]

    @pl.when(s == 0)
    def _():
        def body(j, carry):
            _outer_fwd_slab(z_ref, kf_ref, a_ref, j, 2 * CV_HALF * DFT_R)
            return carry

        lax.fori_loop(0, CV_NSLAB, body, 0)

    @pl.when((s >= 1) & (s <= CV_NF))
    def _():
        f0 = (s - 1) * CV_F1B
        for fl in range(CV_F1B):
            x = a_ref[:, f0 + fl].reshape(2 * FFT_N2, ct)
            u = jnp.dot(fw_ref[fl], x, preferred_element_type=F32)
            ure, uim = u[:FFT_N2], u[FFT_N2:]
            kre, kim = spec_ref[0, fl], spec_ref[1, fl]
            p = jnp.concatenate([ure * kre - uim * kim, ure * kim + uim * kre], axis=0).astype(BF16)
            y = jnp.dot(iv_ref[fl], p, preferred_element_type=F32)
            a_ref[:, f0 + fl] = y.reshape(2, FFT_N2, ct).astype(BF16)

    @pl.when(s == CV_NF + 1)
    def _():
        bias = bias_ref[...]

        def body(j, carry):
            slab = a_ref[:, :, pl.ds(pl.multiple_of(j * CV_SLAB, CV_SLAB), CV_SLAB), :].astype(F32)
            x0 = x0_ref[:, :, j].astype(F32)
            halves = []
            for h in range(2):
                r = slab[:, :, h * DFT_R:(h + 1) * DFT_R, :].reshape(2 * FFT_N1 * DFT_R, ct).astype(BF16)
                y = jnp.dot(ki_ref[...], r, preferred_element_type=F32).reshape(2, CV_HALF, DFT_R, ct)
                z = z_ref[:, :, 2 * j + h]
                halves.append((y + bias * z) * x0[:, :, h * DFT_R:(h + 1) * DFT_R, :])
            o_ref[:, :, j] = jnp.concatenate(halves, axis=2).astype(o_ref.dtype)
            return carry

        lax.fori_loop(0, CV_NSLAB, body, 0)


def _hy_conv(z, x0c, bias, spec, tabs):
    b, L, c = z.shape
    assert b % 2 == 0 and L * 2 == FFT_N
    ct = CV_CT
    z5 = z.reshape(b, CV_HALF, FFT_N2 // DFT_R, DFT_R, c)
    x5 = x0c.reshape(b, CV_HALF, CV_NSLAB, CV_SLAB, c)
    fidx = lambda s: jnp.clip(s - 1, 0, CV_NF - 1)
    once = pl.Buffered(1)
    tab_spec = pl.BlockSpec((CV_F1B, 2 * FFT_N2, 2 * FFT_N2), lambda p, cb, s: (fidx(s), 0, 0))
    io16 = pl.BlockSpec((2, CV_HALF, CV_NSLAB, CV_SLAB, ct), lambda p, cb, s: (p, 0, 0, 0, cb))
    out = pl.pallas_call(
        _hy_conv_kernel,
        grid=(b // 2, c // ct, CV_NF + 2),
        in_specs=[
            pl.BlockSpec((2, CV_HALF, FFT_N2 // DFT_R, DFT_R, ct), lambda p, cb, s: (p, 0, 0, 0, cb),
                         pipeline_mode=once),
            io16,
            pl.BlockSpec((1, ct), lambda p, cb, s: (0, cb)),
            pl.BlockSpec(tabs["k_fwd"].shape, lambda p, cb, s: (0, 0), pipeline_mode=once),
            pl.BlockSpec(tabs["k_inv"].shape, lambda p, cb, s: (0, 0), pipeline_mode=once),
            tab_spec, tab_spec,
            pl.BlockSpec((2, CV_F1B, FFT_N2, ct), lambda p, cb, s: (0, fidx(s), 0, cb)),
        ],
        out_specs=io16,
        out_shape=jax.ShapeDtypeStruct(x5.shape, BF16),
        scratch_shapes=[pltpu.VMEM((2, FFT_N1, FFT_N2, ct), BF16)],
        compiler_params=_cparams(3, 56),
        name="hyena_conv",
    )(z5, x5, bias, tabs["k_fwd"], tabs["k_inv"], tabs["fwd2"], tabs["inv2"], spec)
    return out.reshape(b, L, c)


def _filt_spec_kernel(t_ref, l1_ref, kf_ref, fw_ref, o_ref, a_ref):
    s = pl.program_id(1)
    ct = a_ref.shape[-1]

    @pl.when(s == 0)
    def _():
        def body(j, carry):
            _outer_fwd_slab(t_ref, kf_ref, a_ref, j, FFT_N1 * DFT_R)
            return carry

        lax.fori_loop(0, CV_NSLAB, body, 0)

    @pl.when(s >= 1)
    def _():
        inv_l1 = 1.0 / l1_ref[...]
        f0 = (s - 1) * CV_F1B
        for fl in range(CV_F1B):
            x = a_ref[:, f0 + fl].reshape(2 * FFT_N2, ct)
            y = jnp.dot(fw_ref[fl], x, preferred_element_type=F32) * inv_l1
            o_ref[:, fl] = y.reshape(2, FFT_N2, ct)


def _filt_spec(taps, l1, tabs):
    n, c = taps.shape
    ct = CV_CT
    t4 = taps.reshape(FFT_N1, FFT_N2 // DFT_R, DFT_R, c)
    fidx = lambda s: jnp.clip(s - 1, 0, CV_NF - 1)
    once = pl.Buffered(1)
    return pl.pallas_call(
        _filt_spec_kernel,
        grid=(c // ct, CV_NF + 1),
        in_specs=[
            pl.BlockSpec((FFT_N1, FFT_N2 // DFT_R, DFT_R, ct), lambda cb, s: (0, 0, 0, cb),
                         pipeline_mode=once),
            pl.BlockSpec((1, ct), lambda cb, s: (0, cb)),
            pl.BlockSpec(tabs["k_flt"].shape, lambda cb, s: (0, 0), pipeline_mode=once),
            pl.BlockSpec((CV_F1B, 2 * FFT_N2, 2 * FFT_N2), lambda cb, s: (fidx(s), 0, 0)),
        ],
        out_specs=pl.BlockSpec((2, CV_F1B, FFT_N2, ct), lambda cb, s: (0, fidx(s), 0, cb)),
        out_shape=jax.ShapeDtypeStruct((2, FFT_N1, FFT_N2, c), F32),
        scratch_shapes=[pltpu.VMEM((2, FFT_N1, FFT_N2, ct), BF16)],
        compiler_params=_cparams(2, 48),
        name="filter_spectrum",
    )(t4, l1, tabs["k_flt"], tabs["fwd2"])


FLT_TR = 512


def _filter_kernel(emb_ref, w1_ref, b1_ref, w2_ref, b2_ref, w3_ref, b3_ref, w4_ref, fr_ref, dl_ref,
                   k_ref, l1_ref):
    r = pl.program_id(0)
    hp = lax.Precision.HIGHEST
    e = emb_ref[...]
    fr = fr_ref[...]
    h = jnp.sin(fr * (jnp.dot(e, w1_ref[...], precision=hp, preferred_element_type=F32) + b1_ref[...]))
    h = jnp.sin(fr * (jnp.dot(h, w2_ref[...], precision=hp, preferred_element_type=F32) + b2_ref[...]))
    h = jnp.sin(fr * (jnp.dot(h, w3_ref[...], precision=hp, preferred_element_type=F32) + b3_ref[...]))
    taps = jnp.dot(h, w4_ref[...], precision=hp, preferred_element_type=F32)
    decay = jnp.exp(-e[:, 0:1] * dl_ref[...])
    rows = r * FLT_TR + lax.broadcasted_iota(jnp.int32, taps.shape, 0)
    taps = jnp.where(rows == SEQ, 0.0, taps * decay)
    k_ref[...] = taps.astype(k_ref.dtype)

    @pl.when(r == 0)
    def _():
        l1_ref[...] = jnp.zeros_like(l1_ref)

    l1_ref[...] += jnp.sum(jnp.abs(taps), axis=0, keepdims=True)


def _filter_taps(emb, w1p, b1, w2, b2, w3, b3, w4, freq, deltas):
    n = emb.shape[0]
    fh = FILTER_HIDDEN
    small = lambda shape: pl.BlockSpec(shape, lambda r: (0, 0))
    fwd_tiles = SEQ // FLT_TR
    return pl.pallas_call(
        _filter_kernel,
        grid=(n // FLT_TR,),
        in_specs=[pl.BlockSpec((FLT_TR, fh), lambda r: (r, 0)),
                  small((fh, fh)), small((1, fh)), small((fh, fh)), small((1, fh)),
                  small((fh, fh)), small((1, fh)),
                  pl.BlockSpec((fh, HYENA_W), lambda r: (0, r // fwd_tiles)),
                  small((1, fh)), small((1, HYENA_W))],
        out_specs=[pl.BlockSpec((FLT_TR, HYENA_W), lambda r: (r, 0)),
                   pl.BlockSpec((1, HYENA_W), lambda r: (0, 0))],
        out_shape=[jax.ShapeDtypeStruct((n, HYENA_W), F32),
                   jax.ShapeDtypeStruct((1, HYENA_W), F32)],
        compiler_params=_cparams(1, 32),
        name="filter_taps",
    )(emb, w1p, b1, w2, b2, w3, b3, w4, freq, deltas)


AT_TQ = 256


def _attn_kernel(q_ref, k_ref, v_ref, o_ref):
    k = k_ref[...]
    v = v_ref[...]
    for g in range(Q_PER_KV):
        sl = slice(g * HEAD_DIM, (g + 1) * HEAD_DIM)
        s = lax.dot_general(q_ref[:, sl], k, (((1,), (1,)), ((), ())), preferred_element_type=F32)
        m = jnp.max(s, axis=-1, keepdims=True)
        p = jnp.exp(s - m)
        l = jnp.sum(p, axis=-1, keepdims=True)
        o = jnp.dot(p.astype(BF16), v, preferred_element_type=F32)
        o_ref[:, sl] = (o / l).astype(o_ref.dtype)


def _attention(proj3):
    b, s, _ = proj3.shape
    gw = Q_PER_KV * HEAD_DIM
    return pl.pallas_call(
        _attn_kernel,
        grid=(b, N_KV_HEADS, s // AT_TQ),
        in_specs=[pl.BlockSpec((None, AT_TQ, gw), lambda bi, kv, qi: (bi, qi, COL_Q // gw + kv)),
                  pl.BlockSpec((None, s, HEAD_DIM), lambda bi, kv, qi: (bi, 0, COL_K // HEAD_DIM + kv)),
                  pl.BlockSpec((None, s, HEAD_DIM), lambda bi, kv, qi: (bi, 0, COL_V // HEAD_DIM + kv))],
        out_specs=pl.BlockSpec((None, AT_TQ, gw), lambda bi, kv, qi: (bi, qi, kv)),
        out_shape=jax.ShapeDtypeStruct((b, s, ATTN_W), BF16),
        compiler_params=_cparams(3, 48),
        name="attention",
    )(proj3, proj3, proj3)


MM_TN = 512


def _merge_kernel(yh_ref, ya_ref, gh_ref, ga_ref, wh_ref, wa_ref, o_ref, whb_ref, wab_ref):
    @pl.when(pl.program_id(1) == 0)
    def _():
        whb_ref[...] = wh_ref[...].astype(BF16)
        wab_ref[...] = wa_ref[...].astype(BF16)

    ph = jnp.dot(yh_ref[...], whb_ref[...], preferred_element_type=F32)
    pa = jnp.dot(ya_ref[...], wab_ref[...], preferred_element_type=F32)
    o_ref[...] = (gh_ref[...].astype(F32) * ph + ga_ref[...].astype(F32) * pa).astype(o_ref.dtype)


def _merge(yh, ya, proj, w_h, w_a, tm=1024):
    m = yh.shape[0]
    tn = MM_TN
    nj = D_MODEL // tn
    return pl.pallas_call(
        _merge_kernel,
        grid=(nj, m // tm),
        in_specs=[pl.BlockSpec((tm, HYENA_W), lambda j, i: (i, 0)),
                  pl.BlockSpec((tm, ATTN_W), lambda j, i: (i, 0)),
                  pl.BlockSpec((tm, tn), lambda j, i: (i, COL_G // tn + j)),
                  pl.BlockSpec((tm, tn), lambda j, i: (i, COL_G // tn + nj + j)),
                  pl.BlockSpec((HYENA_W, tn), lambda j, i: (0, j)),
                  pl.BlockSpec((ATTN_W, tn), lambda j, i: (0, j))],
        out_specs=pl.BlockSpec((tm, tn), lambda j, i: (i, j)),
        out_shape=jax.ShapeDtypeStruct((m, D_MODEL), BF16),
        scratch_shapes=[pltpu.VMEM((HYENA_W, tn), BF16), pltpu.VMEM((ATTN_W, tn), BF16)],
        compiler_params=_cparams(2, 48),
        name="merge",
    )(yh, ya, proj, proj, w_h, w_a)


def _resmm_kernel(a_ref, x_ref, w_ref, o_ref, wb_ref):
    @pl.when(pl.program_id(1) == 0)
    def _():
        wb_ref[...] = w_ref[...].astype(BF16)

    o_ref[...] = x_ref[...] + jnp.dot(a_ref[...], wb_ref[...], preferred_element_type=F32)


def _resmm(a, x, w, tm, vmem_mib):
    m, k = a.shape
    n = w.shape[1]
    tn = MM_TN
    return pl.pallas_call(
        _resmm_kernel,
        grid=(n // tn, m // tm),
        in_specs=[pl.BlockSpec((tm, k), lambda j, i: (i, 0)),
                  pl.BlockSpec((tm, tn), lambda j, i: (i, j)),
                  pl.BlockSpec((k, tn), lambda j, i: (0, j))],
        out_specs=pl.BlockSpec((tm, tn), lambda j, i: (i, j)),
        out_shape=jax.ShapeDtypeStruct((m, n), F32),
        scratch_shapes=[pltpu.VMEM((k, tn), BF16)],
        compiler_params=_cparams(2, vmem_mib),
        name="residual_matmul",
    )(a, x, w)


def _swiglu_kernel(h_ref, wg_ref, wu_ref, o_ref, wgb_ref, wub_ref):
    @pl.when(pl.program_id(1) == 0)
    def _():
        wgb_ref[...] = wg_ref[...].astype(BF16)
        wub_ref[...] = wu_ref[...].astype(BF16)

    h = h_ref[...]
    g = jnp.dot(h, wgb_ref[...], preferred_element_type=F32)
    u = jnp.dot(h, wub_ref[...], preferred_element_type=F32)
    o_ref[...] = (jax.nn.silu(g) * u).astype(o_ref.dtype)


def _swiglu(h, w_g, w_u, tm=1024):
    m, d = h.shape
    n = w_g.shape[1]
    tn = MM_TN
    wspec = pl.BlockSpec((d, tn), lambda j, i: (0, j))
    return pl.pallas_call(
        _swiglu_kernel,
        grid=(n // tn, m // tm),
        in_specs=[pl.BlockSpec((tm, d), lambda j, i: (i, 0)), wspec, wspec],
        out_specs=pl.BlockSpec((tm, tn), lambda j, i: (i, j)),
        out_shape=jax.ShapeDtypeStruct((m, n), BF16),
        scratch_shapes=[pltpu.VMEM((d, tn), BF16), pltpu.VMEM((d, tn), BF16)],
        compiler_params=_cparams(2, 48),
        name="swiglu",
    )(h, w_g, w_u)


def _layer(x, mix_norm_g, w_in, b_gate, hy_conv_w, hy_conv_b,
           flt_w1, flt_b1, flt_w2, flt_b2, flt_w3, flt_b3, flt_w4, flt_freq, hy_bias,
           q_norm_g, k_norm_g, w_br_hyena, w_br_attn, w_out,
           ffn_norm_g, w_ffn_gate, w_ffn_up, w_ffn_down):
    b, s, d = x.shape
    m = b * s
    row = lambda a: a.reshape(1, -1)
    tabs = _dft_tables()
    cos, sin = (jnp.asarray(t) for t in _rope_tables_np())
    xm = x.reshape(m, d)

    h = _rmsnorm(xm, row(mix_norm_g))
    proj = _inproj(h, w_in, row(b_gate), row(q_norm_g), row(k_norm_g), cos, sin)
    proj3 = proj.reshape(b, s, IN_W)

    emb, deltas = (jnp.asarray(t) for t in _filter_tables_np())
    w1p = jnp.concatenate([flt_w1, jnp.zeros((FILTER_HIDDEN - FILTER_EMB, FILTER_HIDDEN), F32)], axis=0)
    taps, l1 = _filter_taps(emb, w1p, row(flt_b1), flt_w2, row(flt_b2), flt_w3, row(flt_b3),
                            flt_w4, row(flt_freq), deltas)
    spec = _filt_spec(taps, l1, tabs)

    z, x0c = _hy_pre(proj3, hy_conv_w, row(hy_conv_b))
    y_h = _hy_conv(z, x0c, row(hy_bias), spec, tabs).reshape(m, HYENA_W)

    y_a = _attention(proj3).reshape(m, ATTN_W)

    merged = _merge(y_h, y_a, proj, w_br_hyena, w_br_attn)
    x1 = _resmm(merged, xm, w_out, tm=1024, vmem_mib=48)

    h2 = _rmsnorm(x1, row(ffn_norm_g))
    act = _swiglu(h2, w_ffn_gate, w_ffn_up)
    out = _resmm(act, x1, w_ffn_down, tm=512, vmem_mib=56)
    return out.reshape(b, s, d)


def kernel(x, mix_norm_g, w_in, b_gate, hy_conv_w, hy_conv_b, flt_w1, flt_b1, flt_w2, flt_b2, flt_w3, flt_b3, flt_w4, flt_freq, hy_bias, q_norm_g, k_norm_g, w_br_hyena, w_br_attn, w_out, ffn_norm_g, w_ffn_gate, w_ffn_up, w_ffn_down):
    params = (mix_norm_g, w_in, b_gate, hy_conv_w, hy_conv_b, flt_w1, flt_b1, flt_w2, flt_b2,
              flt_w3, flt_b3, flt_w4, flt_freq, hy_bias, q_norm_g, k_norm_g, w_br_hyena, w_br_attn,
              w_out, ffn_norm_g, w_ffn_gate, w_ffn_up, w_ffn_down)
    for l in range(mix_norm_g.shape[0]):
        x = _layer(x, *(p[l] for p in params))
    return x
```

```python
import functools
import math

import numpy as np
import jax
import jax.numpy as jnp
from jax import lax
from jax.experimental import pallas as pl
from jax.experimental.pallas import tpu as pltpu

F32 = jnp.float32
BF16 = jnp.bfloat16

D_MODEL = 2048
SEQ = 4096
GRID_W = 64
HEAD_DIM = 128
N_Q_HEADS = 8
N_KV_HEADS = 2
Q_PER_KV = N_Q_HEADS // N_KV_HEADS
ATTN_W = N_Q_HEADS * HEAD_DIM
KV_W = N_KV_HEADS * HEAD_DIM
ROPE_THETA = 10000.0
HYENA_W = D_MODEL - ATTN_W
SHORT_TAPS = 3
FILTER_EMB = 33
FILTER_HIDDEN = 64
DECAY_TARGET = 1e-2
FAST_DECAY_PCT = 0.3
SLOW_DECAY_PCT = 1.5
IN_W = 3 * HYENA_W + ATTN_W + 2 * KV_W + 2 * D_MODEL
D_FF = 5632
EPS = 1e-6

COL_Q = 3 * HYENA_W
COL_K = COL_Q + ATTN_W
COL_V = COL_K + KV_W
COL_G = COL_V + KV_W

FFT_N = 2 * SEQ
FFT_N1 = 64
FFT_N2 = 128

MIB = 1024 * 1024


def _cparams(n_axes, vmem_mib):
    return pltpu.CompilerParams(
        dimension_semantics=("arbitrary",) * n_axes,
        vmem_limit_bytes=vmem_mib * MIB,
    )


DFT_R = 8


@functools.lru_cache(maxsize=None)
def _dft_tables_np():
    n, n1, n2, r = FFT_N, FFT_N1, FFT_N2, DFT_R
    eye = np.eye(r)
    f1 = np.arange(n1)
    ang1 = 2.0 * np.pi * ((f1[:, None] * f1[None, :]) % n1) / n1
    c1, s1 = np.cos(ang1), np.sin(ang1)
    h = n1 // 2
    m_fwd = np.block([[c1[:, :h], s1[:, :h]], [-s1[:, :h], c1[:, :h]]])
    m_flt = np.concatenate([c1, -s1], axis=0)
    ct, st = c1[:h, :], s1[:h, :]
    m_inv = np.block([[ct, -st], [st, ct]]) / n
    s2 = np.arange(n2)
    f = f1[:, None, None] + n1 * s2[None, :, None]
    th = 2.0 * np.pi * ((f * s2[None, None, :]) % n) / n
    c, s = np.cos(th), np.sin(th)
    fwd2 = np.concatenate(
        [np.concatenate([c, s], axis=2), np.concatenate([-s, c], axis=2)], axis=1)
    c_t, s_t = np.transpose(c, (0, 2, 1)), np.transpose(s, (0, 2, 1))
    inv2 = np.concatenate(
        [np.concatenate([c_t, -s_t], axis=2), np.concatenate([s_t, c_t], axis=2)], axis=1)
    f32 = lambda a: np.ascontiguousarray(a, dtype=np.float32)
    return dict(k_fwd=f32(np.kron(m_fwd, eye)), k_flt=f32(np.kron(m_flt, eye)),
                k_inv=f32(np.kron(m_inv, eye)), fwd2=f32(fwd2), inv2=f32(inv2))


def _dft_tables():
    return {k: jnp.asarray(v).astype(BF16) for k, v in _dft_tables_np().items()}


@functools.lru_cache(maxsize=None)
def _rope_tables_np():
    half = HEAD_DIM // 2
    inv = ROPE_THETA ** (-np.arange(0, half, 2, dtype=np.float64) / half)
    pos = np.arange(SEQ)
    ang_r = (pos // GRID_W)[:, None] * inv[None, :]
    ang_c = (pos % GRID_W)[:, None] * inv[None, :]
    cos = np.concatenate([np.cos(ang_r)] * 2 + [np.cos(ang_c)] * 2, axis=-1)
    sin = np.concatenate([-np.sin(ang_r), np.sin(ang_r), -np.sin(ang_c), np.sin(ang_c)], axis=-1)
    return cos.astype(np.float32), sin.astype(np.float32)


@functools.lru_cache(maxsize=None)
def _filter_tables_np():
    L = SEQ
    bands = (FILTER_EMB - 1) // 2
    pos = np.concatenate([np.arange(L, dtype=np.float64), L - np.arange(L, dtype=np.float64)])
    t = pos / max(L - 1, 1)
    fb = np.linspace(1e-4, bands - 1, bands)
    ang = (2.0 * math.pi * pos / L)[:, None] * fb[None, :]
    emb = np.concatenate([t[:, None], np.cos(ang), -np.sin(ang),
                          np.zeros((2 * L, FILTER_HIDDEN - FILTER_EMB))], axis=-1)
    max_decay = math.log(DECAY_TARGET) / FAST_DECAY_PCT
    min_decay = math.log(DECAY_TARGET) / SLOW_DECAY_PCT
    deltas = np.abs(np.linspace(min_decay, max_decay, HYENA_W))
    return emb.astype(np.float32), deltas[None, :].astype(np.float32)


def _rmsnorm_kernel(x_ref, g_ref, o_ref):
    x = x_ref[...]
    ms = jnp.mean(x * x, axis=-1, keepdims=True)
    o_ref[...] = (x * lax.rsqrt(ms + EPS) * g_ref[...]).astype(o_ref.dtype)


def _rmsnorm(x, g, tm=512):
    m, d = x.shape
    return pl.pallas_call(
        _rmsnorm_kernel,
        grid=(m // tm,),
        in_specs=[pl.BlockSpec((tm, d), lambda i: (i, 0)),
                  pl.BlockSpec((1, d), lambda i: (0, 0))],
        out_specs=pl.BlockSpec((tm, d), lambda i: (i, 0)),
        out_shape=jax.ShapeDtypeStruct((m, d), BF16),
        compiler_params=_cparams(1, 32),
        name="rmsnorm",
    )(x, g)


MM_TN = 512
MM_RC = 256


def _wres_kernel(*refs, pairs, n_act, n_extra, rc, epilogue):
    n_w = len(pairs)
    acts = refs[:n_act]
    ws = refs[n_act:n_act + n_w]
    extras = refs[n_act + n_w:n_act + n_w + n_extra]
    o_ref = refs[n_act + n_w + n_extra]
    wbs = refs[n_act + n_w + n_extra + 1:]

    @pl.when(pl.program_id(1) == 0)
    def _():
        for w_ref, wb_ref in zip(ws, wbs):
            wb_ref[...] = w_ref[...].astype(BF16)

    for c in range(o_ref.shape[0] // rc):
        rows = pl.ds(c * rc, rc)
        accs = [jnp.dot(acts[a][rows, :], wb_ref[...], preferred_element_type=F32)
                for a, wb_ref in zip(pairs, wbs)]
        o_ref[rows, :] = epilogue(accs, extras, rows).astype(o_ref.dtype)


def _wres_matmul(name, acts, weights, extras, epilogue, *, pairs, n_out, out_dtype, tm,
                 tn=MM_TN, rc=MM_RC, vmem_mib=48):
    m = acts[0].shape[0]
    in_specs = [pl.BlockSpec((tm, a.shape[1]), lambda j, i: (i, 0)) for a in acts]
    in_specs += [pl.BlockSpec((w.shape[0], tn), lambda j, i, off=off: (0, off + j)) for w, off in weights]
    in_specs += [pl.BlockSpec(blk, imap) for _, blk, imap in extras]
    kern = functools.partial(_wres_kernel, pairs=tuple(pairs), n_act=len(acts), n_extra=len(extras),
                             rc=rc, epilogue=epilogue)
    return pl.pallas_call(
        kern,
        grid=(n_out // tn, m // tm),
        in_specs=in_specs,
        out_specs=pl.BlockSpec((tm, tn), lambda j, i: (i, j)),
        out_shape=jax.ShapeDtypeStruct((m, n_out), out_dtype),
        scratch_shapes=[pltpu.VMEM((w.shape[0], tn), BF16) for w, _ in weights],
        compiler_params=_cparams(2, vmem_mib),
        name=name,
    )(*acts, *(w for w, _ in weights), *(e for e, _, _ in extras))


def _head_norm_rope(x, g, cos, sin, scale):
    ms = jnp.mean(x * x, axis=-1, keepdims=True)
    y = x * lax.rsqrt(ms + EPS) * g
    lane = lax.broadcasted_iota(jnp.int32, y.shape, 1)
    swapped = jnp.where((lane % 64) < 32, pltpu.roll(y, 96, 1), pltpu.roll(y, 32, 1))
    out = y * cos + swapped * sin
    return out * scale if scale != 1.0 else out


def _ep_cast(accs, extras, rows):
    return accs[0]


def _ep_heads(accs, extras, rows, *, n_heads, scale):
    g_ref, cos_ref, sin_ref = extras
    acc = accs[0]
    cos, sin = cos_ref[rows, :], sin_ref[rows, :]
    parts = [_head_norm_rope(acc[:, hh * HEAD_DIM:(hh + 1) * HEAD_DIM], g_ref[...], cos, sin, scale)
             for hh in range(n_heads)]
    if n_heads * HEAD_DIM < acc.shape[1]:
        parts.append(acc[:, n_heads * HEAD_DIM:])
    return jnp.concatenate(parts, axis=1)


def _ep_gate(accs, extras, rows):
    return jax.nn.sigmoid(accs[0] + extras[0][...])


def _inproj(h, w_in, b_gate, q_g, k_g, cos, sin, tm=1024):
    tn = MM_TN
    s_blocks = SEQ // tm
    head = lambda g: (g, (1, HEAD_DIM), lambda j, i: (0, 0))
    pos = lambda t: (t, (tm, HEAD_DIM), lambda j, i: (i % s_blocks, 0))
    u_h = _wres_matmul("inproj_hyena", [h], [(w_in, 0)], [], _ep_cast,
                       pairs=[0], n_out=3 * HYENA_W, out_dtype=BF16, tm=tm)
    q = _wres_matmul("inproj_q", [h], [(w_in, COL_Q // tn)], [head(q_g), pos(cos), pos(sin)],
                     functools.partial(_ep_heads, n_heads=tn // HEAD_DIM, scale=HEAD_DIM ** -0.5),
                     pairs=[0], n_out=ATTN_W, out_dtype=BF16, tm=tm)
    kv = _wres_matmul("inproj_kv", [h], [(w_in, COL_K // tn)], [head(k_g), pos(cos), pos(sin)],
                      functools.partial(_ep_heads, n_heads=N_KV_HEADS, scale=1.0),
                      pairs=[0], n_out=2 * KV_W, out_dtype=BF16, tm=tm)
    gates = _wres_matmul("inproj_gate", [h], [(w_in, COL_G // tn)],
                         [(b_gate, (1, tn), lambda j, i: (0, j))], _ep_gate,
                         pairs=[0], n_out=2 * D_MODEL, out_dtype=BF16, tm=tm)
    return u_h, q, kv, gates


HY_CT = 256
HY_RC = 256


def _hy_pre_kernel(x0_ref, x1_ref, v_ref, w0_ref, w1_ref, wv_ref, b0_ref, b1_ref, bv_ref,
                   z_ref, x0c_ref, pad_ref):
    L = x0_ref.shape[0]
    ct = x0_ref.shape[1]
    nchunk = L // HY_RC
    zeros8 = jnp.zeros((8, ct), F32)
    for a, src in enumerate((x0_ref, x1_ref, v_ref)):
        pad_ref[a, 0:8, :] = zeros8
        pad_ref[a, L + 8:L + 16, :] = zeros8

        def fill(r, carry, a=a, src=src):
            base = pl.multiple_of(r * HY_RC, HY_RC)
            pad_ref[a, pl.ds(base + 8, HY_RC), :] = src[pl.ds(base, HY_RC), :].astype(F32)
            return carry

        lax.fori_loop(0, nchunk, fill, 0)

    def conv(a, w_ref, b_ref, base):
        e = pad_ref[a, pl.ds(base, HY_RC + 16), :]
        up = pltpu.roll(e, 1, 0)[8:8 + HY_RC]
        mid = e[8:8 + HY_RC]
        dn = pltpu.roll(e, HY_RC + 15, 0)[8:8 + HY_RC]
        return b_ref[...] + up * w_ref[0:1, :] + mid * w_ref[1:2, :] + dn * w_ref[2:3, :]

    def body(r, carry):
        base = pl.multiple_of(r * HY_RC, HY_RC)
        x0c = conv(0, w0_ref, b0_ref, base)
        x1c = conv(1, w1_ref, b1_ref, base)
        vc = conv(2, wv_ref, bv_ref, base)
        z_ref[pl.ds(base, HY_RC), :] = vc * x1c
        x0c_ref[pl.ds(base, HY_RC), :] = x0c.astype(BF16)
        return carry

    lax.fori_loop(0, nchunk, body, 0)


def _hy_pre(proj3, conv_w, conv_b):
    b, L, _ = proj3.shape
    ct = HY_CT
    nct = HYENA_W // ct
    in_specs = []
    for grp in range(3):
        in_specs.append(pl.BlockSpec((None, L, ct), lambda bi, c, grp=grp: (bi, 0, c + grp * nct)))
    for grp in range(3):
        in_specs.append(pl.BlockSpec((SHORT_TAPS, ct), lambda bi, c, grp=grp: (0, c + grp * nct)))
    for grp in range(3):
        in_specs.append(pl.BlockSpec((1, ct), lambda bi, c, grp=grp: (0, c + grp * nct)))
    out_spec = pl.BlockSpec((None, L, ct), lambda bi, c: (bi, 0, c))
    return pl.pallas_call(
        _hy_pre_kernel,
        grid=(b, nct),
        in_specs=in_specs,
        out_specs=[out_spec, out_spec],
        out_shape=[jax.ShapeDtypeStruct((b, L, HYENA_W), F32),
                   jax.ShapeDtypeStruct((b, L, HYENA_W), BF16)],
        scratch_shapes=[pltpu.VMEM((3, L + 16, ct), F32)],
        compiler_params=_cparams(2, 48),
        name="hyena_pre",
    )(proj3, proj3, proj3, conv_w, conv_w, conv_w, conv_b, conv_b, conv_b)


CV_CT = 256
CV_F1B = 8
CV_NF = FFT_N1 // CV_F1B
CV_SLAB = 2 * DFT_R
CV_NSLAB = FFT_N2 // CV_SLAB
CV_HALF = FFT_N1 // 2


def _outer_fwd_slab(src_ref, k_ref, a_ref, j, rows_in):
    ct = a_ref.shape[-1]
    halves = []
    for h in range(2):
        r = src_ref[:, :, 2 * j + h] if src_ref.ndim == 5 else src_ref[:, 2 * j + h]
        r = r.reshape(rows_in, ct).astype(BF16)
        o = jnp.dot(k_ref[...], r, preferred_element_type=F32)
        halves.append(o.reshape(2 * FFT_N1, DFT_R, ct))
    slab = jnp.concatenate(halves, axis=1).astype(BF16)
    a_ref[:, :, pl.ds(pl.multiple_of(j * CV_SLAB, CV_SLAB), CV_SLAB), :] = slab.reshape(
        2, FFT_N1, CV_SLAB, ct)


def _hy_conv_kernel(z_ref, x0_ref, bias_ref, kf_ref, ki_ref, fw_ref, iv_ref, spec_ref, o_ref, a_ref):
    s = pl.program_id(2)
    ct = a_ref.shape[-1]

    @pl.when(s == 0)
    def _():
        def body(j, carry):
            _outer_fwd_slab(z_ref, kf_ref, a_ref, j, 2 * CV_HALF * DFT_R)
            return carry

        lax.fori_loop(0, CV_NSLAB, body, 0)

    @pl.when((s >= 1) & (s <= CV_NF))
    def _():
        f0 = (s - 1) * CV_F1B
        for fl in range(CV_F1B):
            x = a_ref[:, f0 + fl].reshape(2 * FFT_N2, ct)
            u = jnp.dot(fw_ref[fl], x, preferred_element_type=F32)
            ure, uim = u[:FFT_N2], u[FFT_N2:]
            kre, kim = spec_ref[0, fl], spec_ref[1, fl]
            p = jnp.concatenate([ure * kre - uim * kim, ure * kim + uim * kre], axis=0).astype(BF16)
            y = jnp.dot(iv_ref[fl], p, preferred_element_type=F32)
            a_ref[:, f0 + fl] = y.reshape(2, FFT_N2, ct).astype(BF16)

    @pl.when(s == CV_NF + 1)
    def _():
        bias = bias_ref[...]

        def body(j, carry):
            slab = a_ref[:, :, pl.ds(pl.multiple_of(j * CV_SLAB, CV_SLAB), CV_SLAB), :].astype(F32)
            x0 = x0_ref[:, :, j].astype(F32)
            halves = []
            for h in range(2):
                r = slab[:, :, h * DFT_R:(h + 1) * DFT_R, :].reshape(2 * FFT_N1 * DFT_R, ct).astype(BF16)
                y = jnp.dot(ki_ref[...], r, preferred_element_type=F32).reshape(2, CV_HALF, DFT_R, ct)
                z = z_ref[:, :, 2 * j + h]
                halves.append((y + bias * z) * x0[:, :, h * DFT_R:(h + 1) * DFT_R, :])
            o_ref[:, :, j] = jnp.concatenate(halves, axis=2).astype(o_ref.dtype)
            return carry

        lax.fori_loop(0, CV_NSLAB, body, 0)


def _hy_conv(z, x0c, bias, spec, tabs):
    b, L, c = z.shape
    assert b % 2 == 0 and L * 2 == FFT_N
    ct = CV_CT
    z5 = z.reshape(b, CV_HALF, FFT_N2 // DFT_R, DFT_R, c)
    x5 = x0c.reshape(b, CV_HALF, CV_NSLAB, CV_SLAB, c)
    fidx = lambda s: jnp.clip(s - 1, 0, CV_NF - 1)
    once = pl.Buffered(1)
    tab_spec = pl.BlockSpec((CV_F1B, 2 * FFT_N2, 2 * FFT_N2), lambda p, cb, s: (fidx(s), 0, 0))
    io16 = pl.BlockSpec((2, CV_HALF, CV_NSLAB, CV_SLAB, ct), lambda p, cb, s: (p, 0, 0, 0, cb))
    out = pl.pallas_call(
        _hy_conv_kernel,
        grid=(b // 2, c // ct, CV_NF + 2),
        in_specs=[
            pl.BlockSpec((2, CV_HALF, FFT_N2 // DFT_R, DFT_R, ct), lambda p, cb, s: (p, 0, 0, 0, cb),
                         pipeline_mode=once),
            io16,
            pl.BlockSpec((1, ct), lambda p, cb, s: (0, cb)),
            pl.BlockSpec(tabs["k_fwd"].shape, lambda p, cb, s: (0, 0), pipeline_mode=once),
            pl.BlockSpec(tabs["k_inv"].shape, lambda p, cb, s: (0, 0), pipeline_mode=once),
            tab_spec, tab_spec,
            pl.BlockSpec((2, CV_F1B, FFT_N2, ct), lambda p, cb, s: (0, fidx(s), 0, cb)),
        ],
        out_specs=io16,
        out_shape=jax.ShapeDtypeStruct(x5.shape, BF16),
        scratch_shapes=[pltpu.VMEM((2, FFT_N1, FFT_N2, ct), BF16)],
        compiler_params=_cparams(3, 56),
        name="hyena_conv",
    )(z5, x5, bias, tabs["k_fwd"], tabs["k_inv"], tabs["fwd2"], tabs["inv2"], spec)
    return out.reshape(b, L, c)


def _filt_spec_kernel(t_ref, l1_ref, kf_ref, fw_ref, o_ref, a_ref):
    s = pl.program_id(1)
    ct = a_ref.shape[-1]

    @pl.when(s == 0)
    def _():
        def body(j, carry):
            _outer_fwd_slab(t_ref, kf_ref, a_ref, j, FFT_N1 * DFT_R)
            return carry

        lax.fori_loop(0, CV_NSLAB, body, 0)

    @pl.when(s >= 1)
    def _():
        inv_l1 = 1.0 / l1_ref[...]
        f0 = (s - 1) * CV_F1B
        for fl in range(CV_F1B):
            x = a_ref[:, f0 + fl].reshape(2 * FFT_N2, ct)
            y = jnp.dot(fw_ref[fl], x, preferred_element_type=F32) * inv_l1
            o_ref[:, fl] = y.reshape(2, FFT_N2, ct)


def _filt_spec(taps, l1, tabs):
    n, c = taps.shape
    ct = CV_CT
    t4 = taps.reshape(FFT_N1, FFT_N2 // DFT_R, DFT_R, c)
    fidx = lambda s: jnp.clip(s - 1, 0, CV_NF - 1)
    once = pl.Buffered(1)
    return pl.pallas_call(
        _filt_spec_kernel,
        grid=(c // ct, CV_NF + 1),
        in_specs=[
            pl.BlockSpec((FFT_N1, FFT_N2 // DFT_R, DFT_R, ct), lambda cb, s: (0, 0, 0, cb),
                         pipeline_mode=once),
            pl.BlockSpec((1, ct), lambda cb, s: (0, cb)),
            pl.BlockSpec(tabs["k_flt"].shape, lambda cb, s: (0, 0), pipeline_mode=once),
            pl.BlockSpec((CV_F1B, 2 * FFT_N2, 2 * FFT_N2), lambda cb, s: (fidx(s), 0, 0)),
        ],
        out_specs=pl.BlockSpec((2, CV_F1B, FFT_N2, ct), lambda cb, s: (0, fidx(s), 0, cb)),
        out_shape=jax.ShapeDtypeStruct((2, FFT_N1, FFT_N2, c), F32),
        scratch_shapes=[pltpu.VMEM((2, FFT_N1, FFT_N2, ct), BF16)],
        compiler_params=_cparams(2, 48),
        name="filter_spectrum",
    )(t4, l1, tabs["k_flt"], tabs["fwd2"])


FLT_TR = 512


def _filter_kernel(emb_ref, w1_ref, b1_ref, w2_ref, b2_ref, w3_ref, b3_ref, w4_ref, fr_ref, dl_ref,
                   k_ref, l1_ref):
    r = pl.program_id(0)
    hp = lax.Precision.HIGHEST
    e = emb_ref[...]
    fr = fr_ref[...]
    h = jnp.sin(fr * (jnp.dot(e, w1_ref[...], precision=hp, preferred_element_type=F32) + b1_ref[...]))
    h = jnp.sin(fr * (jnp.dot(h, w2_ref[...], precision=hp, preferred_element_type=F32) + b2_ref[...]))
    h = jnp.sin(fr * (jnp.dot(h, w3_ref[...], precision=hp, preferred_element_type=F32) + b3_ref[...]))
    taps = jnp.dot(h, w4_ref[...], precision=hp, preferred_element_type=F32)
    decay = jnp.exp(-e[:, 0:1] * dl_ref[...])
    rows = r * FLT_TR + lax.broadcasted_iota(jnp.int32, taps.shape, 0)
    taps = jnp.where(rows == SEQ, 0.0, taps * decay)
    k_ref[...] = taps.astype(k_ref.dtype)

    @pl.when(r == 0)
    def _():
        l1_ref[...] = jnp.zeros_like(l1_ref)

    l1_ref[...] += jnp.sum(jnp.abs(taps), axis=0, keepdims=True)


def _filter_taps(emb, w1p, b1, w2, b2, w3, b3, w4, freq, deltas):
    n = emb.shape[0]
    fh = FILTER_HIDDEN
    small = lambda shape: pl.BlockSpec(shape, lambda r: (0, 0))
    fwd_tiles = SEQ // FLT_TR
    return pl.pallas_call(
        _filter_kernel,
        grid=(n // FLT_TR,),
        in_specs=[pl.BlockSpec((FLT_TR, fh), lambda r: (r, 0)),
                  small((fh, fh)), small((1, fh)), small((fh, fh)), small((1, fh)),
                  small((fh, fh)), small((1, fh)),
                  pl.BlockSpec((fh, HYENA_W), lambda r: (0, r // fwd_tiles)),
                  small((1, fh)), small((1, HYENA_W))],
        out_specs=[pl.BlockSpec((FLT_TR, HYENA_W), lambda r: (r, 0)),
                   pl.BlockSpec((1, HYENA_W), lambda r: (0, 0))],
        out_shape=[jax.ShapeDtypeStruct((n, HYENA_W), F32),
                   jax.ShapeDtypeStruct((1, HYENA_W), F32)],
        compiler_params=_cparams(1, 32),
        name="filter_taps",
    )(emb, w1p, b1, w2, b2, w3, b3, w4, freq, deltas)


AT_TQ = 256


def _attn_kernel(q_ref, k_ref, v_ref, o_ref):
    k = k_ref[...]
    v = v_ref[...]
    for g in range(Q_PER_KV):
        sl = slice(g * HEAD_DIM, (g + 1) * HEAD_DIM)
        s = lax.dot_general(q_ref[:, sl], k, (((1,), (1,)), ((), ())), preferred_element_type=F32)
        m = jnp.max(s, axis=-1, keepdims=True)
        p = jnp.exp(s - m)
        l = jnp.sum(p, axis=-1, keepdims=True)
        o = jnp.dot(p.astype(BF16), v, preferred_element_type=F32)
        o_ref[:, sl] = (o / l).astype(o_ref.dtype)


def _attention(q3, kv3):
    b, s, _ = q3.shape
    gw = Q_PER_KV * HEAD_DIM
    return pl.pallas_call(
        _attn_kernel,
        grid=(b, N_KV_HEADS, s // AT_TQ),
        in_specs=[pl.BlockSpec((None, AT_TQ, gw), lambda bi, kv, qi: (bi, qi, kv)),
                  pl.BlockSpec((None, s, HEAD_DIM), lambda bi, kv, qi: (bi, 0, kv)),
                  pl.BlockSpec((None, s, HEAD_DIM), lambda bi, kv, qi: (bi, 0, N_KV_HEADS + kv))],
        out_specs=pl.BlockSpec((None, AT_TQ, gw), lambda bi, kv, qi: (bi, qi, kv)),
        out_shape=jax.ShapeDtypeStruct((b, s, ATTN_W), BF16),
        compiler_params=_cparams(3, 48),
        name="attention",
    )(q3, kv3, kv3)


def _ep_merge(accs, extras, rows):
    gh_ref, ga_ref = extras
    return gh_ref[rows, :].astype(F32) * accs[0] + ga_ref[rows, :].astype(F32) * accs[1]


def _ep_residual(accs, extras, rows):
    return extras[0][rows, :] + accs[0]


def _ep_swiglu(accs, extras, rows):
    return jax.nn.silu(accs[0]) * accs[1]


def _merge(yh, ya, gates, w_h, w_a, tm=1024):
    tn = MM_TN
    nj = D_MODEL // tn
    return _wres_matmul("merge", [yh, ya], [(w_h, 0), (w_a, 0)],
                        [(gates, (tm, tn), lambda j, i: (i, j)),
                         (gates, (tm, tn), lambda j, i: (i, nj + j))],
                        _ep_merge, pairs=[0, 1], n_out=D_MODEL, out_dtype=BF16, tm=tm)


def _resmm(name, a, x, w, tm, vmem_mib):
    tn = MM_TN
    return _wres_matmul(name, [a], [(w, 0)], [(x, (tm, tn), lambda j, i: (i, j))], _ep_residual,
                        pairs=[0], n_out=w.shape[1], out_dtype=F32, tm=tm, vmem_mib=vmem_mib)


def _swiglu(h, w_g, w_u, tm=1024):
    return _wres_matmul("swiglu", [h], [(w_g, 0), (w_u, 0)], [], _ep_swiglu,
                        pairs=[0, 0], n_out=w_g.shape[1], out_dtype=BF16, tm=tm)


def _layer(x, mix_norm_g, w_in, b_gate, hy_conv_w, hy_conv_b,
           flt_w1, flt_b1, flt_w2, flt_b2, flt_w3, flt_b3, flt_w4, flt_freq, hy_bias,
           q_norm_g, k_norm_g, w_br_hyena, w_br_attn, w_out,
           ffn_norm_g, w_ffn_gate, w_ffn_up, w_ffn_down):
    b, s, d = x.shape
    m = b * s
    row = lambda a: a.reshape(1, -1)
    tabs = _dft_tables()
    cos, sin = (jnp.asarray(t) for t in _rope_tables_np())
    xm = x.reshape(m, d)

    h = _rmsnorm(xm, row(mix_norm_g))
    u_h, q, kv, gates = _inproj(h, w_in, row(b_gate), row(q_norm_g), row(k_norm_g), cos, sin)

    emb, deltas = (jnp.asarray(t) for t in _filter_tables_np())
    w1p = jnp.concatenate([flt_w1, jnp.zeros((FILTER_HIDDEN - FILTER_EMB, FILTER_HIDDEN), F32)], axis=0)
    taps, l1 = _filter_taps(emb, w1p, row(flt_b1), flt_w2, row(flt_b2), flt_w3, row(flt_b3),
                            flt_w4, row(flt_freq), deltas)
    spec = _filt_spec(taps, l1, tabs)

    z, x0c = _hy_pre(u_h.reshape(b, s, 3 * HYENA_W), hy_conv_w, row(hy_conv_b))
    y_h = _hy_conv(z, x0c, row(hy_bias), spec, tabs).reshape(m, HYENA_W)

    y_a = _attention(q.reshape(b, s, ATTN_W), kv.reshape(b, s, 2 * KV_W)).reshape(m, ATTN_W)

    merged = _merge(y_h, y_a, gates, w_br_hyena, w_br_attn)
    x1 = _resmm("out_proj", merged, xm, w_out, tm=1024, vmem_mib=48)

    h2 = _rmsnorm(x1, row(ffn_norm_g))
    act = _swiglu(h2, w_ffn_gate, w_ffn_up)
    out = _resmm("ffn_down", act, x1, w_ffn_down, tm=512, vmem_mib=56)
    return out.reshape(b, s, d)


def kernel(x, mix_norm_g, w_in, b_gate, hy_conv_w, hy_conv_b, flt_w1, flt_b1, flt_w2, flt_b2, flt_w3, flt_b3, flt_w4, flt_freq, hy_bias, q_norm_g, k_norm_g, w_br_hyena, w_br_attn, w_out, ffn_norm_g, w_ffn_gate, w_ffn_up, w_ffn_down):
    params = (mix_norm_g, w_in, b_gate, hy_conv_w, hy_conv_b, flt_w1, flt_b1, flt_w2, flt_b2,
              flt_w3, flt_b3, flt_w4, flt_freq, hy_bias, q_norm_g, k_norm_g, w_br_hyena, w_br_attn,
              w_out, ffn_norm_g, w_ffn_gate, w_ffn_up, w_ffn_down)
    for l in range(mix_norm_g.shape[0]):
        x = _layer(x, *(p[l] for p in params))
    return x
```

```python
import functools
import math

import numpy as np
import jax
import jax.numpy as jnp
from jax import lax
from jax.experimental import pallas as pl
from jax.experimental.pallas import tpu as pltpu

F32 = jnp.float32
BF16 = jnp.bfloat16

D_MODEL = 2048
SEQ = 4096
GRID_W = 64
HEAD_DIM = 128
N_Q_HEADS = 8
N_KV_HEADS = 2
Q_PER_KV = N_Q_HEADS // N_KV_HEADS
ATTN_W = N_Q_HEADS * HEAD_DIM
KV_W = N_KV_HEADS * HEAD_DIM
ROPE_THETA = 10000.0
HYENA_W = D_MODEL - ATTN_W
SHORT_TAPS = 3
FILTER_EMB = 33
FILTER_HIDDEN = 64
DECAY_TARGET = 1e-2
FAST_DECAY_PCT = 0.3
SLOW_DECAY_PCT = 1.5
IN_W = 3 * HYENA_W + ATTN_W + 2 * KV_W + 2 * D_MODEL
D_FF = 5632
EPS = 1e-6

COL_Q = 3 * HYENA_W
COL_K = COL_Q + ATTN_W
COL_V = COL_K + KV_W
COL_G = COL_V + KV_W

FFT_N = 2 * SEQ
FFT_N1 = 64
FFT_N2 = 128

MIB = 1024 * 1024


def _cparams(n_axes, vmem_mib):
    return pltpu.CompilerParams(
        dimension_semantics=("arbitrary",) * n_axes,
        vmem_limit_bytes=vmem_mib * MIB,
    )


DFT_R = 8


@functools.lru_cache(maxsize=None)
def _dft_tables_np():
    n, n1, n2, r = FFT_N, FFT_N1, FFT_N2, DFT_R
    eye = np.eye(r)
    f1 = np.arange(n1)
    ang1 = 2.0 * np.pi * ((f1[:, None] * f1[None, :]) % n1) / n1
    c1, s1 = np.cos(ang1), np.sin(ang1)
    h = n1 // 2
    m_fwd = np.block([[c1[:, :h], s1[:, :h]], [-s1[:, :h], c1[:, :h]]])
    m_flt = np.concatenate([c1, -s1], axis=0)
    ct, st = c1[:h, :], s1[:h, :]
    m_inv = np.block([[ct, -st], [st, ct]]) / n
    s2 = np.arange(n2)
    f = f1[:, None, None] + n1 * s2[None, :, None]
    th = 2.0 * np.pi * ((f * s2[None, None, :]) % n) / n
    c, s = np.cos(th), np.sin(th)
    fwd2 = np.concatenate(
        [np.concatenate([c, s], axis=2), np.concatenate([-s, c], axis=2)], axis=1)
    c_t, s_t = np.transpose(c, (0, 2, 1)), np.transpose(s, (0, 2, 1))
    inv2 = np.concatenate(
        [np.concatenate([c_t, -s_t], axis=2), np.concatenate([s_t, c_t], axis=2)], axis=1)
    f32 = lambda a: np.ascontiguousarray(a, dtype=np.float32)
    return dict(k_fwd=f32(np.kron(m_fwd, eye)), k_flt=f32(np.kron(m_flt, eye)),
                k_inv=f32(np.kron(m_inv, eye)), fwd2=f32(fwd2), inv2=f32(inv2))


def _dft_tables():
    return {k: jnp.asarray(v).astype(BF16) for k, v in _dft_tables_np().items()}


@functools.lru_cache(maxsize=None)
def _rope_tables_np():
    half = HEAD_DIM // 2
    inv = ROPE_THETA ** (-np.arange(0, half, 2, dtype=np.float64) / half)
    pos = np.arange(SEQ)
    ang_r = (pos // GRID_W)[:, None] * inv[None, :]
    ang_c = (pos % GRID_W)[:, None] * inv[None, :]
    cos = np.concatenate([np.cos(ang_r)] * 2 + [np.cos(ang_c)] * 2, axis=-1)
    sin = np.concatenate([-np.sin(ang_r), np.sin(ang_r), -np.sin(ang_c), np.sin(ang_c)], axis=-1)
    return cos.astype(np.float32), sin.astype(np.float32)


@functools.lru_cache(maxsize=None)
def _filter_tables_np():
    L = SEQ
    bands = (FILTER_EMB - 1) // 2
    pos = np.concatenate([np.arange(L, dtype=np.float64), L - np.arange(L, dtype=np.float64)])
    t = pos / max(L - 1, 1)
    fb = np.linspace(1e-4, bands - 1, bands)
    ang = (2.0 * math.pi * pos / L)[:, None] * fb[None, :]
    emb = np.concatenate([t[:, None], np.cos(ang), -np.sin(ang),
                          np.zeros((2 * L, FILTER_HIDDEN - FILTER_EMB))], axis=-1)
    max_decay = math.log(DECAY_TARGET) / FAST_DECAY_PCT
    min_decay = math.log(DECAY_TARGET) / SLOW_DECAY_PCT
    deltas = np.abs(np.linspace(min_decay, max_decay, HYENA_W))
    tile, hid = FLT_TR, FILTER_HIDDEN
    emb2 = emb.reshape(2 * L // tile, 2, tile // 2, hid).transpose(0, 2, 1, 3).reshape(L, 2 * hid)
    return np.ascontiguousarray(emb2, dtype=np.float32), deltas[None, :].astype(np.float32)


def _rmsnorm_kernel(x_ref, g_ref, o_ref):
    x = x_ref[...]
    ms = jnp.mean(x * x, axis=-1, keepdims=True)
    o_ref[...] = (x * lax.rsqrt(ms + EPS) * g_ref[...]).astype(o_ref.dtype)


def _rmsnorm(x, g, tm=512):
    m, d = x.shape
    return pl.pallas_call(
        _rmsnorm_kernel,
        grid=(m // tm,),
        in_specs=[pl.BlockSpec((tm, d), lambda i: (i, 0)),
                  pl.BlockSpec((1, d), lambda i: (0, 0))],
        out_specs=pl.BlockSpec((tm, d), lambda i: (i, 0)),
        out_shape=jax.ShapeDtypeStruct((m, d), BF16),
        compiler_params=_cparams(1, 32),
        name="rmsnorm",
    )(x, g)


MM_TN = 512
MM_RC = 256


def _wres_kernel(*refs, pairs, n_act, n_extra, rc, epilogue):
    n_w = len(pairs)
    acts = refs[:n_act]
    ws = refs[n_act:n_act + n_w]
    extras = refs[n_act + n_w:n_act + n_w + n_extra]
    o_ref = refs[n_act + n_w + n_extra]
    wbs = refs[n_act + n_w + n_extra + 1:]

    @pl.when(pl.program_id(1) == 0)
    def _():
        for w_ref, wb_ref in zip(ws, wbs):
            wb_ref[...] = w_ref[...].astype(BF16)

    for c in range(o_ref.shape[0] // rc):
        rows = pl.ds(c * rc, rc)
        accs = [jnp.dot(acts[a][rows, :], wb_ref[...], preferred_element_type=F32)
                for a, wb_ref in zip(pairs, wbs)]
        o_ref[rows, :] = epilogue(accs, extras, rows).astype(o_ref.dtype)


def _wres_matmul(name, acts, weights, extras, epilogue, *, pairs, n_out, out_dtype, tm,
                 tn=MM_TN, rc=MM_RC, vmem_mib=48):
    m = acts[0].shape[0]
    in_specs = [pl.BlockSpec((tm, a.shape[1]), lambda j, i: (i, 0)) for a in acts]
    assert all(off % 128 == 0 for _, off in weights) and tn % 128 == 0
    in_specs += [pl.BlockSpec((pl.Element(w.shape[0]), pl.Element(tn)),
                              lambda j, i, off=off: (0, pl.multiple_of(off + j * tn, 128)))
                 for w, off in weights]
    in_specs += [pl.BlockSpec(blk, imap) for _, blk, imap in extras]
    kern = functools.partial(_wres_kernel, pairs=tuple(pairs), n_act=len(acts), n_extra=len(extras),
                             rc=rc, epilogue=epilogue)
    return pl.pallas_call(
        kern,
        grid=(n_out // tn, m // tm),
        in_specs=in_specs,
        out_specs=pl.BlockSpec((tm, tn), lambda j, i: (i, j)),
        out_shape=jax.ShapeDtypeStruct((m, n_out), out_dtype),
        scratch_shapes=[pltpu.VMEM((w.shape[0], tn), BF16) for w, _ in weights],
        compiler_params=_cparams(2, vmem_mib),
        name=name,
    )(*acts, *(w for w, _ in weights), *(e for e, _, _ in extras))


def _head_norm_rope(x, g, cos, sin, scale):
    ms = jnp.mean(x * x, axis=-1, keepdims=True)
    y = x * lax.rsqrt(ms + EPS) * g
    lane = lax.broadcasted_iota(jnp.int32, y.shape, 1)
    swapped = jnp.where((lane % 64) < 32, pltpu.roll(y, 96, 1), pltpu.roll(y, 32, 1))
    out = y * cos + swapped * sin
    return out * scale if scale != 1.0 else out


def _ep_cast(accs, extras, rows):
    return accs[0]


def _ep_heads(accs, extras, rows, *, n_heads, scale):
    g_ref, cos_ref, sin_ref = extras
    acc = accs[0]
    cos, sin = cos_ref[rows, :], sin_ref[rows, :]
    parts = [_head_norm_rope(acc[:, hh * HEAD_DIM:(hh + 1) * HEAD_DIM], g_ref[...], cos, sin, scale)
             for hh in range(n_heads)]
    if n_heads * HEAD_DIM < acc.shape[1]:
        parts.append(acc[:, n_heads * HEAD_DIM:])
    return jnp.concatenate(parts, axis=1)


def _ep_gate(accs, extras, rows):
    return jax.nn.sigmoid(accs[0] + extras[0][...])


IN_TM = 2048
IN_TN = 1024
LOG2E = math.log2(math.e)


def _inproj(h, w_in, b_gate, q_g, k_g, cos, sin):
    tm, tn = IN_TM, IN_TN
    s_blocks = SEQ // tm
    head = lambda g: (g, (1, HEAD_DIM), lambda j, i: (0, 0))
    pos = lambda t: (t, (tm, HEAD_DIM), lambda j, i: (i % s_blocks, 0))
    common = dict(pairs=[0], out_dtype=BF16, tm=tm, vmem_mib=56)
    u_h = _wres_matmul("inproj_hyena", [h], [(w_in, 0)], [], _ep_cast,
                       n_out=3 * HYENA_W, tn=tn, **common)
    q = _wres_matmul("inproj_q", [h], [(w_in, COL_Q)], [head(q_g), pos(cos), pos(sin)],
                     functools.partial(_ep_heads, n_heads=tn // HEAD_DIM, scale=HEAD_DIM ** -0.5 * LOG2E),
                     n_out=ATTN_W, tn=tn, **common)
    kv = _wres_matmul("inproj_kv", [h], [(w_in, COL_K)], [head(k_g), pos(cos), pos(sin)],
                      functools.partial(_ep_heads, n_heads=N_KV_HEADS, scale=1.0),
                      n_out=2 * KV_W, tn=2 * KV_W, **common)
    gates = _wres_matmul("inproj_gate", [h], [(w_in, COL_G)],
                         [(b_gate, (1, tn), lambda j, i: (0, j))], _ep_gate,
                         n_out=2 * D_MODEL, tn=tn, **common)
    return u_h, q, kv, gates


HY_CT = 256
HY_RC = 256


def _hy_pre_kernel(x0_ref, x1_ref, v_ref, w0_ref, w1_ref, wv_ref, b0_ref, b1_ref, bv_ref,
                   z_ref, x0c_ref, pad_ref):
    L = x0_ref.shape[0]
    ct = x0_ref.shape[1]
    nchunk = L // HY_RC
    zeros8 = jnp.zeros((8, ct), F32)
    for a, src in enumerate((x0_ref, x1_ref, v_ref)):
        pad_ref[a, 0:8, :] = zeros8
        pad_ref[a, L + 8:L + 16, :] = zeros8

        def fill(r, carry, a=a, src=src):
            base = pl.multiple_of(r * HY_RC, HY_RC)
            pad_ref[a, pl.ds(base + 8, HY_RC), :] = src[pl.ds(base, HY_RC), :].astype(F32)
            return carry

        lax.fori_loop(0, nchunk, fill, 0)

    def conv(a, w_ref, b_ref, base):
        e = pad_ref[a, pl.ds(base, HY_RC + 16), :]
        up = pltpu.roll(e, 1, 0)[8:8 + HY_RC]
        mid = e[8:8 + HY_RC]
        dn = pltpu.roll(e, HY_RC + 15, 0)[8:8 + HY_RC]
        return b_ref[...] + up * w_ref[0:1, :] + mid * w_ref[1:2, :] + dn * w_ref[2:3, :]

    def body(r, carry):
        base = pl.multiple_of(r * HY_RC, HY_RC)
        x0c = conv(0, w0_ref, b0_ref, base)
        x1c = conv(1, w1_ref, b1_ref, base)
        vc = conv(2, wv_ref, bv_ref, base)
        z_ref[pl.ds(base, HY_RC), :] = vc * x1c
        x0c_ref[pl.ds(base, HY_RC), :] = x0c.astype(BF16)
        return carry

    lax.fori_loop(0, nchunk, body, 0)


def _hy_pre(proj3, conv_w, conv_b):
    b, L, _ = proj3.shape
    ct = HY_CT
    nct = HYENA_W // ct
    in_specs = []
    for grp in range(3):
        in_specs.append(pl.BlockSpec((None, L, ct), lambda bi, c, grp=grp: (bi, 0, c + grp * nct)))
    for grp in range(3):
        in_specs.append(pl.BlockSpec((SHORT_TAPS, ct), lambda bi, c, grp=grp: (0, c + grp * nct)))
    for grp in range(3):
        in_specs.append(pl.BlockSpec((1, ct), lambda bi, c, grp=grp: (0, c + grp * nct)))
    out_spec = pl.BlockSpec((None, L, ct), lambda bi, c: (bi, 0, c))
    return pl.pallas_call(
        _hy_pre_kernel,
        grid=(b, nct),
        in_specs=in_specs,
        out_specs=[out_spec, out_spec],
        out_shape=[jax.ShapeDtypeStruct((b, L, HYENA_W), F32),
                   jax.ShapeDtypeStruct((b, L, HYENA_W), BF16)],
        scratch_shapes=[pltpu.VMEM((3, L + 16, ct), F32)],
        compiler_params=_cparams(2, 48),
        name="hyena_pre",
    )(proj3, proj3, proj3, conv_w, conv_w, conv_w, conv_b, conv_b, conv_b)


CV_CT = 256
CV_F1B = 8
CV_NF = FFT_N1 // CV_F1B
CV_SLAB = 2 * DFT_R
CV_NSLAB = FFT_N2 // CV_SLAB
CV_HALF = FFT_N1 // 2


def _outer_fwd_slab(src_ref, k_ref, a_ref, j, rows_in):
    ct = a_ref.shape[-1]
    halves = []
    for h in range(2):
        r = src_ref[:, :, 2 * j + h] if src_ref.ndim == 5 else src_ref[:, 2 * j + h]
        r = r.reshape(rows_in, ct).astype(BF16)
        o = jnp.dot(k_ref[...], r, preferred_element_type=F32)
        halves.append(o.reshape(2 * FFT_N1, DFT_R, ct))
    slab = jnp.concatenate(halves, axis=1).astype(BF16)
    a_ref[:, :, pl.ds(pl.multiple_of(j * CV_SLAB, CV_SLAB), CV_SLAB), :] = slab.reshape(
        2, FFT_N1, CV_SLAB, ct)


def _hy_conv_kernel(z_ref, x0_ref, bias_ref, kf_ref, ki_ref, fw_ref, iv_ref, spec_ref, o_ref, a_ref):
    s = pl.program_id(2)
    ct = a_ref.shape[-1]

    @pl.when(s == 0)
    def _():
        def body(j, carry):
            _outer_fwd_slab(z_ref, kf_ref, a_ref, j, 2 * CV_HALF * DFT_R)
            return carry

        lax.fori_loop(0, CV_NSLAB, body, 0)

    @pl.when((s >= 1) & (s <= CV_NF))
    def _():
        f0 = (s - 1) * CV_F1B
        for fl in range(CV_F1B):
            x = a_ref[:, f0 + fl].reshape(2 * FFT_N2, ct)
            u = jnp.dot(fw_ref[fl], x, preferred_element_type=F32)
            ure, uim = u[:FFT_N2], u[FFT_N2:]
            kre, kim = spec_ref[0, fl].astype(F32), spec_ref[1, fl].astype(F32)
            p = jnp.concatenate([ure * kre - uim * kim, ure * kim + uim * kre], axis=0).astype(BF16)
            y = jnp.dot(iv_ref[fl], p, preferred_element_type=F32)
            a_ref[:, f0 + fl] = y.reshape(2, FFT_N2, ct).astype(BF16)

    @pl.when(s == CV_NF + 1)
    def _():
        bias = bias_ref[...]

        def body(j, carry):
            slab = a_ref[:, :, pl.ds(pl.multiple_of(j * CV_SLAB, CV_SLAB), CV_SLAB), :].astype(F32)
            x0 = x0_ref[:, :, j].astype(F32)
            halves = []
            for h in range(2):
                r = slab[:, :, h * DFT_R:(h + 1) * DFT_R, :].reshape(2 * FFT_N1 * DFT_R, ct).astype(BF16)
                y = jnp.dot(ki_ref[...], r, preferred_element_type=F32).reshape(2, CV_HALF, DFT_R, ct)
                z = z_ref[:, :, 2 * j + h]
                halves.append((y + bias * z) * x0[:, :, h * DFT_R:(h + 1) * DFT_R, :])
            o_ref[:, :, j] = jnp.concatenate(halves, axis=2).astype(o_ref.dtype)
            return carry

        lax.fori_loop(0, CV_NSLAB, body, 0)


def _hy_conv(z, x0c, bias, spec, tabs):
    b, L, c = z.shape
    assert b % 2 == 0 and L * 2 == FFT_N
    ct = CV_CT
    z5 = z.reshape(b, CV_HALF, FFT_N2 // DFT_R, DFT_R, c)
    x5 = x0c.reshape(b, CV_HALF, CV_NSLAB, CV_SLAB, c)
    fidx = lambda s: jnp.clip(s - 1, 0, CV_NF - 1)
    once = pl.Buffered(1)
    tab_spec = pl.BlockSpec((CV_F1B, 2 * FFT_N2, 2 * FFT_N2), lambda p, cb, s: (fidx(s), 0, 0))
    io16 = pl.BlockSpec((2, CV_HALF, CV_NSLAB, CV_SLAB, ct), lambda p, cb, s: (p, 0, 0, 0, cb))
    out = pl.pallas_call(
        _hy_conv_kernel,
        grid=(b // 2, c // ct, CV_NF + 2),
        in_specs=[
            pl.BlockSpec((2, CV_HALF, FFT_N2 // DFT_R, DFT_R, ct), lambda p, cb, s: (p, 0, 0, 0, cb),
                         pipeline_mode=once),
            io16,
            pl.BlockSpec((1, ct), lambda p, cb, s: (0, cb)),
            pl.BlockSpec(tabs["k_fwd"].shape, lambda p, cb, s: (0, 0), pipeline_mode=once),
            pl.BlockSpec(tabs["k_inv"].shape, lambda p, cb, s: (0, 0), pipeline_mode=once),
            tab_spec, tab_spec,
            pl.BlockSpec((2, CV_F1B, FFT_N2, ct), lambda p, cb, s: (0, fidx(s), 0, cb)),
        ],
        out_specs=io16,
        out_shape=jax.ShapeDtypeStruct(x5.shape, BF16),
        scratch_shapes=[pltpu.VMEM((2, FFT_N1, FFT_N2, ct), BF16)],
        compiler_params=_cparams(3, 56),
        name="hyena_conv",
    )(z5, x5, bias, tabs["k_fwd"], tabs["k_inv"], tabs["fwd2"], tabs["inv2"], spec)
    return out.reshape(b, L, c)


def _filt_spec_kernel(t_ref, l1_ref, kf_ref, fw_ref, o_ref, a_ref):
    s = pl.program_id(1)
    ct = a_ref.shape[-1]

    @pl.when(s == 0)
    def _():
        def body(j, carry):
            _outer_fwd_slab(t_ref, kf_ref, a_ref, j, FFT_N1 * DFT_R)
            return carry

        lax.fori_loop(0, CV_NSLAB, body, 0)

    @pl.when(s >= 1)
    def _():
        inv_l1 = 1.0 / l1_ref[...]
        f0 = (s - 1) * FS_F1B
        for fl in range(FS_F1B):
            x = a_ref[:, f0 + fl].reshape(2 * FFT_N2, ct)
            y = jnp.dot(fw_ref[fl], x, preferred_element_type=F32) * inv_l1
            o_ref[:, fl] = y.reshape(2, FFT_N2, ct).astype(o_ref.dtype)


FS_F1B = 16
FS_NF = FFT_N1 // FS_F1B


def _filt_spec(taps, l1, tabs):
    n, c = taps.shape
    ct = CV_CT
    t4 = taps.reshape(FFT_N1, FFT_N2 // DFT_R, DFT_R, c)
    fidx = lambda s: jnp.clip(s - 1, 0, FS_NF - 1)
    once = pl.Buffered(1)
    return pl.pallas_call(
        _filt_spec_kernel,
        grid=(c // ct, FS_NF + 1),
        in_specs=[
            pl.BlockSpec((FFT_N1, FFT_N2 // DFT_R, DFT_R, ct), lambda cb, s: (0, 0, 0, cb),
                         pipeline_mode=once),
            pl.BlockSpec((1, ct), lambda cb, s: (0, cb)),
            pl.BlockSpec(tabs["k_flt"].shape, lambda cb, s: (0, 0), pipeline_mode=once),
            pl.BlockSpec((FS_F1B, 2 * FFT_N2, 2 * FFT_N2), lambda cb, s: (fidx(s), 0, 0)),
        ],
        out_specs=pl.BlockSpec((2, FS_F1B, FFT_N2, ct), lambda cb, s: (0, fidx(s), 0, cb)),
        out_shape=jax.ShapeDtypeStruct((2, FFT_N1, FFT_N2, c), BF16),
        scratch_shapes=[pltpu.VMEM((2, FFT_N1, FFT_N2, ct), BF16)],
        compiler_params=_cparams(2, 48),
        name="filter_spectrum",
    )(t4, l1, tabs["k_flt"], tabs["fwd2"])


FLT_TR = 512


FLT_HALF = FLT_TR // 2


def _dot3(a, w):
    a_hi = a.astype(BF16)
    a_lo = (a - a_hi.astype(F32)).astype(BF16)
    w_hi = w.astype(BF16)
    w_lo = (w - w_hi.astype(F32)).astype(BF16)
    lhs = jnp.concatenate([a_hi, a_hi, a_lo], axis=1)
    rhs = jnp.concatenate([w_hi, w_lo, w_hi], axis=0)
    return jnp.dot(lhs, rhs, preferred_element_type=F32)


def _filter_kernel(emb_ref, w1_ref, b1_ref, w2_ref, b2_ref, w3_ref, b3_ref, w4t_ref, w4b_ref, fr_ref, dl_ref,
                   k_ref, l1_ref):
    r = pl.program_id(0)
    e = emb_ref[...]
    fr = fr_ref[...]
    h = jnp.sin(fr * (_dot3(e, w1_ref[...]) + b1_ref[...]))
    h = jnp.sin(fr * (_dot3(h, w2_ref[...]) + b2_ref[...]))
    h = jnp.sin(fr * (_dot3(h, w3_ref[...]) + b3_ref[...]))
    total = jnp.zeros(l1_ref.shape, F32)
    for part, w_ref in enumerate((w4t_ref, w4b_ref)):
        tcol = part * FILTER_HIDDEN
        taps = _dot3(h, w_ref[...])
        decay = jnp.exp(-e[:, tcol:tcol + 1] * dl_ref[...])
        rows = r * FLT_TR + part * FLT_HALF + lax.broadcasted_iota(jnp.int32, taps.shape, 0)
        taps = jnp.where(rows == SEQ, 0.0, taps * decay)
        k_ref[part * FLT_HALF:(part + 1) * FLT_HALF, :] = taps
        total = total + jnp.sum(jnp.abs(taps), axis=0, keepdims=True)

    @pl.when(r == 0)
    def _():
        l1_ref[...] = jnp.zeros_like(l1_ref)

    l1_ref[...] += total


def _filter_taps(emb2, w1, b1, w2, b2, w3, b3, w4, freq, deltas):
    n = 2 * emb2.shape[0]
    fh = FILTER_HIDDEN
    eye2 = jnp.eye(2, dtype=F32)
    w1p = jnp.concatenate([w1, jnp.zeros((fh - w1.shape[0], fh), F32)], axis=0)
    wd = [jnp.kron(eye2, w) for w in (w1p, w2, w3)]
    bd = [jnp.tile(b, (1, 2)) for b in (b1, b2, b3, freq)]
    zeros = jnp.zeros_like(w4)
    w4t = jnp.concatenate([w4, zeros], axis=0)
    w4b = jnp.concatenate([zeros, w4], axis=0)
    small = lambda shape: pl.BlockSpec(shape, lambda r: (0, 0))
    fwd_tiles = SEQ // FLT_TR
    w4_spec = pl.BlockSpec((2 * fh, HYENA_W), lambda r: (0, r // fwd_tiles))
    return pl.pallas_call(
        _filter_kernel,
        grid=(n // FLT_TR,),
        in_specs=[pl.BlockSpec((FLT_HALF, 2 * fh), lambda r: (r, 0)),
                  small((2 * fh, 2 * fh)), small((1, 2 * fh)), small((2 * fh, 2 * fh)), small((1, 2 * fh)),
                  small((2 * fh, 2 * fh)), small((1, 2 * fh)),
                  w4_spec, w4_spec,
                  small((1, 2 * fh)), small((1, HYENA_W))],
        out_specs=[pl.BlockSpec((FLT_TR, HYENA_W), lambda r: (r, 0)),
                   pl.BlockSpec((1, HYENA_W), lambda r: (0, 0))],
        out_shape=[jax.ShapeDtypeStruct((n, HYENA_W), F32),
                   jax.ShapeDtypeStruct((1, HYENA_W), F32)],
        compiler_params=_cparams(1, 32),
        name="filter_taps",
    )(emb2, wd[0], bd[0], wd[1], bd[1], wd[2], bd[2], w4t, w4b, bd[3], deltas)


AT_TQ = 256


def _attn_kernel(q_ref, k_ref, v_ref, o_ref):
    k = k_ref[...]
    v = v_ref[...]
    for g in range(Q_PER_KV):
        sl = slice(g * HEAD_DIM, (g + 1) * HEAD_DIM)
        s = lax.dot_general(q_ref[:, sl], k, (((1,), (1,)), ((), ())), preferred_element_type=F32)
        m = jnp.max(s, axis=-1, keepdims=True)
        p = jnp.exp2(s - m)
        l = jnp.sum(p, axis=-1, keepdims=True)
        o = jnp.dot(p.astype(BF16), v, preferred_element_type=F32)
        o_ref[:, sl] = (o / l).astype(o_ref.dtype)


def _attention(q3, kv3):
    b, s, _ = q3.shape
    gw = Q_PER_KV * HEAD_DIM
    return pl.pallas_call(
        _attn_kernel,
        grid=(b, N_KV_HEADS, s // AT_TQ),
        in_specs=[pl.BlockSpec((None, AT_TQ, gw), lambda bi, kv, qi: (bi, qi, kv)),
                  pl.BlockSpec((None, s, HEAD_DIM), lambda bi, kv, qi: (bi, 0, kv)),
                  pl.BlockSpec((None, s, HEAD_DIM), lambda bi, kv, qi: (bi, 0, N_KV_HEADS + kv))],
        out_specs=pl.BlockSpec((None, AT_TQ, gw), lambda bi, kv, qi: (bi, qi, kv)),
        out_shape=jax.ShapeDtypeStruct((b, s, ATTN_W), BF16),
        compiler_params=_cparams(3, 48),
        name="attention",
    )(q3, kv3, kv3)


def _ep_merge(accs, extras, rows):
    gh_ref, ga_ref = extras
    return gh_ref[rows, :].astype(F32) * accs[0] + ga_ref[rows, :].astype(F32) * accs[1]


def _ep_residual(accs, extras, rows):
    return extras[0][rows, :] + accs[0]


def _ep_swiglu(accs, extras, rows):
    return jax.nn.silu(accs[0]) * accs[1]


def _merge(yh, ya, gates, w_h, w_a, tm=1024):
    tn = MM_TN
    nj = D_MODEL // tn
    return _wres_matmul("merge", [yh, ya], [(w_h, 0), (w_a, 0)],
                        [(gates, (tm, tn), lambda j, i: (i, j)),
                         (gates, (tm, tn), lambda j, i: (i, nj + j))],
                        _ep_merge, pairs=[0, 1], n_out=D_MODEL, out_dtype=BF16, tm=tm)


def _resmm(name, a, x, w, tm, vmem_mib):
    tn = MM_TN
    return _wres_matmul(name, [a], [(w, 0)], [(x, (tm, tn), lambda j, i: (i, j))], _ep_residual,
                        pairs=[0], n_out=w.shape[1], out_dtype=F32, tm=tm, vmem_mib=vmem_mib)


def _swiglu(h, w_g, w_u, tm=2048):
    return _wres_matmul("swiglu", [h], [(w_g, 0), (w_u, 0)], [], _ep_swiglu,
                        pairs=[0, 0], n_out=w_g.shape[1], out_dtype=BF16, tm=tm, vmem_mib=56)


def _layer(x, mix_norm_g, w_in, b_gate, hy_conv_w, hy_conv_b,
           flt_w1, flt_b1, flt_w2, flt_b2, flt_w3, flt_b3, flt_w4, flt_freq, hy_bias,
           q_norm_g, k_norm_g, w_br_hyena, w_br_attn, w_out,
           ffn_norm_g, w_ffn_gate, w_ffn_up, w_ffn_down):
    b, s, d = x.shape
    m = b * s
    row = lambda a: a.reshape(1, -1)
    tabs = _dft_tables()
    cos, sin = (jnp.asarray(t) for t in _rope_tables_np())
    xm = x.reshape(m, d)

    h = _rmsnorm(xm, row(mix_norm_g))
    u_h, q, kv, gates = _inproj(h, w_in, row(b_gate), row(q_norm_g), row(k_norm_g), cos, sin)

    emb2, deltas = (jnp.asarray(t) for t in _filter_tables_np())
    taps, l1 = _filter_taps(emb2, flt_w1, row(flt_b1), flt_w2, row(flt_b2), flt_w3, row(flt_b3),
                            flt_w4, row(flt_freq), deltas)
    spec = _filt_spec(taps, l1, tabs)

    z, x0c = _hy_pre(u_h.reshape(b, s, 3 * HYENA_W), hy_conv_w, row(hy_conv_b))
    y_h = _hy_conv(z, x0c, row(hy_bias), spec, tabs).reshape(m, HYENA_W)

    y_a = _attention(q.reshape(b, s, ATTN_W), kv.reshape(b, s, 2 * KV_W)).reshape(m, ATTN_W)

    merged = _merge(y_h, y_a, gates, w_br_hyena, w_br_attn)
    x1 = _resmm("out_proj", merged, xm, w_out, tm=1024, vmem_mib=48)

    h2 = _rmsnorm(x1, row(ffn_norm_g))
    act = _swiglu(h2, w_ffn_gate, w_ffn_up)
    out = _resmm("ffn_down", act, x1, w_ffn_down, tm=512, vmem_mib=56)
    return out.reshape(b, s, d)


def kernel(x, mix_norm_g, w_in, b_gate, hy_conv_w, hy_conv_b, flt_w1, flt_b1, flt_w2, flt_b2, flt_w3, flt_b3, flt_w4, flt_freq, hy_bias, q_norm_g, k_norm_g, w_br_hyena, w_br_attn, w_out, ffn_norm_g, w_ffn_gate, w_ffn_up, w_ffn_down):
    params = (mix_norm_g, w_in, b_gate, hy_conv_w, hy_conv_b, flt_w1, flt_b1, flt_w2, flt_b2,
              flt_w3, flt_b3, flt_w4, flt_freq, hy_bias, q_norm_g, k_norm_g, w_br_hyena, w_br_attn,
              w_out, ffn_norm_g, w_ffn_gate, w_ffn_up, w_ffn_down)
    for l in range(mix_norm_g.shape[0]):
        x = _layer(x, *(p[l] for p in params))
    return x
```

```python
import functools
import math

import numpy as np
import jax
import jax.numpy as jnp
from jax import lax
from jax.experimental import pallas as pl
from jax.experimental.pallas import tpu as pltpu

F32 = jnp.float32
BF16 = jnp.bfloat16

D_MODEL = 2048
SEQ = 4096
GRID_W = 64
HEAD_DIM = 128
N_Q_HEADS = 8
N_KV_HEADS = 2
Q_PER_KV = N_Q_HEADS // N_KV_HEADS
ATTN_W = N_Q_HEADS * HEAD_DIM
KV_W = N_KV_HEADS * HEAD_DIM
ROPE_THETA = 10000.0
HYENA_W = D_MODEL - ATTN_W
SHORT_TAPS = 3
FILTER_EMB = 33
FILTER_HIDDEN = 64
DECAY_TARGET = 1e-2
FAST_DECAY_PCT = 0.3
SLOW_DECAY_PCT = 1.5
IN_W = 3 * HYENA_W + ATTN_W + 2 * KV_W + 2 * D_MODEL
D_FF = 5632
EPS = 1e-6

COL_Q = 3 * HYENA_W
COL_K = COL_Q + ATTN_W
COL_V = COL_K + KV_W
COL_G = COL_V + KV_W

FFT_N = 2 * SEQ
FFT_N1 = 64
FFT_N2 = 128

MIB = 1024 * 1024


def _cparams(n_axes, vmem_mib):
    return pltpu.CompilerParams(
        dimension_semantics=("arbitrary",) * n_axes,
        vmem_limit_bytes=vmem_mib * MIB,
    )


DFT_R = 8


@functools.lru_cache(maxsize=None)
def _dft_tables_np():
    n, n1, n2, r = FFT_N, FFT_N1, FFT_N2, DFT_R
    eye = np.eye(r)
    f1 = np.arange(n1)
    ang1 = 2.0 * np.pi * ((f1[:, None] * f1[None, :]) % n1) / n1
    c1, s1 = np.cos(ang1), np.sin(ang1)
    h = n1 // 2
    m_fwd = np.block([[c1[:, :h], s1[:, :h]], [-s1[:, :h], c1[:, :h]]])
    m_flt = np.concatenate([c1, -s1], axis=0)
    ct, st = c1[:h, :], s1[:h, :]
    m_inv = np.block([[ct, -st], [st, ct]]) / n
    s2 = np.arange(n2)
    f = f1[:, None, None] + n1 * s2[None, :, None]
    th = 2.0 * np.pi * ((f * s2[None, None, :]) % n) / n
    c, s = np.cos(th), np.sin(th)
    fwd2 = np.concatenate(
        [np.concatenate([c, s], axis=2), np.concatenate([-s, c], axis=2)], axis=1)
    c_t, s_t = np.transpose(c, (0, 2, 1)), np.transpose(s, (0, 2, 1))
    inv2 = np.concatenate(
        [np.concatenate([c_t, -s_t], axis=2), np.concatenate([s_t, c_t], axis=2)], axis=1)
    f32 = lambda a: np.ascontiguousarray(a, dtype=np.float32)
    return dict(k_fwd=f32(np.kron(m_fwd, eye)), k_flt=f32(np.kron(m_flt, eye)),
                k_inv=f32(np.kron(m_inv, eye)), fwd2=f32(fwd2), inv2=f32(inv2))


def _dft_tables():
    return {k: jnp.asarray(v).astype(BF16) for k, v in _dft_tables_np().items()}


@functools.lru_cache(maxsize=None)
def _rope_tables_np():
    half = HEAD_DIM // 2
    inv = ROPE_THETA ** (-np.arange(0, half, 2, dtype=np.float64) / half)
    pos = np.arange(SEQ)
    ang_r = (pos // GRID_W)[:, None] * inv[None, :]
    ang_c = (pos % GRID_W)[:, None] * inv[None, :]
    cos = np.concatenate([np.cos(ang_r)] * 2 + [np.cos(ang_c)] * 2, axis=-1)
    sin = np.concatenate([-np.sin(ang_r), np.sin(ang_r), -np.sin(ang_c), np.sin(ang_c)], axis=-1)
    return cos.astype(np.float32), sin.astype(np.float32)


@functools.lru_cache(maxsize=None)
def _filter_tables_np():
    L = SEQ
    bands = (FILTER_EMB - 1) // 2
    pos = np.concatenate([np.arange(L, dtype=np.float64), L - np.arange(L, dtype=np.float64)])
    t = pos / max(L - 1, 1)
    fb = np.linspace(1e-4, bands - 1, bands)
    ang = (2.0 * math.pi * pos / L)[:, None] * fb[None, :]
    emb = np.concatenate([t[:, None], np.cos(ang), -np.sin(ang),
                          np.zeros((2 * L, FILTER_HIDDEN - FILTER_EMB))], axis=-1)
    max_decay = math.log(DECAY_TARGET) / FAST_DECAY_PCT
    min_decay = math.log(DECAY_TARGET) / SLOW_DECAY_PCT
    deltas = np.abs(np.linspace(min_decay, max_decay, HYENA_W))
    tile, hid = FLT_TR, FILTER_HIDDEN
    emb2 = emb.reshape(2 * L // tile, 2, tile // 2, hid).transpose(0, 2, 1, 3).reshape(L, 2 * hid)
    return np.ascontiguousarray(emb2, dtype=np.float32), deltas[None, :].astype(np.float32)


def _rmsnorm_kernel(x_ref, g_ref, o_ref):
    x = x_ref[...]
    ms = jnp.mean(x * x, axis=-1, keepdims=True)
    o_ref[...] = (x * lax.rsqrt(ms + EPS) * g_ref[...]).astype(o_ref.dtype)


def _rmsnorm(x, g, tm=512):
    m, d = x.shape
    return pl.pallas_call(
        _rmsnorm_kernel,
        grid=(m // tm,),
        in_specs=[pl.BlockSpec((tm, d), lambda i: (i, 0)),
                  pl.BlockSpec((1, d), lambda i: (0, 0))],
        out_specs=pl.BlockSpec((tm, d), lambda i: (i, 0)),
        out_shape=jax.ShapeDtypeStruct((m, d), BF16),
        compiler_params=_cparams(1, 32),
        name="rmsnorm",
    )(x, g)


MM_TN = 512
MM_RC = 256


def _wres_kernel(*refs, pairs, n_act, n_extra, n_out, rc, epilogue):
    n_w = len(pairs)
    acts = refs[:n_act]
    ws = refs[n_act:n_act + n_w]
    extras = refs[n_act + n_w:n_act + n_w + n_extra]
    o_refs = refs[n_act + n_w + n_extra:n_act + n_w + n_extra + n_out]
    wbs = refs[n_act + n_w + n_extra + n_out:]

    @pl.when(pl.program_id(1) == 0)
    def _():
        for w_ref, wb_ref in zip(ws, wbs):
            wb_ref[...] = w_ref[...].astype(BF16)

    for c in range(o_refs[0].shape[0] // rc):
        rows = pl.ds(c * rc, rc)
        accs = [jnp.dot(acts[a][rows, :], wb_ref[...], preferred_element_type=F32)
                for a, wb_ref in zip(pairs, wbs)]
        outs = epilogue(accs, extras, rows)
        for o_ref, out in zip(o_refs, outs if isinstance(outs, tuple) else (outs,)):
            o_ref[rows, :] = out.astype(o_ref.dtype)


def _wres_matmul(name, acts, weights, extras, epilogue, *, pairs, n_out, out_dtype, tm,
                 tn=MM_TN, rc=MM_RC, vmem_mib=48):
    m = acts[0].shape[0]
    out_dtypes = out_dtype if isinstance(out_dtype, tuple) else (out_dtype,)
    in_specs = [pl.BlockSpec((tm, a.shape[1]), lambda j, i: (i, 0)) for a in acts]
    assert all(off % 128 == 0 for _, off in weights) and tn % 128 == 0
    w_mode = dict(pipeline_mode=pl.Buffered(1)) if n_out == tn else {}
    in_specs += [pl.BlockSpec((pl.Element(w.shape[0]), pl.Element(tn)),
                              lambda j, i, off=off: (0, pl.multiple_of(off + j * tn, 128)), **w_mode)
                 for w, off in weights]
    in_specs += [pl.BlockSpec(blk, imap) for _, blk, imap in extras]
    kern = functools.partial(_wres_kernel, pairs=tuple(pairs), n_act=len(acts), n_extra=len(extras),
                             n_out=len(out_dtypes), rc=rc, epilogue=epilogue)
    outs = pl.pallas_call(
        kern,
        grid=(n_out // tn, m // tm),
        in_specs=in_specs,
        out_specs=[pl.BlockSpec((tm, tn), lambda j, i: (i, j)) for _ in out_dtypes],
        out_shape=[jax.ShapeDtypeStruct((m, n_out), dt) for dt in out_dtypes],
        scratch_shapes=[pltpu.VMEM((w.shape[0], tn), BF16) for w, _ in weights],
        compiler_params=_cparams(2, vmem_mib),
        name=name,
    )(*acts, *(w for w, _ in weights), *(e for e, _, _ in extras))
    return outs if isinstance(out_dtype, tuple) else outs[0]


def _head_norm_rope(x, g, cos, sin, scale):
    ms = jnp.mean(x * x, axis=-1, keepdims=True)
    y = x * lax.rsqrt(ms + EPS) * g
    lane = lax.broadcasted_iota(jnp.int32, y.shape, 1)
    swapped = jnp.where((lane % 64) < 32, pltpu.roll(y, 96, 1), pltpu.roll(y, 32, 1))
    out = y * cos + swapped * sin
    return out * scale if scale != 1.0 else out


def _ep_cast(accs, extras, rows):
    return accs[0]


def _ep_heads(accs, extras, rows, *, n_heads, scale):
    g_ref, cos_ref, sin_ref = extras
    acc = accs[0]
    cos, sin = cos_ref[rows, :], sin_ref[rows, :]
    parts = [_head_norm_rope(acc[:, hh * HEAD_DIM:(hh + 1) * HEAD_DIM], g_ref[...], cos, sin, scale)
             for hh in range(n_heads)]
    if n_heads * HEAD_DIM < acc.shape[1]:
        parts.append(acc[:, n_heads * HEAD_DIM:])
    return jnp.concatenate(parts, axis=1)


def _ep_gate(accs, extras, rows):
    return jax.nn.sigmoid(accs[0] + extras[0][...])


IN_TM = 2048
IN_TN = 1024
LOG2E = math.log2(math.e)


def _inproj(h, w_in, b_gate, q_g, k_g, cos, sin):
    tm, tn = IN_TM, IN_TN
    s_blocks = SEQ // tm
    head = lambda g: (g, (1, HEAD_DIM), lambda j, i: (0, 0))
    pos = lambda t: (t, (tm, HEAD_DIM), lambda j, i: (i % s_blocks, 0))
    common = dict(pairs=[0], out_dtype=BF16, tm=tm, vmem_mib=56)
    u_h = _wres_matmul("inproj_hyena", [h], [(w_in, 0)], [], _ep_cast,
                       n_out=3 * HYENA_W, tn=tn, **common)
    q = _wres_matmul("inproj_q", [h], [(w_in, COL_Q)], [head(q_g), pos(cos), pos(sin)],
                     functools.partial(_ep_heads, n_heads=tn // HEAD_DIM, scale=HEAD_DIM ** -0.5 * LOG2E),
                     n_out=ATTN_W, tn=tn, **common)
    kv = _wres_matmul("inproj_kv", [h], [(w_in, COL_K)], [head(k_g), pos(cos), pos(sin)],
                      functools.partial(_ep_heads, n_heads=N_KV_HEADS, scale=1.0),
                      n_out=2 * KV_W, tn=2 * KV_W, **common)
    gates = _wres_matmul("inproj_gate", [h], [(w_in, COL_G)],
                         [(b_gate, (1, tn), lambda j, i: (0, j))], _ep_gate,
                         n_out=2 * D_MODEL, tn=tn, **common)
    return u_h, q, kv, gates


HY_CT = 256
HY_RC = 256


def _hy_pre_kernel(x0_ref, x1_ref, v_ref, w0_ref, w1_ref, wv_ref, b0_ref, b1_ref, bv_ref,
                   z_ref, x0c_ref, pad_ref):
    L = x0_ref.shape[0]
    ct = x0_ref.shape[1]
    nchunk = L // HY_RC
    zeros8 = jnp.zeros((8, ct), F32)
    for a, src in enumerate((x0_ref, x1_ref, v_ref)):
        pad_ref[a, 0:8, :] = zeros8
        pad_ref[a, L + 8:L + 16, :] = zeros8

        def fill(r, carry, a=a, src=src):
            base = pl.multiple_of(r * HY_RC, HY_RC)
            pad_ref[a, pl.ds(base + 8, HY_RC), :] = src[pl.ds(base, HY_RC), :].astype(F32)
            return carry

        lax.fori_loop(0, nchunk, fill, 0)

    def conv(a, w_ref, b_ref, base):
        e = pad_ref[a, pl.ds(base, HY_RC + 16), :]
        up = pltpu.roll(e, 1, 0)[8:8 + HY_RC]
        mid = e[8:8 + HY_RC]
        dn = pltpu.roll(e, HY_RC + 15, 0)[8:8 + HY_RC]
        return b_ref[...] + up * w_ref[0:1, :] + mid * w_ref[1:2, :] + dn * w_ref[2:3, :]

    def body(r, carry):
        base = pl.multiple_of(r * HY_RC, HY_RC)
        x0c = conv(0, w0_ref, b0_ref, base)
        x1c = conv(1, w1_ref, b1_ref, base)
        vc = conv(2, wv_ref, bv_ref, base)
        z_ref[pl.ds(base, HY_RC), :] = vc * x1c
        x0c_ref[pl.ds(base, HY_RC), :] = x0c.astype(BF16)
        return carry

    lax.fori_loop(0, nchunk, body, 0)


def _hy_pre(proj3, conv_w, conv_b):
    b, L, _ = proj3.shape
    ct = HY_CT
    nct = HYENA_W // ct
    in_specs = []
    for grp in range(3):
        in_specs.append(pl.BlockSpec((None, L, ct), lambda bi, c, grp=grp: (bi, 0, c + grp * nct)))
    for grp in range(3):
        in_specs.append(pl.BlockSpec((SHORT_TAPS, ct), lambda bi, c, grp=grp: (0, c + grp * nct)))
    for grp in range(3):
        in_specs.append(pl.BlockSpec((1, ct), lambda bi, c, grp=grp: (0, c + grp * nct)))
    out_spec = pl.BlockSpec((None, L, ct), lambda bi, c: (bi, 0, c))
    return pl.pallas_call(
        _hy_pre_kernel,
        grid=(b, nct),
        in_specs=in_specs,
        out_specs=[out_spec, out_spec],
        out_shape=[jax.ShapeDtypeStruct((b, L, HYENA_W), F32),
                   jax.ShapeDtypeStruct((b, L, HYENA_W), BF16)],
        scratch_shapes=[pltpu.VMEM((3, L + 16, ct), F32)],
        compiler_params=_cparams(2, 48),
        name="hyena_pre",
    )(proj3, proj3, proj3, conv_w, conv_w, conv_w, conv_b, conv_b, conv_b)


CV_CT = 256
CV_F1B = 8
CV_NF = FFT_N1 // CV_F1B
CV_SLAB = 2 * DFT_R
CV_NSLAB = FFT_N2 // CV_SLAB
CV_HALF = FFT_N1 // 2


def _outer_fwd_slab(src_ref, k_ref, a_ref, j, rows_in):
    ct = a_ref.shape[-1]
    halves = []
    for h in range(2):
        r = src_ref[:, :, 2 * j + h] if src_ref.ndim == 5 else src_ref[:, 2 * j + h]
        r = r.reshape(rows_in, ct).astype(BF16)
        o = jnp.dot(k_ref[...], r, preferred_element_type=F32)
        halves.append(o.reshape(2 * FFT_N1, DFT_R, ct))
    slab = jnp.concatenate(halves, axis=1).astype(BF16)
    a_ref[:, :, pl.ds(pl.multiple_of(j * CV_SLAB, CV_SLAB), CV_SLAB), :] = slab.reshape(
        2, FFT_N1, CV_SLAB, ct)


def _hy_conv_kernel(z_ref, x0_ref, bias_ref, kf_ref, ki_ref, fw_ref, iv_ref, spec_ref, o_ref, a_ref):
    s = pl.program_id(2)
    ct = a_ref.shape[-1]

    @pl.when(s == 0)
    def _():
        def body(j, carry):
            _outer_fwd_slab(z_ref, kf_ref, a_ref, j, 2 * CV_HALF * DFT_R)
            return carry

        lax.fori_loop(0, CV_NSLAB, body, 0)

    @pl.when((s >= 1) & (s <= CV_NF))
    def _():
        f0 = (s - 1) * CV_F1B
        for fl in range(CV_F1B):
            x = a_ref[:, f0 + fl].reshape(2 * FFT_N2, ct)
            u = jnp.dot(fw_ref[fl], x, preferred_element_type=F32)
            ure, uim = u[:FFT_N2], u[FFT_N2:]
            kre, kim = spec_ref[0, fl].astype(F32), spec_ref[1, fl].astype(F32)
            p = jnp.concatenate([ure * kre - uim * kim, ure * kim + uim * kre], axis=0).astype(BF16)
            y = jnp.dot(iv_ref[fl], p, preferred_element_type=F32)
            a_ref[:, f0 + fl] = y.reshape(2, FFT_N2, ct).astype(BF16)

    @pl.when(s == CV_NF + 1)
    def _():
        bias = bias_ref[...]

        def body(j, carry):
            slab = a_ref[:, :, pl.ds(pl.multiple_of(j * CV_SLAB, CV_SLAB), CV_SLAB), :].astype(F32)
            x0 = x0_ref[:, :, j].astype(F32)
            halves = []
            for h in range(2):
                r = slab[:, :, h * DFT_R:(h + 1) * DFT_R, :].reshape(2 * FFT_N1 * DFT_R, ct).astype(BF16)
                y = jnp.dot(ki_ref[...], r, preferred_element_type=F32).reshape(2, CV_HALF, DFT_R, ct)
                z = z_ref[:, :, 2 * j + h]
                halves.append((y + bias * z) * x0[:, :, h * DFT_R:(h + 1) * DFT_R, :])
            o_ref[:, :, j] = jnp.concatenate(halves, axis=2).astype(o_ref.dtype)
            return carry

        lax.fori_loop(0, CV_NSLAB, body, 0)


def _hy_conv(z, x0c, bias, spec, tabs):
    b, L, c = z.shape
    assert b % 2 == 0 and L * 2 == FFT_N
    ct = CV_CT
    z5 = z.reshape(b, CV_HALF, FFT_N2 // DFT_R, DFT_R, c)
    x5 = x0c.reshape(b, CV_HALF, CV_NSLAB, CV_SLAB, c)
    fidx = lambda s: jnp.clip(s - 1, 0, CV_NF - 1)
    once = pl.Buffered(1)
    tab_spec = pl.BlockSpec((CV_F1B, 2 * FFT_N2, 2 * FFT_N2), lambda p, cb, s: (fidx(s), 0, 0))
    io16 = pl.BlockSpec((2, CV_HALF, CV_NSLAB, CV_SLAB, ct), lambda p, cb, s: (p, 0, 0, 0, cb))
    out = pl.pallas_call(
        _hy_conv_kernel,
        grid=(b // 2, c // ct, CV_NF + 2),
        in_specs=[
            pl.BlockSpec((2, CV_HALF, FFT_N2 // DFT_R, DFT_R, ct), lambda p, cb, s: (p, 0, 0, 0, cb),
                         pipeline_mode=once),
            io16,
            pl.BlockSpec((1, ct), lambda p, cb, s: (0, cb)),
            pl.BlockSpec(tabs["k_fwd"].shape, lambda p, cb, s: (0, 0), pipeline_mode=once),
            pl.BlockSpec(tabs["k_inv"].shape, lambda p, cb, s: (0, 0), pipeline_mode=once),
            tab_spec, tab_spec,
            pl.BlockSpec((2, CV_F1B, FFT_N2, ct), lambda p, cb, s: (0, fidx(s), 0, cb)),
        ],
        out_specs=io16,
        out_shape=jax.ShapeDtypeStruct(x5.shape, BF16),
        scratch_shapes=[pltpu.VMEM((2, FFT_N1, FFT_N2, ct), BF16)],
        compiler_params=_cparams(3, 56),
        name="hyena_conv",
    )(z5, x5, bias, tabs["k_fwd"], tabs["k_inv"], tabs["fwd2"], tabs["inv2"], spec)
    return out.reshape(b, L, c)


def _filt_spec_kernel(t_ref, l1_ref, kf_ref, fw_ref, o_ref, a_ref):
    s = pl.program_id(1)
    ct = a_ref.shape[-1]

    @pl.when(s == 0)
    def _():
        def body(j, carry):
            _outer_fwd_slab(t_ref, kf_ref, a_ref, j, FFT_N1 * DFT_R)
            return carry

        lax.fori_loop(0, CV_NSLAB, body, 0)

    @pl.when(s >= 1)
    def _():
        inv_l1 = 1.0 / l1_ref[...]
        f0 = (s - 1) * FS_F1B
        for fl in range(FS_F1B):
            x = a_ref[:, f0 + fl].reshape(2 * FFT_N2, ct)
            y = jnp.dot(fw_ref[fl], x, preferred_element_type=F32) * inv_l1
            o_ref[:, fl] = y.reshape(2, FFT_N2, ct).astype(o_ref.dtype)


FS_F1B = 16
FS_NF = FFT_N1 // FS_F1B


def _filt_spec(taps, l1, tabs):
    n, c = taps.shape
    ct = CV_CT
    t4 = taps.reshape(FFT_N1, FFT_N2 // DFT_R, DFT_R, c)
    fidx = lambda s: jnp.clip(s - 1, 0, FS_NF - 1)
    once = pl.Buffered(1)
    return pl.pallas_call(
        _filt_spec_kernel,
        grid=(c // ct, FS_NF + 1),
        in_specs=[
            pl.BlockSpec((FFT_N1, FFT_N2 // DFT_R, DFT_R, ct), lambda cb, s: (0, 0, 0, cb)),
            pl.BlockSpec((1, ct), lambda cb, s: (0, cb)),
            pl.BlockSpec(tabs["k_flt"].shape, lambda cb, s: (0, 0), pipeline_mode=once),
            pl.BlockSpec((FS_F1B, 2 * FFT_N2, 2 * FFT_N2), lambda cb, s: (fidx(s), 0, 0)),
        ],
        out_specs=pl.BlockSpec((2, FS_F1B, FFT_N2, ct), lambda cb, s: (0, fidx(s), 0, cb)),
        out_shape=jax.ShapeDtypeStruct((2, FFT_N1, FFT_N2, c), BF16),
        scratch_shapes=[pltpu.VMEM((2, FFT_N1, FFT_N2, ct), BF16)],
        compiler_params=_cparams(2, 48),
        name="filter_spectrum",
    )(t4, l1, tabs["k_flt"], tabs["fwd2"])


FLT_TR = 512


FLT_HALF = FLT_TR // 2


def _dot3(a, w):
    a_hi = a.astype(BF16)
    a_lo = (a - a_hi.astype(F32)).astype(BF16)
    w_hi = w.astype(BF16)
    w_lo = (w - w_hi.astype(F32)).astype(BF16)
    lhs = jnp.concatenate([a_hi, a_hi, a_lo], axis=1)
    rhs = jnp.concatenate([w_hi, w_lo, w_hi], axis=0)
    return jnp.dot(lhs, rhs, preferred_element_type=F32)


def _filter_kernel(emb_ref, w1_ref, b1_ref, w2_ref, b2_ref, w3_ref, b3_ref, w4t_ref, w4b_ref, fr_ref, dl_ref,
                   k_ref, l1_ref):
    r = pl.program_id(0)
    e = emb_ref[...]
    fr = fr_ref[...]
    h = jnp.sin(fr * (_dot3(e, w1_ref[...]) + b1_ref[...]))
    h = jnp.sin(fr * (_dot3(h, w2_ref[...]) + b2_ref[...]))
    h = jnp.sin(fr * (_dot3(h, w3_ref[...]) + b3_ref[...]))
    total = jnp.zeros(l1_ref.shape, F32)
    for part, w_ref in enumerate((w4t_ref, w4b_ref)):
        tcol = part * FILTER_HIDDEN
        taps = _dot3(h, w_ref[...])
        decay = jnp.exp(-e[:, tcol:tcol + 1] * dl_ref[...])
        rows = r * FLT_TR + part * FLT_HALF + lax.broadcasted_iota(jnp.int32, taps.shape, 0)
        taps = jnp.where(rows == SEQ, 0.0, taps * decay)
        k_ref[part * FLT_HALF:(part + 1) * FLT_HALF, :] = taps
        total = total + jnp.sum(jnp.abs(taps), axis=0, keepdims=True)

    @pl.when(r == 0)
    def _():
        l1_ref[...] = jnp.zeros_like(l1_ref)

    l1_ref[...] += total


def _filter_taps(emb2, w1, b1, w2, b2, w3, b3, w4, freq, deltas):
    n = 2 * emb2.shape[0]
    fh = FILTER_HIDDEN
    eye2 = jnp.eye(2, dtype=F32)
    w1p = jnp.concatenate([w1, jnp.zeros((fh - w1.shape[0], fh), F32)], axis=0)
    wd = [jnp.kron(eye2, w) for w in (w1p, w2, w3)]
    bd = [jnp.tile(b, (1, 2)) for b in (b1, b2, b3, freq)]
    zeros = jnp.zeros_like(w4)
    w4t = jnp.concatenate([w4, zeros], axis=0)
    w4b = jnp.concatenate([zeros, w4], axis=0)
    small = lambda shape: pl.BlockSpec(shape, lambda r: (0, 0))
    fwd_tiles = SEQ // FLT_TR
    w4_spec = pl.BlockSpec((2 * fh, HYENA_W), lambda r: (0, r // fwd_tiles))
    return pl.pallas_call(
        _filter_kernel,
        grid=(n // FLT_TR,),
        in_specs=[pl.BlockSpec((FLT_HALF, 2 * fh), lambda r: (r, 0)),
                  small((2 * fh, 2 * fh)), small((1, 2 * fh)), small((2 * fh, 2 * fh)), small((1, 2 * fh)),
                  small((2 * fh, 2 * fh)), small((1, 2 * fh)),
                  w4_spec, w4_spec,
                  small((1, 2 * fh)), small((1, HYENA_W))],
        out_specs=[pl.BlockSpec((FLT_TR, HYENA_W), lambda r: (r, 0)),
                   pl.BlockSpec((1, HYENA_W), lambda r: (0, 0))],
        out_shape=[jax.ShapeDtypeStruct((n, HYENA_W), F32),
                   jax.ShapeDtypeStruct((1, HYENA_W), F32)],
        compiler_params=_cparams(1, 32),
        name="filter_taps",
    )(emb2, wd[0], bd[0], wd[1], bd[1], wd[2], bd[2], w4t, w4b, bd[3], deltas)


AT_TQ = 512
AT_RQ = 256
AT_TK = 512


def _attn_kernel(q_ref, k_ref, v_ref, o_ref, s0_ref, s1_ref):
    s_refs = (s0_ref, s1_ref)
    nblk = k_ref.shape[0] // AT_TK
    units = [(pl.ds(r * AT_RQ, AT_RQ), slice(g * HEAD_DIM, (g + 1) * HEAD_DIM))
             for r in range(q_ref.shape[0] // AT_RQ) for g in range(Q_PER_KV)]
    lane_blocks = lambda a: [a[:, i:i + 128] for i in range(0, a.shape[1], 128)]

    def scores(u, j, m_run):
        rows, lanes = units[u]
        ks = pl.ds(j * AT_TK, AT_TK)
        s = lax.dot_general(q_ref[rows, lanes], k_ref[ks, :], (((1,), (1,)), ((), ())),
                            preferred_element_type=F32)
        s_refs[u % 2][:, ks] = s
        blk = functools.reduce(jnp.maximum, lane_blocks(s))
        return blk if m_run is None else jnp.maximum(m_run, blk)

    def weighted(u, j, m, l_run, acc):
        ks = pl.ds(j * AT_TK, AT_TK)
        p = jnp.exp2(s_refs[u % 2][:, ks] - m)
        l_blk = functools.reduce(jnp.add, lane_blocks(p))
        o = jnp.dot(p.astype(BF16), v_ref[ks, :], preferred_element_type=F32)
        return (l_blk if l_run is None else l_run + l_blk), (o if acc is None else acc + o)

    m_run = None
    for j in range(nblk):
        m_run = scores(0, j, m_run)
    for u, (rows, lanes) in enumerate(units):
        m = jnp.max(m_run, axis=-1, keepdims=True)
        m_run, l_run, acc = None, None, None
        for j in range(nblk):
            if u + 1 < len(units):
                m_run = scores(u + 1, j, m_run)
            l_run, acc = weighted(u, j, m, l_run, acc)
        l = jnp.sum(l_run, axis=-1, keepdims=True)
        o_ref[rows, lanes] = (acc / l).astype(o_ref.dtype)


def _attention(q3, kv3):
    b, s, _ = q3.shape
    gw = Q_PER_KV * HEAD_DIM
    return pl.pallas_call(
        _attn_kernel,
        grid=(b, N_KV_HEADS, s // AT_TQ),
        in_specs=[pl.BlockSpec((None, AT_TQ, gw), lambda bi, kv, qi: (bi, qi, kv)),
                  pl.BlockSpec((None, s, HEAD_DIM), lambda bi, kv, qi: (bi, 0, kv)),
                  pl.BlockSpec((None, s, HEAD_DIM), lambda bi, kv, qi: (bi, 0, N_KV_HEADS + kv))],
        out_specs=pl.BlockSpec((None, AT_TQ, gw), lambda bi, kv, qi: (bi, qi, kv)),
        out_shape=jax.ShapeDtypeStruct((b, s, ATTN_W), BF16),
        scratch_shapes=[pltpu.VMEM((AT_RQ, s), F32), pltpu.VMEM((AT_RQ, s), F32)],
        compiler_params=_cparams(3, 48),
        name="attention",
    )(q3, kv3, kv3)


def _ep_merge(accs, extras, rows):
    gh_ref, ga_ref = extras
    return gh_ref[rows, :].astype(F32) * accs[0] + ga_ref[rows, :].astype(F32) * accs[1]


def _ep_residual(accs, extras, rows):
    return extras[0][rows, :] + accs[0]


def _ep_swiglu(accs, extras, rows):
    return jax.nn.silu(accs[0]) * accs[1]


def _ep_residual_norm(accs, extras, rows):
    x_ref, g_ref = extras
    x1 = x_ref[rows, :] + accs[0]
    ms = jnp.mean(x1 * x1, axis=-1, keepdims=True)
    return x1, x1 * lax.rsqrt(ms + EPS) * g_ref[...]


def _out_proj(merged, x, w, g, tm=512):
    n = w.shape[1]
    return _wres_matmul("out_proj", [merged], [(w, 0)],
                        [(x, (tm, n), lambda j, i: (i, 0)), (g, (1, n), lambda j, i: (0, 0))],
                        _ep_residual_norm, pairs=[0], n_out=n, out_dtype=(F32, BF16), tm=tm, tn=n,
                        vmem_mib=56)


def _merge(yh, ya, gates, w_h, w_a, tm=1024):
    tn = 1024
    nj = D_MODEL // tn
    return _wres_matmul("merge", [yh, ya], [(w_h, 0), (w_a, 0)],
                        [(gates, (tm, tn), lambda j, i: (i, j)),
                         (gates, (tm, tn), lambda j, i: (i, nj + j))],
                        _ep_merge, pairs=[0, 1], n_out=D_MODEL, out_dtype=BF16, tm=tm, tn=tn)


def _resmm(name, a, x, w, tm, vmem_mib):
    tn = MM_TN
    return _wres_matmul(name, [a], [(w, 0)], [(x, (tm, tn), lambda j, i: (i, j))], _ep_residual,
                        pairs=[0], n_out=w.shape[1], out_dtype=F32, tm=tm, vmem_mib=vmem_mib)


def _swiglu(h, w_g, w_u, tm=2048):
    return _wres_matmul("swiglu", [h], [(w_g, 0), (w_u, 0)], [], _ep_swiglu,
                        pairs=[0, 0], n_out=w_g.shape[1], out_dtype=BF16, tm=tm, vmem_mib=56)


def _layer(x, mix_norm_g, w_in, b_gate, hy_conv_w, hy_conv_b,
           flt_w1, flt_b1, flt_w2, flt_b2, flt_w3, flt_b3, flt_w4, flt_freq, hy_bias,
           q_norm_g, k_norm_g, w_br_hyena, w_br_attn, w_out,
           ffn_norm_g, w_ffn_gate, w_ffn_up, w_ffn_down):
    b, s, d = x.shape
    m = b * s
    row = lambda a: a.reshape(1, -1)
    tabs = _dft_tables()
    cos, sin = (jnp.asarray(t) for t in _rope_tables_np())
    xm = x.reshape(m, d)

    h = _rmsnorm(xm, row(mix_norm_g))
    u_h, q, kv, gates = _inproj(h, w_in, row(b_gate), row(q_norm_g), row(k_norm_g), cos, sin)

    emb2, deltas = (jnp.asarray(t) for t in _filter_tables_np())
    taps, l1 = _filter_taps(emb2, flt_w1, row(flt_b1), flt_w2, row(flt_b2), flt_w3, row(flt_b3),
                            flt_w4, row(flt_freq), deltas)
    spec = _filt_spec(taps, l1, tabs)

    z, x0c = _hy_pre(u_h.reshape(b, s, 3 * HYENA_W), hy_conv_w, row(hy_conv_b))
    y_h = _hy_conv(z, x0c, row(hy_bias), spec, tabs).reshape(m, HYENA_W)

    y_a = _attention(q.reshape(b, s, ATTN_W), kv.reshape(b, s, 2 * KV_W)).reshape(m, ATTN_W)

    merged = _merge(y_h, y_a, gates, w_br_hyena, w_br_attn)
    x1, h2 = _out_proj(merged, xm, w_out, row(ffn_norm_g))
    act = _swiglu(h2, w_ffn_gate, w_ffn_up)
    out = _resmm("ffn_down", act, x1, w_ffn_down, tm=512, vmem_mib=56)
    return out.reshape(b, s, d)


def kernel(x, mix_norm_g, w_in, b_gate, hy_conv_w, hy_conv_b, flt_w1, flt_b1, flt_w2, flt_b2, flt_w3, flt_b3, flt_w4, flt_freq, hy_bias, q_norm_g, k_norm_g, w_br_hyena, w_br_attn, w_out, ffn_norm_g, w_ffn_gate, w_ffn_up, w_ffn_down):
    params = (mix_norm_g, w_in, b_gate, hy_conv_w, hy_conv_b, flt_w1, flt_b1, flt_w2, flt_b2,
              flt_w3, flt_b3, flt_w4, flt_freq, hy_bias, q_norm_g, k_norm_g, w_br_hyena, w_br_attn,
              w_out, ffn_norm_g, w_ffn_gate, w_ffn_up, w_ffn_down)
    for l in range(mix_norm_g.shape[0]):
        x = _layer(x, *(p[l] for p in params))
    return x
```

```python
import functools
import math

import numpy as np
import jax
import jax.numpy as jnp
from jax import lax
from jax.experimental import pallas as pl
from jax.experimental.pallas import tpu as pltpu

F32 = jnp.float32
BF16 = jnp.bfloat16

D_MODEL = 2048
SEQ = 4096
GRID_W = 64
HEAD_DIM = 128
N_Q_HEADS = 8
N_KV_HEADS = 2
Q_PER_KV = N_Q_HEADS // N_KV_HEADS
ATTN_W = N_Q_HEADS * HEAD_DIM
KV_W = N_KV_HEADS * HEAD_DIM
ROPE_THETA = 10000.0
HYENA_W = D_MODEL - ATTN_W
SHORT_TAPS = 3
FILTER_EMB = 33
FILTER_HIDDEN = 64
DECAY_TARGET = 1e-2
FAST_DECAY_PCT = 0.3
SLOW_DECAY_PCT = 1.5
IN_W = 3 * HYENA_W + ATTN_W + 2 * KV_W + 2 * D_MODEL
D_FF = 5632
EPS = 1e-6

COL_Q = 3 * HYENA_W
COL_K = COL_Q + ATTN_W
COL_V = COL_K + KV_W
COL_G = COL_V + KV_W

FFT_N = 2 * SEQ
FFT_N1 = 64
FFT_N2 = 128

MIB = 1024 * 1024


def _cparams(n_axes, vmem_mib):
    return pltpu.CompilerParams(
        dimension_semantics=("arbitrary",) * n_axes,
        vmem_limit_bytes=vmem_mib * MIB,
    )


DFT_R = 8


@functools.lru_cache(maxsize=None)
def _dft_tables_np():
    n, n1, n2, r = FFT_N, FFT_N1, FFT_N2, DFT_R
    eye = np.eye(r)
    f1 = np.arange(n1)
    ang1 = 2.0 * np.pi * ((f1[:, None] * f1[None, :]) % n1) / n1
    c1, s1 = np.cos(ang1), np.sin(ang1)
    h = n1 // 2
    m_fwd = np.block([[c1[:, :h], s1[:, :h]], [-s1[:, :h], c1[:, :h]]])
    m_flt = np.concatenate([c1, -s1], axis=0)
    ct, st = c1[:h, :], s1[:h, :]
    m_inv = np.block([[ct, -st], [st, ct]]) / n
    s2 = np.arange(n2)
    f = f1[:, None, None] + n1 * s2[None, :, None]
    th = 2.0 * np.pi * ((f * s2[None, None, :]) % n) / n
    c, s = np.cos(th), np.sin(th)
    fwd2 = np.concatenate(
        [np.concatenate([c, s], axis=2), np.concatenate([-s, c], axis=2)], axis=1)
    c_t, s_t = np.transpose(c, (0, 2, 1)), np.transpose(s, (0, 2, 1))
    inv2 = np.concatenate(
        [np.concatenate([c_t, -s_t], axis=2), np.concatenate([s_t, c_t], axis=2)], axis=1)
    f32 = lambda a: np.ascontiguousarray(a, dtype=np.float32)
    return dict(k_fwd=f32(np.kron(m_fwd, eye)), k_flt=f32(np.kron(m_flt, eye)),
                k_inv=f32(np.kron(m_inv, eye)), fwd2=f32(fwd2), inv2=f32(inv2))


def _dft_tables():
    return {k: jnp.asarray(v).astype(BF16) for k, v in _dft_tables_np().items()}


@functools.lru_cache(maxsize=None)
def _rope_tables_np():
    half = HEAD_DIM // 2
    inv = ROPE_THETA ** (-np.arange(0, half, 2, dtype=np.float64) / half)
    pos = np.arange(SEQ)
    ang_r = (pos // GRID_W)[:, None] * inv[None, :]
    ang_c = (pos % GRID_W)[:, None] * inv[None, :]
    cos = np.concatenate([np.cos(ang_r)] * 2 + [np.cos(ang_c)] * 2, axis=-1)
    sin = np.concatenate([-np.sin(ang_r), np.sin(ang_r), -np.sin(ang_c), np.sin(ang_c)], axis=-1)
    return cos.astype(np.float32), sin.astype(np.float32)


@functools.lru_cache(maxsize=None)
def _filter_tables_np():
    L = SEQ
    bands = (FILTER_EMB - 1) // 2
    pos = np.concatenate([np.arange(L, dtype=np.float64), L - np.arange(L, dtype=np.float64)])
    t = pos / max(L - 1, 1)
    fb = np.linspace(1e-4, bands - 1, bands)
    ang = (2.0 * math.pi * pos / L)[:, None] * fb[None, :]
    emb = np.concatenate([t[:, None], np.cos(ang), -np.sin(ang),
                          np.zeros((2 * L, FILTER_HIDDEN - FILTER_EMB))], axis=-1)
    max_decay = math.log(DECAY_TARGET) / FAST_DECAY_PCT
    min_decay = math.log(DECAY_TARGET) / SLOW_DECAY_PCT
    deltas = np.abs(np.linspace(min_decay, max_decay, HYENA_W))
    tile, hid = FLT_TR, FILTER_HIDDEN
    emb2 = emb.reshape(2 * L // tile, 2, tile // 2, hid).transpose(0, 2, 1, 3).reshape(L, 2 * hid)
    return np.ascontiguousarray(emb2, dtype=np.float32), deltas[None, :].astype(np.float32)


MM_TN = 512
MM_RC = 256


def _wres_kernel(*refs, pairs, n_act, n_extra, n_out, rc, epilogue, norm_first):
    n_w = len(pairs)
    acts = refs[:n_act]
    ws = refs[n_act:n_act + n_w]
    extras = refs[n_act + n_w:n_act + n_w + n_extra]
    o_refs = refs[n_act + n_w + n_extra:n_act + n_w + n_extra + n_out]
    wbs = refs[n_act + n_w + n_extra + n_out:]
    if norm_first:
        g_ref, extras = extras[-1], extras[:-1]
        h_ref, o_refs = o_refs[-1], o_refs[:-1]

    @pl.when(pl.program_id(1) == 0)
    def _():
        for w_ref, wb_ref in zip(ws, wbs):
            wb_ref[...] = w_ref[...].astype(BF16)

    for c in range(o_refs[0].shape[0] // rc):
        rows = pl.ds(c * rc, rc)
        lhs = [a[rows, :] for a in acts]
        if norm_first:
            x = lhs[0]
            ms = jnp.mean(x * x, axis=-1, keepdims=True)
            lhs[0] = (x * lax.rsqrt(ms + EPS) * g_ref[...]).astype(BF16)
            h_ref[rows, :] = lhs[0]
        accs = [jnp.dot(lhs[a], wb_ref[...], preferred_element_type=F32)
                for a, wb_ref in zip(pairs, wbs)]
        outs = epilogue(accs, extras, rows)
        for o_ref, out in zip(o_refs, outs if isinstance(outs, tuple) else (outs,)):
            o_ref[rows, :] = out.astype(o_ref.dtype)


def _wres_matmul(name, acts, weights, extras, epilogue, *, pairs, n_out, out_dtype, tm,
                 tn=MM_TN, rc=MM_RC, vmem_mib=48, norm_gain=None):
    m = acts[0].shape[0]
    out_dtypes = out_dtype if isinstance(out_dtype, tuple) else (out_dtype,)
    out_widths = [(n_out, tn)] * len(out_dtypes)
    if norm_gain is not None:
        assert n_out == tn
        k0 = acts[0].shape[1]
        extras = list(extras) + [(norm_gain, (1, k0), lambda j, i: (0, 0))]
        out_dtypes = out_dtypes + (BF16,)
        out_widths = out_widths + [(k0, k0)]
    in_specs = [pl.BlockSpec((tm, a.shape[1]), lambda j, i: (i, 0)) for a in acts]
    assert all(off % 128 == 0 for _, off in weights) and tn % 128 == 0
    w_mode = dict(pipeline_mode=pl.Buffered(1)) if n_out == tn else {}
    in_specs += [pl.BlockSpec((pl.Element(w.shape[0]), pl.Element(tn)),
                              lambda j, i, off=off: (0, pl.multiple_of(off + j * tn, 128)), **w_mode)
                 for w, off in weights]
    in_specs += [pl.BlockSpec(blk, imap) for _, blk, imap in extras]
    kern = functools.partial(_wres_kernel, pairs=tuple(pairs), n_act=len(acts), n_extra=len(extras),
                             n_out=len(out_dtypes), rc=rc, epilogue=epilogue,
                             norm_first=norm_gain is not None)
    outs = pl.pallas_call(
        kern,
        grid=(n_out // tn, m // tm),
        in_specs=in_specs,
        out_specs=[pl.BlockSpec((tm, blk), lambda j, i: (i, j)) for _, blk in out_widths],
        out_shape=[jax.ShapeDtypeStruct((m, width), dt) for (width, _), dt in zip(out_widths, out_dtypes)],
        scratch_shapes=[pltpu.VMEM((w.shape[0], tn), BF16) for w, _ in weights],
        compiler_params=_cparams(2, vmem_mib),
        name=name,
    )(*acts, *(w for w, _ in weights), *(e for e, _, _ in extras))
    return outs if len(outs) > 1 else outs[0]


def _head_norm_rope(x, g, cos, sin, scale):
    ms = jnp.mean(x * x, axis=-1, keepdims=True)
    y = x * lax.rsqrt(ms + EPS) * g
    lane = lax.broadcasted_iota(jnp.int32, y.shape, 1)
    swapped = jnp.where((lane % 64) < 32, pltpu.roll(y, 96, 1), pltpu.roll(y, 32, 1))
    out = y * cos + swapped * sin
    return out * scale if scale != 1.0 else out


def _ep_cast(accs, extras, rows):
    return accs[0]


def _ep_heads(accs, extras, rows, *, n_heads, scale):
    g_ref, cos_ref, sin_ref = extras
    acc = accs[0]
    cos, sin = cos_ref[rows, :], sin_ref[rows, :]
    parts = [_head_norm_rope(acc[:, hh * HEAD_DIM:(hh + 1) * HEAD_DIM], g_ref[...], cos, sin, scale)
             for hh in range(n_heads)]
    if n_heads * HEAD_DIM < acc.shape[1]:
        parts.append(acc[:, n_heads * HEAD_DIM:])
    return jnp.concatenate(parts, axis=1)


def _ep_gate(accs, extras, rows):
    return jax.nn.sigmoid(accs[0] + extras[0][...])


IN_TM = 2048
IN_TN = 1024
LOG2E = math.log2(math.e)


def _inproj(x, norm_g, w_in, b_gate, q_g, k_g, cos, sin):
    tm, tn = IN_TM, IN_TN
    head = lambda g: (g, (1, HEAD_DIM), lambda j, i: (0, 0))
    pos = lambda t, rows: (t, (rows, HEAD_DIM), lambda j, i: (i % (SEQ // rows), 0))
    common = dict(pairs=[0], out_dtype=BF16, tm=tm, vmem_mib=56)
    q, h = _wres_matmul("inproj_q", [x], [(w_in, COL_Q)], [head(q_g), pos(cos, tm // 2), pos(sin, tm // 2)],
                        functools.partial(_ep_heads, n_heads=tn // HEAD_DIM, scale=HEAD_DIM ** -0.5 * LOG2E),
                        n_out=ATTN_W, tn=tn, norm_gain=norm_g, **{**common, "tm": tm // 2})
    u_h = _wres_matmul("inproj_hyena", [h], [(w_in, 0)], [], _ep_cast,
                       n_out=3 * HYENA_W, tn=tn, **common)
    kv = _wres_matmul("inproj_kv", [h], [(w_in, COL_K)], [head(k_g), pos(cos, tm), pos(sin, tm)],
                      functools.partial(_ep_heads, n_heads=N_KV_HEADS, scale=1.0),
                      n_out=2 * KV_W, tn=2 * KV_W, **common)
    gates = _wres_matmul("inproj_gate", [h], [(w_in, COL_G)],
                         [(b_gate, (1, tn), lambda j, i: (0, j))], _ep_gate,
                         n_out=2 * D_MODEL, tn=tn, **common)
    return u_h, q, kv, gates


HY_CT = 256
HY_RC = 256


def _hy_pre_kernel(x0_ref, x1_ref, v_ref, w0_ref, w1_ref, wv_ref, b0_ref, b1_ref, bv_ref,
                   z_ref, x0c_ref, pad_ref):
    L = x0_ref.shape[0]
    ct = x0_ref.shape[1]
    nchunk = L // HY_RC
    zeros8 = jnp.zeros((8, ct), F32)
    for a, src in enumerate((x0_ref, x1_ref, v_ref)):
        pad_ref[a, 0:8, :] = zeros8
        pad_ref[a, L + 8:L + 16, :] = zeros8

        def fill(r, carry, a=a, src=src):
            base = pl.multiple_of(r * HY_RC, HY_RC)
            pad_ref[a, pl.ds(base + 8, HY_RC), :] = src[pl.ds(base, HY_RC), :].astype(F32)
            return carry

        lax.fori_loop(0, nchunk, fill, 0)

    def conv(a, w_ref, b_ref, base):
        e = pad_ref[a, pl.ds(base, HY_RC + 16), :]
        up = pltpu.roll(e, 1, 0)[8:8 + HY_RC]
        mid = e[8:8 + HY_RC]
        dn = pltpu.roll(e, HY_RC + 15, 0)[8:8 + HY_RC]
        return b_ref[...] + up * w_ref[0:1, :] + mid * w_ref[1:2, :] + dn * w_ref[2:3, :]

    def body(r, carry):
        base = pl.multiple_of(r * HY_RC, HY_RC)
        x0c = conv(0, w0_ref, b0_ref, base)
        x1c = conv(1, w1_ref, b1_ref, base)
        vc = conv(2, wv_ref, bv_ref, base)
        z_ref[pl.ds(base, HY_RC), :] = vc * x1c
        x0c_ref[pl.ds(base, HY_RC), :] = x0c.astype(BF16)
        return carry

    lax.fori_loop(0, nchunk, body, 0)


def _hy_pre(proj3, conv_w, conv_b):
    b, L, _ = proj3.shape
    ct = HY_CT
    nct = HYENA_W // ct
    in_specs = []
    for grp in range(3):
        in_specs.append(pl.BlockSpec((None, L, ct), lambda bi, c, grp=grp: (bi, 0, c + grp * nct)))
    for grp in range(3):
        in_specs.append(pl.BlockSpec((SHORT_TAPS, ct), lambda bi, c, grp=grp: (0, c + grp * nct)))
    for grp in range(3):
        in_specs.append(pl.BlockSpec((1, ct), lambda bi, c, grp=grp: (0, c + grp * nct)))
    out_spec = pl.BlockSpec((None, L, ct), lambda bi, c: (bi, 0, c))
    return pl.pallas_call(
        _hy_pre_kernel,
        grid=(b, nct),
        in_specs=in_specs,
        out_specs=[out_spec, out_spec],
        out_shape=[jax.ShapeDtypeStruct((b, L, HYENA_W), F32),
                   jax.ShapeDtypeStruct((b, L, HYENA_W), BF16)],
        scratch_shapes=[pltpu.VMEM((3, L + 16, ct), F32)],
        compiler_params=_cparams(2, 48),
        name="hyena_pre",
    )(proj3, proj3, proj3, conv_w, conv_w, conv_w, conv_b, conv_b, conv_b)


CV_CT = 256
CV_F1B = 16
CV_NF = FFT_N1 // CV_F1B
CV_SLAB = 2 * DFT_R
CV_NSLAB = FFT_N2 // CV_SLAB
CV_HALF = FFT_N1 // 2


def _outer_fwd_slab(src_ref, k_ref, a_ref, j, rows_in):
    ct = a_ref.shape[-1]
    halves = []
    for h in range(2):
        r = src_ref[:, :, 2 * j + h] if src_ref.ndim == 5 else src_ref[:, 2 * j + h]
        r = r.reshape(rows_in, ct).astype(BF16)
        o = jnp.dot(k_ref[...], r, preferred_element_type=F32)
        halves.append(o.reshape(2 * FFT_N1, DFT_R, ct))
    slab = jnp.concatenate(halves, axis=1).astype(BF16)
    a_ref[:, :, pl.ds(pl.multiple_of(j * CV_SLAB, CV_SLAB), CV_SLAB), :] = slab.reshape(
        2, FFT_N1, CV_SLAB, ct)


def _hy_conv_kernel(z_ref, x0_ref, bias_ref, kf_ref, ki_ref, fw_ref, iv_ref, spec_ref, o_ref, a_ref):
    s = pl.program_id(2)
    ct = a_ref.shape[-1]

    @pl.when(s == 0)
    def _():
        def body(j, carry):
            _outer_fwd_slab(z_ref, kf_ref, a_ref, j, 2 * CV_HALF * DFT_R)
            return carry

        lax.fori_loop(0, CV_NSLAB, body, 0)

    @pl.when((s >= 1) & (s <= CV_NF))
    def _():
        f0 = (s - 1) * CV_F1B
        for fl in range(CV_F1B):
            x = a_ref[:, f0 + fl].reshape(2 * FFT_N2, ct)
            u = jnp.dot(fw_ref[fl], x, preferred_element_type=F32)
            ure, uim = u[:FFT_N2], u[FFT_N2:]
            kre, kim = spec_ref[0, fl].astype(F32), spec_ref[1, fl].astype(F32)
            p = jnp.concatenate([ure * kre - uim * kim, ure * kim + uim * kre], axis=0).astype(BF16)
            y = jnp.dot(iv_ref[fl], p, preferred_element_type=F32)
            a_ref[:, f0 + fl] = y.reshape(2, FFT_N2, ct).astype(BF16)

    @pl.when(s == CV_NF + 1)
    def _():
        bias = bias_ref[...]

        def body(j, carry):
            slab = a_ref[:, :, pl.ds(pl.multiple_of(j * CV_SLAB, CV_SLAB), CV_SLAB), :].astype(F32)
            x0 = x0_ref[:, :, j].astype(F32)
            halves = []
            for h in range(2):
                r = slab[:, :, h * DFT_R:(h + 1) * DFT_R, :].reshape(2 * FFT_N1 * DFT_R, ct).astype(BF16)
                y = jnp.dot(ki_ref[...], r, preferred_element_type=F32).reshape(2, CV_HALF, DFT_R, ct)
                z = z_ref[:, :, 2 * j + h]
                halves.append((y + bias * z) * x0[:, :, h * DFT_R:(h + 1) * DFT_R, :])
            o_ref[:, :, j] = jnp.concatenate(halves, axis=2).astype(o_ref.dtype)
            return carry

        lax.fori_loop(0, CV_NSLAB, body, 0)


def _hy_conv(z, x0c, bias, spec, tabs):
    b, L, c = z.shape
    assert b % 2 == 0 and L * 2 == FFT_N
    ct = CV_CT
    z5 = z.reshape(b, CV_HALF, FFT_N2 // DFT_R, DFT_R, c)
    x5 = x0c.reshape(b, CV_HALF, CV_NSLAB, CV_SLAB, c)
    fidx = lambda s: jnp.clip(s - 1, 0, CV_NF - 1)
    once = pl.Buffered(1)
    tab_spec = pl.BlockSpec((CV_F1B, 2 * FFT_N2, 2 * FFT_N2), lambda p, cb, s: (fidx(s), 0, 0))
    io16 = pl.BlockSpec((2, CV_HALF, CV_NSLAB, CV_SLAB, ct), lambda p, cb, s: (p, 0, 0, 0, cb))
    out = pl.pallas_call(
        _hy_conv_kernel,
        grid=(b // 2, c // ct, CV_NF + 2),
        in_specs=[
            pl.BlockSpec((2, CV_HALF, FFT_N2 // DFT_R, DFT_R, ct), lambda p, cb, s: (p, 0, 0, 0, cb),
                         pipeline_mode=once),
            io16,
            pl.BlockSpec((1, ct), lambda p, cb, s: (0, cb)),
            pl.BlockSpec(tabs["k_fwd"].shape, lambda p, cb, s: (0, 0), pipeline_mode=once),
            pl.BlockSpec(tabs["k_inv"].shape, lambda p, cb, s: (0, 0), pipeline_mode=once),
            tab_spec, tab_spec,
            pl.BlockSpec((2, CV_F1B, FFT_N2, ct), lambda p, cb, s: (0, fidx(s), 0, cb)),
        ],
        out_specs=io16,
        out_shape=jax.ShapeDtypeStruct(x5.shape, BF16),
        scratch_shapes=[pltpu.VMEM((2, FFT_N1, FFT_N2, ct), BF16)],
        compiler_params=_cparams(3, 56),
        name="hyena_conv",
    )(z5, x5, bias, tabs["k_fwd"], tabs["k_inv"], tabs["fwd2"], tabs["inv2"], spec)
    return out.reshape(b, L, c)


def _filt_spec_kernel(t_ref, l1_ref, kf_ref, fw_ref, o_ref, a_ref):
    s = pl.program_id(1)
    ct = a_ref.shape[-1]

    @pl.when(s == 0)
    def _():
        def body(j, carry):
            _outer_fwd_slab(t_ref, kf_ref, a_ref, j, FFT_N1 * DFT_R)
            return carry

        lax.fori_loop(0, CV_NSLAB, body, 0)

    @pl.when(s >= 1)
    def _():
        inv_l1 = 1.0 / l1_ref[...]
        f0 = (s - 1) * FS_F1B
        for fl in range(FS_F1B):
            x = a_ref[:, f0 + fl].reshape(2 * FFT_N2, ct)
            y = jnp.dot(fw_ref[fl], x, preferred_element_type=F32) * inv_l1
            o_ref[:, fl] = y.reshape(2, FFT_N2, ct).astype(o_ref.dtype)


FS_F1B = 16
FS_NF = FFT_N1 // FS_F1B


def _filt_spec(taps, l1, tabs):
    n, c = taps.shape
    ct = CV_CT
    t4 = taps.reshape(FFT_N1, FFT_N2 // DFT_R, DFT_R, c)
    fidx = lambda s: jnp.clip(s - 1, 0, FS_NF - 1)
    once = pl.Buffered(1)
    return pl.pallas_call(
        _filt_spec_kernel,
        grid=(c // ct, FS_NF + 1),
        in_specs=[
            pl.BlockSpec((FFT_N1, FFT_N2 // DFT_R, DFT_R, ct), lambda cb, s: (0, 0, 0, cb)),
            pl.BlockSpec((1, ct), lambda cb, s: (0, cb)),
            pl.BlockSpec(tabs["k_flt"].shape, lambda cb, s: (0, 0), pipeline_mode=once),
            pl.BlockSpec((FS_F1B, 2 * FFT_N2, 2 * FFT_N2), lambda cb, s: (fidx(s), 0, 0)),
        ],
        out_specs=pl.BlockSpec((2, FS_F1B, FFT_N2, ct), lambda cb, s: (0, fidx(s), 0, cb)),
        out_shape=jax.ShapeDtypeStruct((2, FFT_N1, FFT_N2, c), BF16),
        scratch_shapes=[pltpu.VMEM((2, FFT_N1, FFT_N2, ct), BF16)],
        compiler_params=_cparams(2, 48),
        name="filter_spectrum",
    )(t4, l1, tabs["k_flt"], tabs["fwd2"])


FLT_TR = 512


FLT_HALF = FLT_TR // 2


def _dot3(a, w):
    a_hi = a.astype(BF16)
    a_lo = (a - a_hi.astype(F32)).astype(BF16)
    w_hi = w.astype(BF16)
    w_lo = (w - w_hi.astype(F32)).astype(BF16)
    lhs = jnp.concatenate([a_hi, a_hi, a_lo], axis=1)
    rhs = jnp.concatenate([w_hi, w_lo, w_hi], axis=0)
    return jnp.dot(lhs, rhs, preferred_element_type=F32)


def _filter_kernel(emb_ref, w1_ref, b1_ref, w2_ref, b2_ref, w3_ref, b3_ref, w4t_ref, w4b_ref, fr_ref, dl_ref,
                   k_ref, l1_ref):
    r = pl.program_id(0)
    e = emb_ref[...]
    fr = fr_ref[...]
    h = jnp.sin(fr * (_dot3(e, w1_ref[...]) + b1_ref[...]))
    h = jnp.sin(fr * (_dot3(h, w2_ref[...]) + b2_ref[...]))
    h = jnp.sin(fr * (_dot3(h, w3_ref[...]) + b3_ref[...]))
    total = jnp.zeros(l1_ref.shape, F32)
    for part, w_ref in enumerate((w4t_ref, w4b_ref)):
        tcol = part * FILTER_HIDDEN
        taps = _dot3(h, w_ref[...])
        decay = jnp.exp(-e[:, tcol:tcol + 1] * dl_ref[...])
        rows = r * FLT_TR + part * FLT_HALF + lax.broadcasted_iota(jnp.int32, taps.shape, 0)
        taps = jnp.where(rows == SEQ, 0.0, taps * decay)
        k_ref[part * FLT_HALF:(part + 1) * FLT_HALF, :] = taps
        total = total + jnp.sum(jnp.abs(taps), axis=0, keepdims=True)

    @pl.when(r == 0)
    def _():
        l1_ref[...] = jnp.zeros_like(l1_ref)

    l1_ref[...] += total


def _filter_taps(emb2, w1, b1, w2, b2, w3, b3, w4, freq, deltas):
    n = 2 * emb2.shape[0]
    fh = FILTER_HIDDEN
    eye2 = jnp.eye(2, dtype=F32)
    w1p = jnp.concatenate([w1, jnp.zeros((fh - w1.shape[0], fh), F32)], axis=0)
    wd = [jnp.kron(eye2, w) for w in (w1p, w2, w3)]
    bd = [jnp.tile(b, (1, 2)) for b in (b1, b2, b3, freq)]
    zeros = jnp.zeros_like(w4)
    w4t = jnp.concatenate([w4, zeros], axis=0)
    w4b = jnp.concatenate([zeros, w4], axis=0)
    small = lambda shape: pl.BlockSpec(shape, lambda r: (0, 0))
    fwd_tiles = SEQ // FLT_TR
    w4_spec = pl.BlockSpec((2 * fh, HYENA_W), lambda r: (0, r // fwd_tiles))
    return pl.pallas_call(
        _filter_kernel,
        grid=(n // FLT_TR,),
        in_specs=[pl.BlockSpec((FLT_HALF, 2 * fh), lambda r: (r, 0)),
                  small((2 * fh, 2 * fh)), small((1, 2 * fh)), small((2 * fh, 2 * fh)), small((1, 2 * fh)),
                  small((2 * fh, 2 * fh)), small((1, 2 * fh)),
                  w4_spec, w4_spec,
                  small((1, 2 * fh)), small((1, HYENA_W))],
        out_specs=[pl.BlockSpec((FLT_TR, HYENA_W), lambda r: (r, 0)),
                   pl.BlockSpec((1, HYENA_W), lambda r: (0, 0))],
        out_shape=[jax.ShapeDtypeStruct((n, HYENA_W), F32),
                   jax.ShapeDtypeStruct((1, HYENA_W), F32)],
        compiler_params=_cparams(1, 32),
        name="filter_taps",
    )(emb2, wd[0], bd[0], wd[1], bd[1], wd[2], bd[2], w4t, w4b, bd[3], deltas)


AT_TQ = 512
AT_RQ = 256
AT_TK = 512


def _attn_kernel(q_ref, k_ref, v_ref, o_ref, s0_ref, s1_ref):
    s_refs = (s0_ref, s1_ref)
    nblk = k_ref.shape[0] // AT_TK
    units = [(pl.ds(r * AT_RQ, AT_RQ), slice(g * HEAD_DIM, (g + 1) * HEAD_DIM))
             for r in range(q_ref.shape[0] // AT_RQ) for g in range(Q_PER_KV)]
    lane_blocks = lambda a: [a[:, i:i + 128] for i in range(0, a.shape[1], 128)]

    def scores(u, j, m_run):
        rows, lanes = units[u]
        ks = pl.ds(j * AT_TK, AT_TK)
        s = lax.dot_general(q_ref[rows, lanes], k_ref[ks, :], (((1,), (1,)), ((), ())),
                            preferred_element_type=F32)
        s_refs[u % 2][:, ks] = s
        blk = functools.reduce(jnp.maximum, lane_blocks(s))
        return blk if m_run is None else jnp.maximum(m_run, blk)

    def weighted(u, j, m, l_run, acc):
        ks = pl.ds(j * AT_TK, AT_TK)
        p = jnp.exp2(s_refs[u % 2][:, ks] - m)
        l_blk = functools.reduce(jnp.add, lane_blocks(p))
        o = jnp.dot(p.astype(BF16), v_ref[ks, :], preferred_element_type=F32)
        return (l_blk if l_run is None else l_run + l_blk), (o if acc is None else acc + o)

    m_run = None
    for j in range(nblk):
        m_run = scores(0, j, m_run)
    for u, (rows, lanes) in enumerate(units):
        m = jnp.max(m_run, axis=-1, keepdims=True)
        m_run, l_run, acc = None, None, None
        for j in range(nblk):
            if u + 1 < len(units):
                m_run = scores(u + 1, j, m_run)
            l_run, acc = weighted(u, j, m, l_run, acc)
        l = jnp.sum(l_run, axis=-1, keepdims=True)
        o_ref[rows, lanes] = (acc / l).astype(o_ref.dtype)


def _attention(q3, kv3):
    b, s, _ = q3.shape
    gw = Q_PER_KV * HEAD_DIM
    return pl.pallas_call(
        _attn_kernel,
        grid=(b, N_KV_HEADS, s // AT_TQ),
        in_specs=[pl.BlockSpec((None, AT_TQ, gw), lambda bi, kv, qi: (bi, qi, kv)),
                  pl.BlockSpec((None, s, HEAD_DIM), lambda bi, kv, qi: (bi, 0, kv)),
                  pl.BlockSpec((None, s, HEAD_DIM), lambda bi, kv, qi: (bi, 0, N_KV_HEADS + kv))],
        out_specs=pl.BlockSpec((None, AT_TQ, gw), lambda bi, kv, qi: (bi, qi, kv)),
        out_shape=jax.ShapeDtypeStruct((b, s, ATTN_W), BF16),
        scratch_shapes=[pltpu.VMEM((AT_RQ, s), F32), pltpu.VMEM((AT_RQ, s), F32)],
        compiler_params=_cparams(3, 48),
        name="attention",
    )(q3, kv3, kv3)


def _ep_merge(accs, extras, rows):
    gh_ref, ga_ref = extras
    return gh_ref[rows, :].astype(F32) * accs[0] + ga_ref[rows, :].astype(F32) * accs[1]


def _ep_residual(accs, extras, rows):
    return extras[0][rows, :] + accs[0]


def _ep_swiglu(accs, extras, rows):
    return jax.nn.silu(accs[0]) * accs[1]


def _ep_residual_norm(accs, extras, rows):
    x_ref, g_ref = extras
    x1 = x_ref[rows, :] + accs[0]
    ms = jnp.mean(x1 * x1, axis=-1, keepdims=True)
    return x1, x1 * lax.rsqrt(ms + EPS) * g_ref[...]


def _out_proj(merged, x, w, g, tm=512):
    n = w.shape[1]
    return _wres_matmul("out_proj", [merged], [(w, 0)],
                        [(x, (tm, n), lambda j, i: (i, 0)), (g, (1, n), lambda j, i: (0, 0))],
                        _ep_residual_norm, pairs=[0], n_out=n, out_dtype=(F32, BF16), tm=tm, tn=n,
                        vmem_mib=56)


def _merge(yh, ya, gates, w_h, w_a, tm=1024):
    tn = 1024
    nj = D_MODEL // tn
    return _wres_matmul("merge", [yh, ya], [(w_h, 0), (w_a, 0)],
                        [(gates, (tm, tn), lambda j, i: (i, j)),
                         (gates, (tm, tn), lambda j, i: (i, nj + j))],
                        _ep_merge, pairs=[0, 1], n_out=D_MODEL, out_dtype=BF16, tm=tm, tn=tn)


def _resmm(name, a, x, w, tm, vmem_mib):
    tn = MM_TN
    return _wres_matmul(name, [a], [(w, 0)], [(x, (tm, tn), lambda j, i: (i, j))], _ep_residual,
                        pairs=[0], n_out=w.shape[1], out_dtype=F32, tm=tm, vmem_mib=vmem_mib)


def _swiglu(h, w_g, w_u, tm=2048):
    return _wres_matmul("swiglu", [h], [(w_g, 0), (w_u, 0)], [], _ep_swiglu,
                        pairs=[0, 0], n_out=w_g.shape[1], out_dtype=BF16, tm=tm, vmem_mib=56)


def _layer(x, mix_norm_g, w_in, b_gate, hy_conv_w, hy_conv_b,
           flt_w1, flt_b1, flt_w2, flt_b2, flt_w3, flt_b3, flt_w4, flt_freq, hy_bias,
           q_norm_g, k_norm_g, w_br_hyena, w_br_attn, w_out,
           ffn_norm_g, w_ffn_gate, w_ffn_up, w_ffn_down):
    b, s, d = x.shape
    m = b * s
    row = lambda a: a.reshape(1, -1)
    tabs = _dft_tables()
    cos, sin = (jnp.asarray(t) for t in _rope_tables_np())
    xm = x.reshape(m, d)

    u_h, q, kv, gates = _inproj(xm, row(mix_norm_g), w_in, row(b_gate), row(q_norm_g), row(k_norm_g),
                                cos, sin)

    emb2, deltas = (jnp.asarray(t) for t in _filter_tables_np())
    taps, l1 = _filter_taps(emb2, flt_w1, row(flt_b1), flt_w2, row(flt_b2), flt_w3, row(flt_b3),
                            flt_w4, row(flt_freq), deltas)
    spec = _filt_spec(taps, l1, tabs)

    z, x0c = _hy_pre(u_h.reshape(b, s, 3 * HYENA_W), hy_conv_w, row(hy_conv_b))
    y_h = _hy_conv(z, x0c, row(hy_bias), spec, tabs).reshape(m, HYENA_W)

    y_a = _attention(q.reshape(b, s, ATTN_W), kv.reshape(b, s, 2 * KV_W)).reshape(m, ATTN_W)

    merged = _merge(y_h, y_a, gates, w_br_hyena, w_br_attn)
    x1, h2 = _out_proj(merged, xm, w_out, row(ffn_norm_g))
    act = _swiglu(h2, w_ffn_gate, w_ffn_up)
    out = _resmm("ffn_down", act, x1, w_ffn_down, tm=512, vmem_mib=56)
    return out.reshape(b, s, d)


def kernel(x, mix_norm_g, w_in, b_gate, hy_conv_w, hy_conv_b, flt_w1, flt_b1, flt_w2, flt_b2, flt_w3, flt_b3, flt_w4, flt_freq, hy_bias, q_norm_g, k_norm_g, w_br_hyena, w_br_attn, w_out, ffn_norm_g, w_ffn_gate, w_ffn_up, w_ffn_down):
    params = (mix_norm_g, w_in, b_gate, hy_conv_w, hy_conv_b, flt_w1, flt_b1, flt_w2, flt_b2,
              flt_w3, flt_b3, flt_w4, flt_freq, hy_bias, q_norm_g, k_norm_g, w_br_hyena, w_br_attn,
              w_out, ffn_norm_g, w_ffn_gate, w_ffn_up, w_ffn_down)
    for l in range(mix_norm_g.shape[0]):
        x = _layer(x, *(p[l] for p in params))
    return x
```

```python
import functools
import math

import numpy as np
import jax
import jax.numpy as jnp
from jax import lax
from jax.experimental import pallas as pl
from jax.experimental.pallas import tpu as pltpu

F32 = jnp.float32
BF16 = jnp.bfloat16

D_MODEL = 2048
SEQ = 4096
GRID_W = 64
HEAD_DIM = 128
N_Q_HEADS = 8
N_KV_HEADS = 2
Q_PER_KV = N_Q_HEADS // N_KV_HEADS
ATTN_W = N_Q_HEADS * HEAD_DIM
KV_W = N_KV_HEADS * HEAD_DIM
ROPE_THETA = 10000.0
HYENA_W = D_MODEL - ATTN_W
SHORT_TAPS = 3
FILTER_EMB = 33
FILTER_HIDDEN = 64
DECAY_TARGET = 1e-2
FAST_DECAY_PCT = 0.3
SLOW_DECAY_PCT = 1.5
IN_W = 3 * HYENA_W + ATTN_W + 2 * KV_W + 2 * D_MODEL
D_FF = 5632
EPS = 1e-6

COL_Q = 3 * HYENA_W
COL_K = COL_Q + ATTN_W
COL_V = COL_K + KV_W
COL_G = COL_V + KV_W

FFT_N = 2 * SEQ
FFT_N1 = 64
FFT_N2 = 128

MIB = 1024 * 1024


def _cparams(n_axes, vmem_mib):
    return pltpu.CompilerParams(
        dimension_semantics=("arbitrary",) * n_axes,
        vmem_limit_bytes=vmem_mib * MIB,
    )


DFT_R = 8


@functools.lru_cache(maxsize=None)
def _dft_tables_np():
    n, n1, n2, r = FFT_N, FFT_N1, FFT_N2, DFT_R
    eye = np.eye(r)
    f1 = np.arange(n1)
    ang1 = 2.0 * np.pi * ((f1[:, None] * f1[None, :]) % n1) / n1
    c1, s1 = np.cos(ang1), np.sin(ang1)
    h = n1 // 2
    m_fwd = np.block([[c1[:, :h], s1[:, :h]], [-s1[:, :h], c1[:, :h]]])
    m_flt = np.concatenate([c1, -s1], axis=0)
    ct, st = c1[:h, :], s1[:h, :]
    m_inv = np.block([[ct, -st], [st, ct]]) / n
    s2 = np.arange(n2)
    f = f1[:, None, None] + n1 * s2[None, :, None]
    th = 2.0 * np.pi * ((f * s2[None, None, :]) % n) / n
    c, s = np.cos(th), np.sin(th)
    fwd2 = np.concatenate(
        [np.concatenate([c, s], axis=2), np.concatenate([-s, c], axis=2)], axis=1)
    c_t, s_t = np.transpose(c, (0, 2, 1)), np.transpose(s, (0, 2, 1))
    inv2 = np.concatenate(
        [np.concatenate([c_t, -s_t], axis=2), np.concatenate([s_t, c_t], axis=2)], axis=1)
    f32 = lambda a: np.ascontiguousarray(a, dtype=np.float32)
    return dict(k_fwd=f32(np.kron(m_fwd, eye)), k_flt=f32(np.kron(m_flt, eye)),
                k_inv=f32(np.kron(m_inv, eye)), fwd2=f32(fwd2), inv2=f32(inv2))


def _dft_tables():
    return {k: jnp.asarray(v).astype(BF16) for k, v in _dft_tables_np().items()}


@functools.lru_cache(maxsize=None)
def _rope_tables_np():
    half = HEAD_DIM // 2
    inv = ROPE_THETA ** (-np.arange(0, half, 2, dtype=np.float64) / half)
    pos = np.arange(SEQ)
    ang_r = (pos // GRID_W)[:, None] * inv[None, :]
    ang_c = (pos % GRID_W)[:, None] * inv[None, :]
    cos = np.concatenate([np.cos(ang_r)] * 2 + [np.cos(ang_c)] * 2, axis=-1)
    sin = np.concatenate([-np.sin(ang_r), np.sin(ang_r), -np.sin(ang_c), np.sin(ang_c)], axis=-1)
    return cos.astype(np.float32), sin.astype(np.float32)


@functools.lru_cache(maxsize=None)
def _filter_tables_np():
    L = SEQ
    bands = (FILTER_EMB - 1) // 2
    pos = np.concatenate([np.arange(L, dtype=np.float64), L - np.arange(L, dtype=np.float64)])
    t = pos / max(L - 1, 1)
    fb = np.linspace(1e-4, bands - 1, bands)
    ang = (2.0 * math.pi * pos / L)[:, None] * fb[None, :]
    emb = np.concatenate([t[:, None], np.cos(ang), -np.sin(ang),
                          np.zeros((2 * L, FILTER_HIDDEN - FILTER_EMB))], axis=-1)
    max_decay = math.log(DECAY_TARGET) / FAST_DECAY_PCT
    min_decay = math.log(DECAY_TARGET) / SLOW_DECAY_PCT
    deltas = np.abs(np.linspace(min_decay, max_decay, HYENA_W))
    tile, hid = FLT_TR, FILTER_HIDDEN
    emb2 = emb.reshape(2 * L // tile, 2, tile // 2, hid).transpose(0, 2, 1, 3).reshape(L, 2 * hid)
    return np.ascontiguousarray(emb2, dtype=np.float32), deltas[None, :].astype(np.float32)


MM_TN = 512
MM_RC = 256


def _wres_kernel(*refs, pairs, n_act, n_extra, n_out, n_scratch, rc, epilogue, norm_first, side_cast):
    n_w = len(pairs)
    acts = refs[:n_act]
    ws = refs[n_act:n_act + n_w]
    extras = refs[n_act + n_w:n_act + n_w + n_extra]
    o_refs = refs[n_act + n_w + n_extra:n_act + n_w + n_extra + n_out]
    wbs = refs[n_act + n_w + n_extra + n_out:]
    if side_cast:
        o_refs[-1][...] = extras[-1][...].astype(BF16)
        extras, o_refs = extras[:-1], o_refs[:-1]
    if norm_first:
        g_ref, extras = extras[-1], extras[:-1]
        h_ref, o_refs = o_refs[-1], o_refs[:-1]

    if n_scratch:
        @pl.when(pl.program_id(1) == 0)
        def _():
            for w_ref, wb_ref in zip(ws, wbs):
                wb_ref[...] = w_ref[...].astype(BF16)
    else:
        wbs = ws

    for c in range(o_refs[0].shape[0] // rc):
        rows = pl.ds(c * rc, rc)
        lhs = [a[rows, :] for a in acts]
        if norm_first:
            x = lhs[0]
            ms = jnp.mean(x * x, axis=-1, keepdims=True)
            lhs[0] = (x * lax.rsqrt(ms + EPS) * g_ref[...]).astype(BF16)
            h_ref[rows, :] = lhs[0]
        accs = [jnp.dot(lhs[a], wb_ref[...], preferred_element_type=F32)
                for a, wb_ref in zip(pairs, wbs)]
        outs = epilogue(accs, extras, rows)
        for o_ref, out in zip(o_refs, outs if isinstance(outs, tuple) else (outs,)):
            o_ref[rows, :] = out.astype(o_ref.dtype)


def _wres_matmul(name, acts, weights, extras, epilogue, *, pairs, n_out, out_dtype, tm,
                 tn=MM_TN, rc=MM_RC, vmem_mib=48, norm_gain=None, side_cast=None):
    m = acts[0].shape[0]
    grid = (n_out // tn, m // tm)
    out_dtypes = out_dtype if isinstance(out_dtype, tuple) else (out_dtype,)
    out_specs = [pl.BlockSpec((tm, tn), lambda j, i: (i, j)) for _ in out_dtypes]
    out_shapes = [jax.ShapeDtypeStruct((m, n_out), dt) for dt in out_dtypes]
    extras = list(extras)
    if norm_gain is not None:
        assert n_out == tn
        k0 = acts[0].shape[1]
        extras.append((norm_gain, (1, k0), lambda j, i: (0, 0)))
        out_specs.append(pl.BlockSpec((tm, k0), lambda j, i: (i, 0)))
        out_shapes.append(jax.ShapeDtypeStruct((m, k0), BF16))
    if side_cast is not None:
        slab = side_cast.shape[0] // (grid[0] * grid[1])
        assert slab * grid[0] * grid[1] == side_cast.shape[0] and slab % 16 == 0
        step = lambda j, i: (j * grid[1] + i, 0)
        extras.append((side_cast, (slab, side_cast.shape[1]), step))
        out_specs.append(pl.BlockSpec((slab, side_cast.shape[1]), step))
        out_shapes.append(jax.ShapeDtypeStruct(side_cast.shape, BF16))
    in_specs = [pl.BlockSpec((tm, a.shape[1]), lambda j, i: (i, 0)) for a in acts]
    assert all(off % 128 == 0 for _, off in weights) and tn % 128 == 0
    w_mode = dict(pipeline_mode=pl.Buffered(1)) if n_out == tn else {}
    in_specs += [pl.BlockSpec((pl.Element(w.shape[0]), pl.Element(tn)),
                              lambda j, i, off=off: (0, pl.multiple_of(off + j * tn, 128)), **w_mode)
                 for w, off in weights]
    in_specs += [pl.BlockSpec(blk, imap) for _, blk, imap in extras]
    scratch = [pltpu.VMEM((w.shape[0], tn), BF16) for w, _ in weights if w.dtype != BF16]
    assert len(scratch) in (0, len(weights))
    kern = functools.partial(_wres_kernel, pairs=tuple(pairs), n_act=len(acts), n_extra=len(extras),
                             n_out=len(out_specs), n_scratch=len(scratch), rc=rc, epilogue=epilogue,
                             norm_first=norm_gain is not None, side_cast=side_cast is not None)
    outs = pl.pallas_call(
        kern,
        grid=grid,
        in_specs=in_specs,
        out_specs=out_specs,
        out_shape=out_shapes,
        scratch_shapes=scratch,
        compiler_params=_cparams(2, vmem_mib),
        name=name,
    )(*acts, *(w for w, _ in weights), *(e for e, _, _ in extras))
    return outs if len(outs) > 1 else outs[0]


def _head_norm_rope(x, g, cos, sin, scale):
    ms = jnp.mean(x * x, axis=-1, keepdims=True)
    y = x * lax.rsqrt(ms + EPS) * g
    lane = lax.broadcasted_iota(jnp.int32, y.shape, 1)
    swapped = jnp.where((lane % 64) < 32, pltpu.roll(y, 96, 1), pltpu.roll(y, 32, 1))
    out = y * cos + swapped * sin
    return out * scale if scale != 1.0 else out


def _ep_cast(accs, extras, rows):
    return accs[0]


def _ep_heads(accs, extras, rows, *, n_heads, scale):
    g_ref, cos_ref, sin_ref = extras
    acc = accs[0]
    cos, sin = cos_ref[rows, :], sin_ref[rows, :]
    parts = [_head_norm_rope(acc[:, hh * HEAD_DIM:(hh + 1) * HEAD_DIM], g_ref[...], cos, sin, scale)
             for hh in range(n_heads)]
    if n_heads * HEAD_DIM < acc.shape[1]:
        parts.append(acc[:, n_heads * HEAD_DIM:])
    return jnp.concatenate(parts, axis=1)


def _ep_gate(accs, extras, rows):
    return jax.nn.sigmoid(accs[0] + extras[0][...])


IN_TM = 2048
IN_TN = 1024
LOG2E = math.log2(math.e)


def _inproj(x, norm_g, w_in, b_gate, q_g, k_g, cos, sin):
    tm, tn = IN_TM, IN_TN
    head = lambda g: (g, (1, HEAD_DIM), lambda j, i: (0, 0))
    pos = lambda t, rows: (t, (rows, HEAD_DIM), lambda j, i: (i % (SEQ // rows), 0))
    common = dict(pairs=[0], out_dtype=BF16, tm=tm, vmem_mib=56)
    q, h = _wres_matmul("inproj_q", [x], [(w_in, COL_Q)], [head(q_g), pos(cos, tm // 2), pos(sin, tm // 2)],
                        functools.partial(_ep_heads, n_heads=tn // HEAD_DIM, scale=HEAD_DIM ** -0.5 * LOG2E),
                        n_out=ATTN_W, tn=tn, norm_gain=norm_g, **{**common, "tm": tm // 2})
    u_h = _wres_matmul("inproj_hyena", [h], [(w_in, 0)], [], _ep_cast,
                       n_out=3 * HYENA_W, tn=tn, **common)
    kv = _wres_matmul("inproj_kv", [h], [(w_in, COL_K)], [head(k_g), pos(cos, tm), pos(sin, tm)],
                      functools.partial(_ep_heads, n_heads=N_KV_HEADS, scale=1.0),
                      n_out=2 * KV_W, tn=2 * KV_W, **common)
    gates = _wres_matmul("inproj_gate", [h], [(w_in, COL_G)],
                         [(b_gate, (1, tn), lambda j, i: (0, j))], _ep_gate,
                         n_out=2 * D_MODEL, tn=tn, **common)
    return u_h, q, kv, gates


HY_CT = 256
HY_RC = 256


def _hy_pre_kernel(x0_ref, x1_ref, v_ref, w0_ref, w1_ref, wv_ref, b0_ref, b1_ref, bv_ref,
                   z_ref, x0c_ref, pad_ref):
    L = x0_ref.shape[0]
    ct = x0_ref.shape[1]
    nchunk = L // HY_RC
    zeros8 = jnp.zeros((8, ct), F32)
    for a, src in enumerate((x0_ref, x1_ref, v_ref)):
        pad_ref[a, 0:8, :] = zeros8
        pad_ref[a, L + 8:L + 16, :] = zeros8

        def fill(r, carry, a=a, src=src):
            base = pl.multiple_of(r * HY_RC, HY_RC)
            pad_ref[a, pl.ds(base + 8, HY_RC), :] = src[pl.ds(base, HY_RC), :].astype(F32)
            return carry

        lax.fori_loop(0, nchunk, fill, 0)

    def conv(a, w_ref, b_ref, base):
        e = pad_ref[a, pl.ds(base, HY_RC + 16), :]
        up = pltpu.roll(e, 1, 0)[8:8 + HY_RC]
        mid = e[8:8 + HY_RC]
        dn = pltpu.roll(e, HY_RC + 15, 0)[8:8 + HY_RC]
        return b_ref[...] + up * w_ref[0:1, :] + mid * w_ref[1:2, :] + dn * w_ref[2:3, :]

    def body(r, carry):
        base = pl.multiple_of(r * HY_RC, HY_RC)
        x0c = conv(0, w0_ref, b0_ref, base)
        x1c = conv(1, w1_ref, b1_ref, base)
        vc = conv(2, wv_ref, bv_ref, base)
        z_ref[pl.ds(base, HY_RC), :] = vc * x1c
        x0c_ref[pl.ds(base, HY_RC), :] = x0c.astype(BF16)
        return carry

    lax.fori_loop(0, nchunk, body, 0)


def _hy_pre(proj3, conv_w, conv_b):
    b, L, _ = proj3.shape
    ct = HY_CT
    nct = HYENA_W // ct
    in_specs = []
    for grp in range(3):
        in_specs.append(pl.BlockSpec((None, L, ct), lambda bi, c, grp=grp: (bi, 0, c + grp * nct)))
    for grp in range(3):
        in_specs.append(pl.BlockSpec((SHORT_TAPS, ct), lambda bi, c, grp=grp: (0, c + grp * nct)))
    for grp in range(3):
        in_specs.append(pl.BlockSpec((1, ct), lambda bi, c, grp=grp: (0, c + grp * nct)))
    out_spec = pl.BlockSpec((None, L, ct), lambda bi, c: (bi, 0, c))
    return pl.pallas_call(
        _hy_pre_kernel,
        grid=(b, nct),
        in_specs=in_specs,
        out_specs=[out_spec, out_spec],
        out_shape=[jax.ShapeDtypeStruct((b, L, HYENA_W), F32),
                   jax.ShapeDtypeStruct((b, L, HYENA_W), BF16)],
        scratch_shapes=[pltpu.VMEM((3, L + 16, ct), F32)],
        compiler_params=_cparams(2, 48),
        name="hyena_pre",
    )(proj3, proj3, proj3, conv_w, conv_w, conv_w, conv_b, conv_b, conv_b)


CV_CT = 256
CV_F1B = 16
CV_NF = FFT_N1 // CV_F1B
CV_SLAB = 2 * DFT_R
CV_NSLAB = FFT_N2 // CV_SLAB
CV_HALF = FFT_N1 // 2


def _outer_fwd_slab(src_ref, k_ref, a_ref, j, rows_in):
    ct = a_ref.shape[-1]
    halves = []
    for h in range(2):
        r = src_ref[:, :, 2 * j + h] if src_ref.ndim == 5 else src_ref[:, 2 * j + h]
        r = r.reshape(rows_in, ct).astype(BF16)
        o = jnp.dot(k_ref[...], r, preferred_element_type=F32)
        halves.append(o.reshape(2 * FFT_N1, DFT_R, ct))
    slab = jnp.concatenate(halves, axis=1).astype(BF16)
    a_ref[:, :, pl.ds(pl.multiple_of(j * CV_SLAB, CV_SLAB), CV_SLAB), :] = slab.reshape(
        2, FFT_N1, CV_SLAB, ct)


def _hy_conv_kernel(z_ref, x0_ref, bias_ref, kf_ref, ki_ref, fw_ref, iv_ref, spec_ref, o_ref, a_ref):
    s = pl.program_id(2)
    ct = a_ref.shape[-1]

    @pl.when(s == 0)
    def _():
        def body(j, carry):
            _outer_fwd_slab(z_ref, kf_ref, a_ref, j, 2 * CV_HALF * DFT_R)
            return carry

        lax.fori_loop(0, CV_NSLAB, body, 0)

    @pl.when((s >= 1) & (s <= CV_NF))
    def _():
        f0 = (s - 1) * CV_F1B
        for fl in range(CV_F1B):
            x = a_ref[:, f0 + fl].reshape(2 * FFT_N2, ct)
            u = jnp.dot(fw_ref[fl], x, preferred_element_type=F32)
            ure, uim = u[:FFT_N2], u[FFT_N2:]
            kre, kim = spec_ref[0, fl].astype(F32), spec_ref[1, fl].astype(F32)
            p = jnp.concatenate([ure * kre - uim * kim, ure * kim + uim * kre], axis=0).astype(BF16)
            y = jnp.dot(iv_ref[fl], p, preferred_element_type=F32)
            a_ref[:, f0 + fl] = y.reshape(2, FFT_N2, ct).astype(BF16)

    @pl.when(s == CV_NF + 1)
    def _():
        bias = bias_ref[...]

        def body(j, carry):
            slab = a_ref[:, :, pl.ds(pl.multiple_of(j * CV_SLAB, CV_SLAB), CV_SLAB), :].astype(F32)
            x0 = x0_ref[:, :, j].astype(F32)
            halves = []
            for h in range(2):
                r = slab[:, :, h * DFT_R:(h + 1) * DFT_R, :].reshape(2 * FFT_N1 * DFT_R, ct).astype(BF16)
                y = jnp.dot(ki_ref[...], r, preferred_element_type=F32).reshape(2, CV_HALF, DFT_R, ct)
                z = z_ref[:, :, 2 * j + h]
                halves.append((y + bias * z) * x0[:, :, h * DFT_R:(h + 1) * DFT_R, :])
            o_ref[:, :, j] = jnp.concatenate(halves, axis=2).astype(o_ref.dtype)
            return carry

        lax.fori_loop(0, CV_NSLAB, body, 0)


def _hy_conv(z, x0c, bias, spec, tabs):
    b, L, c = z.shape
    assert b % 2 == 0 and L * 2 == FFT_N
    ct = CV_CT
    z5 = z.reshape(b, CV_HALF, FFT_N2 // DFT_R, DFT_R, c)
    x5 = x0c.reshape(b, CV_HALF, CV_NSLAB, CV_SLAB, c)
    fidx = lambda s: jnp.clip(s - 1, 0, CV_NF - 1)
    once = pl.Buffered(1)
    tab_spec = pl.BlockSpec((CV_F1B, 2 * FFT_N2, 2 * FFT_N2), lambda p, cb, s: (fidx(s), 0, 0))
    io16 = pl.BlockSpec((2, CV_HALF, CV_NSLAB, CV_SLAB, ct), lambda p, cb, s: (p, 0, 0, 0, cb))
    out = pl.pallas_call(
        _hy_conv_kernel,
        grid=(b // 2, c // ct, CV_NF + 2),
        in_specs=[
            pl.BlockSpec((2, CV_HALF, FFT_N2 // DFT_R, DFT_R, ct), lambda p, cb, s: (p, 0, 0, 0, cb),
                         pipeline_mode=once),
            io16,
            pl.BlockSpec((1, ct), lambda p, cb, s: (0, cb)),
            pl.BlockSpec(tabs["k_fwd"].shape, lambda p, cb, s: (0, 0), pipeline_mode=once),
            pl.BlockSpec(tabs["k_inv"].shape, lambda p, cb, s: (0, 0), pipeline_mode=once),
            tab_spec, tab_spec,
            pl.BlockSpec((2, CV_F1B, FFT_N2, ct), lambda p, cb, s: (0, fidx(s), 0, cb)),
        ],
        out_specs=io16,
        out_shape=jax.ShapeDtypeStruct(x5.shape, BF16),
        scratch_shapes=[pltpu.VMEM((2, FFT_N1, FFT_N2, ct), BF16)],
        compiler_params=_cparams(3, 56),
        name="hyena_conv",
    )(z5, x5, bias, tabs["k_fwd"], tabs["k_inv"], tabs["fwd2"], tabs["inv2"], spec)
    return out.reshape(b, L, c)


def _filt_spec_kernel(t_ref, l1_ref, kf_ref, fw_ref, o_ref, a_ref):
    s = pl.program_id(1)
    ct = a_ref.shape[-1]

    @pl.when(s == 0)
    def _():
        def body(j, carry):
            _outer_fwd_slab(t_ref, kf_ref, a_ref, j, FFT_N1 * DFT_R)
            return carry

        lax.fori_loop(0, CV_NSLAB, body, 0)

    @pl.when(s >= 1)
    def _():
        inv_l1 = 1.0 / l1_ref[...]
        f0 = (s - 1) * FS_F1B
        for fl in range(FS_F1B):
            x = a_ref[:, f0 + fl].reshape(2 * FFT_N2, ct)
            y = jnp.dot(fw_ref[fl], x, preferred_element_type=F32) * inv_l1
            o_ref[:, fl] = y.reshape(2, FFT_N2, ct).astype(o_ref.dtype)


FS_F1B = 16
FS_NF = FFT_N1 // FS_F1B


def _filt_spec(taps, l1, tabs):
    n, c = taps.shape
    ct = CV_CT
    t4 = taps.reshape(FFT_N1, FFT_N2 // DFT_R, DFT_R, c)
    fidx = lambda s: jnp.clip(s - 1, 0, FS_NF - 1)
    once = pl.Buffered(1)
    return pl.pallas_call(
        _filt_spec_kernel,
        grid=(c // ct, FS_NF + 1),
        in_specs=[
            pl.BlockSpec((FFT_N1, FFT_N2 // DFT_R, DFT_R, ct), lambda cb, s: (0, 0, 0, cb)),
            pl.BlockSpec((1, ct), lambda cb, s: (0, cb)),
            pl.BlockSpec(tabs["k_flt"].shape, lambda cb, s: (0, 0), pipeline_mode=once),
            pl.BlockSpec((FS_F1B, 2 * FFT_N2, 2 * FFT_N2), lambda cb, s: (fidx(s), 0, 0)),
        ],
        out_specs=pl.BlockSpec((2, FS_F1B, FFT_N2, ct), lambda cb, s: (0, fidx(s), 0, cb)),
        out_shape=jax.ShapeDtypeStruct((2, FFT_N1, FFT_N2, c), BF16),
        scratch_shapes=[pltpu.VMEM((2, FFT_N1, FFT_N2, ct), BF16)],
        compiler_params=_cparams(2, 48),
        name="filter_spectrum",
    )(t4, l1, tabs["k_flt"], tabs["fwd2"])


FLT_TR = 512


FLT_HALF = FLT_TR // 2


def _dot3(a, w):
    a_hi = a.astype(BF16)
    a_lo = (a - a_hi.astype(F32)).astype(BF16)
    w_hi = w.astype(BF16)
    w_lo = (w - w_hi.astype(F32)).astype(BF16)
    lhs = jnp.concatenate([a_hi, a_hi, a_lo], axis=1)
    rhs = jnp.concatenate([w_hi, w_lo, w_hi], axis=0)
    return jnp.dot(lhs, rhs, preferred_element_type=F32)


def _filter_kernel(emb_ref, w1_ref, b1_ref, w2_ref, b2_ref, w3_ref, b3_ref, w4t_ref, w4b_ref, fr_ref, dl_ref,
                   k_ref, l1_ref):
    r = pl.program_id(0)
    e = emb_ref[...]
    fr = fr_ref[...]
    h = jnp.sin(fr * (_dot3(e, w1_ref[...]) + b1_ref[...]))
    h = jnp.sin(fr * (_dot3(h, w2_ref[...]) + b2_ref[...]))
    h = jnp.sin(fr * (_dot3(h, w3_ref[...]) + b3_ref[...]))
    total = jnp.zeros(l1_ref.shape, F32)
    for part, w_ref in enumerate((w4t_ref, w4b_ref)):
        tcol = part * FILTER_HIDDEN
        taps = _dot3(h, w_ref[...])
        decay = jnp.exp(-e[:, tcol:tcol + 1] * dl_ref[...])
        rows = r * FLT_TR + part * FLT_HALF + lax.broadcasted_iota(jnp.int32, taps.shape, 0)
        taps = jnp.where(rows == SEQ, 0.0, taps * decay)
        k_ref[part * FLT_HALF:(part + 1) * FLT_HALF, :] = taps
        total = total + jnp.sum(jnp.abs(taps), axis=0, keepdims=True)

    @pl.when(r == 0)
    def _():
        l1_ref[...] = jnp.zeros_like(l1_ref)

    l1_ref[...] += total


def _filter_taps(emb2, w1, b1, w2, b2, w3, b3, w4, freq, deltas):
    n = 2 * emb2.shape[0]
    fh = FILTER_HIDDEN
    eye2 = jnp.eye(2, dtype=F32)
    w1p = jnp.concatenate([w1, jnp.zeros((fh - w1.shape[0], fh), F32)], axis=0)
    wd = [jnp.kron(eye2, w) for w in (w1p, w2, w3)]
    bd = [jnp.tile(b, (1, 2)) for b in (b1, b2, b3, freq)]
    zeros = jnp.zeros_like(w4)
    w4t = jnp.concatenate([w4, zeros], axis=0)
    w4b = jnp.concatenate([zeros, w4], axis=0)
    small = lambda shape: pl.BlockSpec(shape, lambda r: (0, 0))
    fwd_tiles = SEQ // FLT_TR
    w4_spec = pl.BlockSpec((2 * fh, HYENA_W), lambda r: (0, r // fwd_tiles))
    return pl.pallas_call(
        _filter_kernel,
        grid=(n // FLT_TR,),
        in_specs=[pl.BlockSpec((FLT_HALF, 2 * fh), lambda r: (r, 0)),
                  small((2 * fh, 2 * fh)), small((1, 2 * fh)), small((2 * fh, 2 * fh)), small((1, 2 * fh)),
                  small((2 * fh, 2 * fh)), small((1, 2 * fh)),
                  w4_spec, w4_spec,
                  small((1, 2 * fh)), small((1, HYENA_W))],
        out_specs=[pl.BlockSpec((FLT_TR, HYENA_W), lambda r: (r, 0)),
                   pl.BlockSpec((1, HYENA_W), lambda r: (0, 0))],
        out_shape=[jax.ShapeDtypeStruct((n, HYENA_W), F32),
                   jax.ShapeDtypeStruct((1, HYENA_W), F32)],
        compiler_params=_cparams(1, 32),
        name="filter_taps",
    )(emb2, wd[0], bd[0], wd[1], bd[1], wd[2], bd[2], w4t, w4b, bd[3], deltas)


AT_TQ = 512
AT_RQ = 256
AT_TK = 512


def _attn_kernel(q_ref, k_ref, v_ref, o_ref, s0_ref, s1_ref):
    s_refs = (s0_ref, s1_ref)
    nblk = k_ref.shape[0] // AT_TK
    units = [(pl.ds(r * AT_RQ, AT_RQ), slice(g * HEAD_DIM, (g + 1) * HEAD_DIM))
             for r in range(q_ref.shape[0] // AT_RQ) for g in range(Q_PER_KV)]
    lane_blocks = lambda a: [a[:, i:i + 128] for i in range(0, a.shape[1], 128)]

    def scores(u, j, m_run):
        rows, lanes = units[u]
        ks = pl.ds(j * AT_TK, AT_TK)
        s = lax.dot_general(q_ref[rows, lanes], k_ref[ks, :], (((1,), (1,)), ((), ())),
                            preferred_element_type=F32)
        s_refs[u % 2][:, ks] = s
        blk = functools.reduce(jnp.maximum, lane_blocks(s))
        return blk if m_run is None else jnp.maximum(m_run, blk)

    def weighted(u, j, m, l_run, acc):
        ks = pl.ds(j * AT_TK, AT_TK)
        p = jnp.exp2(s_refs[u % 2][:, ks] - m)
        l_blk = functools.reduce(jnp.add, lane_blocks(p))
        o = jnp.dot(p.astype(BF16), v_ref[ks, :], preferred_element_type=F32)
        return (l_blk if l_run is None else l_run + l_blk), (o if acc is None else acc + o)

    m_run = None
    for j in range(nblk):
        m_run = scores(0, j, m_run)
    for u, (rows, lanes) in enumerate(units):
        m = jnp.max(m_run, axis=-1, keepdims=True)
        m_run, l_run, acc = None, None, None
        for j in range(nblk):
            if u + 1 < len(units):
                m_run = scores(u + 1, j, m_run)
            l_run, acc = weighted(u, j, m, l_run, acc)
        l = jnp.sum(l_run, axis=-1, keepdims=True)
        o_ref[rows, lanes] = (acc / l).astype(o_ref.dtype)


def _attention(q3, kv3):
    b, s, _ = q3.shape
    gw = Q_PER_KV * HEAD_DIM
    return pl.pallas_call(
        _attn_kernel,
        grid=(b, N_KV_HEADS, s // AT_TQ),
        in_specs=[pl.BlockSpec((None, AT_TQ, gw), lambda bi, kv, qi: (bi, qi, kv)),
                  pl.BlockSpec((None, s, HEAD_DIM), lambda bi, kv, qi: (bi, 0, kv)),
                  pl.BlockSpec((None, s, HEAD_DIM), lambda bi, kv, qi: (bi, 0, N_KV_HEADS + kv))],
        out_specs=pl.BlockSpec((None, AT_TQ, gw), lambda bi, kv, qi: (bi, qi, kv)),
        out_shape=jax.ShapeDtypeStruct((b, s, ATTN_W), BF16),
        scratch_shapes=[pltpu.VMEM((AT_RQ, s), F32), pltpu.VMEM((AT_RQ, s), F32)],
        compiler_params=_cparams(3, 48),
        name="attention",
    )(q3, kv3, kv3)


def _ep_merge(accs, extras, rows):
    gh_ref, ga_ref = extras
    return gh_ref[rows, :].astype(F32) * accs[0] + ga_ref[rows, :].astype(F32) * accs[1]


def _ep_residual(accs, extras, rows):
    return extras[0][rows, :] + accs[0]


def _ep_swiglu(accs, extras, rows):
    return jax.nn.silu(accs[0]) * accs[1]


def _ep_residual_norm(accs, extras, rows):
    x_ref, g_ref = extras
    x1 = x_ref[rows, :] + accs[0]
    ms = jnp.mean(x1 * x1, axis=-1, keepdims=True)
    return x1, x1 * lax.rsqrt(ms + EPS) * g_ref[...]


def _out_proj(merged, x, w, g, tm=512):
    n = w.shape[1]
    return _wres_matmul("out_proj", [merged], [(w, 0)],
                        [(x, (tm, n), lambda j, i: (i, 0)), (g, (1, n), lambda j, i: (0, 0))],
                        _ep_residual_norm, pairs=[0], n_out=n, out_dtype=(F32, BF16), tm=tm, tn=n,
                        vmem_mib=56)


def _merge(yh, ya, gates, w_h, w_a, tm=1024):
    tn = 1024
    nj = D_MODEL // tn
    return _wres_matmul("merge", [yh, ya], [(w_h, 0), (w_a, 0)],
                        [(gates, (tm, tn), lambda j, i: (i, j)),
                         (gates, (tm, tn), lambda j, i: (i, nj + j))],
                        _ep_merge, pairs=[0, 1], n_out=D_MODEL, out_dtype=BF16, tm=tm, tn=tn)


def _ffn_down(a, x, w_bf16, tm=512, tn=1024):
    return _wres_matmul("ffn_down", [a], [(w_bf16, 0)], [(x, (tm, tn), lambda j, i: (i, j))], _ep_residual,
                        pairs=[0], n_out=w_bf16.shape[1], out_dtype=F32, tm=tm, tn=tn, vmem_mib=56)


def _swiglu(h, w_g, w_u, w_down, tm=2048):
    return _wres_matmul("swiglu", [h], [(w_g, 0), (w_u, 0)], [], _ep_swiglu,
                        pairs=[0, 0], n_out=w_g.shape[1], out_dtype=BF16, tm=tm, vmem_mib=56,
                        side_cast=w_down)


def _layer(x, mix_norm_g, w_in, b_gate, hy_conv_w, hy_conv_b,
           flt_w1, flt_b1, flt_w2, flt_b2, flt_w3, flt_b3, flt_w4, flt_freq, hy_bias,
           q_norm_g, k_norm_g, w_br_hyena, w_br_attn, w_out,
           ffn_norm_g, w_ffn_gate, w_ffn_up, w_ffn_down):
    b, s, d = x.shape
    m = b * s
    row = lambda a: a.reshape(1, -1)
    tabs = _dft_tables()
    cos, sin = (jnp.asarray(t) for t in _rope_tables_np())
    xm = x.reshape(m, d)

    u_h, q, kv, gates = _inproj(xm, row(mix_norm_g), w_in, row(b_gate), row(q_norm_g), row(k_norm_g),
                                cos, sin)

    emb2, deltas = (jnp.asarray(t) for t in _filter_tables_np())
    taps, l1 = _filter_taps(emb2, flt_w1, row(flt_b1), flt_w2, row(flt_b2), flt_w3, row(flt_b3),
                            flt_w4, row(flt_freq), deltas)
    spec = _filt_spec(taps, l1, tabs)

    z, x0c = _hy_pre(u_h.reshape(b, s, 3 * HYENA_W), hy_conv_w, row(hy_conv_b))
    y_h = _hy_conv(z, x0c, row(hy_bias), spec, tabs).reshape(m, HYENA_W)

    y_a = _attention(q.reshape(b, s, ATTN_W), kv.reshape(b, s, 2 * KV_W)).reshape(m, ATTN_W)

    merged = _merge(y_h, y_a, gates, w_br_hyena, w_br_attn)
    x1, h2 = _out_proj(merged, xm, w_out, row(ffn_norm_g))
    act, w_down_bf16 = _swiglu(h2, w_ffn_gate, w_ffn_up, w_ffn_down)
    out = _ffn_down(act, x1, w_down_bf16)
    return out.reshape(b, s, d)


def kernel(x, mix_norm_g, w_in, b_gate, hy_conv_w, hy_conv_b, flt_w1, flt_b1, flt_w2, flt_b2, flt_w3, flt_b3, flt_w4, flt_freq, hy_bias, q_norm_g, k_norm_g, w_br_hyena, w_br_attn, w_out, ffn_norm_g, w_ffn_gate, w_ffn_up, w_ffn_down):
    params = (mix_norm_g, w_in, b_gate, hy_conv_w, hy_conv_b, flt_w1, flt_b1, flt_w2, flt_b2,
              flt_w3, flt_b3, flt_w4, flt_freq, hy_bias, q_norm_g, k_norm_g, w_br_hyena, w_br_attn,
              w_out, ffn_norm_g, w_ffn_gate, w_ffn_up, w_ffn_down)
    for l in range(mix_norm_g.shape[0]):
        x = _layer(x, *(p[l] for p in params))
    return x
```

```python
import functools
import math

import numpy as np
import jax
import jax.numpy as jnp
from jax import lax
from jax.experimental import pallas as pl
from jax.experimental.pallas import tpu as pltpu

F32 = jnp.float32
BF16 = jnp.bfloat16

D_MODEL = 2048
SEQ = 4096
GRID_W = 64
HEAD_DIM = 128
N_Q_HEADS = 8
N_KV_HEADS = 2
Q_PER_KV = N_Q_HEADS // N_KV_HEADS
ATTN_W = N_Q_HEADS * HEAD_DIM
KV_W = N_KV_HEADS * HEAD_DIM
ROPE_THETA = 10000.0
HYENA_W = D_MODEL - ATTN_W
SHORT_TAPS = 3
FILTER_EMB = 33
FILTER_HIDDEN = 64
DECAY_TARGET = 1e-2
FAST_DECAY_PCT = 0.3
SLOW_DECAY_PCT = 1.5
IN_W = 3 * HYENA_W + ATTN_W + 2 * KV_W + 2 * D_MODEL
D_FF = 5632
EPS = 1e-6

COL_Q = 3 * HYENA_W
COL_K = COL_Q + ATTN_W
COL_V = COL_K + KV_W
COL_G = COL_V + KV_W

FFT_N = 2 * SEQ
FFT_N1 = 64
FFT_N2 = 128

MIB = 1024 * 1024


def _cparams(n_axes, vmem_mib):
    return pltpu.CompilerParams(
        dimension_semantics=("arbitrary",) * n_axes,
        vmem_limit_bytes=vmem_mib * MIB,
    )


DFT_R = 8


@functools.lru_cache(maxsize=None)
def _dft_tables_np():
    n, n1, n2, r = FFT_N, FFT_N1, FFT_N2, DFT_R
    eye = np.eye(r)
    f1 = np.arange(n1)
    ang1 = 2.0 * np.pi * ((f1[:, None] * f1[None, :]) % n1) / n1
    c1, s1 = np.cos(ang1), np.sin(ang1)
    h = n1 // 2
    m_fwd = np.block([[c1[:, :h], s1[:, :h]], [-s1[:, :h], c1[:, :h]]])
    m_flt = np.concatenate([c1, -s1], axis=0)
    ct, st = c1[:h, :], s1[:h, :]
    m_inv = np.block([[ct, -st], [st, ct]]) / n
    s2 = np.arange(n2)
    f = f1[:, None, None] + n1 * s2[None, :, None]
    th = 2.0 * np.pi * ((f * s2[None, None, :]) % n) / n
    c, s = np.cos(th), np.sin(th)
    fwd2 = np.concatenate(
        [np.concatenate([c, s], axis=2), np.concatenate([-s, c], axis=2)], axis=1)
    c_t, s_t = np.transpose(c, (0, 2, 1)), np.transpose(s, (0, 2, 1))
    inv2 = np.concatenate(
        [np.concatenate([c_t, -s_t], axis=2), np.concatenate([s_t, c_t], axis=2)], axis=1)
    f32 = lambda a: np.ascontiguousarray(a, dtype=np.float32)
    return dict(k_fwd=f32(np.kron(m_fwd, eye)), k_flt=f32(np.kron(m_flt, eye)),
                k_inv=f32(np.kron(m_inv, eye)), fwd2=f32(fwd2), inv2=f32(inv2))


def _dft_tables():
    return {k: jnp.asarray(v).astype(BF16) for k, v in _dft_tables_np().items()}


@functools.lru_cache(maxsize=None)
def _rope_tables_np():
    half = HEAD_DIM // 2
    inv = ROPE_THETA ** (-np.arange(0, half, 2, dtype=np.float64) / half)
    pos = np.arange(SEQ)
    ang_r = (pos // GRID_W)[:, None] * inv[None, :]
    ang_c = (pos % GRID_W)[:, None] * inv[None, :]
    cos = np.concatenate([np.cos(ang_r)] * 2 + [np.cos(ang_c)] * 2, axis=-1)
    sin = np.concatenate([-np.sin(ang_r), np.sin(ang_r), -np.sin(ang_c), np.sin(ang_c)], axis=-1)
    return cos.astype(np.float32), sin.astype(np.float32)


@functools.lru_cache(maxsize=None)
def _filter_tables_np():
    L = SEQ
    bands = (FILTER_EMB - 1) // 2
    pos = np.concatenate([np.arange(L, dtype=np.float64), L - np.arange(L, dtype=np.float64)])
    t = pos / max(L - 1, 1)
    fb = np.linspace(1e-4, bands - 1, bands)
    ang = (2.0 * math.pi * pos / L)[:, None] * fb[None, :]
    emb = np.concatenate([t[:, None], np.cos(ang), -np.sin(ang),
                          np.zeros((2 * L, FILTER_HIDDEN - FILTER_EMB))], axis=-1)
    max_decay = math.log(DECAY_TARGET) / FAST_DECAY_PCT
    min_decay = math.log(DECAY_TARGET) / SLOW_DECAY_PCT
    deltas = np.abs(np.linspace(min_decay, max_decay, HYENA_W))
    tile, hid = FLT_TR, FILTER_HIDDEN
    emb2 = emb.reshape(2 * L // tile, 2, tile // 2, hid).transpose(0, 2, 1, 3).reshape(L, 2 * hid)
    return np.ascontiguousarray(emb2, dtype=np.float32), deltas[None, :].astype(np.float32)


MM_TN = 512
MM_RC = 256


def _wres_kernel(*refs, pairs, n_act, n_extra, n_out, n_scratch, rc, epilogue, norm_first, side_cast):
    n_w = len(pairs)
    acts = refs[:n_act]
    ws = refs[n_act:n_act + n_w]
    extras = refs[n_act + n_w:n_act + n_w + n_extra]
    o_refs = refs[n_act + n_w + n_extra:n_act + n_w + n_extra + n_out]
    wbs = refs[n_act + n_w + n_extra + n_out:]
    if side_cast:
        o_refs[-1][...] = extras[-1][...].astype(BF16)
        extras, o_refs = extras[:-1], o_refs[:-1]
    if norm_first:
        g_ref, extras = extras[-1], extras[:-1]
        h_ref, o_refs = o_refs[-1], o_refs[:-1]

    if n_scratch:
        @pl.when(pl.program_id(1) == 0)
        def _():
            for w_ref, wb_ref in zip(ws, wbs):
                wb_ref[...] = w_ref[...].astype(BF16)
    else:
        wbs = ws

    for c in range(o_refs[0].shape[0] // rc):
        rows = pl.ds(c * rc, rc)
        lhs = [a[rows, :] for a in acts]
        if norm_first:
            x = lhs[0]
            ms = jnp.mean(x * x, axis=-1, keepdims=True)
            lhs[0] = (x * lax.rsqrt(ms + EPS) * g_ref[...]).astype(BF16)
            h_ref[rows, :] = lhs[0]
        accs = [jnp.dot(lhs[a], wb_ref[...], preferred_element_type=F32)
                for a, wb_ref in zip(pairs, wbs)]
        outs = epilogue(accs, extras, rows)
        for o_ref, out in zip(o_refs, outs if isinstance(outs, tuple) else (outs,)):
            o_ref[rows, :] = out.astype(o_ref.dtype)


def _wres_matmul(name, acts, weights, extras, epilogue, *, pairs, n_out, out_dtype, tm,
                 tn=MM_TN, rc=MM_RC, vmem_mib=48, norm_gain=None, side_cast=None):
    m = acts[0].shape[0]
    grid = (n_out // tn, m // tm)
    out_dtypes = out_dtype if isinstance(out_dtype, tuple) else (out_dtype,)
    out_specs = [pl.BlockSpec((tm, tn), lambda j, i: (i, j)) for _ in out_dtypes]
    out_shapes = [jax.ShapeDtypeStruct((m, n_out), dt) for dt in out_dtypes]
    extras = list(extras)
    if norm_gain is not None:
        assert n_out == tn
        k0 = acts[0].shape[1]
        extras.append((norm_gain, (1, k0), lambda j, i: (0, 0)))
        out_specs.append(pl.BlockSpec((tm, k0), lambda j, i: (i, 0)))
        out_shapes.append(jax.ShapeDtypeStruct((m, k0), BF16))
    if side_cast is not None:
        slab = side_cast.shape[0] // (grid[0] * grid[1])
        assert slab * grid[0] * grid[1] == side_cast.shape[0] and slab % 16 == 0
        step = lambda j, i: (j * grid[1] + i, 0)
        extras.append((side_cast, (slab, side_cast.shape[1]), step))
        out_specs.append(pl.BlockSpec((slab, side_cast.shape[1]), step))
        out_shapes.append(jax.ShapeDtypeStruct(side_cast.shape, BF16))
    in_specs = [pl.BlockSpec((tm, a.shape[1]), lambda j, i: (i, 0)) for a in acts]
    assert all(off % 128 == 0 for _, off in weights) and tn % 128 == 0
    w_mode = dict(pipeline_mode=pl.Buffered(1)) if n_out == tn else {}
    in_specs += [pl.BlockSpec((pl.Element(w.shape[0]), pl.Element(tn)),
                              lambda j, i, off=off: (0, pl.multiple_of(off + j * tn, 128)), **w_mode)
                 for w, off in weights]
    in_specs += [pl.BlockSpec(blk, imap) for _, blk, imap in extras]
    scratch = [pltpu.VMEM((w.shape[0], tn), BF16) for w, _ in weights if w.dtype != BF16]
    assert len(scratch) in (0, len(weights))
    kern = functools.partial(_wres_kernel, pairs=tuple(pairs), n_act=len(acts), n_extra=len(extras),
                             n_out=len(out_specs), n_scratch=len(scratch), rc=rc, epilogue=epilogue,
                             norm_first=norm_gain is not None, side_cast=side_cast is not None)
    outs = pl.pallas_call(
        kern,
        grid=grid,
        in_specs=in_specs,
        out_specs=out_specs,
        out_shape=out_shapes,
        scratch_shapes=scratch,
        compiler_params=_cparams(2, vmem_mib),
        name=name,
    )(*acts, *(w for w, _ in weights), *(e for e, _, _ in extras))
    return outs if len(outs) > 1 else outs[0]


def _head_norm_rope(x, g, cos, sin, scale):
    ms = jnp.mean(x * x, axis=-1, keepdims=True)
    y = x * lax.rsqrt(ms + EPS) * g
    lane = lax.broadcasted_iota(jnp.int32, y.shape, 1)
    swapped = jnp.where((lane % 64) < 32, pltpu.roll(y, 96, 1), pltpu.roll(y, 32, 1))
    out = y * cos + swapped * sin
    return out * scale if scale != 1.0 else out


def _ep_qkv(accs, extras, rows):
    qg_ref, kg_ref, cos_ref, sin_ref = extras
    acc = accs[0]
    cos, sin = cos_ref[rows, :], sin_ref[rows, :]
    head = lambda hh: acc[:, hh * HEAD_DIM:(hh + 1) * HEAD_DIM]
    parts = [_head_norm_rope(head(hh), qg_ref[...], cos, sin, HEAD_DIM ** -0.5 * LOG2E)
             for hh in range(N_Q_HEADS)]
    parts += [_head_norm_rope(head(N_Q_HEADS + hh), kg_ref[...], cos, sin, 1.0) for hh in range(N_KV_HEADS)]
    parts.append(acc[:, (N_Q_HEADS + N_KV_HEADS) * HEAD_DIM:])
    return jnp.concatenate(parts, axis=1)


def _ep_gate(accs, extras, rows):
    return jax.nn.sigmoid(accs[0] + extras[0][...])


IN_TM = 2048
IN_TN = 1024
LOG2E = math.log2(math.e)


def _inproj(x, norm_g, w_in, b_gate, q_g, k_g, cos, sin):
    tm, tn = IN_TM, IN_TN
    qkv_w = ATTN_W + 2 * KV_W
    qkv_tm = tm // 2
    head = lambda g: (g, (1, HEAD_DIM), lambda j, i: (0, 0))
    pos = lambda t: (t, (qkv_tm, HEAD_DIM), lambda j, i: (i % (SEQ // qkv_tm), 0))
    common = dict(pairs=[0], out_dtype=BF16, vmem_mib=56)
    qkv, h = _wres_matmul("inproj_qkv", [x], [(w_in, COL_Q)], [head(q_g), head(k_g), pos(cos), pos(sin)],
                          _ep_qkv, n_out=qkv_w, tn=qkv_w, tm=qkv_tm, norm_gain=norm_g, **common)
    gates = _wres_matmul("inproj_gate", [h], [(w_in, COL_G)],
                         [(b_gate, (1, tn), lambda j, i: (0, j))], _ep_gate,
                         n_out=2 * D_MODEL, tn=tn, tm=tm, **common)
    return h, qkv, gates


HC_TM = 1024
HC_RC = 512
HC_HALO = 16


def _inproj_conv_kernel(*refs, n_w):
    h_ref, top_ref, bot_ref = refs[:3]
    w_refs = refs[3:3 + n_w]
    cw_refs = refs[3 + n_w:3 + 2 * n_w]
    cb_refs = refs[3 + 2 * n_w:3 + 3 * n_w]
    o_ref = refs[3 + 3 * n_w]
    wb_refs = refs[4 + 3 * n_w:]
    i = pl.program_id(1)
    tm, rc, hl = o_ref.shape[0], HC_RC, HC_HALO

    @pl.when(i == 0)
    def _():
        for w_ref, wb_ref in zip(w_refs, wb_refs):
            wb_ref[...] = w_ref[...].astype(BF16)

    seq_blocks = SEQ // tm
    top = jnp.where(i % seq_blocks != 0, top_ref[...], jnp.zeros_like(top_ref))
    bot = jnp.where(i % seq_blocks != seq_blocks - 1, bot_ref[...], jnp.zeros_like(bot_ref))
    nchunk = tm // rc
    for c in range(nchunk):
        lo, hi = c * rc - hl, (c + 1) * rc + hl
        parts = ([top] if c == 0 else []) + [h_ref[max(lo, 0):min(hi, tm), :]] + ([bot] if c == nchunk - 1 else [])
        lhs = jnp.concatenate(parts, axis=0) if len(parts) > 1 else parts[0]
        outs = []
        for wb_ref, cw_ref, cb_ref in zip(wb_refs, cw_refs, cb_refs):
            e = jnp.dot(lhs, wb_ref[...], preferred_element_type=F32)
            up = pltpu.roll(e, 1, 0)[hl:hl + rc]
            dn = pltpu.roll(e, rc + 2 * hl - 1, 0)[hl:hl + rc]
            outs.append(cb_ref[...] + up * cw_ref[0:1, :] + e[hl:hl + rc] * cw_ref[1:2, :]
                        + dn * cw_ref[2:3, :])
        res = outs[0] if n_w == 1 else outs[0] * outs[1]
        o_ref[c * rc:(c + 1) * rc, :] = res.astype(o_ref.dtype)


def _inproj_conv(name, h, w_in, conv_w, conv_b, col_offs, out_dtype):
    m, d = h.shape
    tm, tn, hl = HC_TM, HYENA_W, HC_HALO
    n_w = len(col_offs)
    once = pl.Buffered(1)
    blk = lambda off: off // tn
    in_specs = [pl.BlockSpec((tm, d), lambda j, i: (i, 0)),
                pl.BlockSpec((hl, d), lambda j, i: (jnp.maximum(i * (tm // hl) - 1, 0), 0)),
                pl.BlockSpec((hl, d), lambda j, i: (jnp.minimum((i + 1) * (tm // hl), m // hl - 1), 0))]
    in_specs += [pl.BlockSpec((d, tn), lambda j, i, o=o: (0, blk(o)), pipeline_mode=once) for o in col_offs]
    in_specs += [pl.BlockSpec((SHORT_TAPS, tn), lambda j, i, o=o: (0, blk(o))) for o in col_offs]
    in_specs += [pl.BlockSpec((1, tn), lambda j, i, o=o: (0, blk(o))) for o in col_offs]
    return pl.pallas_call(
        functools.partial(_inproj_conv_kernel, n_w=n_w),
        grid=(1, m // tm),
        in_specs=in_specs,
        out_specs=pl.BlockSpec((tm, tn), lambda j, i: (i, 0)),
        out_shape=jax.ShapeDtypeStruct((m, tn), out_dtype),
        scratch_shapes=[pltpu.VMEM((d, tn), BF16) for _ in col_offs],
        compiler_params=_cparams(2, 56),
        name=name,
    )(h, h, h, *([w_in] * n_w), *([conv_w] * n_w), *([conv_b] * n_w))


CV_CT = 256
CV_F1B = 16
CV_NF = FFT_N1 // CV_F1B
CV_SLAB = 2 * DFT_R
CV_NSLAB = FFT_N2 // CV_SLAB
CV_HALF = FFT_N1 // 2


def _outer_fwd_slab(src_ref, k_ref, a_ref, j, rows_in):
    ct = a_ref.shape[-1]
    halves = []
    for h in range(2):
        r = src_ref[:, :, 2 * j + h] if src_ref.ndim == 5 else src_ref[:, 2 * j + h]
        r = r.reshape(rows_in, ct).astype(BF16)
        o = jnp.dot(k_ref[...], r, preferred_element_type=F32)
        halves.append(o.reshape(2 * FFT_N1, DFT_R, ct))
    slab = jnp.concatenate(halves, axis=1).astype(BF16)
    a_ref[:, :, pl.ds(pl.multiple_of(j * CV_SLAB, CV_SLAB), CV_SLAB), :] = slab.reshape(
        2, FFT_N1, CV_SLAB, ct)


def _hy_conv_kernel(z_ref, x0_ref, bias_ref, kf_ref, ki_ref, fw_ref, iv_ref, spec_ref, o_ref, a_ref):
    s = pl.program_id(2)
    ct = a_ref.shape[-1]

    @pl.when(s == 0)
    def _():
        def body(j, carry):
            _outer_fwd_slab(z_ref, kf_ref, a_ref, j, 2 * CV_HALF * DFT_R)
            return carry

        lax.fori_loop(0, CV_NSLAB, body, 0)

    @pl.when((s >= 1) & (s <= CV_NF))
    def _():
        f0 = (s - 1) * CV_F1B
        for fl in range(CV_F1B):
            x = a_ref[:, f0 + fl].reshape(2 * FFT_N2, ct)
            u = jnp.dot(fw_ref[fl], x, preferred_element_type=F32)
            ure, uim = u[:FFT_N2], u[FFT_N2:]
            kre, kim = spec_ref[0, fl].astype(F32), spec_ref[1, fl].astype(F32)
            p = jnp.concatenate([ure * kre - uim * kim, ure * kim + uim * kre], axis=0).astype(BF16)
            y = jnp.dot(iv_ref[fl], p, preferred_element_type=F32)
            a_ref[:, f0 + fl] = y.reshape(2, FFT_N2, ct).astype(BF16)

    @pl.when(s == CV_NF + 1)
    def _():
        bias = bias_ref[...]

        def body(j, carry):
            slab = a_ref[:, :, pl.ds(pl.multiple_of(j * CV_SLAB, CV_SLAB), CV_SLAB), :].astype(F32)
            x0 = x0_ref[:, :, j].astype(F32)
            halves = []
            for h in range(2):
                r = slab[:, :, h * DFT_R:(h + 1) * DFT_R, :].reshape(2 * FFT_N1 * DFT_R, ct).astype(BF16)
                y = jnp.dot(ki_ref[...], r, preferred_element_type=F32).reshape(2, CV_HALF, DFT_R, ct)
                z = z_ref[:, :, 2 * j + h]
                halves.append((y + bias * z) * x0[:, :, h * DFT_R:(h + 1) * DFT_R, :])
            o_ref[:, :, j] = jnp.concatenate(halves, axis=2).astype(o_ref.dtype)
            return carry

        lax.fori_loop(0, CV_NSLAB, body, 0)


def _hy_conv(z, x0c, bias, spec, tabs):
    b, L, c = z.shape
    assert b % 2 == 0 and L * 2 == FFT_N
    ct = CV_CT
    z5 = z.reshape(b, CV_HALF, FFT_N2 // DFT_R, DFT_R, c)
    x5 = x0c.reshape(b, CV_HALF, CV_NSLAB, CV_SLAB, c)
    fidx = lambda s: jnp.clip(s - 1, 0, CV_NF - 1)
    once = pl.Buffered(1)
    tab_spec = pl.BlockSpec((CV_F1B, 2 * FFT_N2, 2 * FFT_N2), lambda p, cb, s: (fidx(s), 0, 0))
    io16 = pl.BlockSpec((2, CV_HALF, CV_NSLAB, CV_SLAB, ct), lambda p, cb, s: (p, 0, 0, 0, cb))
    out = pl.pallas_call(
        _hy_conv_kernel,
        grid=(b // 2, c // ct, CV_NF + 2),
        in_specs=[
            pl.BlockSpec((2, CV_HALF, FFT_N2 // DFT_R, DFT_R, ct), lambda p, cb, s: (p, 0, 0, 0, cb),
                         pipeline_mode=once),
            io16,
            pl.BlockSpec((1, ct), lambda p, cb, s: (0, cb)),
            pl.BlockSpec(tabs["k_fwd"].shape, lambda p, cb, s: (0, 0), pipeline_mode=once),
            pl.BlockSpec(tabs["k_inv"].shape, lambda p, cb, s: (0, 0), pipeline_mode=once),
            tab_spec, tab_spec,
            pl.BlockSpec((2, CV_F1B, FFT_N2, ct), lambda p, cb, s: (0, fidx(s), 0, cb)),
        ],
        out_specs=io16,
        out_shape=jax.ShapeDtypeStruct(x5.shape, BF16),
        scratch_shapes=[pltpu.VMEM((2, FFT_N1, FFT_N2, ct), BF16)],
        compiler_params=_cparams(3, 56),
        name="hyena_conv",
    )(z5, x5, bias, tabs["k_fwd"], tabs["k_inv"], tabs["fwd2"], tabs["inv2"], spec)
    return out.reshape(b, L, c)


def _filt_spec_kernel(t_ref, l1_ref, kf_ref, fw_ref, o_ref, a_ref):
    s = pl.program_id(1)
    ct = a_ref.shape[-1]

    @pl.when(s == 0)
    def _():
        def body(j, carry):
            _outer_fwd_slab(t_ref, kf_ref, a_ref, j, FFT_N1 * DFT_R)
            return carry

        lax.fori_loop(0, CV_NSLAB, body, 0)

    @pl.when(s >= 1)
    def _():
        inv_l1 = 1.0 / l1_ref[...]
        f0 = (s - 1) * FS_F1B
        for fl in range(FS_F1B):
            x = a_ref[:, f0 + fl].reshape(2 * FFT_N2, ct)
            y = jnp.dot(fw_ref[fl], x, preferred_element_type=F32) * inv_l1
            o_ref[:, fl] = y.reshape(2, FFT_N2, ct).astype(o_ref.dtype)


FS_F1B = 16
FS_NF = FFT_N1 // FS_F1B


def _filt_spec(taps, l1, tabs):
    n, c = taps.shape
    ct = CV_CT
    t4 = taps.reshape(FFT_N1, FFT_N2 // DFT_R, DFT_R, c)
    fidx = lambda s: jnp.clip(s - 1, 0, FS_NF - 1)
    once = pl.Buffered(1)
    return pl.pallas_call(
        _filt_spec_kernel,
        grid=(c // ct, FS_NF + 1),
        in_specs=[
            pl.BlockSpec((FFT_N1, FFT_N2 // DFT_R, DFT_R, ct), lambda cb, s: (0, 0, 0, cb)),
            pl.BlockSpec((1, ct), lambda cb, s: (0, cb)),
            pl.BlockSpec(tabs["k_flt"].shape, lambda cb, s: (0, 0), pipeline_mode=once),
            pl.BlockSpec((FS_F1B, 2 * FFT_N2, 2 * FFT_N2), lambda cb, s: (fidx(s), 0, 0)),
        ],
        out_specs=pl.BlockSpec((2, FS_F1B, FFT_N2, ct), lambda cb, s: (0, fidx(s), 0, cb)),
        out_shape=jax.ShapeDtypeStruct((2, FFT_N1, FFT_N2, c), BF16),
        scratch_shapes=[pltpu.VMEM((2, FFT_N1, FFT_N2, ct), BF16)],
        compiler_params=_cparams(2, 48),
        name="filter_spectrum",
    )(t4, l1, tabs["k_flt"], tabs["fwd2"])


FLT_TR = 512


FLT_HALF = FLT_TR // 2


def _dot3(a, w):
    a_hi = a.astype(BF16)
    a_lo = (a - a_hi.astype(F32)).astype(BF16)
    w_hi = w.astype(BF16)
    w_lo = (w - w_hi.astype(F32)).astype(BF16)
    lhs = jnp.concatenate([a_hi, a_hi, a_lo], axis=1)
    rhs = jnp.concatenate([w_hi, w_lo, w_hi], axis=0)
    return jnp.dot(lhs, rhs, preferred_element_type=F32)


def _filter_kernel(emb_ref, w1_ref, b1_ref, w2_ref, b2_ref, w3_ref, b3_ref, w4t_ref, w4b_ref, fr_ref, dl_ref,
                   k_ref, l1_ref):
    r = pl.program_id(0)
    e = emb_ref[...]
    fr = fr_ref[...]
    h = jnp.sin(fr * (_dot3(e, w1_ref[...]) + b1_ref[...]))
    h = jnp.sin(fr * (_dot3(h, w2_ref[...]) + b2_ref[...]))
    h = jnp.sin(fr * (_dot3(h, w3_ref[...]) + b3_ref[...]))
    total = jnp.zeros(l1_ref.shape, F32)
    for part, w_ref in enumerate((w4t_ref, w4b_ref)):
        tcol = part * FILTER_HIDDEN
        taps = _dot3(h, w_ref[...])
        decay = jnp.exp(-e[:, tcol:tcol + 1] * dl_ref[...])
        rows = r * FLT_TR + part * FLT_HALF + lax.broadcasted_iota(jnp.int32, taps.shape, 0)
        taps = jnp.where(rows == SEQ, 0.0, taps * decay)
        k_ref[part * FLT_HALF:(part + 1) * FLT_HALF, :] = taps
        total = total + jnp.sum(jnp.abs(taps), axis=0, keepdims=True)

    @pl.when(r == 0)
    def _():
        l1_ref[...] = jnp.zeros_like(l1_ref)

    l1_ref[...] += total


def _filter_taps(emb2, w1, b1, w2, b2, w3, b3, w4, freq, deltas):
    n = 2 * emb2.shape[0]
    fh = FILTER_HIDDEN
    eye2 = jnp.eye(2, dtype=F32)
    w1p = jnp.concatenate([w1, jnp.zeros((fh - w1.shape[0], fh), F32)], axis=0)
    wd = [jnp.kron(eye2, w) for w in (w1p, w2, w3)]
    bd = [jnp.tile(b, (1, 2)) for b in (b1, b2, b3, freq)]
    zeros = jnp.zeros_like(w4)
    w4t = jnp.concatenate([w4, zeros], axis=0)
    w4b = jnp.concatenate([zeros, w4], axis=0)
    small = lambda shape: pl.BlockSpec(shape, lambda r: (0, 0))
    fwd_tiles = SEQ // FLT_TR
    w4_spec = pl.BlockSpec((2 * fh, HYENA_W), lambda r: (0, r // fwd_tiles))
    return pl.pallas_call(
        _filter_kernel,
        grid=(n // FLT_TR,),
        in_specs=[pl.BlockSpec((FLT_HALF, 2 * fh), lambda r: (r, 0)),
                  small((2 * fh, 2 * fh)), small((1, 2 * fh)), small((2 * fh, 2 * fh)), small((1, 2 * fh)),
                  small((2 * fh, 2 * fh)), small((1, 2 * fh)),
                  w4_spec, w4_spec,
                  small((1, 2 * fh)), small((1, HYENA_W))],
        out_specs=[pl.BlockSpec((FLT_TR, HYENA_W), lambda r: (r, 0)),
                   pl.BlockSpec((1, HYENA_W), lambda r: (0, 0))],
        out_shape=[jax.ShapeDtypeStruct((n, HYENA_W), F32),
                   jax.ShapeDtypeStruct((1, HYENA_W), F32)],
        compiler_params=_cparams(1, 32),
        name="filter_taps",
    )(emb2, wd[0], bd[0], wd[1], bd[1], wd[2], bd[2], w4t, w4b, bd[3], deltas)


AT_TQ = 512
AT_RQ = 256
AT_TK = 512


def _attn_kernel(q_ref, k_ref, v_ref, o_ref, s0_ref, s1_ref):
    s_refs = (s0_ref, s1_ref)
    nblk = k_ref.shape[0] // AT_TK
    units = [(pl.ds(r * AT_RQ, AT_RQ), slice(g * HEAD_DIM, (g + 1) * HEAD_DIM))
             for r in range(q_ref.shape[0] // AT_RQ) for g in range(Q_PER_KV)]
    lane_blocks = lambda a: [a[:, i:i + 128] for i in range(0, a.shape[1], 128)]

    def scores(u, j, m_run):
        rows, lanes = units[u]
        ks = pl.ds(j * AT_TK, AT_TK)
        s = lax.dot_general(q_ref[rows, lanes], k_ref[ks, :], (((1,), (1,)), ((), ())),
                            preferred_element_type=F32)
        s_refs[u % 2][:, ks] = s
        blk = functools.reduce(jnp.maximum, lane_blocks(s))
        return blk if m_run is None else jnp.maximum(m_run, blk)

    def weighted(u, j, m, l_run, acc):
        ks = pl.ds(j * AT_TK, AT_TK)
        p = jnp.exp2(s_refs[u % 2][:, ks] - m)
        l_blk = functools.reduce(jnp.add, lane_blocks(p))
        o = jnp.dot(p.astype(BF16), v_ref[ks, :], preferred_element_type=F32)
        return (l_blk if l_run is None else l_run + l_blk), (o if acc is None else acc + o)

    m_run = None
    for j in range(nblk):
        m_run = scores(0, j, m_run)
    for u, (rows, lanes) in enumerate(units):
        m = jnp.max(m_run, axis=-1, keepdims=True)
        m_run, l_run, acc = None, None, None
        for j in range(nblk):
            if u + 1 < len(units):
                m_run = scores(u + 1, j, m_run)
            l_run, acc = weighted(u, j, m, l_run, acc)
        l = jnp.sum(l_run, axis=-1, keepdims=True)
        o_ref[rows, lanes] = (acc / l).astype(o_ref.dtype)


def _attention(qkv3):
    b, s, _ = qkv3.shape
    gw = Q_PER_KV * HEAD_DIM
    k0 = N_Q_HEADS
    v0 = N_Q_HEADS + N_KV_HEADS
    return pl.pallas_call(
        _attn_kernel,
        grid=(b, N_KV_HEADS, s // AT_TQ),
        in_specs=[pl.BlockSpec((None, AT_TQ, gw), lambda bi, kv, qi: (bi, qi, kv)),
                  pl.BlockSpec((None, s, HEAD_DIM), lambda bi, kv, qi: (bi, 0, k0 + kv)),
                  pl.BlockSpec((None, s, HEAD_DIM), lambda bi, kv, qi: (bi, 0, v0 + kv))],
        out_specs=pl.BlockSpec((None, AT_TQ, gw), lambda bi, kv, qi: (bi, qi, kv)),
        out_shape=jax.ShapeDtypeStruct((b, s, ATTN_W), BF16),
        scratch_shapes=[pltpu.VMEM((AT_RQ, s), F32), pltpu.VMEM((AT_RQ, s), F32)],
        compiler_params=_cparams(3, 48),
        name="attention",
    )(qkv3, qkv3, qkv3)


def _ep_merge(accs, extras, rows):
    gh_ref, ga_ref = extras
    return gh_ref[rows, :].astype(F32) * accs[0] + ga_ref[rows, :].astype(F32) * accs[1]


def _ep_residual(accs, extras, rows):
    return extras[0][rows, :] + accs[0]


def _ep_swiglu(accs, extras, rows):
    return jax.nn.silu(accs[0]) * accs[1]


def _ep_residual_norm(accs, extras, rows):
    x_ref, g_ref = extras
    x1 = x_ref[rows, :] + accs[0]
    ms = jnp.mean(x1 * x1, axis=-1, keepdims=True)
    return x1, x1 * lax.rsqrt(ms + EPS) * g_ref[...]


def _out_proj(merged, x, w, g, tm=512):
    n = w.shape[1]
    return _wres_matmul("out_proj", [merged], [(w, 0)],
                        [(x, (tm, n), lambda j, i: (i, 0)), (g, (1, n), lambda j, i: (0, 0))],
                        _ep_residual_norm, pairs=[0], n_out=n, out_dtype=(F32, BF16), tm=tm, tn=n,
                        vmem_mib=56)


def _merge(yh, ya, gates, w_h, w_a, tm=1024):
    tn = 1024
    nj = D_MODEL // tn
    return _wres_matmul("merge", [yh, ya], [(w_h, 0), (w_a, 0)],
                        [(gates, (tm, tn), lambda j, i: (i, j)),
                         (gates, (tm, tn), lambda j, i: (i, nj + j))],
                        _ep_merge, pairs=[0, 1], n_out=D_MODEL, out_dtype=BF16, tm=tm, tn=tn)


def _ffn_down(a, x, w_bf16, tm=512, tn=1024):
    return _wres_matmul("ffn_down", [a], [(w_bf16, 0)], [(x, (tm, tn), lambda j, i: (i, j))], _ep_residual,
                        pairs=[0], n_out=w_bf16.shape[1], out_dtype=F32, tm=tm, tn=tn, vmem_mib=56)


def _swiglu(h, w_g, w_u, w_down, tm=2048):
    return _wres_matmul("swiglu", [h], [(w_g, 0), (w_u, 0)], [], _ep_swiglu,
                        pairs=[0, 0], n_out=w_g.shape[1], out_dtype=BF16, tm=tm, vmem_mib=56,
                        side_cast=w_down)


def _layer(x, mix_norm_g, w_in, b_gate, hy_conv_w, hy_conv_b,
           flt_w1, flt_b1, flt_w2, flt_b2, flt_w3, flt_b3, flt_w4, flt_freq, hy_bias,
           q_norm_g, k_norm_g, w_br_hyena, w_br_attn, w_out,
           ffn_norm_g, w_ffn_gate, w_ffn_up, w_ffn_down):
    b, s, d = x.shape
    m = b * s
    row = lambda a: a.reshape(1, -1)
    tabs = _dft_tables()
    cos, sin = (jnp.asarray(t) for t in _rope_tables_np())
    xm = x.reshape(m, d)

    h, qkv, gates = _inproj(xm, row(mix_norm_g), w_in, row(b_gate), row(q_norm_g), row(k_norm_g),
                            cos, sin)

    emb2, deltas = (jnp.asarray(t) for t in _filter_tables_np())
    taps, l1 = _filter_taps(emb2, flt_w1, row(flt_b1), flt_w2, row(flt_b2), flt_w3, row(flt_b3),
                            flt_w4, row(flt_freq), deltas)
    spec = _filt_spec(taps, l1, tabs)

    conv_b = row(hy_conv_b)
    x0c = _inproj_conv("inproj_x0", h, w_in, hy_conv_w, conv_b, [0], BF16)
    z = _inproj_conv("inproj_z", h, w_in, hy_conv_w, conv_b, [HYENA_W, 2 * HYENA_W], F32)
    y_h = _hy_conv(z.reshape(b, s, HYENA_W), x0c.reshape(b, s, HYENA_W), row(hy_bias), spec,
                   tabs).reshape(m, HYENA_W)

    y_a = _attention(qkv.reshape(b, s, ATTN_W + 2 * KV_W)).reshape(m, ATTN_W)

    merged = _merge(y_h, y_a, gates, w_br_hyena, w_br_attn)
    x1, h2 = _out_proj(merged, xm, w_out, row(ffn_norm_g))
    act, w_down_bf16 = _swiglu(h2, w_ffn_gate, w_ffn_up, w_ffn_down)
    out = _ffn_down(act, x1, w_down_bf16)
    return out.reshape(b, s, d)


def kernel(x, mix_norm_g, w_in, b_gate, hy_conv_w, hy_conv_b, flt_w1, flt_b1, flt_w2, flt_b2, flt_w3, flt_b3, flt_w4, flt_freq, hy_bias, q_norm_g, k_norm_g, w_br_hyena, w_br_attn, w_out, ffn_norm_g, w_ffn_gate, w_ffn_up, w_ffn_down):
    params = (mix_norm_g, w_in, b_gate, hy_conv_w, hy_conv_b, flt_w1, flt_b1, flt_w2, flt_b2,
              flt_w3, flt_b3, flt_w4, flt_freq, hy_bias, q_norm_g, k_norm_g, w_br_hyena, w_br_attn,
              w_out, ffn_norm_g, w_ffn_gate, w_ffn_up, w_ffn_down)
    for l in range(mix_norm_g.shape[0]):
        x = _layer(x, *(p[l] for p in params))
    return x
```

```python
import functools
import math

import numpy as np
import jax
import jax.numpy as jnp
from jax import lax
from jax.experimental import pallas as pl
from jax.experimental.pallas import tpu as pltpu

F32 = jnp.float32
BF16 = jnp.bfloat16

D_MODEL = 2048
SEQ = 4096
GRID_W = 64
HEAD_DIM = 128
N_Q_HEADS = 8
N_KV_HEADS = 2
Q_PER_KV = N_Q_HEADS // N_KV_HEADS
ATTN_W = N_Q_HEADS * HEAD_DIM
KV_W = N_KV_HEADS * HEAD_DIM
ROPE_THETA = 10000.0
HYENA_W = D_MODEL - ATTN_W
SHORT_TAPS = 3
FILTER_EMB = 33
FILTER_HIDDEN = 64
DECAY_TARGET = 1e-2
FAST_DECAY_PCT = 0.3
SLOW_DECAY_PCT = 1.5
IN_W = 3 * HYENA_W + ATTN_W + 2 * KV_W + 2 * D_MODEL
D_FF = 5632
EPS = 1e-6

COL_Q = 3 * HYENA_W
COL_K = COL_Q + ATTN_W
COL_V = COL_K + KV_W
COL_G = COL_V + KV_W

FFT_N = 2 * SEQ
FFT_N1 = 64
FFT_N2 = 128

MIB = 1024 * 1024


def _cparams(n_axes, vmem_mib):
    return pltpu.CompilerParams(
        dimension_semantics=("arbitrary",) * n_axes,
        vmem_limit_bytes=vmem_mib * MIB,
    )


DFT_R = 8


@functools.lru_cache(maxsize=None)
def _dft_tables_np():
    n, n1, n2, r = FFT_N, FFT_N1, FFT_N2, DFT_R
    eye = np.eye(r)
    f1 = np.arange(n1)
    ang1 = 2.0 * np.pi * ((f1[:, None] * f1[None, :]) % n1) / n1
    c1, s1 = np.cos(ang1), np.sin(ang1)
    h = n1 // 2
    m_fwd = np.block([[c1[:, :h], s1[:, :h]], [-s1[:, :h], c1[:, :h]]])
    m_flt = np.concatenate([c1, -s1], axis=0)
    ct, st = c1[:h, :], s1[:h, :]
    m_inv = np.block([[ct, -st], [st, ct]]) / n
    s2 = np.arange(n2)
    f = f1[:, None, None] + n1 * s2[None, :, None]
    th = 2.0 * np.pi * ((f * s2[None, None, :]) % n) / n
    c, s = np.cos(th), np.sin(th)
    fwd2 = np.concatenate(
        [np.concatenate([c, s], axis=2), np.concatenate([-s, c], axis=2)], axis=1)
    c_t, s_t = np.transpose(c, (0, 2, 1)), np.transpose(s, (0, 2, 1))
    inv2 = np.concatenate(
        [np.concatenate([c_t, -s_t], axis=2), np.concatenate([s_t, c_t], axis=2)], axis=1)
    f32 = lambda a: np.ascontiguousarray(a, dtype=np.float32)
    return dict(k_fwd=f32(np.kron(m_fwd, eye)), k_flt=f32(np.kron(m_flt, eye)),
                k_inv=f32(np.kron(m_inv, eye)), fwd2=f32(fwd2), inv2=f32(inv2))


def _dft_tables():
    return {k: jnp.asarray(v).astype(BF16) for k, v in _dft_tables_np().items()}


@functools.lru_cache(maxsize=None)
def _rope_tables_np():
    half = HEAD_DIM // 2
    inv = ROPE_THETA ** (-np.arange(0, half, 2, dtype=np.float64) / half)
    pos = np.arange(SEQ)
    ang_r = (pos // GRID_W)[:, None] * inv[None, :]
    ang_c = (pos % GRID_W)[:, None] * inv[None, :]
    cos = np.concatenate([np.cos(ang_r)] * 2 + [np.cos(ang_c)] * 2, axis=-1)
    sin = np.concatenate([-np.sin(ang_r), np.sin(ang_r), -np.sin(ang_c), np.sin(ang_c)], axis=-1)
    return cos.astype(np.float32), sin.astype(np.float32)


@functools.lru_cache(maxsize=None)
def _filter_tables_np():
    L = SEQ
    bands = (FILTER_EMB - 1) // 2
    pos = np.concatenate([np.arange(L, dtype=np.float64), L - np.arange(L, dtype=np.float64)])
    t = pos / max(L - 1, 1)
    fb = np.linspace(1e-4, bands - 1, bands)
    ang = (2.0 * math.pi * pos / L)[:, None] * fb[None, :]
    emb = np.concatenate([t[:, None], np.cos(ang), -np.sin(ang),
                          np.zeros((2 * L, FILTER_HIDDEN - FILTER_EMB))], axis=-1)
    max_decay = math.log(DECAY_TARGET) / FAST_DECAY_PCT
    min_decay = math.log(DECAY_TARGET) / SLOW_DECAY_PCT
    deltas = np.abs(np.linspace(min_decay, max_decay, HYENA_W))
    tile, hid = FLT_TR, FILTER_HIDDEN
    emb2 = emb.reshape(2 * L // tile, 2, tile // 2, hid).transpose(0, 2, 1, 3).reshape(L, 2 * hid)
    return np.ascontiguousarray(emb2, dtype=np.float32), deltas[None, :].astype(np.float32)


MM_TN = 512
MM_RC = 256


def _wres_kernel(*refs, pairs, n_act, n_extra, n_out, n_scratch, rc, epilogue, norm_first, side_cast):
    n_w = len(pairs)
    acts = refs[:n_act]
    ws = refs[n_act:n_act + n_w]
    extras = refs[n_act + n_w:n_act + n_w + n_extra]
    o_refs = refs[n_act + n_w + n_extra:n_act + n_w + n_extra + n_out]
    wbs = refs[n_act + n_w + n_extra + n_out:]
    if side_cast:
        o_refs[-1][...] = extras[-1][...].astype(BF16)
        extras, o_refs = extras[:-1], o_refs[:-1]
    if norm_first:
        g_ref, extras = extras[-1], extras[:-1]
        h_ref, o_refs = o_refs[-1], o_refs[:-1]

    if n_scratch:
        @pl.when(pl.program_id(1) == 0)
        def _():
            for w_ref, wb_ref in zip(ws, wbs):
                wb_ref[...] = w_ref[...].astype(BF16)
    else:
        wbs = ws

    for c in range(o_refs[0].shape[0] // rc):
        rows = pl.ds(c * rc, rc)
        lhs = [a[rows, :] for a in acts]
        if norm_first:
            x = lhs[0]
            ms = jnp.mean(x * x, axis=-1, keepdims=True)
            lhs[0] = (x * lax.rsqrt(ms + EPS) * g_ref[...]).astype(BF16)
            h_ref[rows, :] = lhs[0]
        accs = [jnp.dot(lhs[a], wb_ref[...], preferred_element_type=F32)
                for a, wb_ref in zip(pairs, wbs)]
        outs = epilogue(accs, extras, rows)
        for o_ref, out in zip(o_refs, outs if isinstance(outs, tuple) else (outs,)):
            o_ref[rows, :] = out.astype(o_ref.dtype)


def _wres_matmul(name, acts, weights, extras, epilogue, *, pairs, n_out, out_dtype, tm,
                 tn=MM_TN, rc=MM_RC, vmem_mib=48, norm_gain=None, side_cast=None):
    m = acts[0].shape[0]
    grid = (n_out // tn, m // tm)
    out_dtypes = out_dtype if isinstance(out_dtype, tuple) else (out_dtype,)
    out_specs = [pl.BlockSpec((tm, tn), lambda j, i: (i, j)) for _ in out_dtypes]
    out_shapes = [jax.ShapeDtypeStruct((m, n_out), dt) for dt in out_dtypes]
    extras = list(extras)
    if norm_gain is not None:
        assert n_out == tn
        k0 = acts[0].shape[1]
        extras.append((norm_gain, (1, k0), lambda j, i: (0, 0)))
        out_specs.append(pl.BlockSpec((tm, k0), lambda j, i: (i, 0)))
        out_shapes.append(jax.ShapeDtypeStruct((m, k0), BF16))
    if side_cast is not None:
        slab = side_cast.shape[0] // (grid[0] * grid[1])
        assert slab * grid[0] * grid[1] == side_cast.shape[0] and slab % 16 == 0
        step = lambda j, i: (j * grid[1] + i, 0)
        extras.append((side_cast, (slab, side_cast.shape[1]), step))
        out_specs.append(pl.BlockSpec((slab, side_cast.shape[1]), step))
        out_shapes.append(jax.ShapeDtypeStruct(side_cast.shape, BF16))
    in_specs = [pl.BlockSpec((tm, a.shape[1]), lambda j, i: (i, 0)) for a in acts]
    assert all(off % 128 == 0 for _, off in weights) and tn % 128 == 0
    w_mode = dict(pipeline_mode=pl.Buffered(1)) if n_out == tn else {}
    in_specs += [pl.BlockSpec((pl.Element(w.shape[0]), pl.Element(tn)),
                              lambda j, i, off=off: (0, pl.multiple_of(off + j * tn, 128)), **w_mode)
                 for w, off in weights]
    in_specs += [pl.BlockSpec(blk, imap) for _, blk, imap in extras]
    scratch = [pltpu.VMEM((w.shape[0], tn), BF16) for w, _ in weights if w.dtype != BF16]
    assert len(scratch) in (0, len(weights))
    kern = functools.partial(_wres_kernel, pairs=tuple(pairs), n_act=len(acts), n_extra=len(extras),
                             n_out=len(out_specs), n_scratch=len(scratch), rc=rc, epilogue=epilogue,
                             norm_first=norm_gain is not None, side_cast=side_cast is not None)
    outs = pl.pallas_call(
        kern,
        grid=grid,
        in_specs=in_specs,
        out_specs=out_specs,
        out_shape=out_shapes,
        scratch_shapes=scratch,
        compiler_params=_cparams(2, vmem_mib),
        name=name,
    )(*acts, *(w for w, _ in weights), *(e for e, _, _ in extras))
    return outs if len(outs) > 1 else outs[0]


def _head_norm_rope(x, g, cos, sin, scale):
    ms = jnp.mean(x * x, axis=-1, keepdims=True)
    y = x * lax.rsqrt(ms + EPS) * g
    lane = lax.broadcasted_iota(jnp.int32, y.shape, 1)
    swapped = jnp.where((lane % 64) < 32, pltpu.roll(y, 96, 1), pltpu.roll(y, 32, 1))
    out = y * cos + swapped * sin
    return out * scale if scale != 1.0 else out


def _ep_qkv(accs, extras, rows):
    qg_ref, kg_ref, cos_ref, sin_ref = extras
    acc = accs[0]
    cos, sin = cos_ref[rows, :], sin_ref[rows, :]
    head = lambda hh: acc[:, hh * HEAD_DIM:(hh + 1) * HEAD_DIM]
    parts = [_head_norm_rope(head(hh), qg_ref[...], cos, sin, HEAD_DIM ** -0.5 * LOG2E)
             for hh in range(N_Q_HEADS)]
    parts += [_head_norm_rope(head(N_Q_HEADS + hh), kg_ref[...], cos, sin, 1.0) for hh in range(N_KV_HEADS)]
    parts.append(acc[:, (N_Q_HEADS + N_KV_HEADS) * HEAD_DIM:])
    return jnp.concatenate(parts, axis=1)


def _ep_gate(accs, extras, rows):
    return jax.nn.sigmoid(accs[0] + extras[0][...])


IN_TM = 2048
IN_TN = 1024
LOG2E = math.log2(math.e)


def _inproj(x, norm_g, w_in, b_gate, q_g, k_g, cos, sin, w_out):
    tm, tn = IN_TM, IN_TN
    qkv_w = ATTN_W + 2 * KV_W
    qkv_tm = tm // 2
    head = lambda g: (g, (1, HEAD_DIM), lambda j, i: (0, 0))
    pos = lambda t: (t, (qkv_tm, HEAD_DIM), lambda j, i: (i % (SEQ // qkv_tm), 0))
    common = dict(pairs=[0], out_dtype=BF16, vmem_mib=56)
    qkv, h = _wres_matmul("inproj_qkv", [x], [(w_in, COL_Q)], [head(q_g), head(k_g), pos(cos), pos(sin)],
                          _ep_qkv, n_out=qkv_w, tn=qkv_w, tm=qkv_tm, norm_gain=norm_g, **common)
    gates, w_out_bf16 = _wres_matmul("inproj_gate", [h], [(w_in, COL_G)],
                                     [(b_gate, (1, tn), lambda j, i: (0, j))], _ep_gate,
                                     n_out=2 * D_MODEL, tn=tn, tm=tm, side_cast=w_out, **common)
    return h, qkv, gates, w_out_bf16


HC_TM = 1024
HC_RC = 512
HC_HALO = 16


def _inproj_conv_kernel(*refs, n_w):
    h_ref, top_ref, bot_ref = refs[:3]
    w_refs = refs[3:3 + n_w]
    cw_refs = refs[3 + n_w:3 + 2 * n_w]
    cb_refs = refs[3 + 2 * n_w:3 + 3 * n_w]
    side_ref, o_ref, side_out_ref = refs[3 + 3 * n_w:6 + 3 * n_w]
    wb_refs = refs[6 + 3 * n_w:]
    side_out_ref[...] = side_ref[...].astype(BF16)
    i = pl.program_id(1)
    tm, rc, hl = o_ref.shape[0], HC_RC, HC_HALO

    @pl.when(i == 0)
    def _():
        for w_ref, wb_ref in zip(w_refs, wb_refs):
            wb_ref[...] = w_ref[...].astype(BF16)

    seq_blocks = SEQ // tm
    top = jnp.where(i % seq_blocks != 0, top_ref[...], jnp.zeros_like(top_ref))
    bot = jnp.where(i % seq_blocks != seq_blocks - 1, bot_ref[...], jnp.zeros_like(bot_ref))
    nchunk = tm // rc
    for c in range(nchunk):
        lo, hi = c * rc - hl, (c + 1) * rc + hl
        parts = ([top] if c == 0 else []) + [h_ref[max(lo, 0):min(hi, tm), :]] + ([bot] if c == nchunk - 1 else [])
        lhs = jnp.concatenate(parts, axis=0) if len(parts) > 1 else parts[0]
        outs = []
        for wb_ref, cw_ref, cb_ref in zip(wb_refs, cw_refs, cb_refs):
            e = jnp.dot(lhs, wb_ref[...], preferred_element_type=F32)
            up = pltpu.roll(e, 1, 0)[hl:hl + rc]
            dn = pltpu.roll(e, rc + 2 * hl - 1, 0)[hl:hl + rc]
            outs.append(cb_ref[...] + up * cw_ref[0:1, :] + e[hl:hl + rc] * cw_ref[1:2, :]
                        + dn * cw_ref[2:3, :])
        res = outs[0] if n_w == 1 else outs[0] * outs[1]
        o_ref[c * rc:(c + 1) * rc, :] = res.astype(o_ref.dtype)


def _inproj_conv(name, h, w_in, conv_w, conv_b, col_offs, out_dtype, side_cast):
    m, d = h.shape
    tm, tn, hl = HC_TM, HYENA_W, HC_HALO
    n_w = len(col_offs)
    once = pl.Buffered(1)
    blk = lambda off: off // tn
    slab = side_cast.shape[0] // (m // tm)
    assert slab * (m // tm) == side_cast.shape[0] and slab % 16 == 0
    side_spec = pl.BlockSpec((slab, side_cast.shape[1]), lambda j, i: (i, 0))
    in_specs = [pl.BlockSpec((tm, d), lambda j, i: (i, 0)),
                pl.BlockSpec((hl, d), lambda j, i: (jnp.maximum(i * (tm // hl) - 1, 0), 0)),
                pl.BlockSpec((hl, d), lambda j, i: (jnp.minimum((i + 1) * (tm // hl), m // hl - 1), 0))]
    in_specs += [pl.BlockSpec((d, tn), lambda j, i, o=o: (0, blk(o)), pipeline_mode=once) for o in col_offs]
    in_specs += [pl.BlockSpec((SHORT_TAPS, tn), lambda j, i, o=o: (0, blk(o))) for o in col_offs]
    in_specs += [pl.BlockSpec((1, tn), lambda j, i, o=o: (0, blk(o))) for o in col_offs]
    in_specs.append(side_spec)
    return pl.pallas_call(
        functools.partial(_inproj_conv_kernel, n_w=n_w),
        grid=(1, m // tm),
        in_specs=in_specs,
        out_specs=[pl.BlockSpec((tm, tn), lambda j, i: (i, 0)), side_spec],
        out_shape=[jax.ShapeDtypeStruct((m, tn), out_dtype), jax.ShapeDtypeStruct(side_cast.shape, BF16)],
        scratch_shapes=[pltpu.VMEM((d, tn), BF16) for _ in col_offs],
        compiler_params=_cparams(2, 56),
        name=name,
    )(h, h, h, *([w_in] * n_w), *([conv_w] * n_w), *([conv_b] * n_w), side_cast)


CV_CT = 256
CV_F1B = 16
CV_NF = FFT_N1 // CV_F1B
CV_SLAB = 2 * DFT_R
CV_NSLAB = FFT_N2 // CV_SLAB
CV_HALF = FFT_N1 // 2


def _outer_fwd_slab(src_ref, k_ref, a_ref, j, rows_in):
    ct = a_ref.shape[-1]
    halves = []
    for h in range(2):
        r = src_ref[:, :, 2 * j + h] if src_ref.ndim == 5 else src_ref[:, 2 * j + h]
        r = r.reshape(rows_in, ct).astype(BF16)
        o = jnp.dot(k_ref[...], r, preferred_element_type=F32)
        halves.append(o.reshape(2 * FFT_N1, DFT_R, ct))
    slab = jnp.concatenate(halves, axis=1).astype(BF16)
    a_ref[:, :, pl.ds(pl.multiple_of(j * CV_SLAB, CV_SLAB), CV_SLAB), :] = slab.reshape(
        2, FFT_N1, CV_SLAB, ct)


def _hy_conv_kernel(z_ref, x0_ref, bias_ref, kf_ref, ki_ref, fw_ref, iv_ref, spec_ref, o_ref, a_ref):
    s = pl.program_id(2)
    ct = a_ref.shape[-1]

    @pl.when(s == 0)
    def _():
        def body(j, carry):
            _outer_fwd_slab(z_ref, kf_ref, a_ref, j, 2 * CV_HALF * DFT_R)
            return carry

        lax.fori_loop(0, CV_NSLAB, body, 0)

    @pl.when((s >= 1) & (s <= CV_NF))
    def _():
        f0 = (s - 1) * CV_F1B
        for fl in range(CV_F1B):
            x = a_ref[:, f0 + fl].reshape(2 * FFT_N2, ct)
            u = jnp.dot(fw_ref[fl], x, preferred_element_type=F32)
            ure, uim = u[:FFT_N2], u[FFT_N2:]
            kre, kim = spec_ref[0, fl].astype(F32), spec_ref[1, fl].astype(F32)
            p = jnp.concatenate([ure * kre - uim * kim, ure * kim + uim * kre], axis=0).astype(BF16)
            y = jnp.dot(iv_ref[fl], p, preferred_element_type=F32)
            a_ref[:, f0 + fl] = y.reshape(2, FFT_N2, ct).astype(BF16)

    @pl.when(s == CV_NF + 1)
    def _():
        bias = bias_ref[...]

        def body(j, carry):
            slab = a_ref[:, :, pl.ds(pl.multiple_of(j * CV_SLAB, CV_SLAB), CV_SLAB), :].astype(F32)
            x0 = x0_ref[:, :, j].astype(F32)
            halves = []
            for h in range(2):
                r = slab[:, :, h * DFT_R:(h + 1) * DFT_R, :].reshape(2 * FFT_N1 * DFT_R, ct).astype(BF16)
                y = jnp.dot(ki_ref[...], r, preferred_element_type=F32).reshape(2, CV_HALF, DFT_R, ct)
                z = z_ref[:, :, 2 * j + h]
                halves.append((y + bias * z) * x0[:, :, h * DFT_R:(h + 1) * DFT_R, :])
            o_ref[:, :, j] = jnp.concatenate(halves, axis=2).astype(o_ref.dtype)
            return carry

        lax.fori_loop(0, CV_NSLAB, body, 0)


def _hy_conv(z, x0c, bias, spec, tabs):
    b, L, c = z.shape
    assert b % 2 == 0 and L * 2 == FFT_N
    ct = CV_CT
    z5 = z.reshape(b, CV_HALF, FFT_N2 // DFT_R, DFT_R, c)
    x5 = x0c.reshape(b, CV_HALF, CV_NSLAB, CV_SLAB, c)
    fidx = lambda s: jnp.clip(s - 1, 0, CV_NF - 1)
    once = pl.Buffered(1)
    tab_spec = pl.BlockSpec((CV_F1B, 2 * FFT_N2, 2 * FFT_N2), lambda p, cb, s: (fidx(s), 0, 0))
    io16 = pl.BlockSpec((2, CV_HALF, CV_NSLAB, CV_SLAB, ct), lambda p, cb, s: (p, 0, 0, 0, cb))
    out = pl.pallas_call(
        _hy_conv_kernel,
        grid=(b // 2, c // ct, CV_NF + 2),
        in_specs=[
            pl.BlockSpec((2, CV_HALF, FFT_N2 // DFT_R, DFT_R, ct), lambda p, cb, s: (p, 0, 0, 0, cb),
                         pipeline_mode=once),
            io16,
            pl.BlockSpec((1, ct), lambda p, cb, s: (0, cb)),
            pl.BlockSpec(tabs["k_fwd"].shape, lambda p, cb, s: (0, 0), pipeline_mode=once),
            pl.BlockSpec(tabs["k_inv"].shape, lambda p, cb, s: (0, 0), pipeline_mode=once),
            tab_spec, tab_spec,
            pl.BlockSpec((2, CV_F1B, FFT_N2, ct), lambda p, cb, s: (0, fidx(s), 0, cb)),
        ],
        out_specs=io16,
        out_shape=jax.ShapeDtypeStruct(x5.shape, BF16),
        scratch_shapes=[pltpu.VMEM((2, FFT_N1, FFT_N2, ct), BF16)],
        compiler_params=_cparams(3, 56),
        name="hyena_conv",
    )(z5, x5, bias, tabs["k_fwd"], tabs["k_inv"], tabs["fwd2"], tabs["inv2"], spec)
    return out.reshape(b, L, c)


def _filt_spec_kernel(t_ref, l1_ref, kf_ref, fw_ref, o_ref, a_ref):
    s = pl.program_id(1)
    ct = a_ref.shape[-1]

    @pl.when(s == 0)
    def _():
        def body(j, carry):
            _outer_fwd_slab(t_ref, kf_ref, a_ref, j, FFT_N1 * DFT_R)
            return carry

        lax.fori_loop(0, CV_NSLAB, body, 0)

    @pl.when(s >= 1)
    def _():
        inv_l1 = 1.0 / l1_ref[...]
        f0 = (s - 1) * FS_F1B
        for fl in range(FS_F1B):
            x = a_ref[:, f0 + fl].reshape(2 * FFT_N2, ct)
            y = jnp.dot(fw_ref[fl], x, preferred_element_type=F32) * inv_l1
            o_ref[:, fl] = y.reshape(2, FFT_N2, ct).astype(o_ref.dtype)


FS_F1B = 16
FS_NF = FFT_N1 // FS_F1B


def _filt_spec(taps, l1, tabs):
    n, c = taps.shape
    ct = CV_CT
    t4 = taps.reshape(FFT_N1, FFT_N2 // DFT_R, DFT_R, c)
    fidx = lambda s: jnp.clip(s - 1, 0, FS_NF - 1)
    once = pl.Buffered(1)
    return pl.pallas_call(
        _filt_spec_kernel,
        grid=(c // ct, FS_NF + 1),
        in_specs=[
            pl.BlockSpec((FFT_N1, FFT_N2 // DFT_R, DFT_R, ct), lambda cb, s: (0, 0, 0, cb)),
            pl.BlockSpec((1, ct), lambda cb, s: (0, cb)),
            pl.BlockSpec(tabs["k_flt"].shape, lambda cb, s: (0, 0), pipeline_mode=once),
            pl.BlockSpec((FS_F1B, 2 * FFT_N2, 2 * FFT_N2), lambda cb, s: (fidx(s), 0, 0)),
        ],
        out_specs=pl.BlockSpec((2, FS_F1B, FFT_N2, ct), lambda cb, s: (0, fidx(s), 0, cb)),
        out_shape=jax.ShapeDtypeStruct((2, FFT_N1, FFT_N2, c), BF16),
        scratch_shapes=[pltpu.VMEM((2, FFT_N1, FFT_N2, ct), BF16)],
        compiler_params=_cparams(2, 48),
        name="filter_spectrum",
    )(t4, l1, tabs["k_flt"], tabs["fwd2"])


FLT_TR = 512


FLT_HALF = FLT_TR // 2


def _dot3(a, w):
    a_hi = a.astype(BF16)
    a_lo = (a - a_hi.astype(F32)).astype(BF16)
    w_hi = w.astype(BF16)
    w_lo = (w - w_hi.astype(F32)).astype(BF16)
    lhs = jnp.concatenate([a_hi, a_hi, a_lo], axis=1)
    rhs = jnp.concatenate([w_hi, w_lo, w_hi], axis=0)
    return jnp.dot(lhs, rhs, preferred_element_type=F32)


def _filter_kernel(emb_ref, w1_ref, b1_ref, w2_ref, b2_ref, w3_ref, b3_ref, w4t_ref, w4b_ref, fr_ref, dl_ref,
                   k_ref, l1_ref):
    r = pl.program_id(0)
    e = emb_ref[...]
    fr = fr_ref[...]
    h = jnp.sin(fr * (_dot3(e, w1_ref[...]) + b1_ref[...]))
    h = jnp.sin(fr * (_dot3(h, w2_ref[...]) + b2_ref[...]))
    h = jnp.sin(fr * (_dot3(h, w3_ref[...]) + b3_ref[...]))
    total = jnp.zeros(l1_ref.shape, F32)
    for part, w_ref in enumerate((w4t_ref, w4b_ref)):
        tcol = part * FILTER_HIDDEN
        taps = _dot3(h, w_ref[...])
        decay = jnp.exp(-e[:, tcol:tcol + 1] * dl_ref[...])
        rows = r * FLT_TR + part * FLT_HALF + lax.broadcasted_iota(jnp.int32, taps.shape, 0)
        taps = jnp.where(rows == SEQ, 0.0, taps * decay)
        k_ref[part * FLT_HALF:(part + 1) * FLT_HALF, :] = taps
        total = total + jnp.sum(jnp.abs(taps), axis=0, keepdims=True)

    @pl.when(r == 0)
    def _():
        l1_ref[...] = jnp.zeros_like(l1_ref)

    l1_ref[...] += total


def _filter_taps(emb2, w1, b1, w2, b2, w3, b3, w4, freq, deltas):
    n = 2 * emb2.shape[0]
    fh = FILTER_HIDDEN
    eye2 = jnp.eye(2, dtype=F32)
    w1p = jnp.concatenate([w1, jnp.zeros((fh - w1.shape[0], fh), F32)], axis=0)
    wd = [jnp.kron(eye2, w) for w in (w1p, w2, w3)]
    bd = [jnp.tile(b, (1, 2)) for b in (b1, b2, b3, freq)]
    zeros = jnp.zeros_like(w4)
    w4t = jnp.concatenate([w4, zeros], axis=0)
    w4b = jnp.concatenate([zeros, w4], axis=0)
    small = lambda shape: pl.BlockSpec(shape, lambda r: (0, 0))
    fwd_tiles = SEQ // FLT_TR
    w4_spec = pl.BlockSpec((2 * fh, HYENA_W), lambda r: (0, r // fwd_tiles))
    return pl.pallas_call(
        _filter_kernel,
        grid=(n // FLT_TR,),
        in_specs=[pl.BlockSpec((FLT_HALF, 2 * fh), lambda r: (r, 0)),
                  small((2 * fh, 2 * fh)), small((1, 2 * fh)), small((2 * fh, 2 * fh)), small((1, 2 * fh)),
                  small((2 * fh, 2 * fh)), small((1, 2 * fh)),
                  w4_spec, w4_spec,
                  small((1, 2 * fh)), small((1, HYENA_W))],
        out_specs=[pl.BlockSpec((FLT_TR, HYENA_W), lambda r: (r, 0)),
                   pl.BlockSpec((1, HYENA_W), lambda r: (0, 0))],
        out_shape=[jax.ShapeDtypeStruct((n, HYENA_W), F32),
                   jax.ShapeDtypeStruct((1, HYENA_W), F32)],
        compiler_params=_cparams(1, 32),
        name="filter_taps",
    )(emb2, wd[0], bd[0], wd[1], bd[1], wd[2], bd[2], w4t, w4b, bd[3], deltas)


AT_TQ = 512
AT_RQ = 256
AT_TK = 512


def _attn_kernel(q_ref, k_ref, v_ref, o_ref, s0_ref, s1_ref):
    s_refs = (s0_ref, s1_ref)
    nblk = k_ref.shape[0] // AT_TK
    units = [(pl.ds(r * AT_RQ, AT_RQ), slice(g * HEAD_DIM, (g + 1) * HEAD_DIM))
             for r in range(q_ref.shape[0] // AT_RQ) for g in range(Q_PER_KV)]
    lane_blocks = lambda a: [a[:, i:i + 128] for i in range(0, a.shape[1], 128)]

    def scores(u, j, m_run):
        rows, lanes = units[u]
        ks = pl.ds(j * AT_TK, AT_TK)
        s = lax.dot_general(q_ref[rows, lanes], k_ref[ks, :], (((1,), (1,)), ((), ())),
                            preferred_element_type=F32)
        s_refs[u % 2][:, ks] = s
        blk = functools.reduce(jnp.maximum, lane_blocks(s))
        return blk if m_run is None else jnp.maximum(m_run, blk)

    def weighted(u, j, m, l_run, acc):
        ks = pl.ds(j * AT_TK, AT_TK)
        p = jnp.exp2(s_refs[u % 2][:, ks] - m)
        l_blk = functools.reduce(jnp.add, lane_blocks(p))
        o = jnp.dot(p.astype(BF16), v_ref[ks, :], preferred_element_type=F32)
        return (l_blk if l_run is None else l_run + l_blk), (o if acc is None else acc + o)

    m_run = None
    for j in range(nblk):
        m_run = scores(0, j, m_run)
    for u, (rows, lanes) in enumerate(units):
        m = jnp.max(m_run, axis=-1, keepdims=True)
        m_run, l_run, acc = None, None, None
        for j in range(nblk):
            if u + 1 < len(units):
                m_run = scores(u + 1, j, m_run)
            l_run, acc = weighted(u, j, m, l_run, acc)
        l = jnp.sum(l_run, axis=-1, keepdims=True)
        o_ref[rows, lanes] = (acc / l).astype(o_ref.dtype)


def _attention(qkv3):
    b, s, _ = qkv3.shape
    gw = Q_PER_KV * HEAD_DIM
    k0 = N_Q_HEADS
    v0 = N_Q_HEADS + N_KV_HEADS
    return pl.pallas_call(
        _attn_kernel,
        grid=(b, N_KV_HEADS, s // AT_TQ),
        in_specs=[pl.BlockSpec((None, AT_TQ, gw), lambda bi, kv, qi: (bi, qi, kv)),
                  pl.BlockSpec((None, s, HEAD_DIM), lambda bi, kv, qi: (bi, 0, k0 + kv)),
                  pl.BlockSpec((None, s, HEAD_DIM), lambda bi, kv, qi: (bi, 0, v0 + kv))],
        out_specs=pl.BlockSpec((None, AT_TQ, gw), lambda bi, kv, qi: (bi, qi, kv)),
        out_shape=jax.ShapeDtypeStruct((b, s, ATTN_W), BF16),
        scratch_shapes=[pltpu.VMEM((AT_RQ, s), F32), pltpu.VMEM((AT_RQ, s), F32)],
        compiler_params=_cparams(3, 48),
        name="attention",
    )(qkv3, qkv3, qkv3)


def _ep_residual(accs, extras, rows):
    return extras[0][rows, :] + accs[0]


def _ep_swiglu(accs, extras, rows):
    return jax.nn.silu(accs[0]) * accs[1]


def _merge_out_kernel(yh_ref, ya_ref, gt_ref, x_ref, g_ref, wh_ref, wa_ref, wo_ref, x1_ref, h2_ref):
    d = x_ref.shape[1]
    for c in range(x_ref.shape[0] // MM_RC):
        rows = pl.ds(c * MM_RC, MM_RC)
        ph = jnp.dot(yh_ref[rows, :], wh_ref[...], preferred_element_type=F32)
        pa = jnp.dot(ya_ref[rows, :], wa_ref[...], preferred_element_type=F32)
        merged = gt_ref[rows, :d].astype(F32) * ph + gt_ref[rows, d:].astype(F32) * pa
        x1 = x_ref[rows, :] + jnp.dot(merged.astype(BF16), wo_ref[...], preferred_element_type=F32)
        ms = jnp.mean(x1 * x1, axis=-1, keepdims=True)
        x1_ref[rows, :] = x1
        h2_ref[rows, :] = (x1 * lax.rsqrt(ms + EPS) * g_ref[...]).astype(h2_ref.dtype)


def _merge_out(yh, ya, gates, x, norm_g, w_h, w_a, w_o, tm=512):
    m, d = x.shape
    once = pl.Buffered(1)
    rows = lambda a: pl.BlockSpec((tm, a.shape[1]), lambda i: (i, 0))
    whole = lambda a: pl.BlockSpec(a.shape, lambda i: (0, 0), pipeline_mode=once)
    return pl.pallas_call(
        _merge_out_kernel,
        grid=(m // tm,),
        in_specs=[rows(yh), rows(ya), rows(gates), rows(x), pl.BlockSpec((1, d), lambda i: (0, 0)),
                  whole(w_h), whole(w_a), whole(w_o)],
        out_specs=[pl.BlockSpec((tm, d), lambda i: (i, 0))] * 2,
        out_shape=[jax.ShapeDtypeStruct((m, d), F32), jax.ShapeDtypeStruct((m, d), BF16)],
        compiler_params=_cparams(1, 56),
        name="merge_out",
    )(yh, ya, gates, x, norm_g, w_h, w_a, w_o)


def _ffn_down(a, x, w_bf16, tm=512, tn=1024):
    return _wres_matmul("ffn_down", [a], [(w_bf16, 0)], [(x, (tm, tn), lambda j, i: (i, j))], _ep_residual,
                        pairs=[0], n_out=w_bf16.shape[1], out_dtype=F32, tm=tm, tn=tn, vmem_mib=56)


def _swiglu(h, w_g, w_u, w_down, tm=2048):
    return _wres_matmul("swiglu", [h], [(w_g, 0), (w_u, 0)], [], _ep_swiglu,
                        pairs=[0, 0], n_out=w_g.shape[1], out_dtype=BF16, tm=tm, vmem_mib=56,
                        side_cast=w_down)


def _layer(x, mix_norm_g, w_in, b_gate, hy_conv_w, hy_conv_b,
           flt_w1, flt_b1, flt_w2, flt_b2, flt_w3, flt_b3, flt_w4, flt_freq, hy_bias,
           q_norm_g, k_norm_g, w_br_hyena, w_br_attn, w_out,
           ffn_norm_g, w_ffn_gate, w_ffn_up, w_ffn_down):
    b, s, d = x.shape
    m = b * s
    row = lambda a: a.reshape(1, -1)
    tabs = _dft_tables()
    cos, sin = (jnp.asarray(t) for t in _rope_tables_np())
    xm = x.reshape(m, d)

    h, qkv, gates, w_out_bf16 = _inproj(xm, row(mix_norm_g), w_in, row(b_gate), row(q_norm_g),
                                        row(k_norm_g), cos, sin, w_out)

    emb2, deltas = (jnp.asarray(t) for t in _filter_tables_np())
    taps, l1 = _filter_taps(emb2, flt_w1, row(flt_b1), flt_w2, row(flt_b2), flt_w3, row(flt_b3),
                            flt_w4, row(flt_freq), deltas)
    spec = _filt_spec(taps, l1, tabs)

    conv_b = row(hy_conv_b)
    x0c, w_ba_bf16 = _inproj_conv("inproj_x0", h, w_in, hy_conv_w, conv_b, [0], BF16, w_br_attn)
    z, w_bh_bf16 = _inproj_conv("inproj_z", h, w_in, hy_conv_w, conv_b, [HYENA_W, 2 * HYENA_W], F32,
                                w_br_hyena)
    y_h = _hy_conv(z.reshape(b, s, HYENA_W), x0c.reshape(b, s, HYENA_W), row(hy_bias), spec,
                   tabs).reshape(m, HYENA_W)

    y_a = _attention(qkv.reshape(b, s, ATTN_W + 2 * KV_W)).reshape(m, ATTN_W)

    x1, h2 = _merge_out(y_h, y_a, gates, xm, row(ffn_norm_g), w_bh_bf16, w_ba_bf16, w_out_bf16)
    act, w_down_bf16 = _swiglu(h2, w_ffn_gate, w_ffn_up, w_ffn_down)
    out = _ffn_down(act, x1, w_down_bf16)
    return out.reshape(b, s, d)


def kernel(x, mix_norm_g, w_in, b_gate, hy_conv_w, hy_conv_b, flt_w1, flt_b1, flt_w2, flt_b2, flt_w3, flt_b3, flt_w4, flt_freq, hy_bias, q_norm_g, k_norm_g, w_br_hyena, w_br_attn, w_out, ffn_norm_g, w_ffn_gate, w_ffn_up, w_ffn_down):
    params = (mix_norm_g, w_in, b_gate, hy_conv_w, hy_conv_b, flt_w1, flt_b1, flt_w2, flt_b2,
              flt_w3, flt_b3, flt_w4, flt_freq, hy_bias, q_norm_g, k_norm_g, w_br_hyena, w_br_attn,
              w_out, ffn_norm_g, w_ffn_gate, w_ffn_up, w_ffn_down)
    for l in range(mix_norm_g.shape[0]):
        x = _layer(x, *(p[l] for p in params))
    return x
```

```python
import functools
import math

import numpy as np
import jax
import jax.numpy as jnp
from jax import lax
from jax.experimental import pallas as pl
from jax.experimental.pallas import tpu as pltpu

F32 = jnp.float32
BF16 = jnp.bfloat16

D_MODEL = 2048
SEQ = 4096
GRID_W = 64
HEAD_DIM = 128
N_Q_HEADS = 8
N_KV_HEADS = 2
Q_PER_KV = N_Q_HEADS // N_KV_HEADS
ATTN_W = N_Q_HEADS * HEAD_DIM
KV_W = N_KV_HEADS * HEAD_DIM
ROPE_THETA = 10000.0
HYENA_W = D_MODEL - ATTN_W
SHORT_TAPS = 3
FILTER_EMB = 33
FILTER_HIDDEN = 64
DECAY_TARGET = 1e-2
FAST_DECAY_PCT = 0.3
SLOW_DECAY_PCT = 1.5
IN_W = 3 * HYENA_W + ATTN_W + 2 * KV_W + 2 * D_MODEL
D_FF = 5632
EPS = 1e-6

COL_Q = 3 * HYENA_W
COL_K = COL_Q + ATTN_W
COL_V = COL_K + KV_W
COL_G = COL_V + KV_W

FFT_N = 2 * SEQ
FFT_N1 = 64
FFT_N2 = 128

MIB = 1024 * 1024


def _cparams(n_axes, vmem_mib):
    return pltpu.CompilerParams(
        dimension_semantics=("arbitrary",) * n_axes,
        vmem_limit_bytes=vmem_mib * MIB,
    )


DFT_R = 8


@functools.lru_cache(maxsize=None)
def _dft_tables_np():
    n, n1, n2, r = FFT_N, FFT_N1, FFT_N2, DFT_R
    eye = np.eye(r)
    f1 = np.arange(n1)
    ang1 = 2.0 * np.pi * ((f1[:, None] * f1[None, :]) % n1) / n1
    c1, s1 = np.cos(ang1), np.sin(ang1)
    h = n1 // 2
    m_fwd = np.block([[c1[:, :h], s1[:, :h]], [-s1[:, :h], c1[:, :h]]])
    m_flt = np.concatenate([c1, -s1], axis=0)
    ct, st = c1[:h, :], s1[:h, :]
    m_inv = np.block([[ct, -st], [st, ct]]) / n
    s2 = np.arange(n2)
    f = f1[:, None, None] + n1 * s2[None, :, None]
    th = 2.0 * np.pi * ((f * s2[None, None, :]) % n) / n
    c, s = np.cos(th), np.sin(th)
    fwd2 = np.concatenate(
        [np.concatenate([c, s], axis=2), np.concatenate([-s, c], axis=2)], axis=1)
    c_t, s_t = np.transpose(c, (0, 2, 1)), np.transpose(s, (0, 2, 1))
    inv2 = np.concatenate(
        [np.concatenate([c_t, -s_t], axis=2), np.concatenate([s_t, c_t], axis=2)], axis=1)
    f32 = lambda a: np.ascontiguousarray(a, dtype=np.float32)
    return dict(k_fwd=f32(np.kron(m_fwd, eye)), k_flt=f32(np.kron(m_flt, eye)),
                k_inv=f32(np.kron(m_inv, eye)), fwd2=f32(fwd2), inv2=f32(inv2))


def _dft_tables():
    return {k: jnp.asarray(v).astype(BF16) for k, v in _dft_tables_np().items()}


@functools.lru_cache(maxsize=None)
def _rope_tables_np():
    half = HEAD_DIM // 2
    inv = ROPE_THETA ** (-np.arange(0, half, 2, dtype=np.float64) / half)
    pos = np.arange(SEQ)
    ang_r = (pos // GRID_W)[:, None] * inv[None, :]
    ang_c = (pos % GRID_W)[:, None] * inv[None, :]
    cos = np.concatenate([np.cos(ang_r)] * 2 + [np.cos(ang_c)] * 2, axis=-1)
    sin = np.concatenate([-np.sin(ang_r), np.sin(ang_r), -np.sin(ang_c), np.sin(ang_c)], axis=-1)
    return cos.astype(np.float32), sin.astype(np.float32)


@functools.lru_cache(maxsize=None)
def _filter_tables_np():
    L = SEQ
    bands = (FILTER_EMB - 1) // 2
    pos = np.concatenate([np.arange(L, dtype=np.float64), L - np.arange(L, dtype=np.float64)])
    t = pos / max(L - 1, 1)
    fb = np.linspace(1e-4, bands - 1, bands)
    ang = (2.0 * math.pi * pos / L)[:, None] * fb[None, :]
    emb = np.concatenate([t[:, None], np.cos(ang), -np.sin(ang),
                          np.zeros((2 * L, FILTER_HIDDEN - FILTER_EMB))], axis=-1)
    max_decay = math.log(DECAY_TARGET) / FAST_DECAY_PCT
    min_decay = math.log(DECAY_TARGET) / SLOW_DECAY_PCT
    deltas = np.abs(np.linspace(min_decay, max_decay, HYENA_W))
    tile, hid = FLT_TR, FILTER_HIDDEN
    emb2 = emb.reshape(2 * L // tile, 2, tile // 2, hid).transpose(0, 2, 1, 3).reshape(L, 2 * hid)
    return np.ascontiguousarray(emb2, dtype=np.float32), deltas[None, :].astype(np.float32)


MM_TN = 512
MM_RC = 256


def _wres_kernel(*refs, pairs, n_act, n_extra, n_out, n_scratch, rc, epilogue, norm_first, side_cast,
                 side_call):
    n_w = len(pairs)
    acts = refs[:n_act]
    ws = refs[n_act:n_act + n_w]
    extras = refs[n_act + n_w:n_act + n_w + n_extra]
    o_refs = refs[n_act + n_w + n_extra:n_act + n_w + n_extra + n_out]
    wbs = refs[n_act + n_w + n_extra + n_out:]
    side_stages = []
    if side_call:
        fn, n_in, n_res = side_call
        step = pl.program_id(0) * pl.num_programs(1) + pl.program_id(1)
        side_stages = fn(step, *extras[-n_in:], *o_refs[-n_res:])
        extras, o_refs = extras[:-n_in], o_refs[:-n_res]
    if side_cast:
        o_refs[-1][...] = extras[-1][...].astype(BF16)
        extras, o_refs = extras[:-1], o_refs[:-1]
    if norm_first:
        g_ref, extras = extras[-1], extras[:-1]
        h_ref, o_refs = o_refs[-1], o_refs[:-1]

    if n_scratch:
        @pl.when(pl.program_id(1) == 0)
        def _():
            for w_ref, wb_ref in zip(ws, wbs):
                wb_ref[...] = w_ref[...].astype(BF16)
    else:
        wbs = ws

    for c in range(o_refs[0].shape[0] // rc):
        rows = pl.ds(c * rc, rc)
        lhs = [a[rows, :] for a in acts]
        if norm_first:
            x = lhs[0]
            ms = jnp.mean(x * x, axis=-1, keepdims=True)
            lhs[0] = (x * lax.rsqrt(ms + EPS) * g_ref[...]).astype(BF16)
            h_ref[rows, :] = lhs[0]
        accs = [jnp.dot(lhs[a], wb_ref[...], preferred_element_type=F32)
                for a, wb_ref in zip(pairs, wbs)]
        if c < len(side_stages):
            side_stages[c]()
        outs = epilogue(accs, extras, rows)
        for o_ref, out in zip(o_refs, outs if isinstance(outs, tuple) else (outs,)):
            o_ref[rows, :] = out.astype(o_ref.dtype)
    assert len(side_stages) <= o_refs[0].shape[0] // rc


def _wres_matmul(name, acts, weights, extras, epilogue, *, pairs, n_out, out_dtype, tm,
                 tn=MM_TN, rc=MM_RC, vmem_mib=48, norm_gain=None, side_cast=None, side_call=None):
    m = acts[0].shape[0]
    grid = (n_out // tn, m // tm)
    out_dtypes = out_dtype if isinstance(out_dtype, tuple) else (out_dtype,)
    out_specs = [pl.BlockSpec((tm, tn), lambda j, i: (i, j)) for _ in out_dtypes]
    out_shapes = [jax.ShapeDtypeStruct((m, n_out), dt) for dt in out_dtypes]
    extras = list(extras)
    if norm_gain is not None:
        assert n_out == tn
        k0 = acts[0].shape[1]
        extras.append((norm_gain, (1, k0), lambda j, i: (0, 0)))
        out_specs.append(pl.BlockSpec((tm, k0), lambda j, i: (i, 0)))
        out_shapes.append(jax.ShapeDtypeStruct((m, k0), BF16))
    if side_cast is not None:
        slab = side_cast.shape[0] // (grid[0] * grid[1])
        assert slab * grid[0] * grid[1] == side_cast.shape[0] and slab % 16 == 0
        step = lambda j, i: (j * grid[1] + i, 0)
        extras.append((side_cast, (slab, side_cast.shape[1]), step))
        out_specs.append(pl.BlockSpec((slab, side_cast.shape[1]), step))
        out_shapes.append(jax.ShapeDtypeStruct(side_cast.shape, BF16))
    side = side_call(grid) if side_call is not None else None
    if side is not None:
        extras += side["inputs"]
        out_specs += [pl.BlockSpec(blk, imap) for _, blk, imap in side["outputs"]]
        out_shapes += [shape for shape, _, _ in side["outputs"]]
    in_specs = [pl.BlockSpec((tm, a.shape[1]), lambda j, i: (i, 0)) for a in acts]
    assert all(off % 128 == 0 for _, off in weights) and tn % 128 == 0
    w_mode = dict(pipeline_mode=pl.Buffered(1)) if n_out == tn else {}
    in_specs += [pl.BlockSpec((pl.Element(w.shape[0]), pl.Element(tn)),
                              lambda j, i, off=off: (0, pl.multiple_of(off + j * tn, 128)), **w_mode)
                 for w, off in weights]
    in_specs += [pl.BlockSpec(blk, imap) for _, blk, imap in extras]
    scratch = [pltpu.VMEM((w.shape[0], tn), BF16) for w, _ in weights if w.dtype != BF16]
    assert len(scratch) in (0, len(weights))
    kern = functools.partial(_wres_kernel, pairs=tuple(pairs), n_act=len(acts), n_extra=len(extras),
                             n_out=len(out_specs), n_scratch=len(scratch), rc=rc, epilogue=epilogue,
                             norm_first=norm_gain is not None, side_cast=side_cast is not None,
                             side_call=(side["fn"], len(side["inputs"]), len(side["outputs"])) if side else None)
    outs = pl.pallas_call(
        kern,
        grid=grid,
        in_specs=in_specs,
        out_specs=out_specs,
        out_shape=out_shapes,
        scratch_shapes=scratch,
        compiler_params=_cparams(2, vmem_mib),
        name=name,
    )(*acts, *(w for w, _ in weights), *(e for e, _, _ in extras))
    return outs if len(outs) > 1 else outs[0]


def _head_norm_rope(x, g, cos, sin, scale):
    ms = jnp.mean(x * x, axis=-1, keepdims=True)
    y = x * lax.rsqrt(ms + EPS) * g
    lane = lax.broadcasted_iota(jnp.int32, y.shape, 1)
    swapped = jnp.where((lane % 64) < 32, pltpu.roll(y, 96, 1), pltpu.roll(y, 32, 1))
    out = y * cos + swapped * sin
    return out * scale if scale != 1.0 else out


def _ep_qkv(accs, extras, rows):
    qg_ref, kg_ref, cos_ref, sin_ref = extras
    acc = accs[0]
    cos, sin = cos_ref[rows, :], sin_ref[rows, :]
    head = lambda hh: acc[:, hh * HEAD_DIM:(hh + 1) * HEAD_DIM]
    parts = [_head_norm_rope(head(hh), qg_ref[...], cos, sin, HEAD_DIM ** -0.5 * LOG2E)
             for hh in range(N_Q_HEADS)]
    parts += [_head_norm_rope(head(N_Q_HEADS + hh), kg_ref[...], cos, sin, 1.0) for hh in range(N_KV_HEADS)]
    parts.append(acc[:, (N_Q_HEADS + N_KV_HEADS) * HEAD_DIM:])
    return jnp.concatenate(parts, axis=1)


def _ep_gate(accs, extras, rows):
    return jax.nn.sigmoid(accs[0] + extras[0][...])


IN_TM = 2048
IN_TN = 1024
LOG2E = math.log2(math.e)


def _inproj(x, norm_g, w_in, b_gate, q_g, k_g, cos, sin, w_out, filter_side):
    tm, tn = IN_TM, IN_TN
    qkv_w = ATTN_W + 2 * KV_W
    qkv_tm = tm // 2
    head = lambda g: (g, (1, HEAD_DIM), lambda j, i: (0, 0))
    pos = lambda t: (t, (qkv_tm, HEAD_DIM), lambda j, i: (i % (SEQ // qkv_tm), 0))
    common = dict(pairs=[0], out_dtype=BF16, vmem_mib=56)
    qkv, h = _wres_matmul("inproj_qkv", [x], [(w_in, COL_Q)], [head(q_g), head(k_g), pos(cos), pos(sin)],
                          _ep_qkv, n_out=qkv_w, tn=qkv_w, tm=qkv_tm, norm_gain=norm_g, **common)
    gates, w_out_bf16, taps, l1 = _wres_matmul(
        "inproj_gate", [h], [(w_in, COL_G)], [(b_gate, (1, tn), lambda j, i: (0, j))], _ep_gate,
        n_out=2 * D_MODEL, tn=tn, tm=tm, side_cast=w_out, side_call=filter_side, **common)
    return h, qkv, gates, w_out_bf16, taps, l1


HC_TM = 1024
HC_RC = 512
HC_HALO = 16


def _inproj_conv_kernel(*refs, n_w):
    h_ref, top_ref, bot_ref = refs[:3]
    w_refs = refs[3:3 + n_w]
    cw_refs = refs[3 + n_w:3 + 2 * n_w]
    cb_refs = refs[3 + 2 * n_w:3 + 3 * n_w]
    side_ref, o_ref, side_out_ref = refs[3 + 3 * n_w:6 + 3 * n_w]
    wb_refs = refs[6 + 3 * n_w:]
    side_out_ref[...] = side_ref[...].astype(BF16)
    i = pl.program_id(1)
    tm, rc, hl = o_ref.shape[0], HC_RC, HC_HALO

    @pl.when(i == 0)
    def _():
        for w_ref, wb_ref in zip(w_refs, wb_refs):
            wb_ref[...] = w_ref[...].astype(BF16)

    seq_blocks = SEQ // tm
    top = jnp.where(i % seq_blocks != 0, top_ref[...], jnp.zeros_like(top_ref))
    bot = jnp.where(i % seq_blocks != seq_blocks - 1, bot_ref[...], jnp.zeros_like(bot_ref))
    nchunk = tm // rc
    for c in range(nchunk):
        lo, hi = c * rc - hl, (c + 1) * rc + hl
        parts = ([top] if c == 0 else []) + [h_ref[max(lo, 0):min(hi, tm), :]] + ([bot] if c == nchunk - 1 else [])
        lhs = jnp.concatenate(parts, axis=0) if len(parts) > 1 else parts[0]
        outs = []
        for wb_ref, cw_ref, cb_ref in zip(wb_refs, cw_refs, cb_refs):
            e = jnp.dot(lhs, wb_ref[...], preferred_element_type=F32)
            up = pltpu.roll(e, 1, 0)[hl:hl + rc]
            dn = pltpu.roll(e, rc + 2 * hl - 1, 0)[hl:hl + rc]
            outs.append(cb_ref[...] + up * cw_ref[0:1, :] + e[hl:hl + rc] * cw_ref[1:2, :]
                        + dn * cw_ref[2:3, :])
        res = outs[0] if n_w == 1 else outs[0] * outs[1]
        o_ref[c * rc:(c + 1) * rc, :] = res.astype(o_ref.dtype)


def _inproj_conv(name, h, w_in, conv_w, conv_b, col_offs, out_dtype, side_cast):
    m, d = h.shape
    tm, tn, hl = HC_TM, HYENA_W, HC_HALO
    n_w = len(col_offs)
    once = pl.Buffered(1)
    blk = lambda off: off // tn
    slab = side_cast.shape[0] // (m // tm)
    assert slab * (m // tm) == side_cast.shape[0] and slab % 16 == 0
    side_spec = pl.BlockSpec((slab, side_cast.shape[1]), lambda j, i: (i, 0))
    in_specs = [pl.BlockSpec((tm, d), lambda j, i: (i, 0)),
                pl.BlockSpec((hl, d), lambda j, i: (jnp.maximum(i * (tm // hl) - 1, 0), 0)),
                pl.BlockSpec((hl, d), lambda j, i: (jnp.minimum((i + 1) * (tm // hl), m // hl - 1), 0))]
    in_specs += [pl.BlockSpec((d, tn), lambda j, i, o=o: (0, blk(o)), pipeline_mode=once) for o in col_offs]
    in_specs += [pl.BlockSpec((SHORT_TAPS, tn), lambda j, i, o=o: (0, blk(o))) for o in col_offs]
    in_specs += [pl.BlockSpec((1, tn), lambda j, i, o=o: (0, blk(o))) for o in col_offs]
    in_specs.append(side_spec)
    return pl.pallas_call(
        functools.partial(_inproj_conv_kernel, n_w=n_w),
        grid=(1, m // tm),
        in_specs=in_specs,
        out_specs=[pl.BlockSpec((tm, tn), lambda j, i: (i, 0)), side_spec],
        out_shape=[jax.ShapeDtypeStruct((m, tn), out_dtype), jax.ShapeDtypeStruct(side_cast.shape, BF16)],
        scratch_shapes=[pltpu.VMEM((d, tn), BF16) for _ in col_offs],
        compiler_params=_cparams(2, 56),
        name=name,
    )(h, h, h, *([w_in] * n_w), *([conv_w] * n_w), *([conv_b] * n_w), side_cast)


CV_CT = 256
CV_F1B = 16
CV_NF = FFT_N1 // CV_F1B
CV_SLAB = 2 * DFT_R
CV_NSLAB = FFT_N2 // CV_SLAB
CV_HALF = FFT_N1 // 2


def _outer_fwd_slab(src_ref, k_ref, a_ref, j, rows_in):
    ct = a_ref.shape[-1]
    halves = []
    for h in range(2):
        r = src_ref[:, :, 2 * j + h] if src_ref.ndim == 5 else src_ref[:, 2 * j + h]
        r = r.reshape(rows_in, ct).astype(BF16)
        o = jnp.dot(k_ref[...], r, preferred_element_type=F32)
        halves.append(o.reshape(2 * FFT_N1, DFT_R, ct))
    slab = jnp.concatenate(halves, axis=1).astype(BF16)
    a_ref[:, :, pl.ds(pl.multiple_of(j * CV_SLAB, CV_SLAB), CV_SLAB), :] = slab.reshape(
        2, FFT_N1, CV_SLAB, ct)


def _hy_conv_kernel(z_ref, x0_ref, bias_ref, kf_ref, ki_ref, fw_ref, iv_ref, spec_ref, o_ref, a_ref):
    s = pl.program_id(2)
    ct = a_ref.shape[-1]

    @pl.when(s == 0)
    def _():
        def body(j, carry):
            _outer_fwd_slab(z_ref, kf_ref, a_ref, j, 2 * CV_HALF * DFT_R)
            return carry

        lax.fori_loop(0, CV_NSLAB, body, 0)

    @pl.when((s >= 1) & (s <= CV_NF))
    def _():
        f0 = (s - 1) * CV_F1B
        for fl in range(CV_F1B):
            x = a_ref[:, f0 + fl].reshape(2 * FFT_N2, ct)
            u = jnp.dot(fw_ref[fl], x, preferred_element_type=F32)
            ure, uim = u[:FFT_N2], u[FFT_N2:]
            kre, kim = spec_ref[0, fl].astype(F32), spec_ref[1, fl].astype(F32)
            p = jnp.concatenate([ure * kre - uim * kim, ure * kim + uim * kre], axis=0).astype(BF16)
            y = jnp.dot(iv_ref[fl], p, preferred_element_type=F32)
            a_ref[:, f0 + fl] = y.reshape(2, FFT_N2, ct).astype(BF16)

    @pl.when(s == CV_NF + 1)
    def _():
        bias = bias_ref[...]

        def body(j, carry):
            slab = a_ref[:, :, pl.ds(pl.multiple_of(j * CV_SLAB, CV_SLAB), CV_SLAB), :].astype(F32)
            x0 = x0_ref[:, :, j].astype(F32)
            halves = []
            for h in range(2):
                r = slab[:, :, h * DFT_R:(h + 1) * DFT_R, :].reshape(2 * FFT_N1 * DFT_R, ct).astype(BF16)
                y = jnp.dot(ki_ref[...], r, preferred_element_type=F32).reshape(2, CV_HALF, DFT_R, ct)
                z = z_ref[:, :, 2 * j + h]
                halves.append((y + bias * z) * x0[:, :, h * DFT_R:(h + 1) * DFT_R, :])
            o_ref[:, :, j] = jnp.concatenate(halves, axis=2).astype(o_ref.dtype)
            return carry

        lax.fori_loop(0, CV_NSLAB, body, 0)


def _hy_conv(z, x0c, bias, spec, tabs):
    b, L, c = z.shape
    assert b % 2 == 0 and L * 2 == FFT_N
    ct = CV_CT
    z5 = z.reshape(b, CV_HALF, FFT_N2 // DFT_R, DFT_R, c)
    x5 = x0c.reshape(b, CV_HALF, CV_NSLAB, CV_SLAB, c)
    fidx = lambda s: jnp.clip(s - 1, 0, CV_NF - 1)
    once = pl.Buffered(1)
    tab_spec = pl.BlockSpec((CV_F1B, 2 * FFT_N2, 2 * FFT_N2), lambda p, cb, s: (fidx(s), 0, 0))
    io16 = pl.BlockSpec((2, CV_HALF, CV_NSLAB, CV_SLAB, ct), lambda p, cb, s: (p, 0, 0, 0, cb))
    out = pl.pallas_call(
        _hy_conv_kernel,
        grid=(b // 2, c // ct, CV_NF + 2),
        in_specs=[
            pl.BlockSpec((2, CV_HALF, FFT_N2 // DFT_R, DFT_R, ct), lambda p, cb, s: (p, 0, 0, 0, cb)),
            io16,
            pl.BlockSpec((1, ct), lambda p, cb, s: (0, cb)),
            pl.BlockSpec(tabs["k_fwd"].shape, lambda p, cb, s: (0, 0), pipeline_mode=once),
            pl.BlockSpec(tabs["k_inv"].shape, lambda p, cb, s: (0, 0), pipeline_mode=once),
            tab_spec, tab_spec,
            pl.BlockSpec((2, CV_F1B, FFT_N2, ct), lambda p, cb, s: (0, fidx(s), 0, cb)),
        ],
        out_specs=io16,
        out_shape=jax.ShapeDtypeStruct(x5.shape, BF16),
        scratch_shapes=[pltpu.VMEM((2, FFT_N1, FFT_N2, ct), BF16)],
        compiler_params=_cparams(3, 60),
        name="hyena_conv",
    )(z5, x5, bias, tabs["k_fwd"], tabs["k_inv"], tabs["fwd2"], tabs["inv2"], spec)
    return out.reshape(b, L, c)


def _filt_spec_kernel(t_ref, l1_ref, kf_ref, fw_ref, o_ref, a_ref):
    s = pl.program_id(1)
    ct = a_ref.shape[-1]

    @pl.when(s == 0)
    def _():
        def body(j, carry):
            _outer_fwd_slab(t_ref, kf_ref, a_ref, j, FFT_N1 * DFT_R)
            return carry

        lax.fori_loop(0, CV_NSLAB, body, 0)

    @pl.when(s >= 1)
    def _():
        inv_l1 = 1.0 / l1_ref[...]
        f0 = (s - 1) * FS_F1B
        for fl in range(FS_F1B):
            x = a_ref[:, f0 + fl].reshape(2 * FFT_N2, ct)
            y = jnp.dot(fw_ref[fl], x, preferred_element_type=F32) * inv_l1
            o_ref[:, fl] = y.reshape(2, FFT_N2, ct).astype(o_ref.dtype)


FS_F1B = 16
FS_NF = FFT_N1 // FS_F1B


def _filt_spec(taps, l1, tabs):
    n, c = taps.shape
    ct = CV_CT
    t4 = taps.reshape(FFT_N1, FFT_N2 // DFT_R, DFT_R, c)
    fidx = lambda s: jnp.clip(s - 1, 0, FS_NF - 1)
    once = pl.Buffered(1)
    return pl.pallas_call(
        _filt_spec_kernel,
        grid=(c // ct, FS_NF + 1),
        in_specs=[
            pl.BlockSpec((FFT_N1, FFT_N2 // DFT_R, DFT_R, ct), lambda cb, s: (0, 0, 0, cb)),
            pl.BlockSpec((1, ct), lambda cb, s: (0, cb)),
            pl.BlockSpec(tabs["k_flt"].shape, lambda cb, s: (0, 0), pipeline_mode=once),
            pl.BlockSpec((FS_F1B, 2 * FFT_N2, 2 * FFT_N2), lambda cb, s: (fidx(s), 0, 0)),
        ],
        out_specs=pl.BlockSpec((2, FS_F1B, FFT_N2, ct), lambda cb, s: (0, fidx(s), 0, cb)),
        out_shape=jax.ShapeDtypeStruct((2, FFT_N1, FFT_N2, c), BF16),
        scratch_shapes=[pltpu.VMEM((2, FFT_N1, FFT_N2, ct), BF16)],
        compiler_params=_cparams(2, 48),
        name="filter_spectrum",
    )(t4, l1, tabs["k_flt"], tabs["fwd2"])


FLT_TR = 512


FLT_HALF = FLT_TR // 2


def _dot3(a, w):
    a_hi = a.astype(BF16)
    a_lo = (a - a_hi.astype(F32)).astype(BF16)
    w_hi = w.astype(BF16)
    w_lo = (w - w_hi.astype(F32)).astype(BF16)
    lhs = jnp.concatenate([a_hi, a_hi, a_lo], axis=1)
    rhs = jnp.concatenate([w_hi, w_lo, w_hi], axis=0)
    return jnp.dot(lhs, rhs, preferred_element_type=F32)


def _filter_step(r, emb_ref, w1_ref, b1_ref, w2_ref, b2_ref, w3_ref, b3_ref, w4t_ref, w4b_ref, fr_ref, dl_ref,
                 k_ref, l1_ref):
    st = {}

    def hidden(w_ref, b_ref, src):
        def stage():
            st["h"] = jnp.sin(fr_ref[...] * (_dot3(st[src] if src == "h" else emb_ref[...], w_ref[...])
                                             + b_ref[...]))
        return stage

    def taps_half(part, w_ref):
        def stage():
            tcol = part * FILTER_HIDDEN
            taps = jnp.dot(st["h"].astype(BF16), w_ref[...].astype(BF16),
                           preferred_element_type=F32)
            decay = jnp.exp(-emb_ref[:, tcol:tcol + 1] * dl_ref[...])
            rows = r * FLT_TR + part * FLT_HALF + lax.broadcasted_iota(jnp.int32, taps.shape, 0)
            taps = jnp.where(rows == SEQ, 0.0, taps * decay)
            k_ref[part * FLT_HALF:(part + 1) * FLT_HALF, :] = taps
            total = jnp.sum(jnp.abs(taps), axis=0, keepdims=True)
            if part == 0:
                st["l1"] = jnp.where(r == 0, 0.0, l1_ref[...]) + total
            else:
                l1_ref[...] = st["l1"] + total
        return stage

    return [hidden(w1_ref, b1_ref, "emb"), hidden(w2_ref, b2_ref, "h"), hidden(w3_ref, b3_ref, "h"),
            taps_half(0, w4t_ref), taps_half(1, w4b_ref)]


def _filter_side_call(emb2, w1, b1, w2, b2, w3, b3, w4, freq, deltas, steps):
    n = 2 * emb2.shape[0]
    assert n // FLT_TR == steps[0] * steps[1]
    fh = FILTER_HIDDEN
    eye2 = jnp.eye(2, dtype=F32)
    w1p = jnp.concatenate([w1, jnp.zeros((fh - w1.shape[0], fh), F32)], axis=0)
    wd = [jnp.kron(eye2, w) for w in (w1p, w2, w3)]
    bd = [jnp.tile(b, (1, 2)) for b in (b1, b2, b3, freq)]
    zeros = jnp.zeros_like(w4)
    w4t = jnp.concatenate([w4, zeros], axis=0)
    w4b = jnp.concatenate([zeros, w4], axis=0)
    step = lambda j, i: j * steps[1] + i
    const = lambda j, i: (0, 0)
    small = lambda a: (a, a.shape, const)
    fwd_tiles = SEQ // FLT_TR
    w4_half = lambda a: (a, (2 * fh, HYENA_W), lambda j, i: (0, step(j, i) // fwd_tiles))
    inputs = [(emb2, (FLT_HALF, 2 * fh), lambda j, i: (step(j, i), 0)),
              small(wd[0]), small(bd[0]), small(wd[1]), small(bd[1]), small(wd[2]), small(bd[2]),
              w4_half(w4t), w4_half(w4b), small(bd[3]), small(deltas)]
    outputs = [(jax.ShapeDtypeStruct((n, HYENA_W), F32), (FLT_TR, HYENA_W), lambda j, i: (step(j, i), 0)),
               (jax.ShapeDtypeStruct((1, HYENA_W), F32), (1, HYENA_W), const)]
    return dict(fn=_filter_step, inputs=inputs, outputs=outputs)


AT_TQ = 512
AT_RQ = 256
AT_TK = 512


def _attn_kernel(q_ref, k_ref, v_ref, o_ref, s0_ref, s1_ref):
    s_refs = (s0_ref, s1_ref)
    nblk = k_ref.shape[0] // AT_TK
    units = [(pl.ds(r * AT_RQ, AT_RQ), slice(g * HEAD_DIM, (g + 1) * HEAD_DIM))
             for r in range(q_ref.shape[0] // AT_RQ) for g in range(Q_PER_KV)]
    lane_blocks = lambda a: [a[:, i:i + 128] for i in range(0, a.shape[1], 128)]

    def scores(u, j, m_run):
        rows, lanes = units[u]
        ks = pl.ds(j * AT_TK, AT_TK)
        s = lax.dot_general(q_ref[rows, lanes], k_ref[ks, :], (((1,), (1,)), ((), ())),
                            preferred_element_type=F32)
        s_refs[u % 2][:, ks] = s
        blk = functools.reduce(jnp.maximum, lane_blocks(s))
        return blk if m_run is None else jnp.maximum(m_run, blk)

    def weighted(u, j, m, l_run, acc):
        ks = pl.ds(j * AT_TK, AT_TK)
        p = jnp.exp2(s_refs[u % 2][:, ks] - m)
        l_blk = functools.reduce(jnp.add, lane_blocks(p))
        o = jnp.dot(p.astype(BF16), v_ref[ks, :], preferred_element_type=F32)
        return (l_blk if l_run is None else l_run + l_blk), (o if acc is None else acc + o)

    m_run = None
    for j in range(nblk):
        m_run = scores(0, j, m_run)
    for u, (rows, lanes) in enumerate(units):
        m = jnp.max(m_run, axis=-1, keepdims=True)
        m_run, l_run, acc = None, None, None
        for j in range(nblk):
            if u + 1 < len(units):
                m_run = scores(u + 1, j, m_run)
            l_run, acc = weighted(u, j, m, l_run, acc)
        l = jnp.sum(l_run, axis=-1, keepdims=True)
        o_ref[rows, lanes] = (acc / l).astype(o_ref.dtype)


def _attention(qkv3):
    b, s, _ = qkv3.shape
    gw = Q_PER_KV * HEAD_DIM
    k0 = N_Q_HEADS
    v0 = N_Q_HEADS + N_KV_HEADS
    return pl.pallas_call(
        _attn_kernel,
        grid=(b, N_KV_HEADS, s // AT_TQ),
        in_specs=[pl.BlockSpec((None, AT_TQ, gw), lambda bi, kv, qi: (bi, qi, kv)),
                  pl.BlockSpec((None, s, HEAD_DIM), lambda bi, kv, qi: (bi, 0, k0 + kv)),
                  pl.BlockSpec((None, s, HEAD_DIM), lambda bi, kv, qi: (bi, 0, v0 + kv))],
        out_specs=pl.BlockSpec((None, AT_TQ, gw), lambda bi, kv, qi: (bi, qi, kv)),
        out_shape=jax.ShapeDtypeStruct((b, s, ATTN_W), BF16),
        scratch_shapes=[pltpu.VMEM((AT_RQ, s), F32), pltpu.VMEM((AT_RQ, s), F32)],
        compiler_params=_cparams(3, 48),
        name="attention",
    )(qkv3, qkv3, qkv3)


def _ep_residual(accs, extras, rows):
    return extras[0][rows, :] + accs[0]


def _ep_swiglu(accs, extras, rows):
    return jax.nn.silu(accs[0]) * accs[1]


def _merge_out_kernel(yh_ref, ya_ref, gt_ref, x_ref, g_ref, wh_ref, wa_ref, wo_ref, x1_ref, h2_ref):
    d = x_ref.shape[1]
    for c in range(x_ref.shape[0] // MM_RC):
        rows = pl.ds(c * MM_RC, MM_RC)
        ph = jnp.dot(yh_ref[rows, :], wh_ref[...], preferred_element_type=F32)
        pa = jnp.dot(ya_ref[rows, :], wa_ref[...], preferred_element_type=F32)
        merged = gt_ref[rows, :d].astype(F32) * ph + gt_ref[rows, d:].astype(F32) * pa
        x1 = x_ref[rows, :] + jnp.dot(merged.astype(BF16), wo_ref[...], preferred_element_type=F32)
        ms = jnp.mean(x1 * x1, axis=-1, keepdims=True)
        x1_ref[rows, :] = x1
        h2_ref[rows, :] = (x1 * lax.rsqrt(ms + EPS) * g_ref[...]).astype(h2_ref.dtype)


def _merge_out(yh, ya, gates, x, norm_g, w_h, w_a, w_o, tm=512):
    m, d = x.shape
    once = pl.Buffered(1)
    rows = lambda a: pl.BlockSpec((tm, a.shape[1]), lambda i: (i, 0))
    whole = lambda a: pl.BlockSpec(a.shape, lambda i: (0, 0), pipeline_mode=once)
    return pl.pallas_call(
        _merge_out_kernel,
        grid=(m // tm,),
        in_specs=[rows(yh), rows(ya), rows(gates), rows(x), pl.BlockSpec((1, d), lambda i: (0, 0)),
                  whole(w_h), whole(w_a), whole(w_o)],
        out_specs=[pl.BlockSpec((tm, d), lambda i: (i, 0))] * 2,
        out_shape=[jax.ShapeDtypeStruct((m, d), F32), jax.ShapeDtypeStruct((m, d), BF16)],
        compiler_params=_cparams(1, 56),
        name="merge_out",
    )(yh, ya, gates, x, norm_g, w_h, w_a, w_o)


def _ffn_down(a, x, w_bf16, tm=512, tn=1024):
    return _wres_matmul("ffn_down", [a], [(w_bf16, 0)], [(x, (tm, tn), lambda j, i: (i, j))], _ep_residual,
                        pairs=[0], n_out=w_bf16.shape[1], out_dtype=F32, tm=tm, tn=tn, vmem_mib=56)


def _swiglu(h, w_g, w_u, w_down, tm=2048):
    return _wres_matmul("swiglu", [h], [(w_g, 0), (w_u, 0)], [], _ep_swiglu,
                        pairs=[0, 0], n_out=w_g.shape[1], out_dtype=BF16, tm=tm, vmem_mib=56,
                        side_cast=w_down)


def _layer(x, mix_norm_g, w_in, b_gate, hy_conv_w, hy_conv_b,
           flt_w1, flt_b1, flt_w2, flt_b2, flt_w3, flt_b3, flt_w4, flt_freq, hy_bias,
           q_norm_g, k_norm_g, w_br_hyena, w_br_attn, w_out,
           ffn_norm_g, w_ffn_gate, w_ffn_up, w_ffn_down):
    b, s, d = x.shape
    m = b * s
    row = lambda a: a.reshape(1, -1)
    tabs = _dft_tables()
    cos, sin = (jnp.asarray(t) for t in _rope_tables_np())
    xm = x.reshape(m, d)

    emb2, deltas = (jnp.asarray(t) for t in _filter_tables_np())
    filter_side = functools.partial(_filter_side_call, emb2, flt_w1, row(flt_b1), flt_w2, row(flt_b2),
                                    flt_w3, row(flt_b3), flt_w4, row(flt_freq), deltas)
    h, qkv, gates, w_out_bf16, taps, l1 = _inproj(xm, row(mix_norm_g), w_in, row(b_gate), row(q_norm_g),
                                                  row(k_norm_g), cos, sin, w_out, filter_side)
    spec = _filt_spec(taps, l1, tabs)

    conv_b = row(hy_conv_b)
    x0c, w_ba_bf16 = _inproj_conv("inproj_x0", h, w_in, hy_conv_w, conv_b, [0], BF16, w_br_attn)
    z, w_bh_bf16 = _inproj_conv("inproj_z", h, w_in, hy_conv_w, conv_b, [HYENA_W, 2 * HYENA_W], F32,
                                w_br_hyena)
    y_h = _hy_conv(z.reshape(b, s, HYENA_W), x0c.reshape(b, s, HYENA_W), row(hy_bias), spec,
                   tabs).reshape(m, HYENA_W)

    y_a = _attention(qkv.reshape(b, s, ATTN_W + 2 * KV_W)).reshape(m, ATTN_W)

    x1, h2 = _merge_out(y_h, y_a, gates, xm, row(ffn_norm_g), w_bh_bf16, w_ba_bf16, w_out_bf16)
    act, w_down_bf16 = _swiglu(h2, w_ffn_gate, w_ffn_up, w_ffn_down)
    out = _ffn_down(act, x1, w_down_bf16)
    return out.reshape(b, s, d)


def kernel(x, mix_norm_g, w_in, b_gate, hy_conv_w, hy_conv_b, flt_w1, flt_b1, flt_w2, flt_b2, flt_w3, flt_b3, flt_w4, flt_freq, hy_bias, q_norm_g, k_norm_g, w_br_hyena, w_br_attn, w_out, ffn_norm_g, w_ffn_gate, w_ffn_up, w_ffn_down):
    params = (mix_norm_g, w_in, b_gate, hy_conv_w, hy_conv_b, flt_w1, flt_b1, flt_w2, flt_b2,
              flt_w3, flt_b3, flt_w4, flt_freq, hy_bias, q_norm_g, k_norm_g, w_br_hyena, w_br_attn,
              w_out, ffn_norm_g, w_ffn_gate, w_ffn_up, w_ffn_down)
    for l in range(mix_norm_g.shape[0]):
        x = _layer(x, *(p[l] for p in params))
    return x
```

```python
import functools
import math

import numpy as np
import jax
import jax.numpy as jnp
from jax import lax
from jax.experimental import pallas as pl
from jax.experimental.pallas import tpu as pltpu

F32 = jnp.float32
BF16 = jnp.bfloat16

D_MODEL = 2048
SEQ = 4096
GRID_W = 64
HEAD_DIM = 128
N_Q_HEADS = 8
N_KV_HEADS = 2
Q_PER_KV = N_Q_HEADS // N_KV_HEADS
ATTN_W = N_Q_HEADS * HEAD_DIM
KV_W = N_KV_HEADS * HEAD_DIM
ROPE_THETA = 10000.0
HYENA_W = D_MODEL - ATTN_W
SHORT_TAPS = 3
FILTER_EMB = 33
FILTER_HIDDEN = 64
DECAY_TARGET = 1e-2
FAST_DECAY_PCT = 0.3
SLOW_DECAY_PCT = 1.5
IN_W = 3 * HYENA_W + ATTN_W + 2 * KV_W + 2 * D_MODEL
D_FF = 5632
EPS = 1e-6

COL_Q = 3 * HYENA_W
COL_K = COL_Q + ATTN_W
COL_V = COL_K + KV_W
COL_G = COL_V + KV_W

FFT_N = 2 * SEQ
FFT_N1 = 64
FFT_N2 = 128

MIB = 1024 * 1024


def _cparams(n_axes, vmem_mib):
    return pltpu.CompilerParams(
        dimension_semantics=("arbitrary",) * n_axes,
        vmem_limit_bytes=vmem_mib * MIB,
    )


DFT_R = 8


@functools.lru_cache(maxsize=None)
def _dft_tables_np():
    n, n1, n2, r = FFT_N, FFT_N1, FFT_N2, DFT_R
    eye = np.eye(r)
    f1 = np.arange(n1)
    ang1 = 2.0 * np.pi * ((f1[:, None] * f1[None, :]) % n1) / n1
    c1, s1 = np.cos(ang1), np.sin(ang1)
    h = n1 // 2
    m_fwd = np.block([[c1[:, :h], s1[:, :h]], [-s1[:, :h], c1[:, :h]]])
    m_flt = np.concatenate([c1, -s1], axis=0)
    ct, st = c1[:h, :], s1[:h, :]
    m_inv = np.block([[ct, -st], [st, ct]]) / n
    s2 = np.arange(n2)
    f = f1[:, None, None] + n1 * s2[None, :, None]
    th = 2.0 * np.pi * ((f * s2[None, None, :]) % n) / n
    c, s = np.cos(th), np.sin(th)
    fwd2 = np.concatenate(
        [np.concatenate([c, s], axis=2), np.concatenate([-s, c], axis=2)], axis=1)
    c_t, s_t = np.transpose(c, (0, 2, 1)), np.transpose(s, (0, 2, 1))
    inv2 = np.concatenate(
        [np.concatenate([c_t, -s_t], axis=2), np.concatenate([s_t, c_t], axis=2)], axis=1)
    f32 = lambda a: np.ascontiguousarray(a, dtype=np.float32)
    return dict(k_fwd=f32(np.kron(m_fwd, eye)), k_flt=f32(np.kron(m_flt, eye)),
                k_inv=f32(np.kron(m_inv, eye)), fwd2=f32(fwd2), inv2=f32(inv2))


def _dft_tables():
    return {k: jnp.asarray(v).astype(BF16) for k, v in _dft_tables_np().items()}


@functools.lru_cache(maxsize=None)
def _rope_tables_np():
    half = HEAD_DIM // 2
    inv = ROPE_THETA ** (-np.arange(0, half, 2, dtype=np.float64) / half)
    pos = np.arange(SEQ)
    ang_r = (pos // GRID_W)[:, None] * inv[None, :]
    ang_c = (pos % GRID_W)[:, None] * inv[None, :]
    cos = np.concatenate([np.cos(ang_r)] * 2 + [np.cos(ang_c)] * 2, axis=-1)
    sin = np.concatenate([-np.sin(ang_r), np.sin(ang_r), -np.sin(ang_c), np.sin(ang_c)], axis=-1)
    return cos.astype(np.float32), sin.astype(np.float32)


@functools.lru_cache(maxsize=None)
def _filter_tables_np():
    L = SEQ
    bands = (FILTER_EMB - 1) // 2
    pos = np.concatenate([np.arange(L, dtype=np.float64), L - np.arange(L, dtype=np.float64)])
    t = pos / max(L - 1, 1)
    fb = np.linspace(1e-4, bands - 1, bands)
    ang = (2.0 * math.pi * pos / L)[:, None] * fb[None, :]
    emb = np.concatenate([t[:, None], np.cos(ang), -np.sin(ang),
                          np.zeros((2 * L, FILTER_HIDDEN - FILTER_EMB))], axis=-1)
    max_decay = math.log(DECAY_TARGET) / FAST_DECAY_PCT
    min_decay = math.log(DECAY_TARGET) / SLOW_DECAY_PCT
    deltas = np.abs(np.linspace(min_decay, max_decay, HYENA_W))
    tile, hid = FLT_TR, FILTER_HIDDEN
    emb2 = emb.reshape(2 * L // tile, 2, tile // 2, hid).transpose(0, 2, 1, 3).reshape(L, 2 * hid)
    return np.ascontiguousarray(emb2, dtype=np.float32), deltas[None, :].astype(np.float32)


MM_TN = 512
MM_RC = 256


def _wres_kernel(*refs, pairs, n_act, n_extra, n_out, n_scratch, rc, epilogue, norm_first, side_cast,
                 side_call):
    n_w = len(pairs)
    acts = refs[:n_act]
    ws = refs[n_act:n_act + n_w]
    extras = refs[n_act + n_w:n_act + n_w + n_extra]
    o_refs = refs[n_act + n_w + n_extra:n_act + n_w + n_extra + n_out]
    wbs = refs[n_act + n_w + n_extra + n_out:]
    side_stages = []
    if side_call:
        fn, n_in, n_res = side_call
        step = pl.program_id(0) * pl.num_programs(1) + pl.program_id(1)
        side_stages = fn(step, *extras[-n_in:], *o_refs[-n_res:])
        extras, o_refs = extras[:-n_in], o_refs[:-n_res]
    if side_cast:
        o_refs[-1][...] = extras[-1][...].astype(BF16)
        extras, o_refs = extras[:-1], o_refs[:-1]
    if norm_first:
        g_ref, extras = extras[-1], extras[:-1]
        h_ref, o_refs = o_refs[-1], o_refs[:-1]

    if n_scratch:
        @pl.when(pl.program_id(1) == 0)
        def _():
            for w_ref, wb_ref in zip(ws, wbs):
                wb_ref[...] = w_ref[...].astype(BF16)
    else:
        wbs = ws

    for c in range(o_refs[0].shape[0] // rc):
        rows = pl.ds(c * rc, rc)
        lhs = [a[rows, :] for a in acts]
        if norm_first:
            x = lhs[0]
            ms = jnp.mean(x * x, axis=-1, keepdims=True)
            lhs[0] = (x * lax.rsqrt(ms + EPS) * g_ref[...]).astype(BF16)
            h_ref[rows, :] = lhs[0]
        accs = [jnp.dot(lhs[a], wb_ref[...], preferred_element_type=F32)
                for a, wb_ref in zip(pairs, wbs)]
        if c < len(side_stages):
            side_stages[c]()
        outs = epilogue(accs, extras, rows)
        for o_ref, out in zip(o_refs, outs if isinstance(outs, tuple) else (outs,)):
            o_ref[rows, :] = out.astype(o_ref.dtype)
    assert len(side_stages) <= o_refs[0].shape[0] // rc


def _wres_matmul(name, acts, weights, extras, epilogue, *, pairs, n_out, out_dtype, tm,
                 tn=MM_TN, rc=MM_RC, vmem_mib=48, norm_gain=None, side_cast=None, side_call=None):
    m = acts[0].shape[0]
    grid = (n_out // tn, m // tm)
    out_dtypes = out_dtype if isinstance(out_dtype, tuple) else (out_dtype,)
    out_specs = [pl.BlockSpec((tm, tn), lambda j, i: (i, j)) for _ in out_dtypes]
    out_shapes = [jax.ShapeDtypeStruct((m, n_out), dt) for dt in out_dtypes]
    extras = list(extras)
    if norm_gain is not None:
        assert n_out == tn
        k0 = acts[0].shape[1]
        extras.append((norm_gain, (1, k0), lambda j, i: (0, 0)))
        out_specs.append(pl.BlockSpec((tm, k0), lambda j, i: (i, 0)))
        out_shapes.append(jax.ShapeDtypeStruct((m, k0), BF16))
    if side_cast is not None:
        slab = side_cast.shape[0] // (grid[0] * grid[1])
        assert slab * grid[0] * grid[1] == side_cast.shape[0] and slab % 16 == 0
        step = lambda j, i: (j * grid[1] + i, 0)
        extras.append((side_cast, (slab, side_cast.shape[1]), step))
        out_specs.append(pl.BlockSpec((slab, side_cast.shape[1]), step))
        out_shapes.append(jax.ShapeDtypeStruct(side_cast.shape, BF16))
    side = side_call(grid) if side_call is not None else None
    if side is not None:
        extras += side["inputs"]
        out_specs += [pl.BlockSpec(blk, imap) for _, blk, imap in side["outputs"]]
        out_shapes += [shape for shape, _, _ in side["outputs"]]
    in_specs = [pl.BlockSpec((tm, a.shape[1]), lambda j, i: (i, 0)) for a in acts]
    assert all(off % 128 == 0 for _, off in weights) and tn % 128 == 0
    w_mode = dict(pipeline_mode=pl.Buffered(1)) if n_out == tn else {}
    in_specs += [pl.BlockSpec((pl.Element(w.shape[0]), pl.Element(tn)),
                              lambda j, i, off=off: (0, pl.multiple_of(off + j * tn, 128)), **w_mode)
                 for w, off in weights]
    in_specs += [pl.BlockSpec(blk, imap) for _, blk, imap in extras]
    scratch = [pltpu.VMEM((w.shape[0], tn), BF16) for w, _ in weights if w.dtype != BF16]
    assert len(scratch) in (0, len(weights))
    kern = functools.partial(_wres_kernel, pairs=tuple(pairs), n_act=len(acts), n_extra=len(extras),
                             n_out=len(out_specs), n_scratch=len(scratch), rc=rc, epilogue=epilogue,
                             norm_first=norm_gain is not None, side_cast=side_cast is not None,
                             side_call=(side["fn"], len(side["inputs"]), len(side["outputs"])) if side else None)
    outs = pl.pallas_call(
        kern,
        grid=grid,
        in_specs=in_specs,
        out_specs=out_specs,
        out_shape=out_shapes,
        scratch_shapes=scratch,
        compiler_params=_cparams(2, vmem_mib),
        name=name,
    )(*acts, *(w for w, _ in weights), *(e for e, _, _ in extras))
    return outs if len(outs) > 1 else outs[0]


def _head_norm_rope(x, g, cos, sin, scale):
    ms = jnp.mean(x * x, axis=-1, keepdims=True)
    y = x * lax.rsqrt(ms + EPS) * g
    lane = lax.broadcasted_iota(jnp.int32, y.shape, 1)
    swapped = jnp.where((lane % 64) < 32, pltpu.roll(y, 96, 1), pltpu.roll(y, 32, 1))
    out = y * cos + swapped * sin
    return out * scale if scale != 1.0 else out


def _ep_qkv(accs, extras, rows):
    qg_ref, kg_ref, cos_ref, sin_ref = extras
    acc = accs[0]
    cos, sin = cos_ref[rows, :], sin_ref[rows, :]
    head = lambda hh: acc[:, hh * HEAD_DIM:(hh + 1) * HEAD_DIM]
    parts = [_head_norm_rope(head(hh), qg_ref[...], cos, sin, HEAD_DIM ** -0.5 * LOG2E)
             for hh in range(N_Q_HEADS)]
    parts += [_head_norm_rope(head(N_Q_HEADS + hh), kg_ref[...], cos, sin, 1.0) for hh in range(N_KV_HEADS)]
    parts.append(acc[:, (N_Q_HEADS + N_KV_HEADS) * HEAD_DIM:])
    return jnp.concatenate(parts, axis=1)


def _ep_gate(accs, extras, rows):
    return jax.nn.sigmoid(accs[0] + extras[0][...])


IN_TM = 2048
IN_TN = 1024
LOG2E = math.log2(math.e)


def _inproj(x, norm_g, w_in, b_gate, q_g, k_g, cos, sin, w_out, filter_side):
    tm, tn = IN_TM, IN_TN
    qkv_w = ATTN_W + 2 * KV_W
    qkv_tm = tm // 2
    head = lambda g: (g, (1, HEAD_DIM), lambda j, i: (0, 0))
    pos = lambda t: (t, (qkv_tm, HEAD_DIM), lambda j, i: (i % (SEQ // qkv_tm), 0))
    common = dict(pairs=[0], out_dtype=BF16, vmem_mib=56)
    qkv, h = _wres_matmul("inproj_qkv", [x], [(w_in, COL_Q)], [head(q_g), head(k_g), pos(cos), pos(sin)],
                          _ep_qkv, n_out=qkv_w, tn=qkv_w, tm=qkv_tm, norm_gain=norm_g, **common)
    gates, w_out_bf16, taps, l1 = _wres_matmul(
        "inproj_gate", [h], [(w_in, COL_G)], [(b_gate, (1, tn), lambda j, i: (0, j))], _ep_gate,
        n_out=2 * D_MODEL, tn=tn, tm=tm, side_cast=w_out, side_call=filter_side, **common)
    return h, qkv, gates, w_out_bf16, taps, l1


HC_TM = 1024
HC_RC = 512
HC_HALO = 16


def _inproj_conv_kernel(*refs, n_w):
    h_ref, top_ref, bot_ref = refs[:3]
    w_refs = refs[3:3 + n_w]
    cw_refs = refs[3 + n_w:3 + 2 * n_w]
    cb_refs = refs[3 + 2 * n_w:3 + 3 * n_w]
    side_ref, o_ref, side_out_ref = refs[3 + 3 * n_w:6 + 3 * n_w]
    wb_refs = refs[6 + 3 * n_w:]
    side_out_ref[...] = side_ref[...].astype(BF16)
    i = pl.program_id(1)
    tm, rc, hl = o_ref.shape[0], HC_RC, HC_HALO

    @pl.when(i == 0)
    def _():
        for w_ref, wb_ref in zip(w_refs, wb_refs):
            wb_ref[...] = w_ref[...].astype(BF16)

    seq_blocks = SEQ // tm
    top = jnp.where(i % seq_blocks != 0, top_ref[...], jnp.zeros_like(top_ref))
    bot = jnp.where(i % seq_blocks != seq_blocks - 1, bot_ref[...], jnp.zeros_like(bot_ref))
    nchunk = tm // rc
    for c in range(nchunk):
        lo, hi = c * rc - hl, (c + 1) * rc + hl
        parts = ([top] if c == 0 else []) + [h_ref[max(lo, 0):min(hi, tm), :]] + ([bot] if c == nchunk - 1 else [])
        lhs = jnp.concatenate(parts, axis=0) if len(parts) > 1 else parts[0]
        outs = []
        for wb_ref, cw_ref, cb_ref in zip(wb_refs, cw_refs, cb_refs):
            e = jnp.dot(lhs, wb_ref[...], preferred_element_type=F32)
            up = pltpu.roll(e, 1, 0)[hl:hl + rc]
            dn = pltpu.roll(e, rc + 2 * hl - 1, 0)[hl:hl + rc]
            outs.append(cb_ref[...] + up * cw_ref[0:1, :] + e[hl:hl + rc] * cw_ref[1:2, :]
                        + dn * cw_ref[2:3, :])
        res = outs[0] if n_w == 1 else outs[0] * outs[1]
        o_ref[c * rc:(c + 1) * rc, :] = res.astype(o_ref.dtype)


def _inproj_conv(name, h, w_in, conv_w, conv_b, col_offs, out_dtype, side_cast):
    m, d = h.shape
    tm, tn, hl = HC_TM, HYENA_W, HC_HALO
    n_w = len(col_offs)
    once = pl.Buffered(1)
    blk = lambda off: off // tn
    slab = side_cast.shape[0] // (m // tm)
    assert slab * (m // tm) == side_cast.shape[0] and slab % 16 == 0
    side_spec = pl.BlockSpec((slab, side_cast.shape[1]), lambda j, i: (i, 0))
    in_specs = [pl.BlockSpec((tm, d), lambda j, i: (i, 0)),
                pl.BlockSpec((hl, d), lambda j, i: (jnp.maximum(i * (tm // hl) - 1, 0), 0)),
                pl.BlockSpec((hl, d), lambda j, i: (jnp.minimum((i + 1) * (tm // hl), m // hl - 1), 0))]
    in_specs += [pl.BlockSpec((d, tn), lambda j, i, o=o: (0, blk(o)), pipeline_mode=once) for o in col_offs]
    in_specs += [pl.BlockSpec((SHORT_TAPS, tn), lambda j, i, o=o: (0, blk(o))) for o in col_offs]
    in_specs += [pl.BlockSpec((1, tn), lambda j, i, o=o: (0, blk(o))) for o in col_offs]
    in_specs.append(side_spec)
    return pl.pallas_call(
        functools.partial(_inproj_conv_kernel, n_w=n_w),
        grid=(1, m // tm),
        in_specs=in_specs,
        out_specs=[pl.BlockSpec((tm, tn), lambda j, i: (i, 0)), side_spec],
        out_shape=[jax.ShapeDtypeStruct((m, tn), out_dtype), jax.ShapeDtypeStruct(side_cast.shape, BF16)],
        scratch_shapes=[pltpu.VMEM((d, tn), BF16) for _ in col_offs],
        compiler_params=_cparams(2, 56),
        name=name,
    )(h, h, h, *([w_in] * n_w), *([conv_w] * n_w), *([conv_b] * n_w), side_cast)


CV_CT = 256
CV_F1B = 16
CV_NF = FFT_N1 // CV_F1B
CV_SLAB = 2 * DFT_R
CV_NSLAB = FFT_N2 // CV_SLAB
CV_HALF = FFT_N1 // 2


def _outer_fwd_slab(src_ref, k_ref, a_ref, j, rows_in):
    ct = a_ref.shape[-1]
    halves = []
    for h in range(2):
        r = src_ref[:, :, 2 * j + h] if src_ref.ndim == 5 else src_ref[:, 2 * j + h]
        r = r.reshape(rows_in, ct).astype(BF16)
        o = jnp.dot(k_ref[...], r, preferred_element_type=F32)
        halves.append(o.reshape(2 * FFT_N1, DFT_R, ct))
    slab = jnp.concatenate(halves, axis=1).astype(BF16)
    a_ref[:, :, pl.ds(pl.multiple_of(j * CV_SLAB, CV_SLAB), CV_SLAB), :] = slab.reshape(
        2, FFT_N1, CV_SLAB, ct)


def _hy_conv_kernel(z_ref, x0_ref, bias_ref, l1_ref, kf_ref, ki_ref, fw_ref, iv_ref, fa_ref, o_ref, a_ref):
    s = pl.program_id(2)
    ct = a_ref.shape[-1]

    @pl.when(s == 0)
    def _():
        def body(j, carry):
            _outer_fwd_slab(z_ref, kf_ref, a_ref, j, 2 * CV_HALF * DFT_R)
            return carry

        lax.fori_loop(0, CV_NSLAB, body, 0)

    @pl.when((s >= 1) & (s <= CV_NF))
    def _():
        f0 = (s - 1) * CV_F1B
        inv_l1 = 1.0 / l1_ref[...]

        def spectra(fl):
            u = jnp.dot(fw_ref[fl], a_ref[:, f0 + fl].reshape(2 * FFT_N2, ct), preferred_element_type=F32)
            k = jnp.dot(fw_ref[fl], fa_ref[:, fl].reshape(2 * FFT_N2, ct), preferred_element_type=F32)
            return u, k * inv_l1

        nxt = spectra(0)
        for fl in range(CV_F1B):
            u, k = nxt
            if fl + 1 < CV_F1B:
                nxt = spectra(fl + 1)
            ure, uim = u[:FFT_N2], u[FFT_N2:]
            kre, kim = k[:FFT_N2], k[FFT_N2:]
            p = jnp.concatenate([ure * kre - uim * kim, ure * kim + uim * kre], axis=0).astype(BF16)
            y = jnp.dot(iv_ref[fl], p, preferred_element_type=F32)
            a_ref[:, f0 + fl] = y.reshape(2, FFT_N2, ct).astype(BF16)

    @pl.when(s == CV_NF + 1)
    def _():
        bias = bias_ref[...]

        def body(j, carry):
            slab = a_ref[:, :, pl.ds(pl.multiple_of(j * CV_SLAB, CV_SLAB), CV_SLAB), :].astype(F32)
            x0 = x0_ref[:, :, j].astype(F32)
            halves = []
            for h in range(2):
                r = slab[:, :, h * DFT_R:(h + 1) * DFT_R, :].reshape(2 * FFT_N1 * DFT_R, ct).astype(BF16)
                y = jnp.dot(ki_ref[...], r, preferred_element_type=F32).reshape(2, CV_HALF, DFT_R, ct)
                z = z_ref[:, :, 2 * j + h]
                halves.append((y + bias * z) * x0[:, :, h * DFT_R:(h + 1) * DFT_R, :])
            o_ref[:, :, j] = jnp.concatenate(halves, axis=2).astype(o_ref.dtype)
            return carry

        lax.fori_loop(0, CV_NSLAB, body, 0)


def _hy_conv(z, x0c, bias, filt_outer, l1, tabs):
    b, L, c = z.shape
    assert b % 2 == 0 and L * 2 == FFT_N
    ct = CV_CT
    z5 = z.reshape(b, CV_HALF, FFT_N2 // DFT_R, DFT_R, c)
    x5 = x0c.reshape(b, CV_HALF, CV_NSLAB, CV_SLAB, c)
    fidx = lambda s: jnp.clip(s - 1, 0, CV_NF - 1)
    once = pl.Buffered(1)
    tab_spec = pl.BlockSpec((CV_F1B, 2 * FFT_N2, 2 * FFT_N2), lambda p, cb, s: (fidx(s), 0, 0))
    io16 = pl.BlockSpec((2, CV_HALF, CV_NSLAB, CV_SLAB, ct), lambda p, cb, s: (p, 0, 0, 0, cb))
    out = pl.pallas_call(
        _hy_conv_kernel,
        grid=(b // 2, c // ct, CV_NF + 2),
        in_specs=[
            pl.BlockSpec((2, CV_HALF, FFT_N2 // DFT_R, DFT_R, ct), lambda p, cb, s: (p, 0, 0, 0, cb)),
            io16,
            pl.BlockSpec((1, ct), lambda p, cb, s: (0, cb)),
            pl.BlockSpec((1, ct), lambda p, cb, s: (0, cb)),
            pl.BlockSpec(tabs["k_fwd"].shape, lambda p, cb, s: (0, 0), pipeline_mode=once),
            pl.BlockSpec(tabs["k_inv"].shape, lambda p, cb, s: (0, 0), pipeline_mode=once),
            tab_spec, tab_spec,
            pl.BlockSpec((2, CV_F1B, FFT_N2, ct), lambda p, cb, s: (0, fidx(s), 0, cb)),
        ],
        out_specs=io16,
        out_shape=jax.ShapeDtypeStruct(x5.shape, BF16),
        scratch_shapes=[pltpu.VMEM((2, FFT_N1, FFT_N2, ct), BF16)],
        compiler_params=_cparams(3, 60),
        name="hyena_conv",
    )(z5, x5, bias, l1, tabs["k_fwd"], tabs["k_inv"], tabs["fwd2"], tabs["inv2"], filt_outer)
    return out.reshape(b, L, c)


def _filt_outer_kernel(t_ref, kf_ref, o_ref):
    def body(j, carry):
        _outer_fwd_slab(t_ref, kf_ref, o_ref, j, FFT_N1 * DFT_R)
        return carry

    lax.fori_loop(0, CV_NSLAB, body, 0)


def _filt_outer(taps, tabs):
    n, c = taps.shape
    ct = CV_CT
    t4 = taps.reshape(FFT_N1, FFT_N2 // DFT_R, DFT_R, c)
    return pl.pallas_call(
        _filt_outer_kernel,
        grid=(c // ct,),
        in_specs=[
            pl.BlockSpec((FFT_N1, FFT_N2 // DFT_R, DFT_R, ct), lambda cb: (0, 0, 0, cb)),
            pl.BlockSpec(tabs["k_flt"].shape, lambda cb: (0, 0), pipeline_mode=pl.Buffered(1)),
        ],
        out_specs=pl.BlockSpec((2, FFT_N1, FFT_N2, ct), lambda cb: (0, 0, 0, cb)),
        out_shape=jax.ShapeDtypeStruct((2, FFT_N1, FFT_N2, c), BF16),
        compiler_params=_cparams(1, 48),
        name="filter_outer",
    )(t4, tabs["k_flt"])


FLT_TR = 512


FLT_HALF = FLT_TR // 2


def _dot3(a, w):
    a_hi = a.astype(BF16)
    a_lo = (a - a_hi.astype(F32)).astype(BF16)
    w_hi = w.astype(BF16)
    w_lo = (w - w_hi.astype(F32)).astype(BF16)
    lhs = jnp.concatenate([a_hi, a_hi, a_lo], axis=1)
    rhs = jnp.concatenate([w_hi, w_lo, w_hi], axis=0)
    return jnp.dot(lhs, rhs, preferred_element_type=F32)


def _filter_step(r, emb_ref, w1_ref, b1_ref, w2_ref, b2_ref, w3_ref, b3_ref, w4t_ref, w4b_ref, fr_ref, dl_ref,
                 k_ref, l1_ref):
    st = {}

    def hidden(w_ref, b_ref, src):
        def stage():
            st["h"] = jnp.sin(fr_ref[...] * (_dot3(st[src] if src == "h" else emb_ref[...], w_ref[...])
                                             + b_ref[...]))
        return stage

    def taps_half(part, w_ref):
        def stage():
            tcol = part * FILTER_HIDDEN
            taps = jnp.dot(st["h"].astype(BF16), w_ref[...].astype(BF16),
                           preferred_element_type=F32)
            decay = jnp.exp(-emb_ref[:, tcol:tcol + 1] * dl_ref[...])
            rows = r * FLT_TR + part * FLT_HALF + lax.broadcasted_iota(jnp.int32, taps.shape, 0)
            taps = jnp.where(rows == SEQ, 0.0, taps * decay)
            k_ref[part * FLT_HALF:(part + 1) * FLT_HALF, :] = taps
            total = jnp.sum(jnp.abs(taps), axis=0, keepdims=True)
            if part == 0:
                st["l1"] = jnp.where(r == 0, 0.0, l1_ref[...]) + total
            else:
                l1_ref[...] = st["l1"] + total
        return stage

    return [hidden(w1_ref, b1_ref, "emb"), hidden(w2_ref, b2_ref, "h"), hidden(w3_ref, b3_ref, "h"),
            taps_half(0, w4t_ref), taps_half(1, w4b_ref)]


def _filter_side_call(emb2, w1, b1, w2, b2, w3, b3, w4, freq, deltas, steps):
    n = 2 * emb2.shape[0]
    assert n // FLT_TR == steps[0] * steps[1]
    fh = FILTER_HIDDEN
    eye2 = jnp.eye(2, dtype=F32)
    w1p = jnp.concatenate([w1, jnp.zeros((fh - w1.shape[0], fh), F32)], axis=0)
    wd = [jnp.kron(eye2, w) for w in (w1p, w2, w3)]
    bd = [jnp.tile(b, (1, 2)) for b in (b1, b2, b3, freq)]
    zeros = jnp.zeros_like(w4)
    w4t = jnp.concatenate([w4, zeros], axis=0)
    w4b = jnp.concatenate([zeros, w4], axis=0)
    step = lambda j, i: j * steps[1] + i
    const = lambda j, i: (0, 0)
    small = lambda a: (a, a.shape, const)
    fwd_tiles = SEQ // FLT_TR
    w4_half = lambda a: (a, (2 * fh, HYENA_W), lambda j, i: (0, step(j, i) // fwd_tiles))
    inputs = [(emb2, (FLT_HALF, 2 * fh), lambda j, i: (step(j, i), 0)),
              small(wd[0]), small(bd[0]), small(wd[1]), small(bd[1]), small(wd[2]), small(bd[2]),
              w4_half(w4t), w4_half(w4b), small(bd[3]), small(deltas)]
    outputs = [(jax.ShapeDtypeStruct((n, HYENA_W), F32), (FLT_TR, HYENA_W), lambda j, i: (step(j, i), 0)),
               (jax.ShapeDtypeStruct((1, HYENA_W), F32), (1, HYENA_W), const)]
    return dict(fn=_filter_step, inputs=inputs, outputs=outputs)


AT_TQ = 512
AT_RQ = 256
AT_TK = 512


def _attn_kernel(q_ref, k_ref, v_ref, o_ref, s0_ref, s1_ref):
    s_refs = (s0_ref, s1_ref)
    nblk = k_ref.shape[0] // AT_TK
    units = [(pl.ds(r * AT_RQ, AT_RQ), slice(g * HEAD_DIM, (g + 1) * HEAD_DIM))
             for r in range(q_ref.shape[0] // AT_RQ) for g in range(Q_PER_KV)]
    lane_blocks = lambda a: [a[:, i:i + 128] for i in range(0, a.shape[1], 128)]

    def scores(u, j, m_run):
        rows, lanes = units[u]
        ks = pl.ds(j * AT_TK, AT_TK)
        s = lax.dot_general(q_ref[rows, lanes], k_ref[ks, :], (((1,), (1,)), ((), ())),
                            preferred_element_type=F32)
        s_refs[u % 2][:, ks] = s
        blk = functools.reduce(jnp.maximum, lane_blocks(s))
        return blk if m_run is None else jnp.maximum(m_run, blk)

    def weighted(u, j, m, l_run, acc):
        ks = pl.ds(j * AT_TK, AT_TK)
        p = jnp.exp2(s_refs[u % 2][:, ks] - m)
        l_blk = functools.reduce(jnp.add, lane_blocks(p))
        o = jnp.dot(p.astype(BF16), v_ref[ks, :], preferred_element_type=F32)
        return (l_blk if l_run is None else l_run + l_blk), (o if acc is None else acc + o)

    m_run = None
    for j in range(nblk):
        m_run = scores(0, j, m_run)
    for u, (rows, lanes) in enumerate(units):
        m = jnp.max(m_run, axis=-1, keepdims=True)
        m_run, l_run, acc = None, None, None
        for j in range(nblk):
            if u + 1 < len(units):
                m_run = scores(u + 1, j, m_run)
            l_run, acc = weighted(u, j, m, l_run, acc)
        l = jnp.sum(l_run, axis=-1, keepdims=True)
        o_ref[rows, lanes] = (acc / l).astype(o_ref.dtype)


def _attention(qkv3):
    b, s, _ = qkv3.shape
    gw = Q_PER_KV * HEAD_DIM
    k0 = N_Q_HEADS
    v0 = N_Q_HEADS + N_KV_HEADS
    return pl.pallas_call(
        _attn_kernel,
        grid=(b, N_KV_HEADS, s // AT_TQ),
        in_specs=[pl.BlockSpec((None, AT_TQ, gw), lambda bi, kv, qi: (bi, qi, kv)),
                  pl.BlockSpec((None, s, HEAD_DIM), lambda bi, kv, qi: (bi, 0, k0 + kv)),
                  pl.BlockSpec((None, s, HEAD_DIM), lambda bi, kv, qi: (bi, 0, v0 + kv))],
        out_specs=pl.BlockSpec((None, AT_TQ, gw), lambda bi, kv, qi: (bi, qi, kv)),
        out_shape=jax.ShapeDtypeStruct((b, s, ATTN_W), BF16),
        scratch_shapes=[pltpu.VMEM((AT_RQ, s), F32), pltpu.VMEM((AT_RQ, s), F32)],
        compiler_params=_cparams(3, 48),
        name="attention",
    )(qkv3, qkv3, qkv3)


def _ep_residual(accs, extras, rows):
    return extras[0][rows, :] + accs[0]


def _ep_swiglu(accs, extras, rows):
    return jax.nn.silu(accs[0]) * accs[1]


def _merge_out_kernel(yh_ref, ya_ref, gt_ref, x_ref, g_ref, wh_ref, wa_ref, wo_ref, x1_ref, h2_ref):
    d = x_ref.shape[1]
    for c in range(x_ref.shape[0] // MM_RC):
        rows = pl.ds(c * MM_RC, MM_RC)
        ph = jnp.dot(yh_ref[rows, :], wh_ref[...], preferred_element_type=F32)
        pa = jnp.dot(ya_ref[rows, :], wa_ref[...], preferred_element_type=F32)
        merged = gt_ref[rows, :d].astype(F32) * ph + gt_ref[rows, d:].astype(F32) * pa
        x1 = x_ref[rows, :] + jnp.dot(merged.astype(BF16), wo_ref[...], preferred_element_type=F32)
        ms = jnp.mean(x1 * x1, axis=-1, keepdims=True)
        x1_ref[rows, :] = x1
        h2_ref[rows, :] = (x1 * lax.rsqrt(ms + EPS) * g_ref[...]).astype(h2_ref.dtype)


def _merge_out(yh, ya, gates, x, norm_g, w_h, w_a, w_o, tm=512):
    m, d = x.shape
    once = pl.Buffered(1)
    rows = lambda a: pl.BlockSpec((tm, a.shape[1]), lambda i: (i, 0))
    whole = lambda a: pl.BlockSpec(a.shape, lambda i: (0, 0), pipeline_mode=once)
    return pl.pallas_call(
        _merge_out_kernel,
        grid=(m // tm,),
        in_specs=[rows(yh), rows(ya), rows(gates), rows(x), pl.BlockSpec((1, d), lambda i: (0, 0)),
                  whole(w_h), whole(w_a), whole(w_o)],
        out_specs=[pl.BlockSpec((tm, d), lambda i: (i, 0))] * 2,
        out_shape=[jax.ShapeDtypeStruct((m, d), F32), jax.ShapeDtypeStruct((m, d), BF16)],
        compiler_params=_cparams(1, 56),
        name="merge_out",
    )(yh, ya, gates, x, norm_g, w_h, w_a, w_o)


def _ffn_down(a, x, w_bf16, tm=512, tn=1024):
    return _wres_matmul("ffn_down", [a], [(w_bf16, 0)], [(x, (tm, tn), lambda j, i: (i, j))], _ep_residual,
                        pairs=[0], n_out=w_bf16.shape[1], out_dtype=F32, tm=tm, tn=tn, vmem_mib=56)


def _swiglu(h, w_g, w_u, w_down, tm=2048):
    return _wres_matmul("swiglu", [h], [(w_g, 0), (w_u, 0)], [], _ep_swiglu,
                        pairs=[0, 0], n_out=w_g.shape[1], out_dtype=BF16, tm=tm, vmem_mib=56,
                        side_cast=w_down)


def _layer(x, mix_norm_g, w_in, b_gate, hy_conv_w, hy_conv_b,
           flt_w1, flt_b1, flt_w2, flt_b2, flt_w3, flt_b3, flt_w4, flt_freq, hy_bias,
           q_norm_g, k_norm_g, w_br_hyena, w_br_attn, w_out,
           ffn_norm_g, w_ffn_gate, w_ffn_up, w_ffn_down):
    b, s, d = x.shape
    m = b * s
    row = lambda a: a.reshape(1, -1)
    tabs = _dft_tables()
    cos, sin = (jnp.asarray(t) for t in _rope_tables_np())
    xm = x.reshape(m, d)

    emb2, deltas = (jnp.asarray(t) for t in _filter_tables_np())
    filter_side = functools.partial(_filter_side_call, emb2, flt_w1, row(flt_b1), flt_w2, row(flt_b2),
                                    flt_w3, row(flt_b3), flt_w4, row(flt_freq), deltas)
    h, qkv, gates, w_out_bf16, taps, l1 = _inproj(xm, row(mix_norm_g), w_in, row(b_gate), row(q_norm_g),
                                                  row(k_norm_g), cos, sin, w_out, filter_side)
    filt_outer = _filt_outer(taps, tabs)

    conv_b = row(hy_conv_b)
    x0c, w_ba_bf16 = _inproj_conv("inproj_x0", h, w_in, hy_conv_w, conv_b, [0], BF16, w_br_attn)
    z, w_bh_bf16 = _inproj_conv("inproj_z", h, w_in, hy_conv_w, conv_b, [HYENA_W, 2 * HYENA_W], F32,
                                w_br_hyena)
    y_h = _hy_conv(z.reshape(b, s, HYENA_W), x0c.reshape(b, s, HYENA_W), row(hy_bias), filt_outer, l1,
                   tabs).reshape(m, HYENA_W)

    y_a = _attention(qkv.reshape(b, s, ATTN_W + 2 * KV_W)).reshape(m, ATTN_W)

    x1, h2 = _merge_out(y_h, y_a, gates, xm, row(ffn_norm_g), w_bh_bf16, w_ba_bf16, w_out_bf16)
    act, w_down_bf16 = _swiglu(h2, w_ffn_gate, w_ffn_up, w_ffn_down)
    out = _ffn_down(act, x1, w_down_bf16)
    return out.reshape(b, s, d)


def kernel(x, mix_norm_g, w_in, b_gate, hy_conv_w, hy_conv_b, flt_w1, flt_b1, flt_w2, flt_b2, flt_w3, flt_b3, flt_w4, flt_freq, hy_bias, q_norm_g, k_norm_g, w_br_hyena, w_br_attn, w_out, ffn_norm_g, w_ffn_gate, w_ffn_up, w_ffn_down):
    params = (mix_norm_g, w_in, b_gate, hy_conv_w, hy_conv_b, flt_w1, flt_b1, flt_w2, flt_b2,
              flt_w3, flt_b3, flt_w4, flt_freq, hy_bias, q_norm_g, k_norm_g, w_br_hyena, w_br_attn,
              w_out, ffn_norm_g, w_ffn_gate, w_ffn_up, w_ffn_down)
    for l in range(mix_norm_g.shape[0]):
        x = _layer(x, *(p[l] for p in params))
    return x
```

```python
import functools
import math

import numpy as np
import jax
import jax.numpy as jnp
from jax import lax
from jax.experimental import pallas as pl
from jax.experimental.pallas import tpu as pltpu

F32 = jnp.float32
BF16 = jnp.bfloat16

D_MODEL = 2048
SEQ = 4096
GRID_W = 64
HEAD_DIM = 128
N_Q_HEADS = 8
N_KV_HEADS = 2
Q_PER_KV = N_Q_HEADS // N_KV_HEADS
ATTN_W = N_Q_HEADS * HEAD_DIM
KV_W = N_KV_HEADS * HEAD_DIM
ROPE_THETA = 10000.0
HYENA_W = D_MODEL - ATTN_W
SHORT_TAPS = 3
FILTER_EMB = 33
FILTER_HIDDEN = 64
DECAY_TARGET = 1e-2
FAST_DECAY_PCT = 0.3
SLOW_DECAY_PCT = 1.5
IN_W = 3 * HYENA_W + ATTN_W + 2 * KV_W + 2 * D_MODEL
D_FF = 5632
EPS = 1e-6

COL_Q = 3 * HYENA_W
COL_K = COL_Q + ATTN_W
COL_V = COL_K + KV_W
COL_G = COL_V + KV_W

FFT_N = 2 * SEQ
FFT_N1 = 64
FFT_N2 = 128

MIB = 1024 * 1024


def _cparams(n_axes, vmem_mib):
    return pltpu.CompilerParams(
        dimension_semantics=("arbitrary",) * n_axes,
        vmem_limit_bytes=vmem_mib * MIB,
    )


DFT_R = 8


@functools.lru_cache(maxsize=None)
def _dft_tables_np():
    n, n1, n2, r = FFT_N, FFT_N1, FFT_N2, DFT_R
    eye = np.eye(r)
    f1 = np.arange(n1)
    ang1 = 2.0 * np.pi * ((f1[:, None] * f1[None, :]) % n1) / n1
    c1, s1 = np.cos(ang1), np.sin(ang1)
    h = n1 // 2
    m_fwd = np.block([[c1[:, :h], s1[:, :h]], [-s1[:, :h], c1[:, :h]]])
    m_flt = np.concatenate([c1, -s1], axis=0)
    ct, st = c1[:h, :], s1[:h, :]
    m_inv = np.block([[ct, -st], [st, ct]]) / n
    s2 = np.arange(n2)
    f = f1[:, None, None] + n1 * s2[None, :, None]
    th = 2.0 * np.pi * ((f * s2[None, None, :]) % n) / n
    c, s = np.cos(th), np.sin(th)
    fwd2 = np.concatenate(
        [np.concatenate([c, s], axis=2), np.concatenate([-s, c], axis=2)], axis=1)
    c_t, s_t = np.transpose(c, (0, 2, 1)), np.transpose(s, (0, 2, 1))
    inv2 = np.concatenate(
        [np.concatenate([c_t, -s_t], axis=2), np.concatenate([s_t, c_t], axis=2)], axis=1)
    f32 = lambda a: np.ascontiguousarray(a, dtype=np.float32)
    return dict(k_fwd=f32(np.kron(m_fwd, eye)), k_flt=f32(np.kron(m_flt, eye)),
                k_inv=f32(np.kron(m_inv, eye)), fwd2=f32(fwd2), inv2=f32(inv2))


def _dft_tables():
    return {k: jnp.asarray(v).astype(BF16) for k, v in _dft_tables_np().items()}


@functools.lru_cache(maxsize=None)
def _rope_tables_np():
    half = HEAD_DIM // 2
    inv = ROPE_THETA ** (-np.arange(0, half, 2, dtype=np.float64) / half)
    pos = np.arange(SEQ)
    ang_r = (pos // GRID_W)[:, None] * inv[None, :]
    ang_c = (pos % GRID_W)[:, None] * inv[None, :]
    cos = np.concatenate([np.cos(ang_r)] * 2 + [np.cos(ang_c)] * 2, axis=-1)
    sin = np.concatenate([-np.sin(ang_r), np.sin(ang_r), -np.sin(ang_c), np.sin(ang_c)], axis=-1)
    return cos.astype(np.float32), sin.astype(np.float32)


@functools.lru_cache(maxsize=None)
def _filter_tables_np():
    L = SEQ
    bands = (FILTER_EMB - 1) // 2
    pos = np.concatenate([np.arange(L, dtype=np.float64), L - np.arange(L, dtype=np.float64)])
    t = pos / max(L - 1, 1)
    fb = np.linspace(1e-4, bands - 1, bands)
    ang = (2.0 * math.pi * pos / L)[:, None] * fb[None, :]
    emb = np.concatenate([t[:, None], np.cos(ang), -np.sin(ang),
                          np.zeros((2 * L, FILTER_HIDDEN - FILTER_EMB))], axis=-1)
    max_decay = math.log(DECAY_TARGET) / FAST_DECAY_PCT
    min_decay = math.log(DECAY_TARGET) / SLOW_DECAY_PCT
    deltas = np.abs(np.linspace(min_decay, max_decay, HYENA_W))
    tile, hid = FLT_TR, FILTER_HIDDEN
    emb2 = emb.reshape(2 * L // tile, 2, tile // 2, hid).transpose(0, 2, 1, 3).reshape(L, 2 * hid)
    return np.ascontiguousarray(emb2, dtype=np.float32), deltas[None, :].astype(np.float32)


MM_TN = 512
MM_RC = 256


def _wres_kernel(*refs, pairs, n_act, n_extra, n_out, n_scratch, rc, epilogue, norm_first, side_cast,
                 side_call):
    n_w = len(pairs)
    acts = refs[:n_act]
    ws = refs[n_act:n_act + n_w]
    extras = refs[n_act + n_w:n_act + n_w + n_extra]
    o_refs = refs[n_act + n_w + n_extra:n_act + n_w + n_extra + n_out]
    wbs = refs[n_act + n_w + n_extra + n_out:]
    side_stages = []
    if side_call:
        fn, n_in, n_res = side_call
        step = pl.program_id(0) * pl.num_programs(1) + pl.program_id(1)
        side_stages = fn(step, *extras[-n_in:], *o_refs[-n_res:])
        extras, o_refs = extras[:-n_in], o_refs[:-n_res]
    for _ in range(side_cast):
        o_refs[-1][...] = extras[-1][...].astype(BF16)
        extras, o_refs = extras[:-1], o_refs[:-1]
    if norm_first:
        g_ref, extras = extras[-1], extras[:-1]
        h_ref, o_refs = o_refs[-1], o_refs[:-1]

    if n_scratch:
        @pl.when(pl.program_id(1) == 0)
        def _():
            for w_ref, wb_ref in zip(ws, wbs):
                wb_ref[...] = w_ref[...].astype(BF16)
    else:
        wbs = ws

    for c in range(o_refs[0].shape[0] // rc):
        rows = pl.ds(c * rc, rc)
        lhs = [a[rows, :] for a in acts]
        if norm_first:
            x = lhs[0]
            ms = jnp.mean(x * x, axis=-1, keepdims=True)
            lhs[0] = (x * lax.rsqrt(ms + EPS) * g_ref[...]).astype(BF16)
            h_ref[rows, :] = lhs[0]
        accs = [jnp.dot(lhs[a], wb_ref[...], preferred_element_type=F32)
                for a, wb_ref in zip(pairs, wbs)]
        if c < len(side_stages):
            side_stages[c]()
        outs = epilogue(accs, extras, rows)
        for o_ref, out in zip(o_refs, outs if isinstance(outs, tuple) else (outs,)):
            o_ref[rows, :] = out.astype(o_ref.dtype)
    assert len(side_stages) <= o_refs[0].shape[0] // rc


def _wres_matmul(name, acts, weights, extras, epilogue, *, pairs, n_out, out_dtype, tm,
                 tn=MM_TN, rc=MM_RC, vmem_mib=48, norm_gain=None, side_cast=None, side_call=None):
    m = acts[0].shape[0]
    grid = (n_out // tn, m // tm)
    out_dtypes = out_dtype if isinstance(out_dtype, tuple) else (out_dtype,)
    out_specs = [pl.BlockSpec((tm, tn), lambda j, i: (i, j)) for _ in out_dtypes]
    out_shapes = [jax.ShapeDtypeStruct((m, n_out), dt) for dt in out_dtypes]
    extras = list(extras)
    if norm_gain is not None:
        assert n_out == tn
        k0 = acts[0].shape[1]
        extras.append((norm_gain, (1, k0), lambda j, i: (0, 0)))
        out_specs.append(pl.BlockSpec((tm, k0), lambda j, i: (i, 0)))
        out_shapes.append(jax.ShapeDtypeStruct((m, k0), BF16))
    side_casts = side_cast if isinstance(side_cast, list) else [side_cast] * (side_cast is not None)
    for arr, ncols in [(a, a.shape[1]) if not isinstance(a, tuple) else a for a in side_casts]:
        slab = arr.shape[0] // (grid[0] * grid[1])
        assert slab * grid[0] * grid[1] == arr.shape[0] and slab % 16 == 0 and ncols % 128 == 0
        step = lambda j, i: (j * grid[1] + i, 0)
        extras.append((arr, (slab, ncols), step))
        out_specs.append(pl.BlockSpec((slab, ncols), step))
        out_shapes.append(jax.ShapeDtypeStruct((arr.shape[0], ncols), BF16))
    side = side_call(grid) if side_call is not None else None
    if side is not None:
        extras += side["inputs"]
        out_specs += [pl.BlockSpec(blk, imap) for _, blk, imap in side["outputs"]]
        out_shapes += [shape for shape, _, _ in side["outputs"]]
    in_specs = [pl.BlockSpec((tm, a.shape[1]), lambda j, i: (i, 0)) for a in acts]
    assert all(off % 128 == 0 for _, off in weights) and tn % 128 == 0
    w_mode = dict(pipeline_mode=pl.Buffered(1)) if n_out == tn else {}
    in_specs += [pl.BlockSpec((pl.Element(w.shape[0]), pl.Element(tn)),
                              lambda j, i, off=off: (0, pl.multiple_of(off + j * tn, 128)), **w_mode)
                 for w, off in weights]
    in_specs += [pl.BlockSpec(blk, imap) for _, blk, imap in extras]
    scratch = [pltpu.VMEM((w.shape[0], tn), BF16) for w, _ in weights if w.dtype != BF16]
    assert len(scratch) in (0, len(weights))
    kern = functools.partial(_wres_kernel, pairs=tuple(pairs), n_act=len(acts), n_extra=len(extras),
                             n_out=len(out_specs), n_scratch=len(scratch), rc=rc, epilogue=epilogue,
                             norm_first=norm_gain is not None, side_cast=len(side_casts),
                             side_call=(side["fn"], len(side["inputs"]), len(side["outputs"])) if side else None)
    outs = pl.pallas_call(
        kern,
        grid=grid,
        in_specs=in_specs,
        out_specs=out_specs,
        out_shape=out_shapes,
        scratch_shapes=scratch,
        compiler_params=_cparams(2, vmem_mib),
        name=name,
    )(*acts, *(w for w, _ in weights), *(e for e, _, _ in extras))
    return outs if len(outs) > 1 else outs[0]


def _head_norm_rope(x, g, cos, sin, scale):
    ms = jnp.mean(x * x, axis=-1, keepdims=True)
    y = x * lax.rsqrt(ms + EPS) * g
    lane = lax.broadcasted_iota(jnp.int32, y.shape, 1)
    swapped = jnp.where((lane % 64) < 32, pltpu.roll(y, 96, 1), pltpu.roll(y, 32, 1))
    out = y * cos + swapped * sin
    return out * scale if scale != 1.0 else out


def _ep_qkv(accs, extras, rows):
    qg_ref, kg_ref, cos_ref, sin_ref = extras
    acc = accs[0]
    cos, sin = cos_ref[rows, :], sin_ref[rows, :]
    head = lambda hh: acc[:, hh * HEAD_DIM:(hh + 1) * HEAD_DIM]
    parts = [_head_norm_rope(head(hh), qg_ref[...], cos, sin, HEAD_DIM ** -0.5 * LOG2E)
             for hh in range(N_Q_HEADS)]
    parts += [_head_norm_rope(head(N_Q_HEADS + hh), kg_ref[...], cos, sin, 1.0) for hh in range(N_KV_HEADS)]
    parts.append(acc[:, (N_Q_HEADS + N_KV_HEADS) * HEAD_DIM:])
    return jnp.concatenate(parts, axis=1)


def _ep_gate(accs, extras, rows):
    return jax.nn.sigmoid(accs[0] + extras[0][...])


IN_TM = 2048
IN_TN = 1024
LOG2E = math.log2(math.e)


def _inproj(x, norm_g, w_in, b_gate, q_g, k_g, cos, sin, w_out, filter_side):
    tm, tn = IN_TM, IN_TN
    qkv_w = ATTN_W + 2 * KV_W
    qkv_tm = tm // 2
    head = lambda g: (g, (1, HEAD_DIM), lambda j, i: (0, 0))
    pos = lambda t: (t, (qkv_tm, HEAD_DIM), lambda j, i: (i % (SEQ // qkv_tm), 0))
    common = dict(pairs=[0], out_dtype=BF16, vmem_mib=56)
    qkv, h = _wres_matmul("inproj_qkv", [x], [(w_in, COL_Q)], [head(q_g), head(k_g), pos(cos), pos(sin)],
                          _ep_qkv, n_out=qkv_w, tn=qkv_w, tm=qkv_tm, norm_gain=norm_g, **common)
    gates, w_out_bf16, w_hy_bf16, taps, l1 = _wres_matmul(
        "inproj_gate", [h], [(w_in, COL_G)], [(b_gate, (1, tn), lambda j, i: (0, j))], _ep_gate,
        n_out=2 * D_MODEL, tn=tn, tm=tm, side_cast=[w_out, (w_in, COL_Q)], side_call=filter_side,
        **{**common, "vmem_mib": 60})
    return h, qkv, gates, w_out_bf16, w_hy_bf16, taps, l1


HC_TM = 2048
HC_RC = 512
HC_HALO = 16


def _inproj_conv_kernel(*refs, n_w):
    h_ref, top_ref, bot_ref = refs[:3]
    wb_refs = refs[3:3 + n_w]
    cw_refs = refs[3 + n_w:3 + 2 * n_w]
    cb_refs = refs[3 + 2 * n_w:3 + 3 * n_w]
    side_ref, o_ref, side_out_ref = refs[3 + 3 * n_w:]
    side_out_ref[...] = side_ref[...].astype(BF16)
    i = pl.program_id(0)
    tm, rc, hl = o_ref.shape[0], HC_RC, HC_HALO
    seq_blocks = SEQ // tm
    top = jnp.where(i % seq_blocks != 0, top_ref[...], jnp.zeros_like(top_ref))
    bot = jnp.where(i % seq_blocks != seq_blocks - 1, bot_ref[...], jnp.zeros_like(bot_ref))
    nchunk = tm // rc
    for c in range(nchunk):
        lo, hi = c * rc - hl, (c + 1) * rc + hl
        parts = ([top] if c == 0 else []) + [h_ref[max(lo, 0):min(hi, tm), :]] + ([bot] if c == nchunk - 1 else [])
        lhs = jnp.concatenate(parts, axis=0) if len(parts) > 1 else parts[0]
        outs = []
        for wb_ref, cw_ref, cb_ref in zip(wb_refs, cw_refs, cb_refs):
            e = jnp.dot(lhs, wb_ref[...], preferred_element_type=F32)
            up = pltpu.roll(e, 1, 0)[hl:hl + rc]
            dn = pltpu.roll(e, rc + 2 * hl - 1, 0)[hl:hl + rc]
            outs.append(cb_ref[...] + up * cw_ref[0:1, :] + e[hl:hl + rc] * cw_ref[1:2, :]
                        + dn * cw_ref[2:3, :])
        res = outs[0] if n_w == 1 else outs[0] * outs[1]
        o_ref[c * rc:(c + 1) * rc, :] = res.astype(o_ref.dtype)


def _inproj_conv(name, h, w_hy, conv_w, conv_b, col_offs, out_dtype, side_cast):
    m, d = h.shape
    tm, tn, hl = HC_TM, HYENA_W, HC_HALO
    n_w = len(col_offs)
    once = pl.Buffered(1)
    blk = lambda off: off // tn
    slab = side_cast.shape[0] // (m // tm)
    assert slab * (m // tm) == side_cast.shape[0] and slab % 16 == 0
    side_spec = pl.BlockSpec((slab, side_cast.shape[1]), lambda i: (i, 0))
    in_specs = [pl.BlockSpec((tm, d), lambda i: (i, 0)),
                pl.BlockSpec((hl, d), lambda i: (jnp.maximum(i * (tm // hl) - 1, 0), 0)),
                pl.BlockSpec((hl, d), lambda i: (jnp.minimum((i + 1) * (tm // hl), m // hl - 1), 0))]
    in_specs += [pl.BlockSpec((d, tn), lambda i, o=o: (0, blk(o)), pipeline_mode=once) for o in col_offs]
    in_specs += [pl.BlockSpec((SHORT_TAPS, tn), lambda i, o=o: (0, blk(o))) for o in col_offs]
    in_specs += [pl.BlockSpec((1, tn), lambda i, o=o: (0, blk(o))) for o in col_offs]
    in_specs.append(side_spec)
    return pl.pallas_call(
        functools.partial(_inproj_conv_kernel, n_w=n_w),
        grid=(m // tm,),
        in_specs=in_specs,
        out_specs=[pl.BlockSpec((tm, tn), lambda i: (i, 0)), side_spec],
        out_shape=[jax.ShapeDtypeStruct((m, tn), out_dtype), jax.ShapeDtypeStruct(side_cast.shape, BF16)],
        compiler_params=_cparams(1, 56),
        name=name,
    )(h, h, h, *([w_hy] * n_w), *([conv_w] * n_w), *([conv_b] * n_w), side_cast)


CV_CT = 256
CV_F1B = 16
CV_NF = FFT_N1 // CV_F1B
CV_SLAB = 2 * DFT_R
CV_NSLAB = FFT_N2 // CV_SLAB
CV_HALF = FFT_N1 // 2


def _outer_fwd_slab(src_ref, k_ref, a_ref, j, rows_in):
    ct = a_ref.shape[-1]
    halves = []
    for h in range(2):
        r = src_ref[:, :, 2 * j + h] if src_ref.ndim == 5 else src_ref[:, 2 * j + h]
        r = r.reshape(rows_in, ct).astype(BF16)
        o = jnp.dot(k_ref[...], r, preferred_element_type=F32)
        halves.append(o.reshape(2 * FFT_N1, DFT_R, ct))
    slab = jnp.concatenate(halves, axis=1).astype(BF16)
    a_ref[:, :, pl.ds(pl.multiple_of(j * CV_SLAB, CV_SLAB), CV_SLAB), :] = slab.reshape(
        2, FFT_N1, CV_SLAB, ct)


def _hy_conv_kernel(z_ref, x0_ref, bias_ref, l1_ref, kf_ref, ki_ref, fw_ref, iv_ref, fa_ref, o_ref, a_ref):
    s = pl.program_id(2)
    ct = a_ref.shape[-1]

    @pl.when(s == 0)
    def _():
        def body(j, carry):
            _outer_fwd_slab(z_ref, kf_ref, a_ref, j, 2 * CV_HALF * DFT_R)
            return carry

        lax.fori_loop(0, CV_NSLAB, body, 0)

    @pl.when((s >= 1) & (s <= CV_NF))
    def _():
        f0 = (s - 1) * CV_F1B
        inv_l1 = 1.0 / l1_ref[...]

        def spectra(fl):
            u = jnp.dot(fw_ref[fl], a_ref[:, f0 + fl].reshape(2 * FFT_N2, ct), preferred_element_type=F32)
            k = jnp.dot(fw_ref[fl], fa_ref[:, fl].reshape(2 * FFT_N2, ct), preferred_element_type=F32)
            return u, k * inv_l1

        nxt = spectra(0)
        for fl in range(CV_F1B):
            u, k = nxt
            if fl + 1 < CV_F1B:
                nxt = spectra(fl + 1)
            ure, uim = u[:FFT_N2], u[FFT_N2:]
            kre, kim = k[:FFT_N2], k[FFT_N2:]
            p = jnp.concatenate([ure * kre - uim * kim, ure * kim + uim * kre], axis=0).astype(BF16)
            y = jnp.dot(iv_ref[fl], p, preferred_element_type=F32)
            a_ref[:, f0 + fl] = y.reshape(2, FFT_N2, ct).astype(BF16)

    @pl.when(s == CV_NF + 1)
    def _():
        bias = bias_ref[...]

        def body(j, carry):
            slab = a_ref[:, :, pl.ds(pl.multiple_of(j * CV_SLAB, CV_SLAB), CV_SLAB), :].astype(F32)
            x0 = x0_ref[:, :, j].astype(F32)
            halves = []
            for h in range(2):
                r = slab[:, :, h * DFT_R:(h + 1) * DFT_R, :].reshape(2 * FFT_N1 * DFT_R, ct).astype(BF16)
                y = jnp.dot(ki_ref[...], r, preferred_element_type=F32).reshape(2, CV_HALF, DFT_R, ct)
                z = z_ref[:, :, 2 * j + h]
                halves.append((y + bias * z) * x0[:, :, h * DFT_R:(h + 1) * DFT_R, :])
            o_ref[:, :, j] = jnp.concatenate(halves, axis=2).astype(o_ref.dtype)
            return carry

        lax.fori_loop(0, CV_NSLAB, body, 0)


def _hy_conv(z, x0c, bias, filt_outer, l1, tabs):
    b, L, c = z.shape
    assert b % 2 == 0 and L * 2 == FFT_N
    ct = CV_CT
    z5 = z.reshape(b, CV_HALF, FFT_N2 // DFT_R, DFT_R, c)
    x5 = x0c.reshape(b, CV_HALF, CV_NSLAB, CV_SLAB, c)
    fidx = lambda s: jnp.clip(s - 1, 0, CV_NF - 1)
    once = pl.Buffered(1)
    tab_spec = pl.BlockSpec((CV_F1B, 2 * FFT_N2, 2 * FFT_N2), lambda p, cb, s: (fidx(s), 0, 0))
    io16 = pl.BlockSpec((2, CV_HALF, CV_NSLAB, CV_SLAB, ct), lambda p, cb, s: (p, 0, 0, 0, cb))
    out = pl.pallas_call(
        _hy_conv_kernel,
        grid=(b // 2, c // ct, CV_NF + 2),
        in_specs=[
            pl.BlockSpec((2, CV_HALF, FFT_N2 // DFT_R, DFT_R, ct), lambda p, cb, s: (p, 0, 0, 0, cb)),
            io16,
            pl.BlockSpec((1, ct), lambda p, cb, s: (0, cb)),
            pl.BlockSpec((1, ct), lambda p, cb, s: (0, cb)),
            pl.BlockSpec(tabs["k_fwd"].shape, lambda p, cb, s: (0, 0), pipeline_mode=once),
            pl.BlockSpec(tabs["k_inv"].shape, lambda p, cb, s: (0, 0), pipeline_mode=once),
            tab_spec, tab_spec,
            pl.BlockSpec((2, CV_F1B, FFT_N2, ct), lambda p, cb, s: (0, fidx(s), 0, cb)),
        ],
        out_specs=io16,
        out_shape=jax.ShapeDtypeStruct(x5.shape, BF16),
        scratch_shapes=[pltpu.VMEM((2, FFT_N1, FFT_N2, ct), BF16)],
        compiler_params=_cparams(3, 60),
        name="hyena_conv",
    )(z5, x5, bias, l1, tabs["k_fwd"], tabs["k_inv"], tabs["fwd2"], tabs["inv2"], filt_outer)
    return out.reshape(b, L, c)


def _filt_outer_kernel(t_ref, kf_ref, o_ref):
    def body(j, carry):
        _outer_fwd_slab(t_ref, kf_ref, o_ref, j, FFT_N1 * DFT_R)
        return carry

    lax.fori_loop(0, CV_NSLAB, body, 0)


def _filt_outer(taps, tabs):
    n, c = taps.shape
    ct = CV_CT
    t4 = taps.reshape(FFT_N1, FFT_N2 // DFT_R, DFT_R, c)
    return pl.pallas_call(
        _filt_outer_kernel,
        grid=(c // ct,),
        in_specs=[
            pl.BlockSpec((FFT_N1, FFT_N2 // DFT_R, DFT_R, ct), lambda cb: (0, 0, 0, cb)),
            pl.BlockSpec(tabs["k_flt"].shape, lambda cb: (0, 0), pipeline_mode=pl.Buffered(1)),
        ],
        out_specs=pl.BlockSpec((2, FFT_N1, FFT_N2, ct), lambda cb: (0, 0, 0, cb)),
        out_shape=jax.ShapeDtypeStruct((2, FFT_N1, FFT_N2, c), BF16),
        compiler_params=_cparams(1, 48),
        name="filter_outer",
    )(t4, tabs["k_flt"])


FLT_TR = 512


FLT_HALF = FLT_TR // 2


def _dot3(a, w):
    a_hi = a.astype(BF16)
    a_lo = (a - a_hi.astype(F32)).astype(BF16)
    w_hi = w.astype(BF16)
    w_lo = (w - w_hi.astype(F32)).astype(BF16)
    lhs = jnp.concatenate([a_hi, a_hi, a_lo], axis=1)
    rhs = jnp.concatenate([w_hi, w_lo, w_hi], axis=0)
    return jnp.dot(lhs, rhs, preferred_element_type=F32)


def _filter_step(r, emb_ref, w1_ref, b1_ref, w2_ref, b2_ref, w3_ref, b3_ref, w4t_ref, w4b_ref, fr_ref, dl_ref,
                 k_ref, l1_ref):
    st = {}

    def hidden(w_ref, b_ref, src):
        def stage():
            st["h"] = jnp.sin(fr_ref[...] * (_dot3(st[src] if src == "h" else emb_ref[...], w_ref[...])
                                             + b_ref[...]))
        return stage

    def taps_half(part, w_ref):
        def stage():
            tcol = part * FILTER_HIDDEN
            taps = jnp.dot(st["h"].astype(BF16), w_ref[...].astype(BF16),
                           preferred_element_type=F32)
            decay = jnp.exp(-emb_ref[:, tcol:tcol + 1] * dl_ref[...])
            rows = r * FLT_TR + part * FLT_HALF + lax.broadcasted_iota(jnp.int32, taps.shape, 0)
            taps = jnp.where(rows == SEQ, 0.0, taps * decay)
            k_ref[part * FLT_HALF:(part + 1) * FLT_HALF, :] = taps
            total = jnp.sum(jnp.abs(taps), axis=0, keepdims=True)
            if part == 0:
                st["l1"] = jnp.where(r == 0, 0.0, l1_ref[...]) + total
            else:
                l1_ref[...] = st["l1"] + total
        return stage

    return [hidden(w1_ref, b1_ref, "emb"), hidden(w2_ref, b2_ref, "h"), hidden(w3_ref, b3_ref, "h"),
            taps_half(0, w4t_ref), taps_half(1, w4b_ref)]


def _filter_side_call(emb2, w1, b1, w2, b2, w3, b3, w4, freq, deltas, steps):
    n = 2 * emb2.shape[0]
    assert n // FLT_TR == steps[0] * steps[1]
    fh = FILTER_HIDDEN
    eye2 = jnp.eye(2, dtype=F32)
    w1p = jnp.concatenate([w1, jnp.zeros((fh - w1.shape[0], fh), F32)], axis=0)
    wd = [jnp.kron(eye2, w) for w in (w1p, w2, w3)]
    bd = [jnp.tile(b, (1, 2)) for b in (b1, b2, b3, freq)]
    zeros = jnp.zeros_like(w4)
    w4t = jnp.concatenate([w4, zeros], axis=0)
    w4b = jnp.concatenate([zeros, w4], axis=0)
    step = lambda j, i: j * steps[1] + i
    const = lambda j, i: (0, 0)
    small = lambda a: (a, a.shape, const)
    fwd_tiles = SEQ // FLT_TR
    w4_half = lambda a: (a, (2 * fh, HYENA_W), lambda j, i: (0, step(j, i) // fwd_tiles))
    inputs = [(emb2, (FLT_HALF, 2 * fh), lambda j, i: (step(j, i), 0)),
              small(wd[0]), small(bd[0]), small(wd[1]), small(bd[1]), small(wd[2]), small(bd[2]),
              w4_half(w4t), w4_half(w4b), small(bd[3]), small(deltas)]
    outputs = [(jax.ShapeDtypeStruct((n, HYENA_W), F32), (FLT_TR, HYENA_W), lambda j, i: (step(j, i), 0)),
               (jax.ShapeDtypeStruct((1, HYENA_W), F32), (1, HYENA_W), const)]
    return dict(fn=_filter_step, inputs=inputs, outputs=outputs)


AT_TQ = 512
AT_RQ = 256
AT_TK = 512


def _attn_kernel(q_ref, k_ref, v_ref, o_ref, s0_ref, s1_ref):
    s_refs = (s0_ref, s1_ref)
    nblk = k_ref.shape[0] // AT_TK
    units = [(pl.ds(r * AT_RQ, AT_RQ), slice(g * HEAD_DIM, (g + 1) * HEAD_DIM))
             for r in range(q_ref.shape[0] // AT_RQ) for g in range(Q_PER_KV)]
    lane_blocks = lambda a: [a[:, i:i + 128] for i in range(0, a.shape[1], 128)]

    def scores(u, j, m_run):
        rows, lanes = units[u]
        ks = pl.ds(j * AT_TK, AT_TK)
        s = lax.dot_general(q_ref[rows, lanes], k_ref[ks, :], (((1,), (1,)), ((), ())),
                            preferred_element_type=F32)
        s_refs[u % 2][:, ks] = s
        blk = functools.reduce(jnp.maximum, lane_blocks(s))
        return blk if m_run is None else jnp.maximum(m_run, blk)

    def weighted(u, j, m, l_run, acc):
        ks = pl.ds(j * AT_TK, AT_TK)
        p = jnp.exp2((s_refs[u % 2][:, ks] - m).astype(BF16))
        l_blk = functools.reduce(jnp.add, lane_blocks(p)).astype(F32)
        o = jnp.dot(p, v_ref[ks, :], preferred_element_type=F32)
        return (l_blk if l_run is None else l_run + l_blk), (o if acc is None else acc + o)

    m_run = None
    for j in range(nblk):
        m_run = scores(0, j, m_run)
    for u, (rows, lanes) in enumerate(units):
        m = jnp.max(m_run, axis=-1, keepdims=True)
        m_run, l_run, acc = None, None, None
        for j in range(nblk):
            if u + 1 < len(units):
                m_run = scores(u + 1, j, m_run)
            l_run, acc = weighted(u, j, m, l_run, acc)
        l = jnp.sum(l_run, axis=-1, keepdims=True)
        o_ref[rows, lanes] = (acc / l).astype(o_ref.dtype)


def _attention(qkv3):
    b, s, _ = qkv3.shape
    gw = Q_PER_KV * HEAD_DIM
    k0 = N_Q_HEADS
    v0 = N_Q_HEADS + N_KV_HEADS
    return pl.pallas_call(
        _attn_kernel,
        grid=(b, N_KV_HEADS, s // AT_TQ),
        in_specs=[pl.BlockSpec((None, AT_TQ, gw), lambda bi, kv, qi: (bi, qi, kv)),
                  pl.BlockSpec((None, s, HEAD_DIM), lambda bi, kv, qi: (bi, 0, k0 + kv)),
                  pl.BlockSpec((None, s, HEAD_DIM), lambda bi, kv, qi: (bi, 0, v0 + kv))],
        out_specs=pl.BlockSpec((None, AT_TQ, gw), lambda bi, kv, qi: (bi, qi, kv)),
        out_shape=jax.ShapeDtypeStruct((b, s, ATTN_W), BF16),
        scratch_shapes=[pltpu.VMEM((AT_RQ, s), F32), pltpu.VMEM((AT_RQ, s), F32)],
        compiler_params=_cparams(3, 48),
        name="attention",
    )(qkv3, qkv3, qkv3)


def _ep_residual(accs, extras, rows):
    return extras[0][rows, :] + accs[0]


def _ep_swiglu(accs, extras, rows):
    return jax.nn.silu(accs[0]) * accs[1]


def _merge_out_kernel(yh_ref, ya_ref, gt_ref, x_ref, g_ref, wh_ref, wa_ref, wo_ref, x1_ref, h2_ref):
    d = x_ref.shape[1]
    for c in range(x_ref.shape[0] // MM_RC):
        rows = pl.ds(c * MM_RC, MM_RC)
        ph = jnp.dot(yh_ref[rows, :], wh_ref[...], preferred_element_type=F32)
        pa = jnp.dot(ya_ref[rows, :], wa_ref[...], preferred_element_type=F32)
        merged = gt_ref[rows, :d].astype(F32) * ph + gt_ref[rows, d:].astype(F32) * pa
        x1 = x_ref[rows, :] + jnp.dot(merged.astype(BF16), wo_ref[...], preferred_element_type=F32)
        ms = jnp.mean(x1 * x1, axis=-1, keepdims=True)
        x1_ref[rows, :] = x1
        h2_ref[rows, :] = (x1 * lax.rsqrt(ms + EPS) * g_ref[...]).astype(h2_ref.dtype)


def _merge_out(yh, ya, gates, x, norm_g, w_h, w_a, w_o, tm=512):
    m, d = x.shape
    once = pl.Buffered(1)
    rows = lambda a: pl.BlockSpec((tm, a.shape[1]), lambda i: (i, 0))
    whole = lambda a: pl.BlockSpec(a.shape, lambda i: (0, 0), pipeline_mode=once)
    return pl.pallas_call(
        _merge_out_kernel,
        grid=(m // tm,),
        in_specs=[rows(yh), rows(ya), rows(gates), rows(x), pl.BlockSpec((1, d), lambda i: (0, 0)),
                  whole(w_h), whole(w_a), whole(w_o)],
        out_specs=[pl.BlockSpec((tm, d), lambda i: (i, 0))] * 2,
        out_shape=[jax.ShapeDtypeStruct((m, d), F32), jax.ShapeDtypeStruct((m, d), BF16)],
        compiler_params=_cparams(1, 56),
        name="merge_out",
    )(yh, ya, gates, x, norm_g, w_h, w_a, w_o)


def _ffn_down(a, x, w_bf16, tm=512, tn=1024):
    return _wres_matmul("ffn_down", [a], [(w_bf16, 0)], [(x, (tm, tn), lambda j, i: (i, j))], _ep_residual,
                        pairs=[0], n_out=w_bf16.shape[1], out_dtype=F32, tm=tm, tn=tn, vmem_mib=56)


def _swiglu(h, w_g, w_u, w_down, tm=2048):
    return _wres_matmul("swiglu", [h], [(w_g, 0), (w_u, 0)], [], _ep_swiglu,
                        pairs=[0, 0], n_out=w_g.shape[1], out_dtype=BF16, tm=tm, vmem_mib=56,
                        side_cast=w_down)


def _layer(x, mix_norm_g, w_in, b_gate, hy_conv_w, hy_conv_b,
           flt_w1, flt_b1, flt_w2, flt_b2, flt_w3, flt_b3, flt_w4, flt_freq, hy_bias,
           q_norm_g, k_norm_g, w_br_hyena, w_br_attn, w_out,
           ffn_norm_g, w_ffn_gate, w_ffn_up, w_ffn_down):
    b, s, d = x.shape
    m = b * s
    row = lambda a: a.reshape(1, -1)
    tabs = _dft_tables()
    cos, sin = (jnp.asarray(t) for t in _rope_tables_np())
    xm = x.reshape(m, d)

    emb2, deltas = (jnp.asarray(t) for t in _filter_tables_np())
    filter_side = functools.partial(_filter_side_call, emb2, flt_w1, row(flt_b1), flt_w2, row(flt_b2),
                                    flt_w3, row(flt_b3), flt_w4, row(flt_freq), deltas)
    h, qkv, gates, w_out_bf16, w_hy_bf16, taps, l1 = _inproj(
        xm, row(mix_norm_g), w_in, row(b_gate), row(q_norm_g), row(k_norm_g), cos, sin, w_out, filter_side)
    filt_outer = _filt_outer(taps, tabs)

    conv_b = row(hy_conv_b)
    x0c, w_ba_bf16 = _inproj_conv("inproj_x0", h, w_hy_bf16, hy_conv_w, conv_b, [0], BF16, w_br_attn)
    z, w_bh_bf16 = _inproj_conv("inproj_z", h, w_hy_bf16, hy_conv_w, conv_b, [HYENA_W, 2 * HYENA_W], F32,
                                w_br_hyena)
    y_h = _hy_conv(z.reshape(b, s, HYENA_W), x0c.reshape(b, s, HYENA_W), row(hy_bias), filt_outer, l1,
                   tabs).reshape(m, HYENA_W)

    y_a = _attention(qkv.reshape(b, s, ATTN_W + 2 * KV_W)).reshape(m, ATTN_W)

    x1, h2 = _merge_out(y_h, y_a, gates, xm, row(ffn_norm_g), w_bh_bf16, w_ba_bf16, w_out_bf16)
    act, w_down_bf16 = _swiglu(h2, w_ffn_gate, w_ffn_up, w_ffn_down)
    out = _ffn_down(act, x1, w_down_bf16)
    return out.reshape(b, s, d)


def kernel(x, mix_norm_g, w_in, b_gate, hy_conv_w, hy_conv_b, flt_w1, flt_b1, flt_w2, flt_b2, flt_w3, flt_b3, flt_w4, flt_freq, hy_bias, q_norm_g, k_norm_g, w_br_hyena, w_br_attn, w_out, ffn_norm_g, w_ffn_gate, w_ffn_up, w_ffn_down):
    params = (mix_norm_g, w_in, b_gate, hy_conv_w, hy_conv_b, flt_w1, flt_b1, flt_w2, flt_b2,
              flt_w3, flt_b3, flt_w4, flt_freq, hy_bias, q_norm_g, k_norm_g, w_br_hyena, w_br_attn,
              w_out, ffn_norm_g, w_ffn_gate, w_ffn_up, w_ffn_down)
    for l in range(mix_norm_g.shape[0]):
        x = _layer(x, *(p[l] for p in params))
    return x
```

```python
import functools
import math

import numpy as np
import jax
import jax.numpy as jnp
from jax import lax
from jax.experimental import pallas as pl
from jax.experimental.pallas import tpu as pltpu

F32 = jnp.float32
BF16 = jnp.bfloat16

D_MODEL = 2048
SEQ = 4096
GRID_W = 64
HEAD_DIM = 128
N_Q_HEADS = 8
N_KV_HEADS = 2
Q_PER_KV = N_Q_HEADS // N_KV_HEADS
ATTN_W = N_Q_HEADS * HEAD_DIM
KV_W = N_KV_HEADS * HEAD_DIM
ROPE_THETA = 10000.0
HYENA_W = D_MODEL - ATTN_W
SHORT_TAPS = 3
FILTER_EMB = 33
FILTER_HIDDEN = 64
DECAY_TARGET = 1e-2
FAST_DECAY_PCT = 0.3
SLOW_DECAY_PCT = 1.5
IN_W = 3 * HYENA_W + ATTN_W + 2 * KV_W + 2 * D_MODEL
D_FF = 5632
EPS = 1e-6

COL_Q = 3 * HYENA_W
COL_K = COL_Q + ATTN_W
COL_V = COL_K + KV_W
COL_G = COL_V + KV_W

FFT_N = 2 * SEQ
FFT_N1 = 64
FFT_N2 = 128

MIB = 1024 * 1024


def _cparams(n_axes, vmem_mib):
    return pltpu.CompilerParams(
        dimension_semantics=("arbitrary",) * n_axes,
        vmem_limit_bytes=vmem_mib * MIB,
    )


DFT_R = 8


@functools.lru_cache(maxsize=None)
def _dft_tables_np():
    n, n1, n2, r = FFT_N, FFT_N1, FFT_N2, DFT_R
    eye = np.eye(r)
    f1 = np.arange(n1)
    ang1 = 2.0 * np.pi * ((f1[:, None] * f1[None, :]) % n1) / n1
    c1, s1 = np.cos(ang1), np.sin(ang1)
    h = n1 // 2
    m_fwd = np.block([[c1[:, :h], s1[:, :h]], [-s1[:, :h], c1[:, :h]]])
    m_flt = np.concatenate([c1, -s1], axis=0)
    ct, st = c1[:h, :], s1[:h, :]
    m_inv = np.block([[ct, -st], [st, ct]]) / n
    s2 = np.arange(n2)
    f = f1[:, None, None] + n1 * s2[None, :, None]
    th = 2.0 * np.pi * ((f * s2[None, None, :]) % n) / n
    c, s = np.cos(th), np.sin(th)
    fwd2 = np.concatenate(
        [np.concatenate([c, s], axis=2), np.concatenate([-s, c], axis=2)], axis=1)
    c_t, s_t = np.transpose(c, (0, 2, 1)), np.transpose(s, (0, 2, 1))
    inv2 = np.concatenate(
        [np.concatenate([c_t, -s_t], axis=2), np.concatenate([s_t, c_t], axis=2)], axis=1)
    f32 = lambda a: np.ascontiguousarray(a, dtype=np.float32)
    return dict(k_fwd=f32(np.kron(m_fwd, eye)), k_flt=f32(np.kron(m_flt, eye)),
                k_inv=f32(np.kron(m_inv, eye)), fwd2=f32(fwd2), inv2=f32(inv2))


def _dft_tables():
    return {k: jnp.asarray(v).astype(BF16) for k, v in _dft_tables_np().items()}


@functools.lru_cache(maxsize=None)
def _rope_tables_np():
    half = HEAD_DIM // 2
    inv = ROPE_THETA ** (-np.arange(0, half, 2, dtype=np.float64) / half)
    pos = np.arange(SEQ)
    ang_r = (pos // GRID_W)[:, None] * inv[None, :]
    ang_c = (pos % GRID_W)[:, None] * inv[None, :]
    cos = np.concatenate([np.cos(ang_r)] * 2 + [np.cos(ang_c)] * 2, axis=-1)
    sin = np.concatenate([-np.sin(ang_r), np.sin(ang_r), -np.sin(ang_c), np.sin(ang_c)], axis=-1)
    return cos.astype(np.float32), sin.astype(np.float32)


@functools.lru_cache(maxsize=None)
def _filter_tables_np():
    L = SEQ
    bands = (FILTER_EMB - 1) // 2
    pos = np.concatenate([np.arange(L, dtype=np.float64), L - np.arange(L, dtype=np.float64)])
    t = pos / max(L - 1, 1)
    fb = np.linspace(1e-4, bands - 1, bands)
    ang = (2.0 * math.pi * pos / L)[:, None] * fb[None, :]
    emb = np.concatenate([t[:, None], np.cos(ang), -np.sin(ang),
                          np.zeros((2 * L, FILTER_HIDDEN - FILTER_EMB))], axis=-1)
    max_decay = math.log(DECAY_TARGET) / FAST_DECAY_PCT
    min_decay = math.log(DECAY_TARGET) / SLOW_DECAY_PCT
    deltas = np.abs(np.linspace(min_decay, max_decay, HYENA_W))
    tile, hid = FLT_TR, FILTER_HIDDEN
    emb2 = emb.reshape(2 * L // tile, 2, tile // 2, hid).transpose(0, 2, 1, 3).reshape(L, 2 * hid)
    return np.ascontiguousarray(emb2, dtype=np.float32), deltas[None, :].astype(np.float32)


MM_TN = 512
MM_RC = 256


def _wres_kernel(*refs, pairs, n_act, n_extra, n_out, n_scratch, rc, epilogue, norm_first, side_cast,
                 side_call):
    n_w = len(pairs)
    acts = refs[:n_act]
    ws = refs[n_act:n_act + n_w]
    extras = refs[n_act + n_w:n_act + n_w + n_extra]
    o_refs = refs[n_act + n_w + n_extra:n_act + n_w + n_extra + n_out]
    wbs = refs[n_act + n_w + n_extra + n_out:]
    side_stages = []
    if side_call:
        fn, n_in, n_res = side_call
        step = pl.program_id(0) * pl.num_programs(1) + pl.program_id(1)
        side_stages = fn(step, *extras[-n_in:], *o_refs[-n_res:])
        extras, o_refs = extras[:-n_in], o_refs[:-n_res]
    for _ in range(side_cast):
        o_refs[-1][...] = extras[-1][...].astype(BF16)
        extras, o_refs = extras[:-1], o_refs[:-1]
    if norm_first:
        g_ref, extras = extras[-1], extras[:-1]
        h_ref, o_refs = o_refs[-1], o_refs[:-1]

    if n_scratch:
        @pl.when(pl.program_id(1) == 0)
        def _():
            for w_ref, wb_ref in zip(ws, wbs):
                wb_ref[...] = w_ref[...].astype(BF16)
    else:
        wbs = ws

    for c in range(o_refs[0].shape[0] // rc):
        rows = pl.ds(c * rc, rc)
        lhs = [a[rows, :] for a in acts]
        if norm_first:
            x = lhs[0]
            ms = jnp.mean(x * x, axis=-1, keepdims=True)
            lhs[0] = (x * lax.rsqrt(ms + EPS) * g_ref[...]).astype(BF16)
            h_ref[rows, :] = lhs[0]
        accs = [jnp.dot(lhs[a], wb_ref[...], preferred_element_type=F32)
                for a, wb_ref in zip(pairs, wbs)]
        if c < len(side_stages):
            side_stages[c]()
        outs = epilogue(accs, extras, rows)
        for o_ref, out in zip(o_refs, outs if isinstance(outs, tuple) else (outs,)):
            o_ref[rows, :] = out.astype(o_ref.dtype)
    assert len(side_stages) <= o_refs[0].shape[0] // rc


def _wres_matmul(name, acts, weights, extras, epilogue, *, pairs, n_out, out_dtype, tm,
                 tn=MM_TN, rc=MM_RC, vmem_mib=48, norm_gain=None, side_cast=None, side_call=None):
    m = acts[0].shape[0]
    grid = (n_out // tn, m // tm)
    out_dtypes = out_dtype if isinstance(out_dtype, tuple) else (out_dtype,)
    out_specs = [pl.BlockSpec((tm, tn), lambda j, i: (i, j)) for _ in out_dtypes]
    out_shapes = [jax.ShapeDtypeStruct((m, n_out), dt) for dt in out_dtypes]
    extras = list(extras)
    if norm_gain is not None:
        assert n_out == tn
        k0 = acts[0].shape[1]
        extras.append((norm_gain, (1, k0), lambda j, i: (0, 0)))
        out_specs.append(pl.BlockSpec((tm, k0), lambda j, i: (i, 0)))
        out_shapes.append(jax.ShapeDtypeStruct((m, k0), BF16))
    side_casts = side_cast if isinstance(side_cast, list) else [side_cast] * (side_cast is not None)
    for arr, ncols in [(a, a.shape[1]) if not isinstance(a, tuple) else a for a in side_casts]:
        slab = arr.shape[0] // (grid[0] * grid[1])
        assert slab * grid[0] * grid[1] == arr.shape[0] and slab % 16 == 0 and ncols % 128 == 0
        step = lambda j, i: (j * grid[1] + i, 0)
        extras.append((arr, (slab, ncols), step))
        out_specs.append(pl.BlockSpec((slab, ncols), step))
        out_shapes.append(jax.ShapeDtypeStruct((arr.shape[0], ncols), BF16))
    side = side_call(grid) if side_call is not None else None
    if side is not None:
        extras += side["inputs"]
        out_specs += [pl.BlockSpec(blk, imap) for _, blk, imap in side["outputs"]]
        out_shapes += [shape for shape, _, _ in side["outputs"]]
    in_specs = [pl.BlockSpec((tm, a.shape[1]), lambda j, i: (i, 0)) for a in acts]
    assert all(off % 128 == 0 for _, off in weights) and tn % 128 == 0
    w_mode = dict(pipeline_mode=pl.Buffered(1)) if n_out == tn else {}
    in_specs += [pl.BlockSpec((pl.Element(w.shape[0]), pl.Element(tn)),
                              lambda j, i, off=off: (0, pl.multiple_of(off + j * tn, 128)), **w_mode)
                 for w, off in weights]
    in_specs += [pl.BlockSpec(blk, imap) for _, blk, imap in extras]
    scratch = [pltpu.VMEM((w.shape[0], tn), BF16) for w, _ in weights if w.dtype != BF16]
    assert len(scratch) in (0, len(weights))
    kern = functools.partial(_wres_kernel, pairs=tuple(pairs), n_act=len(acts), n_extra=len(extras),
                             n_out=len(out_specs), n_scratch=len(scratch), rc=rc, epilogue=epilogue,
                             norm_first=norm_gain is not None, side_cast=len(side_casts),
                             side_call=(side["fn"], len(side["inputs"]), len(side["outputs"])) if side else None)
    outs = pl.pallas_call(
        kern,
        grid=grid,
        in_specs=in_specs,
        out_specs=out_specs,
        out_shape=out_shapes,
        scratch_shapes=scratch,
        compiler_params=_cparams(2, vmem_mib),
        name=name,
    )(*acts, *(w for w, _ in weights), *(e for e, _, _ in extras))
    return outs if len(outs) > 1 else outs[0]


def _head_norm_rope(x, g, cos, sin, scale):
    ms = jnp.mean(x * x, axis=-1, keepdims=True)
    y = x * lax.rsqrt(ms + EPS) * g
    lane = lax.broadcasted_iota(jnp.int32, y.shape, 1)
    swapped = jnp.where((lane % 64) < 32, pltpu.roll(y, 96, 1), pltpu.roll(y, 32, 1))
    out = y * cos + swapped * sin
    return out * scale if scale != 1.0 else out


def _ep_qkv(accs, extras, rows):
    qg_ref, kg_ref, cos_ref, sin_ref = extras
    acc = accs[0]
    cos, sin = cos_ref[rows, :], sin_ref[rows, :]
    head = lambda hh: acc[:, hh * HEAD_DIM:(hh + 1) * HEAD_DIM]
    parts = [_head_norm_rope(head(hh), qg_ref[...], cos, sin, HEAD_DIM ** -0.5 * LOG2E)
             for hh in range(N_Q_HEADS)]
    parts += [_head_norm_rope(head(N_Q_HEADS + hh), kg_ref[...], cos, sin, 1.0) for hh in range(N_KV_HEADS)]
    parts.append(acc[:, (N_Q_HEADS + N_KV_HEADS) * HEAD_DIM:])
    return jnp.concatenate(parts, axis=1)


def _ep_gate(accs, extras, rows):
    return jax.nn.sigmoid(accs[0] + extras[0][...])


IN_TM = 2048
IN_TN = 1024
LOG2E = math.log2(math.e)


def _inproj(x, norm_g, w_in, b_gate, q_g, k_g, cos, sin, w_out, filter_side):
    tm, tn = IN_TM, IN_TN
    qkv_w = ATTN_W + 2 * KV_W
    qkv_tm = tm // 2
    head = lambda g: (g, (1, HEAD_DIM), lambda j, i: (0, 0))
    pos = lambda t: (t, (qkv_tm, HEAD_DIM), lambda j, i: (i % (SEQ // qkv_tm), 0))
    common = dict(pairs=[0], out_dtype=BF16, vmem_mib=56)
    qkv, h = _wres_matmul("inproj_qkv", [x], [(w_in, COL_Q)], [head(q_g), head(k_g), pos(cos), pos(sin)],
                          _ep_qkv, n_out=qkv_w, tn=qkv_w, tm=qkv_tm, norm_gain=norm_g, **common)
    gates, w_out_bf16, w_hy_bf16, taps, l1 = _wres_matmul(
        "inproj_gate", [h], [(w_in, COL_G)], [(b_gate, (1, tn), lambda j, i: (0, j))], _ep_gate,
        n_out=2 * D_MODEL, tn=tn, tm=tm, side_cast=[w_out, (w_in, COL_Q)], side_call=filter_side,
        **{**common, "vmem_mib": 60})
    return h, qkv, gates, w_out_bf16, w_hy_bf16, taps, l1


HC_TM = 1024
HC_RC = 512
HC_HALO = 16


def _inproj_conv_kernel(*refs, n_w):
    h_ref, top_ref, bot_ref = refs[:3]
    wb_refs = refs[3:3 + n_w]
    cw_refs = refs[3 + n_w:3 + 2 * n_w]
    cb_refs = refs[3 + 2 * n_w:3 + 3 * n_w]
    side_ref, o_ref, side_out_ref = refs[3 + 3 * n_w:]
    side_out_ref[...] = side_ref[...].astype(BF16)
    i = pl.program_id(0)
    tm, rc, hl = o_ref.shape[0], HC_RC, HC_HALO
    seq_blocks = SEQ // tm
    top = jnp.where(i % seq_blocks != 0, top_ref[...], jnp.zeros_like(top_ref))
    bot = jnp.where(i % seq_blocks != seq_blocks - 1, bot_ref[...], jnp.zeros_like(bot_ref))
    nchunk = tm // rc
    for c in range(nchunk):
        lo, hi = c * rc - hl, (c + 1) * rc + hl
        parts = ([top] if c == 0 else []) + [h_ref[max(lo, 0):min(hi, tm), :]] + ([bot] if c == nchunk - 1 else [])
        lhs = jnp.concatenate(parts, axis=0) if len(parts) > 1 else parts[0]
        outs = []
        for wb_ref, cw_ref, cb_ref in zip(wb_refs, cw_refs, cb_refs):
            e = jnp.dot(lhs, wb_ref[...], preferred_element_type=F32)
            up = pltpu.roll(e, 1, 0)[hl:hl + rc]
            dn = pltpu.roll(e, rc + 2 * hl - 1, 0)[hl:hl + rc]
            outs.append(cb_ref[...] + up * cw_ref[0:1, :] + e[hl:hl + rc] * cw_ref[1:2, :]
                        + dn * cw_ref[2:3, :])
        res = outs[0] if n_w == 1 else outs[0] * outs[1]
        o_ref[c * rc:(c + 1) * rc, :] = res.astype(o_ref.dtype)


def _inproj_conv(name, h, w_hy, conv_w, conv_b, col_offs, out_dtype, side_cast):
    m, d = h.shape
    tm, tn, hl = HC_TM, HYENA_W, HC_HALO
    n_w = len(col_offs)
    once = pl.Buffered(1)
    blk = lambda off: off // tn
    slab = side_cast.shape[0] // (m // tm)
    assert slab * (m // tm) == side_cast.shape[0] and slab % 16 == 0
    side_spec = pl.BlockSpec((slab, side_cast.shape[1]), lambda i: (i, 0))
    in_specs = [pl.BlockSpec((tm, d), lambda i: (i, 0)),
                pl.BlockSpec((hl, d), lambda i: (jnp.maximum(i * (tm // hl) - 1, 0), 0)),
                pl.BlockSpec((hl, d), lambda i: (jnp.minimum((i + 1) * (tm // hl), m // hl - 1), 0))]
    in_specs += [pl.BlockSpec((d, tn), lambda i, o=o: (0, blk(o)), pipeline_mode=once) for o in col_offs]
    in_specs += [pl.BlockSpec((SHORT_TAPS, tn), lambda i, o=o: (0, blk(o))) for o in col_offs]
    in_specs += [pl.BlockSpec((1, tn), lambda i, o=o: (0, blk(o))) for o in col_offs]
    in_specs.append(side_spec)
    return pl.pallas_call(
        functools.partial(_inproj_conv_kernel, n_w=n_w),
        grid=(m // tm,),
        in_specs=in_specs,
        out_specs=[pl.BlockSpec((tm, tn), lambda i: (i, 0)), side_spec],
        out_shape=[jax.ShapeDtypeStruct((m, tn), out_dtype), jax.ShapeDtypeStruct(side_cast.shape, BF16)],
        compiler_params=_cparams(1, 56),
        name=name,
    )(h, h, h, *([w_hy] * n_w), *([conv_w] * n_w), *([conv_b] * n_w), side_cast)


CV_CT = 256
CV_F1B = 16
CV_NF = FFT_N1 // CV_F1B
CV_SLAB = 2 * DFT_R
CV_NSLAB = FFT_N2 // CV_SLAB
CV_HALF = FFT_N1 // 2


def _outer_fwd_slab(src_ref, k_ref, a_ref, j, rows_in):
    ct = a_ref.shape[-1]
    halves = []
    for h in range(2):
        r = src_ref[:, :, 2 * j + h] if src_ref.ndim == 5 else src_ref[:, 2 * j + h]
        r = r.reshape(rows_in, ct).astype(BF16)
        o = jnp.dot(k_ref[...], r, preferred_element_type=F32)
        halves.append(o.reshape(2 * FFT_N1, DFT_R, ct))
    slab = jnp.concatenate(halves, axis=1).astype(BF16)
    a_ref[:, :, pl.ds(pl.multiple_of(j * CV_SLAB, CV_SLAB), CV_SLAB), :] = slab.reshape(
        2, FFT_N1, CV_SLAB, ct)


def _hy_conv_kernel(z_ref, x0_ref, bias_ref, l1_ref, kf_ref, ki_ref, fw_ref, iv_ref, fa_ref, o_ref, a_ref):
    s = pl.program_id(2)
    ct = a_ref.shape[-1]

    @pl.when(s == 0)
    def _():
        def body(j, carry):
            _outer_fwd_slab(z_ref, kf_ref, a_ref, j, 2 * CV_HALF * DFT_R)
            return carry

        lax.fori_loop(0, CV_NSLAB, body, 0, unroll=4)

    @pl.when((s >= 1) & (s <= CV_NF))
    def _():
        f0 = (s - 1) * CV_F1B
        inv_l1 = 1.0 / l1_ref[...]

        def spectra(fl):
            u = jnp.dot(fw_ref[fl], a_ref[:, f0 + fl].reshape(2 * FFT_N2, ct), preferred_element_type=F32)
            k = jnp.dot(fw_ref[fl], fa_ref[:, fl].reshape(2 * FFT_N2, ct), preferred_element_type=F32)
            return u, k * inv_l1

        nxt = spectra(0)
        for fl in range(CV_F1B):
            u, k = nxt
            if fl + 1 < CV_F1B:
                nxt = spectra(fl + 1)
            ure, uim = u[:FFT_N2], u[FFT_N2:]
            kre, kim = k[:FFT_N2], k[FFT_N2:]
            p = jnp.concatenate([ure * kre - uim * kim, ure * kim + uim * kre], axis=0).astype(BF16)
            y = jnp.dot(iv_ref[fl], p, preferred_element_type=F32)
            a_ref[:, f0 + fl] = y.reshape(2, FFT_N2, ct).astype(BF16)

    @pl.when(s == CV_NF + 1)
    def _():
        bias = bias_ref[...]

        def body(j, carry):
            slab = a_ref[:, :, pl.ds(pl.multiple_of(j * CV_SLAB, CV_SLAB), CV_SLAB), :].astype(F32)
            x0 = x0_ref[:, :, j].astype(F32)
            halves = []
            for h in range(2):
                r = slab[:, :, h * DFT_R:(h + 1) * DFT_R, :].reshape(2 * FFT_N1 * DFT_R, ct).astype(BF16)
                y = jnp.dot(ki_ref[...], r, preferred_element_type=F32).reshape(2, CV_HALF, DFT_R, ct)
                z = z_ref[:, :, 2 * j + h]
                halves.append((y + bias * z) * x0[:, :, h * DFT_R:(h + 1) * DFT_R, :])
            o_ref[:, :, j] = jnp.concatenate(halves, axis=2).astype(o_ref.dtype)
            return carry

        lax.fori_loop(0, CV_NSLAB, body, 0, unroll=4)


def _hy_conv(z, x0c, bias, filt_outer, l1, tabs):
    b, L, c = z.shape
    assert b % 2 == 0 and L * 2 == FFT_N
    ct = CV_CT
    z5 = z.reshape(b, CV_HALF, FFT_N2 // DFT_R, DFT_R, c)
    x5 = x0c.reshape(b, CV_HALF, CV_NSLAB, CV_SLAB, c)
    fidx = lambda s: jnp.clip(s - 1, 0, CV_NF - 1)
    once = pl.Buffered(1)
    tab_spec = pl.BlockSpec((CV_F1B, 2 * FFT_N2, 2 * FFT_N2), lambda p, cb, s: (fidx(s), 0, 0))
    io16 = pl.BlockSpec((2, CV_HALF, CV_NSLAB, CV_SLAB, ct), lambda p, cb, s: (p, 0, 0, 0, cb))
    out = pl.pallas_call(
        _hy_conv_kernel,
        grid=(b // 2, c // ct, CV_NF + 2),
        in_specs=[
            pl.BlockSpec((2, CV_HALF, FFT_N2 // DFT_R, DFT_R, ct), lambda p, cb, s: (p, 0, 0, 0, cb)),
            io16,
            pl.BlockSpec((1, ct), lambda p, cb, s: (0, cb)),
            pl.BlockSpec((1, ct), lambda p, cb, s: (0, cb)),
            pl.BlockSpec(tabs["k_fwd"].shape, lambda p, cb, s: (0, 0), pipeline_mode=once),
            pl.BlockSpec(tabs["k_inv"].shape, lambda p, cb, s: (0, 0), pipeline_mode=once),
            tab_spec, tab_spec,
            pl.BlockSpec((2, CV_F1B, FFT_N2, ct), lambda p, cb, s: (0, fidx(s), 0, cb)),
        ],
        out_specs=io16,
        out_shape=jax.ShapeDtypeStruct(x5.shape, BF16),
        scratch_shapes=[pltpu.VMEM((2, FFT_N1, FFT_N2, ct), BF16)],
        compiler_params=_cparams(3, 60),
        name="hyena_conv",
    )(z5, x5, bias, l1, tabs["k_fwd"], tabs["k_inv"], tabs["fwd2"], tabs["inv2"], filt_outer)
    return out.reshape(b, L, c)


def _filt_outer_kernel(t_ref, kf_ref, o_ref):
    def body(j, carry):
        _outer_fwd_slab(t_ref, kf_ref, o_ref, j, FFT_N1 * DFT_R)
        return carry

    lax.fori_loop(0, CV_NSLAB, body, 0, unroll=4)


def _filt_outer(taps, tabs):
    n, c = taps.shape
    ct = CV_CT
    t4 = taps.reshape(FFT_N1, FFT_N2 // DFT_R, DFT_R, c)
    return pl.pallas_call(
        _filt_outer_kernel,
        grid=(c // ct,),
        in_specs=[
            pl.BlockSpec((FFT_N1, FFT_N2 // DFT_R, DFT_R, ct), lambda cb: (0, 0, 0, cb)),
            pl.BlockSpec(tabs["k_flt"].shape, lambda cb: (0, 0), pipeline_mode=pl.Buffered(1)),
        ],
        out_specs=pl.BlockSpec((2, FFT_N1, FFT_N2, ct), lambda cb: (0, 0, 0, cb)),
        out_shape=jax.ShapeDtypeStruct((2, FFT_N1, FFT_N2, c), BF16),
        compiler_params=_cparams(1, 48),
        name="filter_outer",
    )(t4, tabs["k_flt"])


FLT_TR = 512


FLT_HALF = FLT_TR // 2


def _dot3(a, w):
    a_hi = a.astype(BF16)
    a_lo = (a - a_hi.astype(F32)).astype(BF16)
    w_hi = w.astype(BF16)
    w_lo = (w - w_hi.astype(F32)).astype(BF16)
    lhs = jnp.concatenate([a_hi, a_hi, a_lo], axis=1)
    rhs = jnp.concatenate([w_hi, w_lo, w_hi], axis=0)
    return jnp.dot(lhs, rhs, preferred_element_type=F32)


def _filter_step(r, emb_ref, w1_ref, b1_ref, w2_ref, b2_ref, w3_ref, b3_ref, w4t_ref, w4b_ref, fr_ref, dl_ref,
                 k_ref, l1_ref):
    st = {}

    def hidden(w_ref, b_ref, src):
        def stage():
            st["h"] = jnp.sin(fr_ref[...] * (_dot3(st[src] if src == "h" else emb_ref[...], w_ref[...])
                                             + b_ref[...]))
        return stage

    def taps_half(part, w_ref):
        def stage():
            tcol = part * FILTER_HIDDEN
            taps = jnp.dot(st["h"].astype(BF16), w_ref[...].astype(BF16),
                           preferred_element_type=F32)
            decay = jnp.exp(-emb_ref[:, tcol:tcol + 1] * dl_ref[...])
            rows = r * FLT_TR + part * FLT_HALF + lax.broadcasted_iota(jnp.int32, taps.shape, 0)
            taps = jnp.where(rows == SEQ, 0.0, taps * decay)
            k_ref[part * FLT_HALF:(part + 1) * FLT_HALF, :] = taps
            total = jnp.sum(jnp.abs(taps), axis=0, keepdims=True)
            if part == 0:
                st["l1"] = jnp.where(r == 0, 0.0, l1_ref[...]) + total
            else:
                l1_ref[...] = st["l1"] + total
        return stage

    return [hidden(w1_ref, b1_ref, "emb"), hidden(w2_ref, b2_ref, "h"), hidden(w3_ref, b3_ref, "h"),
            taps_half(0, w4t_ref), taps_half(1, w4b_ref)]


def _filter_side_call(emb2, w1, b1, w2, b2, w3, b3, w4, freq, deltas, steps):
    n = 2 * emb2.shape[0]
    assert n // FLT_TR == steps[0] * steps[1]
    fh = FILTER_HIDDEN
    eye2 = jnp.eye(2, dtype=F32)
    w1p = jnp.concatenate([w1, jnp.zeros((fh - w1.shape[0], fh), F32)], axis=0)
    wd = [jnp.kron(eye2, w) for w in (w1p, w2, w3)]
    bd = [jnp.tile(b, (1, 2)) for b in (b1, b2, b3, freq)]
    zeros = jnp.zeros_like(w4)
    w4t = jnp.concatenate([w4, zeros], axis=0)
    w4b = jnp.concatenate([zeros, w4], axis=0)
    step = lambda j, i: j * steps[1] + i
    const = lambda j, i: (0, 0)
    small = lambda a: (a, a.shape, const)
    fwd_tiles = SEQ // FLT_TR
    w4_half = lambda a: (a, (2 * fh, HYENA_W), lambda j, i: (0, step(j, i) // fwd_tiles))
    inputs = [(emb2, (FLT_HALF, 2 * fh), lambda j, i: (step(j, i), 0)),
              small(wd[0]), small(bd[0]), small(wd[1]), small(bd[1]), small(wd[2]), small(bd[2]),
              w4_half(w4t), w4_half(w4b), small(bd[3]), small(deltas)]
    outputs = [(jax.ShapeDtypeStruct((n, HYENA_W), F32), (FLT_TR, HYENA_W), lambda j, i: (step(j, i), 0)),
               (jax.ShapeDtypeStruct((1, HYENA_W), F32), (1, HYENA_W), const)]
    return dict(fn=_filter_step, inputs=inputs, outputs=outputs)


AT_TQ = 512
AT_RQ = 256
AT_TK = 512


def _attn_kernel(q_ref, k_ref, v_ref, o_ref, s0_ref, s1_ref):
    s_refs = (s0_ref, s1_ref)
    nblk = k_ref.shape[0] // AT_TK
    units = [(pl.ds(r * AT_RQ, AT_RQ), slice(g * HEAD_DIM, (g + 1) * HEAD_DIM))
             for r in range(q_ref.shape[0] // AT_RQ) for g in range(Q_PER_KV)]
    lane_blocks = lambda a: [a[:, i:i + 128] for i in range(0, a.shape[1], 128)]

    def scores(u, j, m_run):
        rows, lanes = units[u]
        ks = pl.ds(j * AT_TK, AT_TK)
        s = lax.dot_general(q_ref[rows, lanes], k_ref[ks, :], (((1,), (1,)), ((), ())),
                            preferred_element_type=F32)
        s_refs[u % 2][:, ks] = s
        blk = functools.reduce(jnp.maximum, lane_blocks(s))
        return blk if m_run is None else jnp.maximum(m_run, blk)

    def weighted(u, j, m, l_run, acc):
        ks = pl.ds(j * AT_TK, AT_TK)
        p = jnp.exp2((s_refs[u % 2][:, ks] - m).astype(BF16))
        l_blk = functools.reduce(jnp.add, lane_blocks(p)).astype(F32)
        o = jnp.dot(p, v_ref[ks, :], preferred_element_type=F32)
        return (l_blk if l_run is None else l_run + l_blk), (o if acc is None else acc + o)

    m_run = None
    for j in range(nblk):
        m_run = scores(0, j, m_run)
    for u, (rows, lanes) in enumerate(units):
        m = jnp.max(m_run, axis=-1, keepdims=True)
        m_run, l_run, acc = None, None, None
        for j in range(nblk):
            if u + 1 < len(units):
                m_run = scores(u + 1, j, m_run)
            l_run, acc = weighted(u, j, m, l_run, acc)
        l = jnp.sum(l_run, axis=-1, keepdims=True)
        o_ref[rows, lanes] = (acc / l).astype(o_ref.dtype)


def _attention(qkv3):
    b, s, _ = qkv3.shape
    gw = Q_PER_KV * HEAD_DIM
    k0 = N_Q_HEADS
    v0 = N_Q_HEADS + N_KV_HEADS
    return pl.pallas_call(
        _attn_kernel,
        grid=(b, N_KV_HEADS, s // AT_TQ),
        in_specs=[pl.BlockSpec((None, AT_TQ, gw), lambda bi, kv, qi: (bi, qi, kv)),
                  pl.BlockSpec((None, s, HEAD_DIM), lambda bi, kv, qi: (bi, 0, k0 + kv)),
                  pl.BlockSpec((None, s, HEAD_DIM), lambda bi, kv, qi: (bi, 0, v0 + kv))],
        out_specs=pl.BlockSpec((None, AT_TQ, gw), lambda bi, kv, qi: (bi, qi, kv)),
        out_shape=jax.ShapeDtypeStruct((b, s, ATTN_W), BF16),
        scratch_shapes=[pltpu.VMEM((AT_RQ, s), F32), pltpu.VMEM((AT_RQ, s), F32)],
        compiler_params=_cparams(3, 48),
        name="attention",
    )(qkv3, qkv3, qkv3)


def _ep_residual(accs, extras, rows):
    return extras[0][rows, :] + accs[0]


def _ep_swiglu(accs, extras, rows):
    return jax.nn.silu(accs[0]) * accs[1]


def _merge_out_kernel(yh_ref, ya_ref, gt_ref, x_ref, g_ref, wh_ref, wa_ref, wo_ref, x1_ref, h2_ref):
    d = x_ref.shape[1]
    for c in range(x_ref.shape[0] // MM_RC):
        rows = pl.ds(c * MM_RC, MM_RC)
        ph = jnp.dot(yh_ref[rows, :], wh_ref[...], preferred_element_type=F32)
        pa = jnp.dot(ya_ref[rows, :], wa_ref[...], preferred_element_type=F32)
        merged = gt_ref[rows, :d].astype(F32) * ph + gt_ref[rows, d:].astype(F32) * pa
        x1 = x_ref[rows, :] + jnp.dot(merged.astype(BF16), wo_ref[...], preferred_element_type=F32)
        ms = jnp.mean(x1 * x1, axis=-1, keepdims=True)
        x1_ref[rows, :] = x1
        h2_ref[rows, :] = (x1 * lax.rsqrt(ms + EPS) * g_ref[...]).astype(h2_ref.dtype)


def _merge_out(yh, ya, gates, x, norm_g, w_h, w_a, w_o, tm=512):
    m, d = x.shape
    once = pl.Buffered(1)
    rows = lambda a: pl.BlockSpec((tm, a.shape[1]), lambda i: (i, 0))
    whole = lambda a: pl.BlockSpec(a.shape, lambda i: (0, 0), pipeline_mode=once)
    return pl.pallas_call(
        _merge_out_kernel,
        grid=(m // tm,),
        in_specs=[rows(yh), rows(ya), rows(gates), rows(x), pl.BlockSpec((1, d), lambda i: (0, 0)),
                  whole(w_h), whole(w_a), whole(w_o)],
        out_specs=[pl.BlockSpec((tm, d), lambda i: (i, 0))] * 2,
        out_shape=[jax.ShapeDtypeStruct((m, d), F32), jax.ShapeDtypeStruct((m, d), BF16)],
        compiler_params=_cparams(1, 56),
        name="merge_out",
    )(yh, ya, gates, x, norm_g, w_h, w_a, w_o)


def _ffn_down(a, x, w_bf16, tm=512, tn=1024):
    return _wres_matmul("ffn_down", [a], [(w_bf16, 0)], [(x, (tm, tn), lambda j, i: (i, j))], _ep_residual,
                        pairs=[0], n_out=w_bf16.shape[1], out_dtype=F32, tm=tm, tn=tn, vmem_mib=56)


def _swiglu(h, w_g, w_u, w_down, tm=2048):
    return _wres_matmul("swiglu", [h], [(w_g, 0), (w_u, 0)], [], _ep_swiglu,
                        pairs=[0, 0], n_out=w_g.shape[1], out_dtype=BF16, tm=tm, vmem_mib=56,
                        side_cast=w_down)


def _layer(x, mix_norm_g, w_in, b_gate, hy_conv_w, hy_conv_b,
           flt_w1, flt_b1, flt_w2, flt_b2, flt_w3, flt_b3, flt_w4, flt_freq, hy_bias,
           q_norm_g, k_norm_g, w_br_hyena, w_br_attn, w_out,
           ffn_norm_g, w_ffn_gate, w_ffn_up, w_ffn_down):
    b, s, d = x.shape
    m = b * s
    row = lambda a: a.reshape(1, -1)
    tabs = _dft_tables()
    cos, sin = (jnp.asarray(t) for t in _rope_tables_np())
    xm = x.reshape(m, d)

    emb2, deltas = (jnp.asarray(t) for t in _filter_tables_np())
    filter_side = functools.partial(_filter_side_call, emb2, flt_w1, row(flt_b1), flt_w2, row(flt_b2),
                                    flt_w3, row(flt_b3), flt_w4, row(flt_freq), deltas)
    h, qkv, gates, w_out_bf16, w_hy_bf16, taps, l1 = _inproj(
        xm, row(mix_norm_g), w_in, row(b_gate), row(q_norm_g), row(k_norm_g), cos, sin, w_out, filter_side)
    filt_outer = _filt_outer(taps, tabs)

    conv_b = row(hy_conv_b)
    x0c, w_ba_bf16 = _inproj_conv("inproj_x0", h, w_hy_bf16, hy_conv_w, conv_b, [0], BF16, w_br_attn)
    z, w_bh_bf16 = _inproj_conv("inproj_z", h, w_hy_bf16, hy_conv_w, conv_b, [HYENA_W, 2 * HYENA_W], F32,
                                w_br_hyena)
    y_h = _hy_conv(z.reshape(b, s, HYENA_W), x0c.reshape(b, s, HYENA_W), row(hy_bias), filt_outer, l1,
                   tabs).reshape(m, HYENA_W)

    y_a = _attention(qkv.reshape(b, s, ATTN_W + 2 * KV_W)).reshape(m, ATTN_W)

    x1, h2 = _merge_out(y_h, y_a, gates, xm, row(ffn_norm_g), w_bh_bf16, w_ba_bf16, w_out_bf16)
    act, w_down_bf16 = _swiglu(h2, w_ffn_gate, w_ffn_up, w_ffn_down)
    out = _ffn_down(act, x1, w_down_bf16)
    return out.reshape(b, s, d)


def kernel(x, mix_norm_g, w_in, b_gate, hy_conv_w, hy_conv_b, flt_w1, flt_b1, flt_w2, flt_b2, flt_w3, flt_b3, flt_w4, flt_freq, hy_bias, q_norm_g, k_norm_g, w_br_hyena, w_br_attn, w_out, ffn_norm_g, w_ffn_gate, w_ffn_up, w_ffn_down):
    params = (mix_norm_g, w_in, b_gate, hy_conv_w, hy_conv_b, flt_w1, flt_b1, flt_w2, flt_b2,
              flt_w3, flt_b3, flt_w4, flt_freq, hy_bias, q_norm_g, k_norm_g, w_br_hyena, w_br_attn,
              w_out, ffn_norm_g, w_ffn_gate, w_ffn_up, w_ffn_down)
    for l in range(mix_norm_g.shape[0]):
        x = _layer(x, *(p[l] for p in params))
    return x
```

```python
import functools
import math

import numpy as np
import jax
import jax.numpy as jnp
from jax import lax
from jax.experimental import pallas as pl
from jax.experimental.pallas import tpu as pltpu

F32 = jnp.float32
BF16 = jnp.bfloat16

D_MODEL = 2048
SEQ = 4096
GRID_W = 64
HEAD_DIM = 128
N_Q_HEADS = 8
N_KV_HEADS = 2
Q_PER_KV = N_Q_HEADS // N_KV_HEADS
ATTN_W = N_Q_HEADS * HEAD_DIM
KV_W = N_KV_HEADS * HEAD_DIM
ROPE_THETA = 10000.0
HYENA_W = D_MODEL - ATTN_W
SHORT_TAPS = 3
FILTER_EMB = 33
FILTER_HIDDEN = 64
DECAY_TARGET = 1e-2
FAST_DECAY_PCT = 0.3
SLOW_DECAY_PCT = 1.5
EPS = 1e-6

COL_Q = 3 * HYENA_W
COL_G = COL_Q + ATTN_W + 2 * KV_W

LANES = 128
SUBLANES_F32 = 8
ROWS_BF16 = 16

FFT_N = 2 * SEQ
FFT_N1 = 64
FFT_N2 = 128

MIB = 1024 * 1024


def _cparams(n_axes, vmem_mib):
    return pltpu.CompilerParams(
        dimension_semantics=("arbitrary",) * n_axes,
        vmem_limit_bytes=vmem_mib * MIB,
    )


DFT_R = SUBLANES_F32


@functools.lru_cache(maxsize=None)
def _dft_tables_np():
    n, n1, n2, r = FFT_N, FFT_N1, FFT_N2, DFT_R
    eye = np.eye(r)
    f1 = np.arange(n1)
    ang1 = 2.0 * np.pi * ((f1[:, None] * f1[None, :]) % n1) / n1
    c1, s1 = np.cos(ang1), np.sin(ang1)
    h = n1 // 2
    m_fwd = np.block([[c1[:, :h], s1[:, :h]], [-s1[:, :h], c1[:, :h]]])
    m_flt = np.concatenate([c1, -s1], axis=0)
    ct, st = c1[:h, :], s1[:h, :]
    m_inv = np.block([[ct, -st], [st, ct]]) / n
    s2 = np.arange(n2)
    f = f1[:, None, None] + n1 * s2[None, :, None]
    th = 2.0 * np.pi * ((f * s2[None, None, :]) % n) / n
    c, s = np.cos(th), np.sin(th)
    fwd2 = np.concatenate(
        [np.concatenate([c, s], axis=2), np.concatenate([-s, c], axis=2)], axis=1)
    c_t, s_t = np.transpose(c, (0, 2, 1)), np.transpose(s, (0, 2, 1))
    inv2 = np.concatenate(
        [np.concatenate([c_t, -s_t], axis=2), np.concatenate([s_t, c_t], axis=2)], axis=1)
    f32 = lambda a: np.ascontiguousarray(a, dtype=np.float32)
    return dict(k_fwd=f32(np.kron(m_fwd, eye)), k_flt=f32(np.kron(m_flt, eye)),
                k_inv=f32(np.kron(m_inv, eye)), fwd2=f32(fwd2), inv2=f32(inv2))


def _dft_tables():
    return {k: jnp.asarray(v).astype(BF16) for k, v in _dft_tables_np().items()}


@functools.lru_cache(maxsize=None)
def _rope_tables_np():
    half = HEAD_DIM // 2
    inv = ROPE_THETA ** (-np.arange(0, half, 2, dtype=np.float64) / half)
    pos = np.arange(SEQ)
    ang_r = (pos // GRID_W)[:, None] * inv[None, :]
    ang_c = (pos % GRID_W)[:, None] * inv[None, :]
    cos = np.concatenate([np.cos(ang_r)] * 2 + [np.cos(ang_c)] * 2, axis=-1)
    sin = np.concatenate([-np.sin(ang_r), np.sin(ang_r), -np.sin(ang_c), np.sin(ang_c)], axis=-1)
    return cos.astype(np.float32), sin.astype(np.float32)


@functools.lru_cache(maxsize=None)
def _filter_tables_np():
    L = SEQ
    bands = (FILTER_EMB - 1) // 2
    pos = np.concatenate([np.arange(L, dtype=np.float64), L - np.arange(L, dtype=np.float64)])
    t = pos / max(L - 1, 1)
    fb = np.linspace(1e-4, bands - 1, bands)
    ang = (2.0 * math.pi * pos / L)[:, None] * fb[None, :]
    emb = np.concatenate([t[:, None], np.cos(ang), -np.sin(ang),
                          np.zeros((2 * L, FILTER_HIDDEN - FILTER_EMB))], axis=-1)
    max_decay = math.log(DECAY_TARGET) / FAST_DECAY_PCT
    min_decay = math.log(DECAY_TARGET) / SLOW_DECAY_PCT
    deltas = np.abs(np.linspace(min_decay, max_decay, HYENA_W))
    tile, hid = FLT_TR, FILTER_HIDDEN
    emb2 = emb.reshape(2 * L // tile, 2, tile // 2, hid).transpose(0, 2, 1, 3).reshape(L, 2 * hid)
    return np.ascontiguousarray(emb2, dtype=np.float32), deltas[None, :].astype(np.float32)


MM_TN = 512
MM_RC = 256


def _wres_kernel(*refs, pairs, n_act, n_extra, n_out, n_scratch, rc, epilogue, norm_first, side_cast,
                 side_call):
    n_w = len(pairs)
    acts = refs[:n_act]
    ws = refs[n_act:n_act + n_w]
    extras = refs[n_act + n_w:n_act + n_w + n_extra]
    o_refs = refs[n_act + n_w + n_extra:n_act + n_w + n_extra + n_out]
    wbs = refs[n_act + n_w + n_extra + n_out:]
    side_stages = []
    if side_call:
        fn, n_in, n_res = side_call
        step = pl.program_id(0) * pl.num_programs(1) + pl.program_id(1)
        side_stages = fn(step, *extras[-n_in:], *o_refs[-n_res:])
        extras, o_refs = extras[:-n_in], o_refs[:-n_res]
    for _ in range(side_cast):
        o_refs[-1][...] = extras[-1][...].astype(BF16)
        extras, o_refs = extras[:-1], o_refs[:-1]
    if norm_first:
        g_ref, extras = extras[-1], extras[:-1]
        h_ref, o_refs = o_refs[-1], o_refs[:-1]

    if n_scratch:
        @pl.when(pl.program_id(1) == 0)
        def _():
            for w_ref, wb_ref in zip(ws, wbs):
                wb_ref[...] = w_ref[...].astype(BF16)
    else:
        wbs = ws

    for c in range(o_refs[0].shape[0] // rc):
        rows = pl.ds(c * rc, rc)
        lhs = [a[rows, :] for a in acts]
        if norm_first:
            x = lhs[0]
            ms = jnp.mean(x * x, axis=-1, keepdims=True)
            lhs[0] = (x * lax.rsqrt(ms + EPS) * g_ref[...]).astype(BF16)
            h_ref[rows, :] = lhs[0]
        accs = [jnp.dot(lhs[a], wb_ref[...], preferred_element_type=F32)
                for a, wb_ref in zip(pairs, wbs)]
        if c < len(side_stages):
            side_stages[c]()
        outs = epilogue(accs, extras, rows)
        for o_ref, out in zip(o_refs, outs if isinstance(outs, tuple) else (outs,)):
            o_ref[rows, :] = out.astype(o_ref.dtype)
    assert len(side_stages) <= o_refs[0].shape[0] // rc


def _wres_matmul(name, acts, weights, extras, epilogue, *, pairs, n_out, out_dtype, tm,
                 tn=MM_TN, rc=MM_RC, vmem_mib=48, norm_gain=None, side_cast=None, side_call=None):
    m = acts[0].shape[0]
    grid = (n_out // tn, m // tm)
    out_dtypes = out_dtype if isinstance(out_dtype, tuple) else (out_dtype,)
    out_specs = [pl.BlockSpec((tm, tn), lambda j, i: (i, j)) for _ in out_dtypes]
    out_shapes = [jax.ShapeDtypeStruct((m, n_out), dt) for dt in out_dtypes]
    extras = list(extras)
    if norm_gain is not None:
        assert n_out == tn
        k0 = acts[0].shape[1]
        extras.append((norm_gain, (1, k0), lambda j, i: (0, 0)))
        out_specs.append(pl.BlockSpec((tm, k0), lambda j, i: (i, 0)))
        out_shapes.append(jax.ShapeDtypeStruct((m, k0), BF16))
    side_casts = side_cast if isinstance(side_cast, list) else [side_cast] * (side_cast is not None)
    for arr, ncols in [(a, a.shape[1]) if not isinstance(a, tuple) else a for a in side_casts]:
        slab = arr.shape[0] // (grid[0] * grid[1])
        assert slab * grid[0] * grid[1] == arr.shape[0] and slab % ROWS_BF16 == 0 and ncols % LANES == 0
        step = lambda j, i: (j * grid[1] + i, 0)
        extras.append((arr, (slab, ncols), step))
        out_specs.append(pl.BlockSpec((slab, ncols), step))
        out_shapes.append(jax.ShapeDtypeStruct((arr.shape[0], ncols), BF16))
    side = side_call(grid) if side_call is not None else None
    if side is not None:
        extras += side["inputs"]
        out_specs += [pl.BlockSpec(blk, imap) for _, blk, imap in side["outputs"]]
        out_shapes += [shape for shape, _, _ in side["outputs"]]
    in_specs = [pl.BlockSpec((tm, a.shape[1]), lambda j, i: (i, 0)) for a in acts]
    assert all(off % LANES == 0 for _, off in weights) and tn % LANES == 0
    w_mode = dict(pipeline_mode=pl.Buffered(1)) if n_out == tn else {}
    in_specs += [pl.BlockSpec((pl.Element(w.shape[0]), pl.Element(tn)),
                              lambda j, i, off=off: (0, pl.multiple_of(off + j * tn, LANES)), **w_mode)
                 for w, off in weights]
    in_specs += [pl.BlockSpec(blk, imap) for _, blk, imap in extras]
    scratch = [pltpu.VMEM((w.shape[0], tn), BF16) for w, _ in weights if w.dtype != BF16]
    assert len(scratch) in (0, len(weights))
    kern = functools.partial(_wres_kernel, pairs=tuple(pairs), n_act=len(acts), n_extra=len(extras),
                             n_out=len(out_specs), n_scratch=len(scratch), rc=rc, epilogue=epilogue,
                             norm_first=norm_gain is not None, side_cast=len(side_casts),
                             side_call=(side["fn"], len(side["inputs"]), len(side["outputs"])) if side else None)
    outs = pl.pallas_call(
        kern,
        grid=grid,
        in_specs=in_specs,
        out_specs=out_specs,
        out_shape=out_shapes,
        scratch_shapes=scratch,
        compiler_params=_cparams(2, vmem_mib),
        name=name,
    )(*acts, *(w for w, _ in weights), *(e for e, _, _ in extras))
    return outs if len(outs) > 1 else outs[0]


def _head_norm_rope(x, g, cos, sin, scale):
    ms = jnp.mean(x * x, axis=-1, keepdims=True)
    y = x * lax.rsqrt(ms + EPS) * g
    lane = lax.broadcasted_iota(jnp.int32, y.shape, 1)
    swapped = jnp.where((lane % 64) < 32, pltpu.roll(y, 96, 1), pltpu.roll(y, 32, 1))
    out = y * cos + swapped * sin
    return out * scale if scale != 1.0 else out


def _ep_qkv(accs, extras, rows):
    qg_ref, kg_ref, cos_ref, sin_ref = extras
    acc = accs[0]
    cos, sin = cos_ref[rows, :], sin_ref[rows, :]
    head = lambda hh: acc[:, hh * HEAD_DIM:(hh + 1) * HEAD_DIM]
    parts = [_head_norm_rope(head(hh), qg_ref[...], cos, sin, HEAD_DIM ** -0.5 * LOG2E)
             for hh in range(N_Q_HEADS)]
    parts += [_head_norm_rope(head(N_Q_HEADS + hh), kg_ref[...], cos, sin, 1.0) for hh in range(N_KV_HEADS)]
    parts.append(acc[:, (N_Q_HEADS + N_KV_HEADS) * HEAD_DIM:])
    return jnp.concatenate(parts, axis=1)


def _ep_gate(accs, extras, rows):
    return jax.nn.sigmoid(accs[0] + extras[0][...])


IN_TM = 2048
IN_TN = 1024
LOG2E = math.log2(math.e)


def _inproj(x, norm_g, w_in, b_gate, q_g, k_g, cos, sin, w_out, filter_side):
    tm, tn = IN_TM, IN_TN
    qkv_w = ATTN_W + 2 * KV_W
    qkv_tm = tm // 2
    head = lambda g: (g, (1, HEAD_DIM), lambda j, i: (0, 0))
    pos = lambda t: (t, (qkv_tm, HEAD_DIM), lambda j, i: (i % (SEQ // qkv_tm), 0))
    common = dict(pairs=[0], out_dtype=BF16, vmem_mib=56)
    qkv, h = _wres_matmul("inproj_qkv", [x], [(w_in, COL_Q)], [head(q_g), head(k_g), pos(cos), pos(sin)],
                          _ep_qkv, n_out=qkv_w, tn=qkv_w, tm=qkv_tm, norm_gain=norm_g, **common)
    gates, w_out_bf16, w_hy_bf16, taps, l1 = _wres_matmul(
        "inproj_gate", [h], [(w_in, COL_G)], [(b_gate, (1, tn), lambda j, i: (0, j))], _ep_gate,
        n_out=2 * D_MODEL, tn=tn, tm=tm, side_cast=[w_out, (w_in, COL_Q)], side_call=filter_side,
        **{**common, "vmem_mib": 60})
    return h, qkv, gates, w_out_bf16, w_hy_bf16, taps, l1


HC_TM = 1024
HC_RC = 512
HC_HALO = ROWS_BF16


def _inproj_conv_kernel(*refs, n_w):
    h_ref, top_ref, bot_ref = refs[:3]
    wb_refs = refs[3:3 + n_w]
    cw_refs = refs[3 + n_w:3 + 2 * n_w]
    cb_refs = refs[3 + 2 * n_w:3 + 3 * n_w]
    side_ref, o_ref, side_out_ref = refs[3 + 3 * n_w:]
    side_out_ref[...] = side_ref[...].astype(BF16)
    i = pl.program_id(0)
    tm, rc, hl = o_ref.shape[0], HC_RC, HC_HALO
    seq_blocks = SEQ // tm
    top = jnp.where(i % seq_blocks != 0, top_ref[...], jnp.zeros_like(top_ref))
    bot = jnp.where(i % seq_blocks != seq_blocks - 1, bot_ref[...], jnp.zeros_like(bot_ref))
    nchunk = tm // rc
    for c in range(nchunk):
        lo, hi = c * rc - hl, (c + 1) * rc + hl
        parts = ([top] if c == 0 else []) + [h_ref[max(lo, 0):min(hi, tm), :]] + ([bot] if c == nchunk - 1 else [])
        lhs = jnp.concatenate(parts, axis=0) if len(parts) > 1 else parts[0]
        outs = []
        for wb_ref, cw_ref, cb_ref in zip(wb_refs, cw_refs, cb_refs):
            e = jnp.dot(lhs, wb_ref[...], preferred_element_type=F32)
            up = pltpu.roll(e, 1, 0)[hl:hl + rc]
            dn = pltpu.roll(e, rc + 2 * hl - 1, 0)[hl:hl + rc]
            outs.append(cb_ref[...] + up * cw_ref[0:1, :] + e[hl:hl + rc] * cw_ref[1:2, :]
                        + dn * cw_ref[2:3, :])
        res = outs[0] if n_w == 1 else outs[0] * outs[1]
        o_ref[c * rc:(c + 1) * rc, :] = res.astype(o_ref.dtype)


def _inproj_conv(name, h, w_hy, conv_w, conv_b, col_offs, out_dtype, side_cast):
    m, d = h.shape
    tm, tn, hl = HC_TM, HYENA_W, HC_HALO
    n_w = len(col_offs)
    once = pl.Buffered(1)
    blk = lambda off: off // tn
    slab = side_cast.shape[0] // (m // tm)
    assert slab * (m // tm) == side_cast.shape[0] and slab % ROWS_BF16 == 0
    side_spec = pl.BlockSpec((slab, side_cast.shape[1]), lambda i: (i, 0))
    in_specs = [pl.BlockSpec((tm, d), lambda i: (i, 0)),
                pl.BlockSpec((hl, d), lambda i: (jnp.maximum(i * (tm // hl) - 1, 0), 0)),
                pl.BlockSpec((hl, d), lambda i: (jnp.minimum((i + 1) * (tm // hl), m // hl - 1), 0))]
    in_specs += [pl.BlockSpec((d, tn), lambda i, o=o: (0, blk(o)), pipeline_mode=once) for o in col_offs]
    in_specs += [pl.BlockSpec((SHORT_TAPS, tn), lambda i, o=o: (0, blk(o))) for o in col_offs]
    in_specs += [pl.BlockSpec((1, tn), lambda i, o=o: (0, blk(o))) for o in col_offs]
    in_specs.append(side_spec)
    return pl.pallas_call(
        functools.partial(_inproj_conv_kernel, n_w=n_w),
        grid=(m // tm,),
        in_specs=in_specs,
        out_specs=[pl.BlockSpec((tm, tn), lambda i: (i, 0)), side_spec],
        out_shape=[jax.ShapeDtypeStruct((m, tn), out_dtype), jax.ShapeDtypeStruct(side_cast.shape, BF16)],
        compiler_params=_cparams(1, 56),
        name=name,
    )(h, h, h, *([w_hy] * n_w), *([conv_w] * n_w), *([conv_b] * n_w), side_cast)


CV_CT = 256
CV_F1B = 16
CV_NF = FFT_N1 // CV_F1B
CV_SLAB = ROWS_BF16
CV_NSLAB = FFT_N2 // CV_SLAB
CV_HALF = FFT_N1 // 2


def _outer_fwd_slab(src_ref, k_ref, a_ref, j, rows_in):
    ct = a_ref.shape[-1]
    halves = []
    for h in range(2):
        r = src_ref[:, :, 2 * j + h] if src_ref.ndim == 5 else src_ref[:, 2 * j + h]
        r = r.reshape(rows_in, ct).astype(BF16)
        o = jnp.dot(k_ref[...], r, preferred_element_type=F32)
        halves.append(o.reshape(2 * FFT_N1, DFT_R, ct))
    slab = jnp.concatenate(halves, axis=1).astype(BF16)
    a_ref[:, :, pl.ds(pl.multiple_of(j * CV_SLAB, CV_SLAB), CV_SLAB), :] = slab.reshape(
        2, FFT_N1, CV_SLAB, ct)


def _hy_conv_kernel(z_ref, x0_ref, bias_ref, l1_ref, kf_ref, ki_ref, fw_ref, iv_ref, fa_ref, o_ref, a_ref):
    s = pl.program_id(2)
    ct = a_ref.shape[-1]

    @pl.when(s == 0)
    def _():
        def body(j, carry):
            _outer_fwd_slab(z_ref, kf_ref, a_ref, j, 2 * CV_HALF * DFT_R)
            return carry

        lax.fori_loop(0, CV_NSLAB, body, 0, unroll=4)

    @pl.when((s >= 1) & (s <= CV_NF))
    def _():
        f0 = (s - 1) * CV_F1B
        inv_l1 = 1.0 / l1_ref[...]

        def spectra(fl):
            u = jnp.dot(fw_ref[fl], a_ref[:, f0 + fl].reshape(2 * FFT_N2, ct), preferred_element_type=F32)
            k = jnp.dot(fw_ref[fl], fa_ref[:, fl].reshape(2 * FFT_N2, ct), preferred_element_type=F32)
            return u, k * inv_l1

        nxt = spectra(0)
        for fl in range(CV_F1B):
            u, k = nxt
            if fl + 1 < CV_F1B:
                nxt = spectra(fl + 1)
            ure, uim = u[:FFT_N2], u[FFT_N2:]
            kre, kim = k[:FFT_N2], k[FFT_N2:]
            p = jnp.concatenate([ure * kre - uim * kim, ure * kim + uim * kre], axis=0).astype(BF16)
            y = jnp.dot(iv_ref[fl], p, preferred_element_type=F32)
            a_ref[:, f0 + fl] = y.reshape(2, FFT_N2, ct).astype(BF16)

    @pl.when(s == CV_NF + 1)
    def _():
        bias = bias_ref[...]

        def body(j, carry):
            slab = a_ref[:, :, pl.ds(pl.multiple_of(j * CV_SLAB, CV_SLAB), CV_SLAB), :].astype(F32)
            x0 = x0_ref[:, :, j].astype(F32)
            halves = []
            for h in range(2):
                r = slab[:, :, h * DFT_R:(h + 1) * DFT_R, :].reshape(2 * FFT_N1 * DFT_R, ct).astype(BF16)
                y = jnp.dot(ki_ref[...], r, preferred_element_type=F32).reshape(2, CV_HALF, DFT_R, ct)
                z = z_ref[:, :, 2 * j + h]
                halves.append((y + bias * z) * x0[:, :, h * DFT_R:(h + 1) * DFT_R, :])
            o_ref[:, :, j] = jnp.concatenate(halves, axis=2).astype(o_ref.dtype)
            return carry

        lax.fori_loop(0, CV_NSLAB, body, 0, unroll=4)


def _hy_conv(z, x0c, bias, filt_outer, l1, tabs):
    b, L, c = z.shape
    assert b % 2 == 0 and L * 2 == FFT_N
    ct = CV_CT
    z5 = z.reshape(b, CV_HALF, FFT_N2 // DFT_R, DFT_R, c)
    x5 = x0c.reshape(b, CV_HALF, CV_NSLAB, CV_SLAB, c)
    fidx = lambda s: jnp.clip(s - 1, 0, CV_NF - 1)
    once = pl.Buffered(1)
    tab_spec = pl.BlockSpec((CV_F1B, 2 * FFT_N2, 2 * FFT_N2), lambda p, cb, s: (fidx(s), 0, 0))
    io16 = pl.BlockSpec((2, CV_HALF, CV_NSLAB, CV_SLAB, ct), lambda p, cb, s: (p, 0, 0, 0, cb))
    out = pl.pallas_call(
        _hy_conv_kernel,
        grid=(b // 2, c // ct, CV_NF + 2),
        in_specs=[
            pl.BlockSpec((2, CV_HALF, FFT_N2 // DFT_R, DFT_R, ct), lambda p, cb, s: (p, 0, 0, 0, cb)),
            io16,
            pl.BlockSpec((1, ct), lambda p, cb, s: (0, cb)),
            pl.BlockSpec((1, ct), lambda p, cb, s: (0, cb)),
            pl.BlockSpec(tabs["k_fwd"].shape, lambda p, cb, s: (0, 0), pipeline_mode=once),
            pl.BlockSpec(tabs["k_inv"].shape, lambda p, cb, s: (0, 0), pipeline_mode=once),
            tab_spec, tab_spec,
            pl.BlockSpec((2, CV_F1B, FFT_N2, ct), lambda p, cb, s: (0, fidx(s), 0, cb)),
        ],
        out_specs=io16,
        out_shape=jax.ShapeDtypeStruct(x5.shape, BF16),
        scratch_shapes=[pltpu.VMEM((2, FFT_N1, FFT_N2, ct), BF16)],
        compiler_params=_cparams(3, 60),
        name="hyena_conv",
    )(z5, x5, bias, l1, tabs["k_fwd"], tabs["k_inv"], tabs["fwd2"], tabs["inv2"], filt_outer)
    return out.reshape(b, L, c)


def _filt_outer_kernel(t_ref, kf_ref, o_ref):
    def body(j, carry):
        _outer_fwd_slab(t_ref, kf_ref, o_ref, j, FFT_N1 * DFT_R)
        return carry

    lax.fori_loop(0, CV_NSLAB, body, 0, unroll=4)


def _filt_outer(taps, tabs):
    n, c = taps.shape
    ct = CV_CT
    t4 = taps.reshape(FFT_N1, FFT_N2 // DFT_R, DFT_R, c)
    return pl.pallas_call(
        _filt_outer_kernel,
        grid=(c // ct,),
        in_specs=[
            pl.BlockSpec((FFT_N1, FFT_N2 // DFT_R, DFT_R, ct), lambda cb: (0, 0, 0, cb)),
            pl.BlockSpec(tabs["k_flt"].shape, lambda cb: (0, 0), pipeline_mode=pl.Buffered(1)),
        ],
        out_specs=pl.BlockSpec((2, FFT_N1, FFT_N2, ct), lambda cb: (0, 0, 0, cb)),
        out_shape=jax.ShapeDtypeStruct((2, FFT_N1, FFT_N2, c), BF16),
        compiler_params=_cparams(1, 48),
        name="filter_outer",
    )(t4, tabs["k_flt"])


FLT_TR = 512
FLT_HALF = FLT_TR // 2


def _dot3(a, w):
    a_hi = a.astype(BF16)
    a_lo = (a - a_hi.astype(F32)).astype(BF16)
    w_hi = w.astype(BF16)
    w_lo = (w - w_hi.astype(F32)).astype(BF16)
    lhs = jnp.concatenate([a_hi, a_hi, a_lo], axis=1)
    rhs = jnp.concatenate([w_hi, w_lo, w_hi], axis=0)
    return jnp.dot(lhs, rhs, preferred_element_type=F32)


def _filter_step(r, emb_ref, w1_ref, b1_ref, w2_ref, b2_ref, w3_ref, b3_ref, w4t_ref, w4b_ref, fr_ref, dl_ref,
                 k_ref, l1_ref):
    st = {}

    def hidden(w_ref, b_ref, src):
        def stage():
            st["h"] = jnp.sin(fr_ref[...] * (_dot3(st[src] if src == "h" else emb_ref[...], w_ref[...])
                                             + b_ref[...]))
        return stage

    def taps_half(part, w_ref):
        def stage():
            tcol = part * FILTER_HIDDEN
            taps = jnp.dot(st["h"].astype(BF16), w_ref[...].astype(BF16),
                           preferred_element_type=F32)
            decay = jnp.exp(-emb_ref[:, tcol:tcol + 1] * dl_ref[...])
            rows = r * FLT_TR + part * FLT_HALF + lax.broadcasted_iota(jnp.int32, taps.shape, 0)
            taps = jnp.where(rows == SEQ, 0.0, taps * decay)
            k_ref[part * FLT_HALF:(part + 1) * FLT_HALF, :] = taps
            total = jnp.sum(jnp.abs(taps), axis=0, keepdims=True)
            if part == 0:
                st["l1"] = jnp.where(r == 0, 0.0, l1_ref[...]) + total
            else:
                l1_ref[...] = st["l1"] + total
        return stage

    return [hidden(w1_ref, b1_ref, "emb"), hidden(w2_ref, b2_ref, "h"), hidden(w3_ref, b3_ref, "h"),
            taps_half(0, w4t_ref), taps_half(1, w4b_ref)]


def _filter_side_call(emb2, w1, b1, w2, b2, w3, b3, w4, freq, deltas, steps):
    n = 2 * emb2.shape[0]
    assert n // FLT_TR == steps[0] * steps[1]
    fh = FILTER_HIDDEN
    eye2 = jnp.eye(2, dtype=F32)
    w1p = jnp.concatenate([w1, jnp.zeros((fh - w1.shape[0], fh), F32)], axis=0)
    wd = [jnp.kron(eye2, w) for w in (w1p, w2, w3)]
    bd = [jnp.tile(b, (1, 2)) for b in (b1, b2, b3, freq)]
    zeros = jnp.zeros_like(w4)
    w4t = jnp.concatenate([w4, zeros], axis=0)
    w4b = jnp.concatenate([zeros, w4], axis=0)
    step = lambda j, i: j * steps[1] + i
    const = lambda j, i: (0, 0)
    small = lambda a: (a, a.shape, const)
    fwd_tiles = SEQ // FLT_TR
    w4_half = lambda a: (a, (2 * fh, HYENA_W), lambda j, i: (0, step(j, i) // fwd_tiles))
    inputs = [(emb2, (FLT_HALF, 2 * fh), lambda j, i: (step(j, i), 0)),
              small(wd[0]), small(bd[0]), small(wd[1]), small(bd[1]), small(wd[2]), small(bd[2]),
              w4_half(w4t), w4_half(w4b), small(bd[3]), small(deltas)]
    outputs = [(jax.ShapeDtypeStruct((n, HYENA_W), F32), (FLT_TR, HYENA_W), lambda j, i: (step(j, i), 0)),
               (jax.ShapeDtypeStruct((1, HYENA_W), F32), (1, HYENA_W), const)]
    return dict(fn=_filter_step, inputs=inputs, outputs=outputs)


AT_TQ = 512
AT_RQ = 256
AT_TK = 512


def _attn_kernel(q_ref, k_ref, v_ref, o_ref, s0_ref, s1_ref):
    s_refs = (s0_ref, s1_ref)
    nblk = k_ref.shape[0] // AT_TK
    units = [(pl.ds(r * AT_RQ, AT_RQ), slice(g * HEAD_DIM, (g + 1) * HEAD_DIM))
             for r in range(q_ref.shape[0] // AT_RQ) for g in range(Q_PER_KV)]
    lane_blocks = lambda a: [a[:, i:i + LANES] for i in range(0, a.shape[1], LANES)]

    def scores(u, j, m_run):
        rows, lanes = units[u]
        ks = pl.ds(j * AT_TK, AT_TK)
        s = lax.dot_general(q_ref[rows, lanes], k_ref[ks, :], (((1,), (1,)), ((), ())),
                            preferred_element_type=F32)
        s_refs[u % 2][:, ks] = s
        blk = functools.reduce(jnp.maximum, lane_blocks(s))
        return blk if m_run is None else jnp.maximum(m_run, blk)

    def weighted(u, j, m, l_run, acc):
        ks = pl.ds(j * AT_TK, AT_TK)
        p = jnp.exp2((s_refs[u % 2][:, ks] - m).astype(BF16))
        l_blk = functools.reduce(jnp.add, lane_blocks(p)).astype(F32)
        o = jnp.dot(p, v_ref[ks, :], preferred_element_type=F32)
        return (l_blk if l_run is None else l_run + l_blk), (o if acc is None else acc + o)

    m_run = None
    for j in range(nblk):
        m_run = scores(0, j, m_run)
    for u, (rows, lanes) in enumerate(units):
        m = jnp.max(m_run, axis=-1, keepdims=True)
        m_run, l_run, acc = None, None, None
        for j in range(nblk):
            if u + 1 < len(units):
                m_run = scores(u + 1, j, m_run)
            l_run, acc = weighted(u, j, m, l_run, acc)
        l = jnp.sum(l_run, axis=-1, keepdims=True)
        o_ref[rows, lanes] = (acc / l).astype(o_ref.dtype)


def _attention(qkv3):
    b, s, _ = qkv3.shape
    gw = Q_PER_KV * HEAD_DIM
    k0 = N_Q_HEADS
    v0 = N_Q_HEADS + N_KV_HEADS
    return pl.pallas_call(
        _attn_kernel,
        grid=(b, N_KV_HEADS, s // AT_TQ),
        in_specs=[pl.BlockSpec((None, AT_TQ, gw), lambda bi, kv, qi: (bi, qi, kv)),
                  pl.BlockSpec((None, s, HEAD_DIM), lambda bi, kv, qi: (bi, 0, k0 + kv)),
                  pl.BlockSpec((None, s, HEAD_DIM), lambda bi, kv, qi: (bi, 0, v0 + kv))],
        out_specs=pl.BlockSpec((None, AT_TQ, gw), lambda bi, kv, qi: (bi, qi, kv)),
        out_shape=jax.ShapeDtypeStruct((b, s, ATTN_W), BF16),
        scratch_shapes=[pltpu.VMEM((AT_RQ, s), F32), pltpu.VMEM((AT_RQ, s), F32)],
        compiler_params=_cparams(3, 48),
        name="attention",
    )(qkv3, qkv3, qkv3)


def _ep_residual(accs, extras, rows):
    return extras[0][rows, :] + accs[0]


def _ep_swiglu(accs, extras, rows):
    return jax.nn.silu(accs[0]) * accs[1]


def _merge_out_kernel(yh_ref, ya_ref, gt_ref, x_ref, g_ref, wh_ref, wa_ref, wo_ref, x1_ref, h2_ref):
    d = x_ref.shape[1]
    for c in range(x_ref.shape[0] // MM_RC):
        rows = pl.ds(c * MM_RC, MM_RC)
        ph = jnp.dot(yh_ref[rows, :], wh_ref[...], preferred_element_type=F32)
        pa = jnp.dot(ya_ref[rows, :], wa_ref[...], preferred_element_type=F32)
        merged = gt_ref[rows, :d].astype(F32) * ph + gt_ref[rows, d:].astype(F32) * pa
        x1 = x_ref[rows, :] + jnp.dot(merged.astype(BF16), wo_ref[...], preferred_element_type=F32)
        ms = jnp.mean(x1 * x1, axis=-1, keepdims=True)
        x1_ref[rows, :] = x1
        h2_ref[rows, :] = (x1 * lax.rsqrt(ms + EPS) * g_ref[...]).astype(h2_ref.dtype)


def _merge_out(yh, ya, gates, x, norm_g, w_h, w_a, w_o, tm=512):
    m, d = x.shape
    once = pl.Buffered(1)
    rows = lambda a: pl.BlockSpec((tm, a.shape[1]), lambda i: (i, 0))
    whole = lambda a: pl.BlockSpec(a.shape, lambda i: (0, 0), pipeline_mode=once)
    return pl.pallas_call(
        _merge_out_kernel,
        grid=(m // tm,),
        in_specs=[rows(yh), rows(ya), rows(gates), rows(x), pl.BlockSpec((1, d), lambda i: (0, 0)),
                  whole(w_h), whole(w_a), whole(w_o)],
        out_specs=[pl.BlockSpec((tm, d), lambda i: (i, 0))] * 2,
        out_shape=[jax.ShapeDtypeStruct((m, d), F32), jax.ShapeDtypeStruct((m, d), BF16)],
        compiler_params=_cparams(1, 56),
        name="merge_out",
    )(yh, ya, gates, x, norm_g, w_h, w_a, w_o)


def _ffn_down(a, x, w_bf16, tm=512, tn=1024):
    return _wres_matmul("ffn_down", [a], [(w_bf16, 0)], [(x, (tm, tn), lambda j, i: (i, j))], _ep_residual,
                        pairs=[0], n_out=w_bf16.shape[1], out_dtype=F32, tm=tm, tn=tn, vmem_mib=56)


def _swiglu(h, w_g, w_u, w_down, tm=2048):
    return _wres_matmul("swiglu", [h], [(w_g, 0), (w_u, 0)], [], _ep_swiglu,
                        pairs=[0, 0], n_out=w_g.shape[1], out_dtype=BF16, tm=tm, vmem_mib=56,
                        side_cast=w_down)


def _layer(x, mix_norm_g, w_in, b_gate, hy_conv_w, hy_conv_b,
           flt_w1, flt_b1, flt_w2, flt_b2, flt_w3, flt_b3, flt_w4, flt_freq, hy_bias,
           q_norm_g, k_norm_g, w_br_hyena, w_br_attn, w_out,
           ffn_norm_g, w_ffn_gate, w_ffn_up, w_ffn_down):
    b, s, d = x.shape
    m = b * s
    row = lambda a: a.reshape(1, -1)
    tabs = _dft_tables()
    cos, sin = (jnp.asarray(t) for t in _rope_tables_np())
    xm = x.reshape(m, d)

    emb2, deltas = (jnp.asarray(t) for t in _filter_tables_np())
    filter_side = functools.partial(_filter_side_call, emb2, flt_w1, row(flt_b1), flt_w2, row(flt_b2),
                                    flt_w3, row(flt_b3), flt_w4, row(flt_freq), deltas)
    h, qkv, gates, w_out_bf16, w_hy_bf16, taps, l1 = _inproj(
        xm, row(mix_norm_g), w_in, row(b_gate), row(q_norm_g), row(k_norm_g), cos, sin, w_out, filter_side)
    filt_outer = _filt_outer(taps, tabs)

    conv_b = row(hy_conv_b)
    x0c, w_ba_bf16 = _inproj_conv("inproj_x0", h, w_hy_bf16, hy_conv_w, conv_b, [0], BF16, w_br_attn)
    z, w_bh_bf16 = _inproj_conv("inproj_z", h, w_hy_bf16, hy_conv_w, conv_b, [HYENA_W, 2 * HYENA_W], F32,
                                w_br_hyena)
    y_h = _hy_conv(z.reshape(b, s, HYENA_W), x0c.reshape(b, s, HYENA_W), row(hy_bias), filt_outer, l1,
                   tabs).reshape(m, HYENA_W)

    y_a = _attention(qkv.reshape(b, s, ATTN_W + 2 * KV_W)).reshape(m, ATTN_W)

    x1, h2 = _merge_out(y_h, y_a, gates, xm, row(ffn_norm_g), w_bh_bf16, w_ba_bf16, w_out_bf16)
    act, w_down_bf16 = _swiglu(h2, w_ffn_gate, w_ffn_up, w_ffn_down)
    out = _ffn_down(act, x1, w_down_bf16)
    return out.reshape(b, s, d)


def kernel(x, mix_norm_g, w_in, b_gate, hy_conv_w, hy_conv_b, flt_w1, flt_b1, flt_w2, flt_b2, flt_w3, flt_b3, flt_w4, flt_freq, hy_bias, q_norm_g, k_norm_g, w_br_hyena, w_br_attn, w_out, ffn_norm_g, w_ffn_gate, w_ffn_up, w_ffn_down):
    params = (mix_norm_g, w_in, b_gate, hy_conv_w, hy_conv_b, flt_w1, flt_b1, flt_w2, flt_b2,
              flt_w3, flt_b3, flt_w4, flt_freq, hy_bias, q_norm_g, k_norm_g, w_br_hyena, w_br_attn,
              w_out, ffn_norm_g, w_ffn_gate, w_ffn_up, w_ffn_down)
    for l in range(mix_norm_g.shape[0]):
        x = _layer(x, *(p[l] for p in params))
    return x
```

```python
import functools
import math

import numpy as np
import jax
import jax.numpy as jnp
from jax import lax
from jax.experimental import pallas as pl
from jax.experimental.pallas import tpu as pltpu

F32 = jnp.float32
BF16 = jnp.bfloat16

D_MODEL = 2048
SEQ = 4096
GRID_W = 64
HEAD_DIM = 128
N_Q_HEADS = 8
N_KV_HEADS = 2
Q_PER_KV = N_Q_HEADS // N_KV_HEADS
ATTN_W = N_Q_HEADS * HEAD_DIM
KV_W = N_KV_HEADS * HEAD_DIM
ROPE_THETA = 10000.0
HYENA_W = D_MODEL - ATTN_W
SHORT_TAPS = 3
FILTER_EMB = 33
FILTER_HIDDEN = 64
DECAY_TARGET = 1e-2
FAST_DECAY_PCT = 0.3
SLOW_DECAY_PCT = 1.5
EPS = 1e-6

COL_Q = 3 * HYENA_W
COL_G = COL_Q + ATTN_W + 2 * KV_W

LANES = 128
SUBLANES_F32 = 8
ROWS_BF16 = 16

FFT_N = 2 * SEQ
FFT_N1 = 64
FFT_N2 = 128

MIB = 1024 * 1024


def _cparams(n_axes, vmem_mib):
    return pltpu.CompilerParams(
        dimension_semantics=("arbitrary",) * n_axes,
        vmem_limit_bytes=vmem_mib * MIB,
    )


DFT_R = SUBLANES_F32


@functools.lru_cache(maxsize=None)
def _dft_tables_np():
    n, n1, n2, r = FFT_N, FFT_N1, FFT_N2, DFT_R
    eye = np.eye(r)
    f1 = np.arange(n1)
    ang1 = 2.0 * np.pi * ((f1[:, None] * f1[None, :]) % n1) / n1
    c1, s1 = np.cos(ang1), np.sin(ang1)
    h = n1 // 2
    m_fwd = np.block([[c1[:, :h], s1[:, :h]], [-s1[:, :h], c1[:, :h]]])
    m_flt = np.concatenate([c1, -s1], axis=0)
    ct, st = c1[:h, :], s1[:h, :]
    m_inv = np.block([[ct, -st], [st, ct]]) / n
    s2 = np.arange(n2)
    f = f1[:, None, None] + n1 * s2[None, :, None]
    th = 2.0 * np.pi * ((f * s2[None, None, :]) % n) / n
    c, s = np.cos(th), np.sin(th)
    fwd2 = np.concatenate(
        [np.concatenate([c, s], axis=2), np.concatenate([-s, c], axis=2)], axis=1)
    c_t, s_t = np.transpose(c, (0, 2, 1)), np.transpose(s, (0, 2, 1))
    inv2 = np.concatenate(
        [np.concatenate([c_t, -s_t], axis=2), np.concatenate([s_t, c_t], axis=2)], axis=1)
    f32 = lambda a: np.ascontiguousarray(a, dtype=np.float32)
    return dict(k_fwd=f32(np.kron(m_fwd, eye)), k_flt=f32(np.kron(m_flt, eye)),
                k_inv=f32(np.kron(m_inv, eye)), fwd2=f32(fwd2), inv2=f32(inv2))


def _dft_tables():
    return {k: jnp.asarray(v).astype(BF16) for k, v in _dft_tables_np().items()}


@functools.lru_cache(maxsize=None)
def _rope_tables_np():
    half = HEAD_DIM // 2
    inv = ROPE_THETA ** (-np.arange(0, half, 2, dtype=np.float64) / half)
    pos = np.arange(SEQ)
    ang_r = (pos // GRID_W)[:, None] * inv[None, :]
    ang_c = (pos % GRID_W)[:, None] * inv[None, :]
    cos = np.concatenate([np.cos(ang_r)] * 2 + [np.cos(ang_c)] * 2, axis=-1)
    sin = np.concatenate([-np.sin(ang_r), np.sin(ang_r), -np.sin(ang_c), np.sin(ang_c)], axis=-1)
    return cos.astype(np.float32), sin.astype(np.float32)


@functools.lru_cache(maxsize=None)
def _filter_tables_np():
    L = SEQ
    bands = (FILTER_EMB - 1) // 2
    pos = np.concatenate([np.arange(L, dtype=np.float64), L - np.arange(L, dtype=np.float64)])
    t = pos / max(L - 1, 1)
    fb = np.linspace(1e-4, bands - 1, bands)
    ang = (2.0 * math.pi * pos / L)[:, None] * fb[None, :]
    emb = np.concatenate([t[:, None], np.cos(ang), -np.sin(ang),
                          np.zeros((2 * L, FILTER_HIDDEN - FILTER_EMB))], axis=-1)
    max_decay = math.log(DECAY_TARGET) / FAST_DECAY_PCT
    min_decay = math.log(DECAY_TARGET) / SLOW_DECAY_PCT
    deltas = np.abs(np.linspace(min_decay, max_decay, HYENA_W))
    tile, hid = FLT_TR, FILTER_HIDDEN
    emb2 = emb.reshape(2 * L // tile, 2, tile // 2, hid).transpose(0, 2, 1, 3).reshape(L, 2 * hid)
    return np.ascontiguousarray(emb2, dtype=np.float32), deltas[None, :].astype(np.float32)


MM_TN = 512
MM_RC = 256


def _wres_kernel(*refs, pairs, n_act, n_extra, n_out, n_scratch, rc, epilogue, norm_first, side_cast,
                 side_call):
    n_w = len(pairs)
    acts = refs[:n_act]
    ws = refs[n_act:n_act + n_w]
    extras = refs[n_act + n_w:n_act + n_w + n_extra]
    o_refs = refs[n_act + n_w + n_extra:n_act + n_w + n_extra + n_out]
    wbs = refs[n_act + n_w + n_extra + n_out:]
    side_stages = []
    if side_call:
        fn, n_in, n_res = side_call
        step = pl.program_id(0) * pl.num_programs(1) + pl.program_id(1)
        side_stages = fn(step, *extras[-n_in:], *o_refs[-n_res:])
        extras, o_refs = extras[:-n_in], o_refs[:-n_res]
    for _ in range(side_cast):
        o_refs[-1][...] = extras[-1][...].astype(BF16)
        extras, o_refs = extras[:-1], o_refs[:-1]
    if norm_first:
        g_ref, extras = extras[-1], extras[:-1]
        h_ref, o_refs = o_refs[-1], o_refs[:-1]

    if n_scratch:
        @pl.when(pl.program_id(1) == 0)
        def _():
            for w_ref, wb_ref in zip(ws, wbs):
                wb_ref[...] = w_ref[...].astype(BF16)
    else:
        wbs = ws

    for c in range(o_refs[0].shape[0] // rc):
        rows = pl.ds(c * rc, rc)
        lhs = [a[rows, :] for a in acts]
        if norm_first:
            x = lhs[0]
            ms = jnp.mean(x * x, axis=-1, keepdims=True)
            lhs[0] = (x * lax.rsqrt(ms + EPS) * g_ref[...]).astype(BF16)
            h_ref[rows, :] = lhs[0]
        accs = [jnp.dot(lhs[a], wb_ref[...], preferred_element_type=F32)
                for a, wb_ref in zip(pairs, wbs)]
        if c < len(side_stages):
            side_stages[c]()
        outs = epilogue(accs, extras, rows)
        for o_ref, out in zip(o_refs, outs if isinstance(outs, tuple) else (outs,)):
            o_ref[rows, :] = out.astype(o_ref.dtype)
    assert len(side_stages) <= o_refs[0].shape[0] // rc


def _wres_matmul(name, acts, weights, extras, epilogue, *, pairs, n_out, out_dtype, tm,
                 tn=MM_TN, rc=MM_RC, vmem_mib=48, norm_gain=None, side_cast=None, side_call=None):
    m = acts[0].shape[0]
    grid = (n_out // tn, m // tm)
    out_dtypes = out_dtype if isinstance(out_dtype, tuple) else (out_dtype,)
    out_specs = [pl.BlockSpec((tm, tn), lambda j, i: (i, j)) for _ in out_dtypes]
    out_shapes = [jax.ShapeDtypeStruct((m, n_out), dt) for dt in out_dtypes]
    extras = list(extras)
    if norm_gain is not None:
        assert n_out == tn
        k0 = acts[0].shape[1]
        extras.append((norm_gain, (1, k0), lambda j, i: (0, 0)))
        out_specs.append(pl.BlockSpec((tm, k0), lambda j, i: (i, 0)))
        out_shapes.append(jax.ShapeDtypeStruct((m, k0), BF16))
    side_casts = side_cast if isinstance(side_cast, list) else [side_cast] * (side_cast is not None)
    for arr, ncols in [(a, a.shape[1]) if not isinstance(a, tuple) else a for a in side_casts]:
        slab = arr.shape[0] // (grid[0] * grid[1])
        assert slab * grid[0] * grid[1] == arr.shape[0] and slab % ROWS_BF16 == 0 and ncols % LANES == 0
        step = lambda j, i: (j * grid[1] + i, 0)
        extras.append((arr, (slab, ncols), step))
        out_specs.append(pl.BlockSpec((slab, ncols), step))
        out_shapes.append(jax.ShapeDtypeStruct((arr.shape[0], ncols), BF16))
    side = side_call(grid) if side_call is not None else None
    if side is not None:
        extras += side["inputs"]
        out_specs += [pl.BlockSpec(blk, imap) for _, blk, imap in side["outputs"]]
        out_shapes += [shape for shape, _, _ in side["outputs"]]
    in_specs = [pl.BlockSpec((tm, a.shape[1]), lambda j, i: (i, 0)) for a in acts]
    assert all(off % LANES == 0 for _, off in weights) and tn % LANES == 0
    w_mode = dict(pipeline_mode=pl.Buffered(1)) if n_out == tn else {}
    in_specs += [pl.BlockSpec((pl.Element(w.shape[0]), pl.Element(tn)),
                              lambda j, i, off=off: (0, pl.multiple_of(off + j * tn, LANES)), **w_mode)
                 for w, off in weights]
    in_specs += [pl.BlockSpec(blk, imap) for _, blk, imap in extras]
    scratch = [pltpu.VMEM((w.shape[0], tn), BF16) for w, _ in weights if w.dtype != BF16]
    assert len(scratch) in (0, len(weights))
    kern = functools.partial(_wres_kernel, pairs=tuple(pairs), n_act=len(acts), n_extra=len(extras),
                             n_out=len(out_specs), n_scratch=len(scratch), rc=rc, epilogue=epilogue,
                             norm_first=norm_gain is not None, side_cast=len(side_casts),
                             side_call=(side["fn"], len(side["inputs"]), len(side["outputs"])) if side else None)
    outs = pl.pallas_call(
        kern,
        grid=grid,
        in_specs=in_specs,
        out_specs=out_specs,
        out_shape=out_shapes,
        scratch_shapes=scratch,
        compiler_params=_cparams(2, vmem_mib),
        name=name,
    )(*acts, *(w for w, _ in weights), *(e for e, _, _ in extras))
    return outs if len(outs) > 1 else outs[0]


def _head_norm_rope(x, g, cos, sin, scale):
    ms = jnp.mean(x * x, axis=-1, keepdims=True)
    y = x * lax.rsqrt(ms + EPS) * g
    lane = lax.broadcasted_iota(jnp.int32, y.shape, 1)
    swapped = jnp.where((lane % 64) < 32, pltpu.roll(y, 96, 1), pltpu.roll(y, 32, 1))
    out = y * cos + swapped * sin
    return out * scale if scale != 1.0 else out


def _ep_qkv(accs, extras, rows):
    qg_ref, kg_ref, cos_ref, sin_ref = extras
    acc = accs[0]
    cos, sin = cos_ref[rows, :], sin_ref[rows, :]
    head = lambda hh: acc[:, hh * HEAD_DIM:(hh + 1) * HEAD_DIM]
    parts = [_head_norm_rope(head(hh), qg_ref[...], cos, sin, HEAD_DIM ** -0.5 * LOG2E)
             for hh in range(N_Q_HEADS)]
    parts += [_head_norm_rope(head(N_Q_HEADS + hh), kg_ref[...], cos, sin, 1.0) for hh in range(N_KV_HEADS)]
    parts.append(acc[:, (N_Q_HEADS + N_KV_HEADS) * HEAD_DIM:])
    return jnp.concatenate(parts, axis=1)


def _ep_gate(accs, extras, rows):
    return jax.nn.sigmoid(accs[0] + extras[0][...])


IN_TM = 2048
IN_TN = 1024
LOG2E = math.log2(math.e)


def _inproj(x, norm_g, w_in, b_gate, q_g, k_g, cos, sin, w_out, filter_side):
    tm, tn = IN_TM, IN_TN
    qkv_w = ATTN_W + 2 * KV_W
    qkv_tm = tm // 2
    head = lambda g: (g, (1, HEAD_DIM), lambda j, i: (0, 0))
    pos = lambda t: (t, (qkv_tm, HEAD_DIM), lambda j, i: (i % (SEQ // qkv_tm), 0))
    common = dict(pairs=[0], out_dtype=BF16, vmem_mib=56)
    qkv, h = _wres_matmul("inproj_qkv", [x], [(w_in, COL_Q)], [head(q_g), head(k_g), pos(cos), pos(sin)],
                          _ep_qkv, n_out=qkv_w, tn=qkv_w, tm=qkv_tm, norm_gain=norm_g, **common)
    gates, w_out_bf16, w_hy_bf16, taps, l1 = _wres_matmul(
        "inproj_gate", [h], [(w_in, COL_G)], [(b_gate, (1, tn), lambda j, i: (0, j))], _ep_gate,
        n_out=2 * D_MODEL, tn=tn, tm=tm, side_cast=[w_out, (w_in, COL_Q)], side_call=filter_side,
        **{**common, "vmem_mib": 60})
    return h, qkv, gates, w_out_bf16, w_hy_bf16, taps, l1


HC_TM = 1024
HC_RC = 512
HC_HALO = ROWS_BF16
HC_GROUPS = 3


def _inproj_conv_kernel(*refs):
    n_w = HC_GROUPS
    h_ref, top_ref, bot_ref = refs[:3]
    wb_refs = refs[3:3 + n_w]
    cw_refs = refs[3 + n_w:3 + 2 * n_w]
    cb_refs = refs[3 + 2 * n_w:3 + 3 * n_w]
    sides = refs[3 + 3 * n_w:5 + 3 * n_w]
    x0_ref, z_ref = refs[5 + 3 * n_w:7 + 3 * n_w]
    side_outs = refs[7 + 3 * n_w:]
    for side_ref, side_out_ref in zip(sides, side_outs):
        side_out_ref[...] = side_ref[...].astype(BF16)
    i = pl.program_id(0)
    tm, rc, hl = x0_ref.shape[0], HC_RC, HC_HALO
    seq_blocks = SEQ // tm
    top = jnp.where(i % seq_blocks != 0, top_ref[...], jnp.zeros_like(top_ref))
    bot = jnp.where(i % seq_blocks != seq_blocks - 1, bot_ref[...], jnp.zeros_like(bot_ref))
    nchunk = tm // rc
    for c in range(nchunk):
        lo, hi = c * rc - hl, (c + 1) * rc + hl
        parts = ([top] if c == 0 else []) + [h_ref[max(lo, 0):min(hi, tm), :]] + ([bot] if c == nchunk - 1 else [])
        lhs = jnp.concatenate(parts, axis=0) if len(parts) > 1 else parts[0]
        outs = []
        for wb_ref, cw_ref, cb_ref in zip(wb_refs, cw_refs, cb_refs):
            e = jnp.dot(lhs, wb_ref[...], preferred_element_type=F32)
            up = pltpu.roll(e, 1, 0)[hl:hl + rc]
            dn = pltpu.roll(e, rc + 2 * hl - 1, 0)[hl:hl + rc]
            outs.append(cb_ref[...] + up * cw_ref[0:1, :] + e[hl:hl + rc] * cw_ref[1:2, :]
                        + dn * cw_ref[2:3, :])
        x0_ref[c * rc:(c + 1) * rc, :] = outs[0].astype(x0_ref.dtype)
        z_ref[c * rc:(c + 1) * rc, :] = outs[2] * outs[1]


def _inproj_conv(h, w_hy, conv_w, conv_b, side_casts):
    m, d = h.shape
    tm, tn, hl = HC_TM, HYENA_W, HC_HALO
    steps = m // tm
    once = pl.Buffered(1)
    groups = range(HC_GROUPS)
    side_specs = []
    for a in side_casts:
        slab = a.shape[0] // steps
        assert slab * steps == a.shape[0] and slab % ROWS_BF16 == 0
        side_specs.append(pl.BlockSpec((slab, a.shape[1]), lambda i: (i, 0)))
    in_specs = [pl.BlockSpec((tm, d), lambda i: (i, 0)),
                pl.BlockSpec((hl, d), lambda i: (jnp.maximum(i * (tm // hl) - 1, 0), 0)),
                pl.BlockSpec((hl, d), lambda i: (jnp.minimum((i + 1) * (tm // hl), m // hl - 1), 0))]
    in_specs += [pl.BlockSpec((d, tn), lambda i, g=g: (0, g), pipeline_mode=once) for g in groups]
    in_specs += [pl.BlockSpec((SHORT_TAPS, tn), lambda i, g=g: (0, g)) for g in groups]
    in_specs += [pl.BlockSpec((1, tn), lambda i, g=g: (0, g)) for g in groups]
    in_specs += side_specs
    row_spec = pl.BlockSpec((tm, tn), lambda i: (i, 0))
    return pl.pallas_call(
        _inproj_conv_kernel,
        grid=(steps,),
        in_specs=in_specs,
        out_specs=[row_spec, row_spec] + side_specs,
        out_shape=[jax.ShapeDtypeStruct((m, tn), BF16), jax.ShapeDtypeStruct((m, tn), F32)]
        + [jax.ShapeDtypeStruct(a.shape, BF16) for a in side_casts],
        compiler_params=_cparams(1, 56),
        name="inproj_hyena",
    )(h, h, h, *([w_hy] * HC_GROUPS), *([conv_w] * HC_GROUPS), *([conv_b] * HC_GROUPS), *side_casts)


CV_CT = 256
CV_F1B = 16
CV_NF = FFT_N1 // CV_F1B
CV_SLAB = ROWS_BF16
CV_NSLAB = FFT_N2 // CV_SLAB
CV_HALF = FFT_N1 // 2


def _outer_fwd_slab(src_ref, k_ref, a_ref, j, rows_in):
    ct = a_ref.shape[-1]
    halves = []
    for h in range(2):
        r = src_ref[:, :, 2 * j + h] if src_ref.ndim == 5 else src_ref[:, 2 * j + h]
        r = r.reshape(rows_in, ct).astype(BF16)
        o = jnp.dot(k_ref[...], r, preferred_element_type=F32)
        halves.append(o.reshape(2 * FFT_N1, DFT_R, ct))
    slab = jnp.concatenate(halves, axis=1).astype(BF16)
    a_ref[:, :, pl.ds(pl.multiple_of(j * CV_SLAB, CV_SLAB), CV_SLAB), :] = slab.reshape(
        2, FFT_N1, CV_SLAB, ct)


def _hy_conv_kernel(z_ref, x0_ref, bias_ref, l1_ref, kf_ref, ki_ref, fw_ref, iv_ref, fa_ref, o_ref, a_ref):
    s = pl.program_id(2)
    ct = a_ref.shape[-1]

    @pl.when(s == 0)
    def _():
        def body(j, carry):
            _outer_fwd_slab(z_ref, kf_ref, a_ref, j, 2 * CV_HALF * DFT_R)
            return carry

        lax.fori_loop(0, CV_NSLAB, body, 0, unroll=4)

    @pl.when((s >= 1) & (s <= CV_NF))
    def _():
        f0 = (s - 1) * CV_F1B
        inv_l1 = 1.0 / l1_ref[...]

        def spectra(fl):
            u = jnp.dot(fw_ref[fl], a_ref[:, f0 + fl].reshape(2 * FFT_N2, ct), preferred_element_type=F32)
            k = jnp.dot(fw_ref[fl], fa_ref[:, fl].reshape(2 * FFT_N2, ct), preferred_element_type=F32)
            return u, k * inv_l1

        nxt = spectra(0)
        for fl in range(CV_F1B):
            u, k = nxt
            if fl + 1 < CV_F1B:
                nxt = spectra(fl + 1)
            ure, uim = u[:FFT_N2], u[FFT_N2:]
            kre, kim = k[:FFT_N2], k[FFT_N2:]
            p = jnp.concatenate([ure * kre - uim * kim, ure * kim + uim * kre], axis=0).astype(BF16)
            y = jnp.dot(iv_ref[fl], p, preferred_element_type=F32)
            a_ref[:, f0 + fl] = y.reshape(2, FFT_N2, ct).astype(BF16)

    @pl.when(s == CV_NF + 1)
    def _():
        bias = bias_ref[...]

        def body(j, carry):
            slab = a_ref[:, :, pl.ds(pl.multiple_of(j * CV_SLAB, CV_SLAB), CV_SLAB), :].astype(F32)
            x0 = x0_ref[:, :, j].astype(F32)
            halves = []
            for h in range(2):
                r = slab[:, :, h * DFT_R:(h + 1) * DFT_R, :].reshape(2 * FFT_N1 * DFT_R, ct).astype(BF16)
                y = jnp.dot(ki_ref[...], r, preferred_element_type=F32).reshape(2, CV_HALF, DFT_R, ct)
                z = z_ref[:, :, 2 * j + h]
                halves.append((y + bias * z) * x0[:, :, h * DFT_R:(h + 1) * DFT_R, :])
            o_ref[:, :, j] = jnp.concatenate(halves, axis=2).astype(o_ref.dtype)
            return carry

        lax.fori_loop(0, CV_NSLAB, body, 0, unroll=4)


def _hy_conv(z, x0c, bias, filt_outer, l1, tabs):
    b, L, c = z.shape
    assert b % 2 == 0 and L * 2 == FFT_N
    ct = CV_CT
    z5 = z.reshape(b, CV_HALF, FFT_N2 // DFT_R, DFT_R, c)
    x5 = x0c.reshape(b, CV_HALF, CV_NSLAB, CV_SLAB, c)
    fidx = lambda s: jnp.clip(s - 1, 0, CV_NF - 1)
    once = pl.Buffered(1)
    tab_spec = pl.BlockSpec((CV_F1B, 2 * FFT_N2, 2 * FFT_N2), lambda p, cb, s: (fidx(s), 0, 0))
    io16 = pl.BlockSpec((2, CV_HALF, CV_NSLAB, CV_SLAB, ct), lambda p, cb, s: (p, 0, 0, 0, cb))
    out = pl.pallas_call(
        _hy_conv_kernel,
        grid=(b // 2, c // ct, CV_NF + 2),
        in_specs=[
            pl.BlockSpec((2, CV_HALF, FFT_N2 // DFT_R, DFT_R, ct), lambda p, cb, s: (p, 0, 0, 0, cb)),
            io16,
            pl.BlockSpec((1, ct), lambda p, cb, s: (0, cb)),
            pl.BlockSpec((1, ct), lambda p, cb, s: (0, cb)),
            pl.BlockSpec(tabs["k_fwd"].shape, lambda p, cb, s: (0, 0), pipeline_mode=once),
            pl.BlockSpec(tabs["k_inv"].shape, lambda p, cb, s: (0, 0), pipeline_mode=once),
            tab_spec, tab_spec,
            pl.BlockSpec((2, CV_F1B, FFT_N2, ct), lambda p, cb, s: (0, fidx(s), 0, cb)),
        ],
        out_specs=io16,
        out_shape=jax.ShapeDtypeStruct(x5.shape, BF16),
        scratch_shapes=[pltpu.VMEM((2, FFT_N1, FFT_N2, ct), BF16)],
        compiler_params=_cparams(3, 60),
        name="hyena_conv",
    )(z5, x5, bias, l1, tabs["k_fwd"], tabs["k_inv"], tabs["fwd2"], tabs["inv2"], filt_outer)
    return out.reshape(b, L, c)


def _filt_outer_kernel(t_ref, kf_ref, o_ref):
    def body(j, carry):
        _outer_fwd_slab(t_ref, kf_ref, o_ref, j, FFT_N1 * DFT_R)
        return carry

    lax.fori_loop(0, CV_NSLAB, body, 0, unroll=4)


def _filt_outer(taps, tabs):
    n, c = taps.shape
    ct = CV_CT
    t4 = taps.reshape(FFT_N1, FFT_N2 // DFT_R, DFT_R, c)
    return pl.pallas_call(
        _filt_outer_kernel,
        grid=(c // ct,),
        in_specs=[
            pl.BlockSpec((FFT_N1, FFT_N2 // DFT_R, DFT_R, ct), lambda cb: (0, 0, 0, cb)),
            pl.BlockSpec(tabs["k_flt"].shape, lambda cb: (0, 0), pipeline_mode=pl.Buffered(1)),
        ],
        out_specs=pl.BlockSpec((2, FFT_N1, FFT_N2, ct), lambda cb: (0, 0, 0, cb)),
        out_shape=jax.ShapeDtypeStruct((2, FFT_N1, FFT_N2, c), BF16),
        compiler_params=_cparams(1, 48),
        name="filter_outer",
    )(t4, tabs["k_flt"])


FLT_TR = 512
FLT_HALF = FLT_TR // 2


def _dot3(a, w):
    a_hi = a.astype(BF16)
    a_lo = (a - a_hi.astype(F32)).astype(BF16)
    w_hi = w.astype(BF16)
    w_lo = (w - w_hi.astype(F32)).astype(BF16)
    lhs = jnp.concatenate([a_hi, a_hi, a_lo], axis=1)
    rhs = jnp.concatenate([w_hi, w_lo, w_hi], axis=0)
    return jnp.dot(lhs, rhs, preferred_element_type=F32)


def _filter_step(r, emb_ref, w1_ref, b1_ref, w2_ref, b2_ref, w3_ref, b3_ref, w4t_ref, w4b_ref, fr_ref, dl_ref,
                 k_ref, l1_ref):
    st = {}

    def hidden(w_ref, b_ref, src):
        def stage():
            st["h"] = jnp.sin(fr_ref[...] * (_dot3(st[src] if src == "h" else emb_ref[...], w_ref[...])
                                             + b_ref[...]))
        return stage

    def taps_half(part, w_ref):
        def stage():
            tcol = part * FILTER_HIDDEN
            taps = jnp.dot(st["h"].astype(BF16), w_ref[...].astype(BF16),
                           preferred_element_type=F32)
            decay = jnp.exp(-emb_ref[:, tcol:tcol + 1] * dl_ref[...])
            rows = r * FLT_TR + part * FLT_HALF + lax.broadcasted_iota(jnp.int32, taps.shape, 0)
            taps = jnp.where(rows == SEQ, 0.0, taps * decay)
            k_ref[part * FLT_HALF:(part + 1) * FLT_HALF, :] = taps
            total = jnp.sum(jnp.abs(taps), axis=0, keepdims=True)
            if part == 0:
                st["l1"] = jnp.where(r == 0, 0.0, l1_ref[...]) + total
            else:
                l1_ref[...] = st["l1"] + total
        return stage

    return [hidden(w1_ref, b1_ref, "emb"), hidden(w2_ref, b2_ref, "h"), hidden(w3_ref, b3_ref, "h"),
            taps_half(0, w4t_ref), taps_half(1, w4b_ref)]


def _filter_side_call(emb2, w1, b1, w2, b2, w3, b3, w4, freq, deltas, steps):
    n = 2 * emb2.shape[0]
    assert n // FLT_TR == steps[0] * steps[1]
    fh = FILTER_HIDDEN
    eye2 = jnp.eye(2, dtype=F32)
    w1p = jnp.concatenate([w1, jnp.zeros((fh - w1.shape[0], fh), F32)], axis=0)
    wd = [jnp.kron(eye2, w) for w in (w1p, w2, w3)]
    bd = [jnp.tile(b, (1, 2)) for b in (b1, b2, b3, freq)]
    zeros = jnp.zeros_like(w4)
    w4t = jnp.concatenate([w4, zeros], axis=0)
    w4b = jnp.concatenate([zeros, w4], axis=0)
    step = lambda j, i: j * steps[1] + i
    const = lambda j, i: (0, 0)
    small = lambda a: (a, a.shape, const)
    fwd_tiles = SEQ // FLT_TR
    w4_half = lambda a: (a, (2 * fh, HYENA_W), lambda j, i: (0, step(j, i) // fwd_tiles))
    inputs = [(emb2, (FLT_HALF, 2 * fh), lambda j, i: (step(j, i), 0)),
              small(wd[0]), small(bd[0]), small(wd[1]), small(bd[1]), small(wd[2]), small(bd[2]),
              w4_half(w4t), w4_half(w4b), small(bd[3]), small(deltas)]
    outputs = [(jax.ShapeDtypeStruct((n, HYENA_W), F32), (FLT_TR, HYENA_W), lambda j, i: (step(j, i), 0)),
               (jax.ShapeDtypeStruct((1, HYENA_W), F32), (1, HYENA_W), const)]
    return dict(fn=_filter_step, inputs=inputs, outputs=outputs)


AT_TQ = 512
AT_RQ = 256
AT_TK = 512


def _attn_kernel(q_ref, k_ref, v_ref, o_ref, s0_ref, s1_ref):
    s_refs = (s0_ref, s1_ref)
    nblk = k_ref.shape[0] // AT_TK
    units = [(pl.ds(r * AT_RQ, AT_RQ), slice(g * HEAD_DIM, (g + 1) * HEAD_DIM))
             for r in range(q_ref.shape[0] // AT_RQ) for g in range(Q_PER_KV)]
    lane_blocks = lambda a: [a[:, i:i + LANES] for i in range(0, a.shape[1], LANES)]

    def scores(u, j, m_run):
        rows, lanes = units[u]
        ks = pl.ds(j * AT_TK, AT_TK)
        s = lax.dot_general(q_ref[rows, lanes], k_ref[ks, :], (((1,), (1,)), ((), ())),
                            preferred_element_type=F32)
        s_refs[u % 2][:, ks] = s
        blk = functools.reduce(jnp.maximum, lane_blocks(s))
        return blk if m_run is None else jnp.maximum(m_run, blk)

    def weighted(u, j, m, l_run, acc):
        ks = pl.ds(j * AT_TK, AT_TK)
        p = jnp.exp2((s_refs[u % 2][:, ks] - m).astype(BF16))
        l_blk = functools.reduce(jnp.add, lane_blocks(p)).astype(F32)
        o = jnp.dot(p, v_ref[ks, :], preferred_element_type=F32)
        return (l_blk if l_run is None else l_run + l_blk), (o if acc is None else acc + o)

    m_run = None
    for j in range(nblk):
        m_run = scores(0, j, m_run)
    for u, (rows, lanes) in enumerate(units):
        m = jnp.max(m_run, axis=-1, keepdims=True)
        m_run, l_run, acc = None, None, None
        for j in range(nblk):
            if u + 1 < len(units):
                m_run = scores(u + 1, j, m_run)
            l_run, acc = weighted(u, j, m, l_run, acc)
        l = jnp.sum(l_run, axis=-1, keepdims=True)
        o_ref[rows, lanes] = (acc / l).astype(o_ref.dtype)


def _attention(qkv3):
    b, s, _ = qkv3.shape
    gw = Q_PER_KV * HEAD_DIM
    k0 = N_Q_HEADS
    v0 = N_Q_HEADS + N_KV_HEADS
    return pl.pallas_call(
        _attn_kernel,
        grid=(b, N_KV_HEADS, s // AT_TQ),
        in_specs=[pl.BlockSpec((None, AT_TQ, gw), lambda bi, kv, qi: (bi, qi, kv)),
                  pl.BlockSpec((None, s, HEAD_DIM), lambda bi, kv, qi: (bi, 0, k0 + kv)),
                  pl.BlockSpec((None, s, HEAD_DIM), lambda bi, kv, qi: (bi, 0, v0 + kv))],
        out_specs=pl.BlockSpec((None, AT_TQ, gw), lambda bi, kv, qi: (bi, qi, kv)),
        out_shape=jax.ShapeDtypeStruct((b, s, ATTN_W), BF16),
        scratch_shapes=[pltpu.VMEM((AT_RQ, s), F32), pltpu.VMEM((AT_RQ, s), F32)],
        compiler_params=_cparams(3, 48),
        name="attention",
    )(qkv3, qkv3, qkv3)


def _ep_residual(accs, extras, rows):
    return extras[0][rows, :] + accs[0]


def _ep_swiglu(accs, extras, rows):
    return jax.nn.silu(accs[0]) * accs[1]


def _merge_out_kernel(yh_ref, ya_ref, gt_ref, x_ref, g_ref, wh_ref, wa_ref, wo_ref, x1_ref, h2_ref):
    d = x_ref.shape[1]
    for c in range(x_ref.shape[0] // MM_RC):
        rows = pl.ds(c * MM_RC, MM_RC)
        ph = jnp.dot(yh_ref[rows, :], wh_ref[...], preferred_element_type=F32)
        pa = jnp.dot(ya_ref[rows, :], wa_ref[...], preferred_element_type=F32)
        merged = gt_ref[rows, :d].astype(F32) * ph + gt_ref[rows, d:].astype(F32) * pa
        x1 = x_ref[rows, :] + jnp.dot(merged.astype(BF16), wo_ref[...], preferred_element_type=F32)
        ms = jnp.mean(x1 * x1, axis=-1, keepdims=True)
        x1_ref[rows, :] = x1
        h2_ref[rows, :] = (x1 * lax.rsqrt(ms + EPS) * g_ref[...]).astype(h2_ref.dtype)


def _merge_out(yh, ya, gates, x, norm_g, w_h, w_a, w_o, tm=512):
    m, d = x.shape
    once = pl.Buffered(1)
    rows = lambda a: pl.BlockSpec((tm, a.shape[1]), lambda i: (i, 0))
    whole = lambda a: pl.BlockSpec(a.shape, lambda i: (0, 0), pipeline_mode=once)
    return pl.pallas_call(
        _merge_out_kernel,
        grid=(m // tm,),
        in_specs=[rows(yh), rows(ya), rows(gates), rows(x), pl.BlockSpec((1, d), lambda i: (0, 0)),
                  whole(w_h), whole(w_a), whole(w_o)],
        out_specs=[pl.BlockSpec((tm, d), lambda i: (i, 0))] * 2,
        out_shape=[jax.ShapeDtypeStruct((m, d), F32), jax.ShapeDtypeStruct((m, d), BF16)],
        compiler_params=_cparams(1, 56),
        name="merge_out",
    )(yh, ya, gates, x, norm_g, w_h, w_a, w_o)


def _ffn_down(a, x, w_bf16, tm=512, tn=1024):
    return _wres_matmul("ffn_down", [a], [(w_bf16, 0)], [(x, (tm, tn), lambda j, i: (i, j))], _ep_residual,
                        pairs=[0], n_out=w_bf16.shape[1], out_dtype=F32, tm=tm, tn=tn, vmem_mib=56)


def _swiglu(h, w_g, w_u, w_down, tm=2048):
    return _wres_matmul("swiglu", [h], [(w_g, 0), (w_u, 0)], [], _ep_swiglu,
                        pairs=[0, 0], n_out=w_g.shape[1], out_dtype=BF16, tm=tm, vmem_mib=56,
                        side_cast=w_down)


def _layer(x, mix_norm_g, w_in, b_gate, hy_conv_w, hy_conv_b,
           flt_w1, flt_b1, flt_w2, flt_b2, flt_w3, flt_b3, flt_w4, flt_freq, hy_bias,
           q_norm_g, k_norm_g, w_br_hyena, w_br_attn, w_out,
           ffn_norm_g, w_ffn_gate, w_ffn_up, w_ffn_down):
    b, s, d = x.shape
    m = b * s
    row = lambda a: a.reshape(1, -1)
    tabs = _dft_tables()
    cos, sin = (jnp.asarray(t) for t in _rope_tables_np())
    xm = x.reshape(m, d)

    emb2, deltas = (jnp.asarray(t) for t in _filter_tables_np())
    filter_side = functools.partial(_filter_side_call, emb2, flt_w1, row(flt_b1), flt_w2, row(flt_b2),
                                    flt_w3, row(flt_b3), flt_w4, row(flt_freq), deltas)
    h, qkv, gates, w_out_bf16, w_hy_bf16, taps, l1 = _inproj(
        xm, row(mix_norm_g), w_in, row(b_gate), row(q_norm_g), row(k_norm_g), cos, sin, w_out, filter_side)
    filt_outer = _filt_outer(taps, tabs)

    x0c, z, w_ba_bf16, w_bh_bf16 = _inproj_conv(h, w_hy_bf16, hy_conv_w, row(hy_conv_b),
                                                [w_br_attn, w_br_hyena])
    y_h = _hy_conv(z.reshape(b, s, HYENA_W), x0c.reshape(b, s, HYENA_W), row(hy_bias), filt_outer, l1,
                   tabs).reshape(m, HYENA_W)

    y_a = _attention(qkv.reshape(b, s, ATTN_W + 2 * KV_W)).reshape(m, ATTN_W)

    x1, h2 = _merge_out(y_h, y_a, gates, xm, row(ffn_norm_g), w_bh_bf16, w_ba_bf16, w_out_bf16)
    act, w_down_bf16 = _swiglu(h2, w_ffn_gate, w_ffn_up, w_ffn_down)
    out = _ffn_down(act, x1, w_down_bf16)
    return out.reshape(b, s, d)


def kernel(x, mix_norm_g, w_in, b_gate, hy_conv_w, hy_conv_b, flt_w1, flt_b1, flt_w2, flt_b2, flt_w3, flt_b3, flt_w4, flt_freq, hy_bias, q_norm_g, k_norm_g, w_br_hyena, w_br_attn, w_out, ffn_norm_g, w_ffn_gate, w_ffn_up, w_ffn_down):
    params = (mix_norm_g, w_in, b_gate, hy_conv_w, hy_conv_b, flt_w1, flt_b1, flt_w2, flt_b2,
              flt_w3, flt_b3, flt_w4, flt_freq, hy_bias, q_norm_g, k_norm_g, w_br_hyena, w_br_attn,
              w_out, ffn_norm_g, w_ffn_gate, w_ffn_up, w_ffn_down)
    for l in range(mix_norm_g.shape[0]):
        x = _layer(x, *(p[l] for p in params))
    return x
```

```python
import functools
import math

import numpy as np
import jax
import jax.numpy as jnp
from jax import lax
from jax.experimental import pallas as pl
from jax.experimental.pallas import tpu as pltpu

F32 = jnp.float32
BF16 = jnp.bfloat16

D_MODEL = 2048
SEQ = 4096
GRID_W = 64
HEAD_DIM = 128
N_Q_HEADS = 8
N_KV_HEADS = 2
Q_PER_KV = N_Q_HEADS // N_KV_HEADS
ATTN_W = N_Q_HEADS * HEAD_DIM
KV_W = N_KV_HEADS * HEAD_DIM
ROPE_THETA = 10000.0
HYENA_W = D_MODEL - ATTN_W
SHORT_TAPS = 3
FILTER_EMB = 33
FILTER_HIDDEN = 64
DECAY_TARGET = 1e-2
FAST_DECAY_PCT = 0.3
SLOW_DECAY_PCT = 1.5
EPS = 1e-6

COL_Q = 3 * HYENA_W
COL_G = COL_Q + ATTN_W + 2 * KV_W

LANES = 128
SUBLANES_F32 = 8
ROWS_BF16 = 16

FFT_N = 2 * SEQ
FFT_N1 = 64
FFT_N2 = 128

MIB = 1024 * 1024


def _cparams(n_axes, vmem_mib):
    return pltpu.CompilerParams(
        dimension_semantics=("arbitrary",) * n_axes,
        vmem_limit_bytes=vmem_mib * MIB,
    )


DFT_R = SUBLANES_F32


@functools.lru_cache(maxsize=None)
def _dft_tables_np():
    n, n1, n2, r = FFT_N, FFT_N1, FFT_N2, DFT_R
    eye = np.eye(r)
    f1 = np.arange(n1)
    ang1 = 2.0 * np.pi * ((f1[:, None] * f1[None, :]) % n1) / n1
    c1, s1 = np.cos(ang1), np.sin(ang1)
    h = n1 // 2
    m_fwd = np.block([[c1[:, :h], s1[:, :h]], [-s1[:, :h], c1[:, :h]]])
    m_flt = np.concatenate([c1, -s1], axis=0)
    ct, st = c1[:h, :], s1[:h, :]
    m_inv = np.block([[ct, -st], [st, ct]]) / n
    s2 = np.arange(n2)
    f = f1[:, None, None] + n1 * s2[None, :, None]
    th = 2.0 * np.pi * ((f * s2[None, None, :]) % n) / n
    c, s = np.cos(th), np.sin(th)
    fwd2 = np.concatenate(
        [np.concatenate([c, s], axis=2), np.concatenate([-s, c], axis=2)], axis=1)
    c_t, s_t = np.transpose(c, (0, 2, 1)), np.transpose(s, (0, 2, 1))
    inv2 = np.concatenate(
        [np.concatenate([c_t, -s_t], axis=2), np.concatenate([s_t, c_t], axis=2)], axis=1)
    f32 = lambda a: np.ascontiguousarray(a, dtype=np.float32)
    return dict(k_fwd=f32(np.kron(m_fwd, eye)), k_flt=f32(np.kron(m_flt, eye)),
                k_inv=f32(np.kron(m_inv, eye)), fwd2=f32(fwd2), inv2=f32(inv2))


def _dft_tables():
    return {k: jnp.asarray(v).astype(BF16) for k, v in _dft_tables_np().items()}


@functools.lru_cache(maxsize=None)
def _rope_tables_np():
    half = HEAD_DIM // 2
    inv = ROPE_THETA ** (-np.arange(0, half, 2, dtype=np.float64) / half)
    pos = np.arange(SEQ)
    ang_r = (pos // GRID_W)[:, None] * inv[None, :]
    ang_c = (pos % GRID_W)[:, None] * inv[None, :]
    cos = np.concatenate([np.cos(ang_r)] * 2 + [np.cos(ang_c)] * 2, axis=-1)
    sin = np.concatenate([-np.sin(ang_r), np.sin(ang_r), -np.sin(ang_c), np.sin(ang_c)], axis=-1)
    return cos.astype(np.float32), sin.astype(np.float32)


@functools.lru_cache(maxsize=None)
def _filter_tables_np():
    L = SEQ
    bands = (FILTER_EMB - 1) // 2
    pos = np.concatenate([np.arange(L, dtype=np.float64), L - np.arange(L, dtype=np.float64)])
    t = pos / max(L - 1, 1)
    fb = np.linspace(1e-4, bands - 1, bands)
    ang = (2.0 * math.pi * pos / L)[:, None] * fb[None, :]
    emb = np.concatenate([t[:, None], np.cos(ang), -np.sin(ang),
                          np.zeros((2 * L, FILTER_HIDDEN - FILTER_EMB))], axis=-1)
    max_decay = math.log(DECAY_TARGET) / FAST_DECAY_PCT
    min_decay = math.log(DECAY_TARGET) / SLOW_DECAY_PCT
    deltas = np.abs(np.linspace(min_decay, max_decay, HYENA_W))
    tile, hid = FLT_TR, FILTER_HIDDEN
    emb2 = emb.reshape(2 * L // tile, 2, tile // 2, hid).transpose(0, 2, 1, 3).reshape(L, 2 * hid)
    return np.ascontiguousarray(emb2, dtype=np.float32), deltas[None, :].astype(np.float32)


MM_TN = 512
MM_RC = 256


def _wres_kernel(*refs, pairs, n_act, n_extra, n_out, n_scratch, rc, epilogue, norm_first, side_cast,
                 side_call):
    n_w = len(pairs)
    acts = refs[:n_act]
    ws = refs[n_act:n_act + n_w]
    extras = refs[n_act + n_w:n_act + n_w + n_extra]
    o_refs = refs[n_act + n_w + n_extra:n_act + n_w + n_extra + n_out]
    wbs = refs[n_act + n_w + n_extra + n_out:]
    side_stages = []
    if side_call:
        fn, n_in, n_res = side_call
        step = pl.program_id(0) * pl.num_programs(1) + pl.program_id(1)
        side_stages = fn(step, *extras[-n_in:], *o_refs[-n_res:])
        extras, o_refs = extras[:-n_in], o_refs[:-n_res]
    for _ in range(side_cast):
        o_refs[-1][...] = extras[-1][...].astype(BF16)
        extras, o_refs = extras[:-1], o_refs[:-1]
    if norm_first:
        g_ref, extras = extras[-1], extras[:-1]
        h_ref, o_refs = o_refs[-1], o_refs[:-1]

    if n_scratch:
        @pl.when(pl.program_id(1) == 0)
        def _():
            for w_ref, wb_ref in zip(ws, wbs):
                wb_ref[...] = w_ref[...].astype(BF16)
    else:
        wbs = ws

    for c in range(o_refs[0].shape[0] // rc):
        rows = pl.ds(c * rc, rc)
        lhs = [a[rows, :] for a in acts]
        if norm_first:
            x = lhs[0]
            ms = jnp.mean(x * x, axis=-1, keepdims=True)
            lhs[0] = (x * lax.rsqrt(ms + EPS) * g_ref[...]).astype(BF16)
            h_ref[rows, :] = lhs[0]
        accs = [jnp.dot(lhs[a], wb_ref[...], preferred_element_type=F32)
                for a, wb_ref in zip(pairs, wbs)]
        if c < len(side_stages):
            side_stages[c]()
        outs = epilogue(accs, extras, rows)
        for o_ref, out in zip(o_refs, outs if isinstance(outs, tuple) else (outs,)):
            o_ref[rows, :] = out.astype(o_ref.dtype)
    assert len(side_stages) <= o_refs[0].shape[0] // rc


def _wres_matmul(name, acts, weights, extras, epilogue, *, pairs, n_out, out_dtype, tm,
                 tn=MM_TN, rc=MM_RC, vmem_mib=48, norm_gain=None, side_cast=None, side_call=None):
    m = acts[0].shape[0]
    grid = (n_out // tn, m // tm)
    out_dtypes = out_dtype if isinstance(out_dtype, tuple) else (out_dtype,)
    out_specs = [pl.BlockSpec((tm, tn), lambda j, i: (i, j)) for _ in out_dtypes]
    out_shapes = [jax.ShapeDtypeStruct((m, n_out), dt) for dt in out_dtypes]
    extras = list(extras)
    if norm_gain is not None:
        assert n_out == tn
        k0 = acts[0].shape[1]
        extras.append((norm_gain, (1, k0), lambda j, i: (0, 0)))
        out_specs.append(pl.BlockSpec((tm, k0), lambda j, i: (i, 0)))
        out_shapes.append(jax.ShapeDtypeStruct((m, k0), BF16))
    side_casts = side_cast if isinstance(side_cast, list) else [side_cast] * (side_cast is not None)
    for arr, ncols in [(a, a.shape[1]) if not isinstance(a, tuple) else a for a in side_casts]:
        slab = arr.shape[0] // (grid[0] * grid[1])
        assert slab * grid[0] * grid[1] == arr.shape[0] and slab % ROWS_BF16 == 0 and ncols % LANES == 0
        step = lambda j, i: (j * grid[1] + i, 0)
        extras.append((arr, (slab, ncols), step))
        out_specs.append(pl.BlockSpec((slab, ncols), step))
        out_shapes.append(jax.ShapeDtypeStruct((arr.shape[0], ncols), BF16))
    side = side_call(grid) if side_call is not None else None
    if side is not None:
        extras += side["inputs"]
        out_specs += [pl.BlockSpec(blk, imap) for _, blk, imap in side["outputs"]]
        out_shapes += [shape for shape, _, _ in side["outputs"]]
    in_specs = [pl.BlockSpec((tm, a.shape[1]), lambda j, i: (i, 0)) for a in acts]
    assert all(off % LANES == 0 for _, off in weights) and tn % LANES == 0
    w_mode = dict(pipeline_mode=pl.Buffered(1)) if n_out == tn else {}
    in_specs += [pl.BlockSpec((pl.Element(w.shape[0]), pl.Element(tn)),
                              lambda j, i, off=off: (0, pl.multiple_of(off + j * tn, LANES)), **w_mode)
                 for w, off in weights]
    in_specs += [pl.BlockSpec(blk, imap) for _, blk, imap in extras]
    scratch = [pltpu.VMEM((w.shape[0], tn), BF16) for w, _ in weights if w.dtype != BF16]
    assert len(scratch) in (0, len(weights))
    kern = functools.partial(_wres_kernel, pairs=tuple(pairs), n_act=len(acts), n_extra=len(extras),
                             n_out=len(out_specs), n_scratch=len(scratch), rc=rc, epilogue=epilogue,
                             norm_first=norm_gain is not None, side_cast=len(side_casts),
                             side_call=(side["fn"], len(side["inputs"]), len(side["outputs"])) if side else None)
    outs = pl.pallas_call(
        kern,
        grid=grid,
        in_specs=in_specs,
        out_specs=out_specs,
        out_shape=out_shapes,
        scratch_shapes=scratch,
        compiler_params=_cparams(2, vmem_mib),
        name=name,
    )(*acts, *(w for w, _ in weights), *(e for e, _, _ in extras))
    return outs if len(outs) > 1 else outs[0]


def _head_norm_rope(x, g, cos, sin, scale):
    ms = jnp.mean(x * x, axis=-1, keepdims=True)
    y = x * lax.rsqrt(ms + EPS) * g
    lane = lax.broadcasted_iota(jnp.int32, y.shape, 1)
    swapped = jnp.where((lane % 64) < 32, pltpu.roll(y, 96, 1), pltpu.roll(y, 32, 1))
    out = y * cos + swapped * sin
    return out * scale if scale != 1.0 else out


def _ep_qkv(accs, extras, rows):
    qg_ref, kg_ref, cos_ref, sin_ref = extras
    acc = accs[0]
    cos, sin = cos_ref[rows, :], sin_ref[rows, :]
    head = lambda hh: acc[:, hh * HEAD_DIM:(hh + 1) * HEAD_DIM]
    parts = [_head_norm_rope(head(hh), qg_ref[...], cos, sin, HEAD_DIM ** -0.5 * LOG2E)
             for hh in range(N_Q_HEADS)]
    parts += [_head_norm_rope(head(N_Q_HEADS + hh), kg_ref[...], cos, sin, 1.0) for hh in range(N_KV_HEADS)]
    parts.append(acc[:, (N_Q_HEADS + N_KV_HEADS) * HEAD_DIM:])
    return jnp.concatenate(parts, axis=1)


def _ep_gate(accs, extras, rows):
    return jax.nn.sigmoid(accs[0] + extras[0][...])


IN_TM = 2048
IN_TN = 1024
LOG2E = math.log2(math.e)


def _inproj_qkv(x, norm_g, w_in, q_g, k_g, cos, sin):
    qkv_w = ATTN_W + 2 * KV_W
    tm = IN_TM // 2
    head = lambda g: (g, (1, HEAD_DIM), lambda j, i: (0, 0))
    pos = lambda t: (t, (tm, HEAD_DIM), lambda j, i: (i % (SEQ // tm), 0))
    return _wres_matmul("inproj_qkv", [x], [(w_in, COL_Q)], [head(q_g), head(k_g), pos(cos), pos(sin)],
                        _ep_qkv, pairs=[0], n_out=qkv_w, out_dtype=BF16, tn=qkv_w, tm=tm, norm_gain=norm_g,
                        vmem_mib=56)


def _inproj_gate(h, w_in, b_gate, filter_side):
    tn = IN_TN
    return _wres_matmul("inproj_gate", [h], [(w_in, COL_G)], [(b_gate, (1, tn), lambda j, i: (0, j))],
                        _ep_gate, pairs=[0], n_out=2 * D_MODEL, out_dtype=BF16, tn=tn, tm=IN_TM,
                        side_call=filter_side, vmem_mib=56)


HC_TM = 1024
HC_RC = 512
HC_HALO = ROWS_BF16
HC_GROUPS = 3


def _inproj_conv_kernel(*refs):
    n_w = HC_GROUPS
    h_ref, top_ref, bot_ref = refs[:3]
    wb_refs = refs[3:3 + n_w]
    cw_refs = refs[3 + n_w:3 + 2 * n_w]
    cb_refs = refs[3 + 2 * n_w:3 + 3 * n_w]
    x0_ref, z_ref = refs[3 + 3 * n_w:]
    i = pl.program_id(0)
    tm, rc, hl = x0_ref.shape[0], HC_RC, HC_HALO
    seq_blocks = SEQ // tm
    top = jnp.where(i % seq_blocks != 0, top_ref[...], jnp.zeros_like(top_ref))
    bot = jnp.where(i % seq_blocks != seq_blocks - 1, bot_ref[...], jnp.zeros_like(bot_ref))
    nchunk = tm // rc
    for c in range(nchunk):
        lo, hi = c * rc - hl, (c + 1) * rc + hl
        parts = ([top] if c == 0 else []) + [h_ref[max(lo, 0):min(hi, tm), :]] + ([bot] if c == nchunk - 1 else [])
        lhs = jnp.concatenate(parts, axis=0) if len(parts) > 1 else parts[0]
        outs = []
        for wb_ref, cw_ref, cb_ref in zip(wb_refs, cw_refs, cb_refs):
            e = jnp.dot(lhs, wb_ref[...], preferred_element_type=F32)
            up = pltpu.roll(e, 1, 0)[hl:hl + rc]
            dn = pltpu.roll(e, rc + 2 * hl - 1, 0)[hl:hl + rc]
            outs.append(cb_ref[...] + up * cw_ref[0:1, :] + e[hl:hl + rc] * cw_ref[1:2, :]
                        + dn * cw_ref[2:3, :])
        x0_ref[c * rc:(c + 1) * rc, :] = outs[0].astype(x0_ref.dtype)
        z_ref[c * rc:(c + 1) * rc, :] = outs[2] * outs[1]


def _inproj_conv(h, w_hy, conv_w, conv_b):
    m, d = h.shape
    tm, tn, hl = HC_TM, HYENA_W, HC_HALO
    once = pl.Buffered(1)
    groups = range(HC_GROUPS)
    in_specs = [pl.BlockSpec((tm, d), lambda i: (i, 0)),
                pl.BlockSpec((hl, d), lambda i: (jnp.maximum(i * (tm // hl) - 1, 0), 0)),
                pl.BlockSpec((hl, d), lambda i: (jnp.minimum((i + 1) * (tm // hl), m // hl - 1), 0))]
    in_specs += [pl.BlockSpec((d, tn), lambda i, g=g: (0, g), pipeline_mode=once) for g in groups]
    in_specs += [pl.BlockSpec((SHORT_TAPS, tn), lambda i, g=g: (0, g)) for g in groups]
    in_specs += [pl.BlockSpec((1, tn), lambda i, g=g: (0, g)) for g in groups]
    row_spec = pl.BlockSpec((tm, tn), lambda i: (i, 0))
    return pl.pallas_call(
        _inproj_conv_kernel,
        grid=(m // tm,),
        in_specs=in_specs,
        out_specs=[row_spec, row_spec],
        out_shape=[jax.ShapeDtypeStruct((m, tn), BF16), jax.ShapeDtypeStruct((m, tn), F32)],
        compiler_params=_cparams(1, 56),
        name="inproj_hyena",
    )(h, h, h, *([w_hy] * HC_GROUPS), *([conv_w] * HC_GROUPS), *([conv_b] * HC_GROUPS))


CV_CT = 256
CV_F1B = 16
CV_NF = FFT_N1 // CV_F1B
CV_SLAB = ROWS_BF16
CV_NSLAB = FFT_N2 // CV_SLAB
CV_HALF = FFT_N1 // 2


def _outer_fwd_slab(src_ref, k_ref, a_ref, j, rows_in):
    ct = a_ref.shape[-1]
    halves = []
    for h in range(2):
        r = src_ref[:, :, 2 * j + h] if src_ref.ndim == 5 else src_ref[:, 2 * j + h]
        r = r.reshape(rows_in, ct).astype(BF16)
        o = jnp.dot(k_ref[...], r, preferred_element_type=F32)
        halves.append(o.reshape(2 * FFT_N1, DFT_R, ct))
    slab = jnp.concatenate(halves, axis=1).astype(BF16)
    a_ref[:, :, pl.ds(pl.multiple_of(j * CV_SLAB, CV_SLAB), CV_SLAB), :] = slab.reshape(
        2, FFT_N1, CV_SLAB, ct)


def _hy_conv_kernel(z_ref, x0_ref, bias_ref, l1_ref, kf_ref, ki_ref, fw_ref, iv_ref, fa_ref, o_ref, a_ref):
    s = pl.program_id(2)
    ct = a_ref.shape[-1]

    @pl.when(s == 0)
    def _():
        def body(j, carry):
            _outer_fwd_slab(z_ref, kf_ref, a_ref, j, 2 * CV_HALF * DFT_R)
            return carry

        lax.fori_loop(0, CV_NSLAB, body, 0, unroll=4)

    @pl.when((s >= 1) & (s <= CV_NF))
    def _():
        f0 = (s - 1) * CV_F1B
        inv_l1 = 1.0 / l1_ref[...]

        def spectra(fl):
            u = jnp.dot(fw_ref[fl], a_ref[:, f0 + fl].reshape(2 * FFT_N2, ct), preferred_element_type=F32)
            k = jnp.dot(fw_ref[fl], fa_ref[:, fl].reshape(2 * FFT_N2, ct), preferred_element_type=F32)
            return u, k * inv_l1

        nxt = spectra(0)
        for fl in range(CV_F1B):
            u, k = nxt
            if fl + 1 < CV_F1B:
                nxt = spectra(fl + 1)
            ure, uim = u[:FFT_N2], u[FFT_N2:]
            kre, kim = k[:FFT_N2], k[FFT_N2:]
            p = jnp.concatenate([ure * kre - uim * kim, ure * kim + uim * kre], axis=0).astype(BF16)
            y = jnp.dot(iv_ref[fl], p, preferred_element_type=F32)
            a_ref[:, f0 + fl] = y.reshape(2, FFT_N2, ct).astype(BF16)

    @pl.when(s == CV_NF + 1)
    def _():
        bias = bias_ref[...]

        def body(j, carry):
            slab = a_ref[:, :, pl.ds(pl.multiple_of(j * CV_SLAB, CV_SLAB), CV_SLAB), :].astype(F32)
            x0 = x0_ref[:, :, j].astype(F32)
            halves = []
            for h in range(2):
                r = slab[:, :, h * DFT_R:(h + 1) * DFT_R, :].reshape(2 * FFT_N1 * DFT_R, ct).astype(BF16)
                y = jnp.dot(ki_ref[...], r, preferred_element_type=F32).reshape(2, CV_HALF, DFT_R, ct)
                z = z_ref[:, :, 2 * j + h]
                halves.append((y + bias * z) * x0[:, :, h * DFT_R:(h + 1) * DFT_R, :])
            o_ref[:, :, j] = jnp.concatenate(halves, axis=2).astype(o_ref.dtype)
            return carry

        lax.fori_loop(0, CV_NSLAB, body, 0, unroll=4)


def _hy_conv(z, x0c, bias, filt_outer, l1, tabs):
    b, L, c = z.shape
    assert b % 2 == 0 and L * 2 == FFT_N
    ct = CV_CT
    z5 = z.reshape(b, CV_HALF, FFT_N2 // DFT_R, DFT_R, c)
    x5 = x0c.reshape(b, CV_HALF, CV_NSLAB, CV_SLAB, c)
    fidx = lambda s: jnp.clip(s - 1, 0, CV_NF - 1)
    once = pl.Buffered(1)
    tab_spec = pl.BlockSpec((CV_F1B, 2 * FFT_N2, 2 * FFT_N2), lambda p, cb, s: (fidx(s), 0, 0))
    io16 = pl.BlockSpec((2, CV_HALF, CV_NSLAB, CV_SLAB, ct), lambda p, cb, s: (p, 0, 0, 0, cb))
    out = pl.pallas_call(
        _hy_conv_kernel,
        grid=(b // 2, c // ct, CV_NF + 2),
        in_specs=[
            pl.BlockSpec((2, CV_HALF, FFT_N2 // DFT_R, DFT_R, ct), lambda p, cb, s: (p, 0, 0, 0, cb)),
            io16,
            pl.BlockSpec((1, ct), lambda p, cb, s: (0, cb)),
            pl.BlockSpec((1, ct), lambda p, cb, s: (0, cb)),
            pl.BlockSpec(tabs["k_fwd"].shape, lambda p, cb, s: (0, 0), pipeline_mode=once),
            pl.BlockSpec(tabs["k_inv"].shape, lambda p, cb, s: (0, 0), pipeline_mode=once),
            tab_spec, tab_spec,
            pl.BlockSpec((2, CV_F1B, FFT_N2, ct), lambda p, cb, s: (0, fidx(s), 0, cb)),
        ],
        out_specs=io16,
        out_shape=jax.ShapeDtypeStruct(x5.shape, BF16),
        scratch_shapes=[pltpu.VMEM((2, FFT_N1, FFT_N2, ct), BF16)],
        compiler_params=_cparams(3, 60),
        name="hyena_conv",
    )(z5, x5, bias, l1, tabs["k_fwd"], tabs["k_inv"], tabs["fwd2"], tabs["inv2"], filt_outer)
    return out.reshape(b, L, c)


def _filt_outer_kernel(t_ref, kf_ref, o_ref):
    def body(j, carry):
        _outer_fwd_slab(t_ref, kf_ref, o_ref, j, FFT_N1 * DFT_R)
        return carry

    lax.fori_loop(0, CV_NSLAB, body, 0, unroll=4)


def _filt_outer(taps, tabs):
    n, c = taps.shape
    ct = CV_CT
    t4 = taps.reshape(FFT_N1, FFT_N2 // DFT_R, DFT_R, c)
    return pl.pallas_call(
        _filt_outer_kernel,
        grid=(c // ct,),
        in_specs=[
            pl.BlockSpec((FFT_N1, FFT_N2 // DFT_R, DFT_R, ct), lambda cb: (0, 0, 0, cb)),
            pl.BlockSpec(tabs["k_flt"].shape, lambda cb: (0, 0), pipeline_mode=pl.Buffered(1)),
        ],
        out_specs=pl.BlockSpec((2, FFT_N1, FFT_N2, ct), lambda cb: (0, 0, 0, cb)),
        out_shape=jax.ShapeDtypeStruct((2, FFT_N1, FFT_N2, c), BF16),
        compiler_params=_cparams(1, 48),
        name="filter_outer",
    )(t4, tabs["k_flt"])


FLT_TR = 512
FLT_HALF = FLT_TR // 2


def _dot3(a, w):
    a_hi = a.astype(BF16)
    a_lo = (a - a_hi.astype(F32)).astype(BF16)
    w_hi = w.astype(BF16)
    w_lo = (w - w_hi.astype(F32)).astype(BF16)
    lhs = jnp.concatenate([a_hi, a_hi, a_lo], axis=1)
    rhs = jnp.concatenate([w_hi, w_lo, w_hi], axis=0)
    return jnp.dot(lhs, rhs, preferred_element_type=F32)


def _filter_step(r, emb_ref, w1_ref, b1_ref, w2_ref, b2_ref, w3_ref, b3_ref, w4t_ref, w4b_ref, fr_ref, dl_ref,
                 k_ref, l1_ref):
    st = {}

    def hidden(w_ref, b_ref, src):
        def stage():
            st["h"] = jnp.sin(fr_ref[...] * (_dot3(st[src] if src == "h" else emb_ref[...], w_ref[...])
                                             + b_ref[...]))
        return stage

    def taps_half(part, w_ref):
        def stage():
            tcol = part * FILTER_HIDDEN
            taps = jnp.dot(st["h"].astype(BF16), w_ref[...].astype(BF16),
                           preferred_element_type=F32)
            decay = jnp.exp(-emb_ref[:, tcol:tcol + 1] * dl_ref[...])
            rows = r * FLT_TR + part * FLT_HALF + lax.broadcasted_iota(jnp.int32, taps.shape, 0)
            taps = jnp.where(rows == SEQ, 0.0, taps * decay)
            k_ref[part * FLT_HALF:(part + 1) * FLT_HALF, :] = taps
            total = jnp.sum(jnp.abs(taps), axis=0, keepdims=True)
            if part == 0:
                st["l1"] = jnp.where(r == 0, 0.0, l1_ref[...]) + total
            else:
                l1_ref[...] = st["l1"] + total
        return stage

    return [hidden(w1_ref, b1_ref, "emb"), hidden(w2_ref, b2_ref, "h"), hidden(w3_ref, b3_ref, "h"),
            taps_half(0, w4t_ref), taps_half(1, w4b_ref)]


def _filter_side_call(emb2, w1, b1, w2, b2, w3, b3, w4, freq, deltas, steps):
    n = 2 * emb2.shape[0]
    assert n // FLT_TR == steps[0] * steps[1]
    fh = FILTER_HIDDEN
    eye2 = jnp.eye(2, dtype=F32)
    w1p = jnp.concatenate([w1, jnp.zeros((fh - w1.shape[0], fh), F32)], axis=0)
    wd = [jnp.kron(eye2, w) for w in (w1p, w2, w3)]
    bd = [jnp.tile(b, (1, 2)) for b in (b1, b2, b3, freq)]
    zeros = jnp.zeros_like(w4)
    w4t = jnp.concatenate([w4, zeros], axis=0)
    w4b = jnp.concatenate([zeros, w4], axis=0)
    step = lambda j, i: j * steps[1] + i
    const = lambda j, i: (0, 0)
    small = lambda a: (a, a.shape, const)
    fwd_tiles = SEQ // FLT_TR
    w4_half = lambda a: (a, (2 * fh, HYENA_W), lambda j, i: (0, step(j, i) // fwd_tiles))
    inputs = [(emb2, (FLT_HALF, 2 * fh), lambda j, i: (step(j, i), 0)),
              small(wd[0]), small(bd[0]), small(wd[1]), small(bd[1]), small(wd[2]), small(bd[2]),
              w4_half(w4t), w4_half(w4b), small(bd[3]), small(deltas)]
    outputs = [(jax.ShapeDtypeStruct((n, HYENA_W), F32), (FLT_TR, HYENA_W), lambda j, i: (step(j, i), 0)),
               (jax.ShapeDtypeStruct((1, HYENA_W), F32), (1, HYENA_W), const)]
    return dict(fn=_filter_step, inputs=inputs, outputs=outputs)


AT_TQ = 512
AT_RQ = 256
AT_TK = 512


def _attn_kernel(*refs, n_side):
    q_ref, k_ref, v_ref = refs[:3]
    o_ref = refs[3 + n_side]
    s_refs = refs[4 + 2 * n_side:]
    for side_ref, side_out_ref in zip(refs[3:3 + n_side], refs[4 + n_side:4 + 2 * n_side]):
        side_out_ref[...] = side_ref[...].astype(BF16)
    nblk = k_ref.shape[0] // AT_TK
    units = [(pl.ds(r * AT_RQ, AT_RQ), slice(g * HEAD_DIM, (g + 1) * HEAD_DIM))
             for r in range(q_ref.shape[0] // AT_RQ) for g in range(Q_PER_KV)]
    lane_blocks = lambda a: [a[:, i:i + LANES] for i in range(0, a.shape[1], LANES)]

    def scores(u, j, m_run):
        rows, lanes = units[u]
        ks = pl.ds(j * AT_TK, AT_TK)
        s = lax.dot_general(q_ref[rows, lanes], k_ref[ks, :], (((1,), (1,)), ((), ())),
                            preferred_element_type=F32)
        s_refs[u % 2][:, ks] = s
        blk = functools.reduce(jnp.maximum, lane_blocks(s))
        return blk if m_run is None else jnp.maximum(m_run, blk)

    def weighted(u, j, m, l_run, acc):
        ks = pl.ds(j * AT_TK, AT_TK)
        p = jnp.exp2((s_refs[u % 2][:, ks] - m).astype(BF16))
        l_blk = functools.reduce(jnp.add, lane_blocks(p)).astype(F32)
        o = jnp.dot(p, v_ref[ks, :], preferred_element_type=F32)
        return (l_blk if l_run is None else l_run + l_blk), (o if acc is None else acc + o)

    m_run = None
    for j in range(nblk):
        m_run = scores(0, j, m_run)
    for u, (rows, lanes) in enumerate(units):
        m = jnp.max(m_run, axis=-1, keepdims=True)
        m_run, l_run, acc = None, None, None
        for j in range(nblk):
            if u + 1 < len(units):
                m_run = scores(u + 1, j, m_run)
            l_run, acc = weighted(u, j, m, l_run, acc)
        l = jnp.sum(l_run, axis=-1, keepdims=True)
        o_ref[rows, lanes] = (acc / l).astype(o_ref.dtype)


def _attention(qkv3, side_casts):
    b, s, _ = qkv3.shape
    gw = Q_PER_KV * HEAD_DIM
    k0 = N_Q_HEADS
    v0 = N_Q_HEADS + N_KV_HEADS
    nq = s // AT_TQ
    steps = b * N_KV_HEADS * nq
    step = lambda bi, kv, qi: ((bi * N_KV_HEADS + kv) * nq + qi, 0)
    sides = [a if isinstance(a, tuple) else (a, a.shape[1]) for a in side_casts]
    side_specs = []
    for arr, ncols in sides:
        slab = arr.shape[0] // steps
        assert slab * steps == arr.shape[0] and slab % ROWS_BF16 == 0 and ncols % LANES == 0
        side_specs.append(pl.BlockSpec((slab, ncols), step))
    return pl.pallas_call(
        functools.partial(_attn_kernel, n_side=len(sides)),
        grid=(b, N_KV_HEADS, nq),
        in_specs=[pl.BlockSpec((None, AT_TQ, gw), lambda bi, kv, qi: (bi, qi, kv)),
                  pl.BlockSpec((None, s, HEAD_DIM), lambda bi, kv, qi: (bi, 0, k0 + kv)),
                  pl.BlockSpec((None, s, HEAD_DIM), lambda bi, kv, qi: (bi, 0, v0 + kv))] + side_specs,
        out_specs=[pl.BlockSpec((None, AT_TQ, gw), lambda bi, kv, qi: (bi, qi, kv))] + side_specs,
        out_shape=[jax.ShapeDtypeStruct((b, s, ATTN_W), BF16)]
        + [jax.ShapeDtypeStruct((arr.shape[0], ncols), BF16) for arr, ncols in sides],
        scratch_shapes=[pltpu.VMEM((AT_RQ, s), F32), pltpu.VMEM((AT_RQ, s), F32)],
        compiler_params=_cparams(3, 56),
        name="attention",
    )(qkv3, qkv3, qkv3, *(arr for arr, _ in sides))


def _ep_residual(accs, extras, rows):
    return extras[0][rows, :] + accs[0]


def _ep_swiglu(accs, extras, rows):
    return jax.nn.silu(accs[0]) * accs[1]


def _merge_out_kernel(yh_ref, ya_ref, gt_ref, x_ref, g_ref, wh_ref, wa_ref, wo_ref, x1_ref, h2_ref):
    d = x_ref.shape[1]
    for c in range(x_ref.shape[0] // MM_RC):
        rows = pl.ds(c * MM_RC, MM_RC)
        ph = jnp.dot(yh_ref[rows, :], wh_ref[...], preferred_element_type=F32)
        pa = jnp.dot(ya_ref[rows, :], wa_ref[...], preferred_element_type=F32)
        merged = gt_ref[rows, :d].astype(F32) * ph + gt_ref[rows, d:].astype(F32) * pa
        x1 = x_ref[rows, :] + jnp.dot(merged.astype(BF16), wo_ref[...], preferred_element_type=F32)
        ms = jnp.mean(x1 * x1, axis=-1, keepdims=True)
        x1_ref[rows, :] = x1
        h2_ref[rows, :] = (x1 * lax.rsqrt(ms + EPS) * g_ref[...]).astype(h2_ref.dtype)


def _merge_out(yh, ya, gates, x, norm_g, w_h, w_a, w_o, tm=512):
    m, d = x.shape
    once = pl.Buffered(1)
    rows = lambda a: pl.BlockSpec((tm, a.shape[1]), lambda i: (i, 0))
    whole = lambda a: pl.BlockSpec(a.shape, lambda i: (0, 0), pipeline_mode=once)
    return pl.pallas_call(
        _merge_out_kernel,
        grid=(m // tm,),
        in_specs=[rows(yh), rows(ya), rows(gates), rows(x), pl.BlockSpec((1, d), lambda i: (0, 0)),
                  whole(w_h), whole(w_a), whole(w_o)],
        out_specs=[pl.BlockSpec((tm, d), lambda i: (i, 0))] * 2,
        out_shape=[jax.ShapeDtypeStruct((m, d), F32), jax.ShapeDtypeStruct((m, d), BF16)],
        compiler_params=_cparams(1, 56),
        name="merge_out",
    )(yh, ya, gates, x, norm_g, w_h, w_a, w_o)


def _ffn_down(a, x, w_bf16, tm=512, tn=1024):
    return _wres_matmul("ffn_down", [a], [(w_bf16, 0)], [(x, (tm, tn), lambda j, i: (i, j))], _ep_residual,
                        pairs=[0], n_out=w_bf16.shape[1], out_dtype=F32, tm=tm, tn=tn, vmem_mib=56)


def _swiglu(h, w_g_bf16, w_u_bf16, w_down, tm=4096):
    return _wres_matmul("swiglu", [h], [(w_g_bf16, 0), (w_u_bf16, 0)], [], _ep_swiglu,
                        pairs=[0, 0], n_out=w_g_bf16.shape[1], out_dtype=BF16, tm=tm, vmem_mib=60,
                        side_cast=w_down)


def _layer(x, mix_norm_g, w_in, b_gate, hy_conv_w, hy_conv_b,
           flt_w1, flt_b1, flt_w2, flt_b2, flt_w3, flt_b3, flt_w4, flt_freq, hy_bias,
           q_norm_g, k_norm_g, w_br_hyena, w_br_attn, w_out,
           ffn_norm_g, w_ffn_gate, w_ffn_up, w_ffn_down):
    b, s, d = x.shape
    m = b * s
    row = lambda a: a.reshape(1, -1)
    tabs = _dft_tables()
    cos, sin = (jnp.asarray(t) for t in _rope_tables_np())
    xm = x.reshape(m, d)

    emb2, deltas = (jnp.asarray(t) for t in _filter_tables_np())
    filter_side = functools.partial(_filter_side_call, emb2, flt_w1, row(flt_b1), flt_w2, row(flt_b2),
                                    flt_w3, row(flt_b3), flt_w4, row(flt_freq), deltas)
    qkv, h = _inproj_qkv(xm, row(mix_norm_g), w_in, row(q_norm_g), row(k_norm_g), cos, sin)

    y_a, w_hy_bf16, w_bh_bf16, w_ba_bf16, w_out_bf16, w_g_bf16, w_u_bf16 = _attention(
        qkv.reshape(b, s, ATTN_W + 2 * KV_W),
        [(w_in, COL_Q), w_br_hyena, w_br_attn, w_out, w_ffn_gate, w_ffn_up])
    y_a = y_a.reshape(m, ATTN_W)

    gates, taps, l1 = _inproj_gate(h, w_in, row(b_gate), filter_side)
    filt_outer = _filt_outer(taps, tabs)

    x0c, z = _inproj_conv(h, w_hy_bf16, hy_conv_w, row(hy_conv_b))
    y_h = _hy_conv(z.reshape(b, s, HYENA_W), x0c.reshape(b, s, HYENA_W), row(hy_bias), filt_outer, l1,
                   tabs).reshape(m, HYENA_W)

    x1, h2 = _merge_out(y_h, y_a, gates, xm, row(ffn_norm_g), w_bh_bf16, w_ba_bf16, w_out_bf16)
    act, w_down_bf16 = _swiglu(h2, w_g_bf16, w_u_bf16, w_ffn_down)
    out = _ffn_down(act, x1, w_down_bf16)
    return out.reshape(b, s, d)


def kernel(x, mix_norm_g, w_in, b_gate, hy_conv_w, hy_conv_b, flt_w1, flt_b1, flt_w2, flt_b2, flt_w3, flt_b3, flt_w4, flt_freq, hy_bias, q_norm_g, k_norm_g, w_br_hyena, w_br_attn, w_out, ffn_norm_g, w_ffn_gate, w_ffn_up, w_ffn_down):
    params = (mix_norm_g, w_in, b_gate, hy_conv_w, hy_conv_b, flt_w1, flt_b1, flt_w2, flt_b2,
              flt_w3, flt_b3, flt_w4, flt_freq, hy_bias, q_norm_g, k_norm_g, w_br_hyena, w_br_attn,
              w_out, ffn_norm_g, w_ffn_gate, w_ffn_up, w_ffn_down)
    for l in range(mix_norm_g.shape[0]):
        x = _layer(x, *(p[l] for p in params))
    return x
```

```python
import functools
import math

import numpy as np
import jax
import jax.numpy as jnp
from jax import lax
from jax.experimental import pallas as pl
from jax.experimental.pallas import tpu as pltpu

F32 = jnp.float32
BF16 = jnp.bfloat16

D_MODEL = 2048
SEQ = 4096
GRID_W = 64
HEAD_DIM = 128
N_Q_HEADS = 8
N_KV_HEADS = 2
Q_PER_KV = N_Q_HEADS // N_KV_HEADS
ATTN_W = N_Q_HEADS * HEAD_DIM
KV_W = N_KV_HEADS * HEAD_DIM
ROPE_THETA = 10000.0
HYENA_W = D_MODEL - ATTN_W
SHORT_TAPS = 3
FILTER_EMB = 33
FILTER_HIDDEN = 64
DECAY_TARGET = 1e-2
FAST_DECAY_PCT = 0.3
SLOW_DECAY_PCT = 1.5
EPS = 1e-6

COL_Q = 3 * HYENA_W
COL_G = COL_Q + ATTN_W + 2 * KV_W

LANES = 128
SUBLANES_F32 = 8
ROWS_BF16 = 16

FFT_N = 2 * SEQ
FFT_N1 = 64
FFT_N2 = 128

MIB = 1024 * 1024


def _cparams(n_axes, vmem_mib):
    return pltpu.CompilerParams(
        dimension_semantics=("arbitrary",) * n_axes,
        vmem_limit_bytes=vmem_mib * MIB,
    )


DFT_R = SUBLANES_F32


@functools.lru_cache(maxsize=None)
def _dft_tables_np():
    n, n1, n2, r = FFT_N, FFT_N1, FFT_N2, DFT_R
    eye = np.eye(r)
    f1 = np.arange(n1)
    ang1 = 2.0 * np.pi * ((f1[:, None] * f1[None, :]) % n1) / n1
    c1, s1 = np.cos(ang1), np.sin(ang1)
    h = n1 // 2
    m_fwd = np.block([[c1[:, :h], s1[:, :h]], [-s1[:, :h], c1[:, :h]]])
    m_flt = np.concatenate([c1, -s1], axis=0)
    ct, st = c1[:h, :], s1[:h, :]
    m_inv = np.block([[ct, -st], [st, ct]]) / n
    s2 = np.arange(n2)
    f = f1[:, None, None] + n1 * s2[None, :, None]
    th = 2.0 * np.pi * ((f * s2[None, None, :]) % n) / n
    c, s = np.cos(th), np.sin(th)
    fwd2 = np.concatenate(
        [np.concatenate([c, s], axis=2), np.concatenate([-s, c], axis=2)], axis=1)
    c_t, s_t = np.transpose(c, (0, 2, 1)), np.transpose(s, (0, 2, 1))
    inv2 = np.concatenate(
        [np.concatenate([c_t, -s_t], axis=2), np.concatenate([s_t, c_t], axis=2)], axis=1)
    f32 = lambda a: np.ascontiguousarray(a, dtype=np.float32)
    return dict(k_fwd=f32(np.kron(m_fwd, eye)), k_flt=f32(np.kron(m_flt, eye)),
                k_inv=f32(np.kron(m_inv, eye)), fwd2=f32(fwd2), inv2=f32(inv2))


def _dft_tables():
    return {k: jnp.asarray(v).astype(BF16) for k, v in _dft_tables_np().items()}


@functools.lru_cache(maxsize=None)
def _rope_tables_np():
    half = HEAD_DIM // 2
    inv = ROPE_THETA ** (-np.arange(0, half, 2, dtype=np.float64) / half)
    pos = np.arange(SEQ)
    ang_r = (pos // GRID_W)[:, None] * inv[None, :]
    ang_c = (pos % GRID_W)[:, None] * inv[None, :]
    cos = np.concatenate([np.cos(ang_r)] * 2 + [np.cos(ang_c)] * 2, axis=-1)
    sin = np.concatenate([-np.sin(ang_r), np.sin(ang_r), -np.sin(ang_c), np.sin(ang_c)], axis=-1)
    return cos.astype(np.float32), sin.astype(np.float32)


@functools.lru_cache(maxsize=None)
def _filter_tables_np():
    L = SEQ
    bands = (FILTER_EMB - 1) // 2
    pos = np.concatenate([np.arange(L, dtype=np.float64), L - np.arange(L, dtype=np.float64)])
    t = pos / max(L - 1, 1)
    fb = np.linspace(1e-4, bands - 1, bands)
    ang = (2.0 * math.pi * pos / L)[:, None] * fb[None, :]
    emb = np.concatenate([t[:, None], np.cos(ang), -np.sin(ang),
                          np.zeros((2 * L, FILTER_HIDDEN - FILTER_EMB))], axis=-1)
    max_decay = math.log(DECAY_TARGET) / FAST_DECAY_PCT
    min_decay = math.log(DECAY_TARGET) / SLOW_DECAY_PCT
    deltas = np.abs(np.linspace(min_decay, max_decay, HYENA_W))
    tile, hid = FLT_TR, FILTER_HIDDEN
    emb2 = emb.reshape(2 * L // tile, 2, tile // 2, hid).transpose(0, 2, 1, 3).reshape(L, 2 * hid)
    return np.ascontiguousarray(emb2, dtype=np.float32), deltas[None, :].astype(np.float32)


MM_TN = 512
MM_RC = 256


def _wres_kernel(*refs, pairs, n_act, n_extra, n_out, n_scratch, rc, epilogue, norm_first, side_cast,
                 side_call):
    n_w = len(pairs)
    acts = refs[:n_act]
    ws = refs[n_act:n_act + n_w]
    extras = refs[n_act + n_w:n_act + n_w + n_extra]
    o_refs = refs[n_act + n_w + n_extra:n_act + n_w + n_extra + n_out]
    wbs = refs[n_act + n_w + n_extra + n_out:]
    side_stages = []
    if side_call:
        fn, n_in, n_res = side_call
        step = pl.program_id(0) * pl.num_programs(1) + pl.program_id(1)
        side_stages = fn(step, *extras[-n_in:], *o_refs[-n_res:])
        extras, o_refs = extras[:-n_in], o_refs[:-n_res]
    for _ in range(side_cast):
        o_refs[-1][...] = extras[-1][...].astype(BF16)
        extras, o_refs = extras[:-1], o_refs[:-1]
    if norm_first:
        g_ref, extras = extras[-1], extras[:-1]
        h_ref, o_refs = o_refs[-1], o_refs[:-1]

    if n_scratch:
        @pl.when(pl.program_id(1) == 0)
        def _():
            for w_ref, wb_ref in zip(ws, wbs):
                wb_ref[...] = w_ref[...].astype(BF16)
    else:
        wbs = ws

    for c in range(o_refs[0].shape[0] // rc):
        rows = pl.ds(c * rc, rc)
        lhs = [a[rows, :] for a in acts]
        if norm_first:
            x = lhs[0]
            ms = jnp.mean(x * x, axis=-1, keepdims=True)
            lhs[0] = (x * lax.rsqrt(ms + EPS) * g_ref[...]).astype(BF16)
            h_ref[rows, :] = lhs[0]
        accs = [jnp.dot(lhs[a], wb_ref[...], preferred_element_type=F32)
                for a, wb_ref in zip(pairs, wbs)]
        if c < len(side_stages):
            side_stages[c]()
        outs = epilogue(accs, extras, rows)
        for o_ref, out in zip(o_refs, outs if isinstance(outs, tuple) else (outs,)):
            o_ref[rows, :] = out.astype(o_ref.dtype)
    assert len(side_stages) <= o_refs[0].shape[0] // rc


def _wres_matmul(name, acts, weights, extras, epilogue, *, pairs, n_out, out_dtype, tm,
                 tn=MM_TN, rc=MM_RC, vmem_mib=48, norm_gain=None, side_cast=None, side_call=None):
    m = acts[0].shape[0]
    grid = (n_out // tn, m // tm)
    out_dtypes = out_dtype if isinstance(out_dtype, tuple) else (out_dtype,)
    out_specs = [pl.BlockSpec((tm, tn), lambda j, i: (i, j)) for _ in out_dtypes]
    out_shapes = [jax.ShapeDtypeStruct((m, n_out), dt) for dt in out_dtypes]
    extras = list(extras)
    if norm_gain is not None:
        assert n_out == tn
        k0 = acts[0].shape[1]
        extras.append((norm_gain, (1, k0), lambda j, i: (0, 0)))
        out_specs.append(pl.BlockSpec((tm, k0), lambda j, i: (i, 0)))
        out_shapes.append(jax.ShapeDtypeStruct((m, k0), BF16))
    side_casts = side_cast if isinstance(side_cast, list) else [side_cast] * (side_cast is not None)
    for arr, ncols in [(a, a.shape[1]) if not isinstance(a, tuple) else a for a in side_casts]:
        slab = arr.shape[0] // (grid[0] * grid[1])
        assert slab * grid[0] * grid[1] == arr.shape[0] and slab % ROWS_BF16 == 0 and ncols % LANES == 0
        step = lambda j, i: (j * grid[1] + i, 0)
        extras.append((arr, (slab, ncols), step))
        out_specs.append(pl.BlockSpec((slab, ncols), step))
        out_shapes.append(jax.ShapeDtypeStruct((arr.shape[0], ncols), BF16))
    side = side_call(grid) if side_call is not None else None
    if side is not None:
        extras += side["inputs"]
        out_specs += [pl.BlockSpec(blk, imap) for _, blk, imap in side["outputs"]]
        out_shapes += [shape for shape, _, _ in side["outputs"]]
    in_specs = [pl.BlockSpec((tm, a.shape[1]), lambda j, i: (i, 0)) for a in acts]
    assert all(off % LANES == 0 for _, off in weights) and tn % LANES == 0
    w_mode = dict(pipeline_mode=pl.Buffered(1)) if n_out == tn else {}
    in_specs += [pl.BlockSpec((pl.Element(w.shape[0]), pl.Element(tn)),
                              lambda j, i, off=off: (0, pl.multiple_of(off + j * tn, LANES)), **w_mode)
                 for w, off in weights]
    in_specs += [pl.BlockSpec(blk, imap) for _, blk, imap in extras]
    scratch = [pltpu.VMEM((w.shape[0], tn), BF16) for w, _ in weights if w.dtype != BF16]
    assert len(scratch) in (0, len(weights))
    kern = functools.partial(_wres_kernel, pairs=tuple(pairs), n_act=len(acts), n_extra=len(extras),
                             n_out=len(out_specs), n_scratch=len(scratch), rc=rc, epilogue=epilogue,
                             norm_first=norm_gain is not None, side_cast=len(side_casts),
                             side_call=(side["fn"], len(side["inputs"]), len(side["outputs"])) if side else None)
    outs = pl.pallas_call(
        kern,
        grid=grid,
        in_specs=in_specs,
        out_specs=out_specs,
        out_shape=out_shapes,
        scratch_shapes=scratch,
        compiler_params=_cparams(2, vmem_mib),
        name=name,
    )(*acts, *(w for w, _ in weights), *(e for e, _, _ in extras))
    return outs if len(outs) > 1 else outs[0]


def _head_norm_rope(x, g, cos, sin, scale):
    ms = jnp.mean(x * x, axis=-1, keepdims=True)
    y = x * lax.rsqrt(ms + EPS) * g
    lane = lax.broadcasted_iota(jnp.int32, y.shape, 1)
    swapped = jnp.where((lane % 64) < 32, pltpu.roll(y, 96, 1), pltpu.roll(y, 32, 1))
    out = y * cos + swapped * sin
    return out * scale if scale != 1.0 else out


def _ep_qkv(accs, extras, rows):
    qg_ref, kg_ref, cos_ref, sin_ref = extras
    acc = accs[0]
    cos, sin = cos_ref[rows, :], sin_ref[rows, :]
    head = lambda hh: acc[:, hh * HEAD_DIM:(hh + 1) * HEAD_DIM]
    parts = [_head_norm_rope(head(hh), qg_ref[...], cos, sin, HEAD_DIM ** -0.5 * LOG2E)
             for hh in range(N_Q_HEADS)]
    parts += [_head_norm_rope(head(N_Q_HEADS + hh), kg_ref[...], cos, sin, 1.0) for hh in range(N_KV_HEADS)]
    parts.append(acc[:, (N_Q_HEADS + N_KV_HEADS) * HEAD_DIM:])
    return jnp.concatenate(parts, axis=1)


def _ep_gate(accs, extras, rows):
    return jax.nn.sigmoid(accs[0] + extras[0][...])


IN_TM = 2048
IN_TN = 1024
LOG2E = math.log2(math.e)


def _inproj_qkv(x, norm_g, w_in, q_g, k_g, cos, sin):
    qkv_w = ATTN_W + 2 * KV_W
    tm = IN_TM // 2
    head = lambda g: (g, (1, HEAD_DIM), lambda j, i: (0, 0))
    pos = lambda t: (t, (tm, HEAD_DIM), lambda j, i: (i % (SEQ // tm), 0))
    return _wres_matmul("inproj_qkv", [x], [(w_in, COL_Q)], [head(q_g), head(k_g), pos(cos), pos(sin)],
                        _ep_qkv, pairs=[0], n_out=qkv_w, out_dtype=BF16, tn=qkv_w, tm=tm, norm_gain=norm_g,
                        vmem_mib=56)


def _inproj_gate(h, w_in, b_gate, filter_side):
    tn = IN_TN
    return _wres_matmul("inproj_gate", [h], [(w_in, COL_G)], [(b_gate, (1, tn), lambda j, i: (0, j))],
                        _ep_gate, pairs=[0], n_out=2 * D_MODEL, out_dtype=BF16, tn=tn, tm=IN_TM,
                        side_call=filter_side, vmem_mib=56)


HC_TM = 1024
HC_RC = 1024
HC_HALO = ROWS_BF16
HC_GROUPS = 3


def _inproj_conv_kernel(*refs):
    n_w = HC_GROUPS
    h_ref, top_ref, bot_ref = refs[:3]
    wb_refs = refs[3:3 + n_w]
    cw_refs = refs[3 + n_w:3 + 2 * n_w]
    cb_refs = refs[3 + 2 * n_w:3 + 3 * n_w]
    x0_ref, z_ref = refs[3 + 3 * n_w:]
    i = pl.program_id(0)
    tm, rc, hl = x0_ref.shape[0], HC_RC, HC_HALO
    seq_blocks = SEQ // tm
    top = jnp.where(i % seq_blocks != 0, top_ref[...], jnp.zeros_like(top_ref))
    bot = jnp.where(i % seq_blocks != seq_blocks - 1, bot_ref[...], jnp.zeros_like(bot_ref))
    nchunk = tm // rc
    for c in range(nchunk):
        lo, hi = c * rc - hl, (c + 1) * rc + hl
        parts = ([top] if c == 0 else []) + [h_ref[max(lo, 0):min(hi, tm), :]] + ([bot] if c == nchunk - 1 else [])
        lhs = jnp.concatenate(parts, axis=0) if len(parts) > 1 else parts[0]
        outs = []
        for wb_ref, cw_ref, cb_ref in zip(wb_refs, cw_refs, cb_refs):
            e = jnp.dot(lhs, wb_ref[...], preferred_element_type=F32)
            up = pltpu.roll(e, 1, 0)[hl:hl + rc]
            dn = pltpu.roll(e, rc + 2 * hl - 1, 0)[hl:hl + rc]
            outs.append(cb_ref[...] + up * cw_ref[0:1, :] + e[hl:hl + rc] * cw_ref[1:2, :]
                        + dn * cw_ref[2:3, :])
        x0_ref[c * rc:(c + 1) * rc, :] = outs[0].astype(x0_ref.dtype)
        z_ref[c * rc:(c + 1) * rc, :] = outs[2] * outs[1]


def _inproj_conv(h, w_hy, conv_w, conv_b):
    m, d = h.shape
    tm, tn, hl = HC_TM, HYENA_W, HC_HALO
    once = pl.Buffered(1)
    groups = range(HC_GROUPS)
    in_specs = [pl.BlockSpec((tm, d), lambda i: (i, 0)),
                pl.BlockSpec((hl, d), lambda i: (jnp.maximum(i * (tm // hl) - 1, 0), 0)),
                pl.BlockSpec((hl, d), lambda i: (jnp.minimum((i + 1) * (tm // hl), m // hl - 1), 0))]
    in_specs += [pl.BlockSpec((d, tn), lambda i, g=g: (0, g), pipeline_mode=once) for g in groups]
    in_specs += [pl.BlockSpec((SHORT_TAPS, tn), lambda i, g=g: (0, g)) for g in groups]
    in_specs += [pl.BlockSpec((1, tn), lambda i, g=g: (0, g)) for g in groups]
    row_spec = pl.BlockSpec((tm, tn), lambda i: (i, 0))
    return pl.pallas_call(
        _inproj_conv_kernel,
        grid=(m // tm,),
        in_specs=in_specs,
        out_specs=[row_spec, row_spec],
        out_shape=[jax.ShapeDtypeStruct((m, tn), BF16), jax.ShapeDtypeStruct((m, tn), F32)],
        compiler_params=_cparams(1, 56),
        name="inproj_hyena",
    )(h, h, h, *([w_hy] * HC_GROUPS), *([conv_w] * HC_GROUPS), *([conv_b] * HC_GROUPS))


CV_CT = 256
CV_F1B = 16
CV_NF = FFT_N1 // CV_F1B
CV_SLAB = ROWS_BF16
CV_NSLAB = FFT_N2 // CV_SLAB
CV_HALF = FFT_N1 // 2


def _outer_fwd_slab(src_ref, k_ref, a_ref, j, rows_in):
    ct = a_ref.shape[-1]
    halves = []
    for h in range(2):
        r = src_ref[:, :, 2 * j + h] if src_ref.ndim == 5 else src_ref[:, 2 * j + h]
        r = r.reshape(rows_in, ct).astype(BF16)
        o = jnp.dot(k_ref[...], r, preferred_element_type=F32)
        halves.append(o.reshape(2 * FFT_N1, DFT_R, ct))
    slab = jnp.concatenate(halves, axis=1).astype(BF16)
    a_ref[:, :, pl.ds(pl.multiple_of(j * CV_SLAB, CV_SLAB), CV_SLAB), :] = slab.reshape(
        2, FFT_N1, CV_SLAB, ct)


def _hy_conv_kernel(z_ref, x0_ref, bias_ref, l1_ref, kf_ref, ki_ref, fw_ref, iv_ref, fa_ref, o_ref, a_ref):
    s = pl.program_id(2)
    ct = a_ref.shape[-1]

    @pl.when(s == 0)
    def _():
        def body(j, carry):
            _outer_fwd_slab(z_ref, kf_ref, a_ref, j, 2 * CV_HALF * DFT_R)
            return carry

        lax.fori_loop(0, CV_NSLAB, body, 0, unroll=4)

    @pl.when((s >= 1) & (s <= CV_NF))
    def _():
        f0 = (s - 1) * CV_F1B
        inv_l1 = 1.0 / l1_ref[...]

        def spectra(fl):
            u = jnp.dot(fw_ref[fl], a_ref[:, f0 + fl].reshape(2 * FFT_N2, ct), preferred_element_type=F32)
            k = jnp.dot(fw_ref[fl], fa_ref[:, fl].reshape(2 * FFT_N2, ct), preferred_element_type=F32)
            return u, k * inv_l1

        nxt = spectra(0)
        for fl in range(CV_F1B):
            u, k = nxt
            if fl + 1 < CV_F1B:
                nxt = spectra(fl + 1)
            ure, uim = u[:FFT_N2], u[FFT_N2:]
            kre, kim = k[:FFT_N2], k[FFT_N2:]
            p = jnp.concatenate([ure * kre - uim * kim, ure * kim + uim * kre], axis=0).astype(BF16)
            y = jnp.dot(iv_ref[fl], p, preferred_element_type=F32)
            a_ref[:, f0 + fl] = y.reshape(2, FFT_N2, ct).astype(BF16)

    @pl.when(s == CV_NF + 1)
    def _():
        bias = bias_ref[...]

        def body(j, carry):
            slab = a_ref[:, :, pl.ds(pl.multiple_of(j * CV_SLAB, CV_SLAB), CV_SLAB), :].astype(F32)
            x0 = x0_ref[:, :, j].astype(F32)
            halves = []
            for h in range(2):
                r = slab[:, :, h * DFT_R:(h + 1) * DFT_R, :].reshape(2 * FFT_N1 * DFT_R, ct).astype(BF16)
                y = jnp.dot(ki_ref[...], r, preferred_element_type=F32).reshape(2, CV_HALF, DFT_R, ct)
                z = z_ref[:, :, 2 * j + h]
                halves.append((y + bias * z) * x0[:, :, h * DFT_R:(h + 1) * DFT_R, :])
            o_ref[:, :, j] = jnp.concatenate(halves, axis=2).astype(o_ref.dtype)
            return carry

        lax.fori_loop(0, CV_NSLAB, body, 0, unroll=4)


def _hy_conv(z, x0c, bias, filt_outer, l1, tabs):
    b, L, c = z.shape
    assert b % 2 == 0 and L * 2 == FFT_N
    ct = CV_CT
    z5 = z.reshape(b, CV_HALF, FFT_N2 // DFT_R, DFT_R, c)
    x5 = x0c.reshape(b, CV_HALF, CV_NSLAB, CV_SLAB, c)
    fidx = lambda s: jnp.clip(s - 1, 0, CV_NF - 1)
    once = pl.Buffered(1)
    tab_spec = pl.BlockSpec((CV_F1B, 2 * FFT_N2, 2 * FFT_N2), lambda p, cb, s: (fidx(s), 0, 0))
    io16 = pl.BlockSpec((2, CV_HALF, CV_NSLAB, CV_SLAB, ct), lambda p, cb, s: (p, 0, 0, 0, cb))
    out = pl.pallas_call(
        _hy_conv_kernel,
        grid=(b // 2, c // ct, CV_NF + 2),
        in_specs=[
            pl.BlockSpec((2, CV_HALF, FFT_N2 // DFT_R, DFT_R, ct), lambda p, cb, s: (p, 0, 0, 0, cb)),
            io16,
            pl.BlockSpec((1, ct), lambda p, cb, s: (0, cb)),
            pl.BlockSpec((1, ct), lambda p, cb, s: (0, cb)),
            pl.BlockSpec(tabs["k_fwd"].shape, lambda p, cb, s: (0, 0), pipeline_mode=once),
            pl.BlockSpec(tabs["k_inv"].shape, lambda p, cb, s: (0, 0), pipeline_mode=once),
            tab_spec, tab_spec,
            pl.BlockSpec((2, CV_F1B, FFT_N2, ct), lambda p, cb, s: (0, fidx(s), 0, cb)),
        ],
        out_specs=io16,
        out_shape=jax.ShapeDtypeStruct(x5.shape, BF16),
        scratch_shapes=[pltpu.VMEM((2, FFT_N1, FFT_N2, ct), BF16)],
        compiler_params=_cparams(3, 60),
        name="hyena_conv",
    )(z5, x5, bias, l1, tabs["k_fwd"], tabs["k_inv"], tabs["fwd2"], tabs["inv2"], filt_outer)
    return out.reshape(b, L, c)


def _filt_outer_kernel(t_ref, kf_ref, o_ref):
    def body(j, carry):
        _outer_fwd_slab(t_ref, kf_ref, o_ref, j, FFT_N1 * DFT_R)
        return carry

    lax.fori_loop(0, CV_NSLAB, body, 0, unroll=4)


def _filt_outer(taps, tabs):
    n, c = taps.shape
    ct = CV_CT
    t4 = taps.reshape(FFT_N1, FFT_N2 // DFT_R, DFT_R, c)
    return pl.pallas_call(
        _filt_outer_kernel,
        grid=(c // ct,),
        in_specs=[
            pl.BlockSpec((FFT_N1, FFT_N2 // DFT_R, DFT_R, ct), lambda cb: (0, 0, 0, cb)),
            pl.BlockSpec(tabs["k_flt"].shape, lambda cb: (0, 0), pipeline_mode=pl.Buffered(1)),
        ],
        out_specs=pl.BlockSpec((2, FFT_N1, FFT_N2, ct), lambda cb: (0, 0, 0, cb)),
        out_shape=jax.ShapeDtypeStruct((2, FFT_N1, FFT_N2, c), BF16),
        compiler_params=_cparams(1, 48),
        name="filter_outer",
    )(t4, tabs["k_flt"])


FLT_TR = 512
FLT_HALF = FLT_TR // 2


def _dot3(a, w):
    a_hi = a.astype(BF16)
    a_lo = (a - a_hi.astype(F32)).astype(BF16)
    w_hi = w.astype(BF16)
    w_lo = (w - w_hi.astype(F32)).astype(BF16)
    lhs = jnp.concatenate([a_hi, a_hi, a_lo], axis=1)
    rhs = jnp.concatenate([w_hi, w_lo, w_hi], axis=0)
    return jnp.dot(lhs, rhs, preferred_element_type=F32)


def _filter_step(r, emb_ref, w1_ref, b1_ref, w2_ref, b2_ref, w3_ref, b3_ref, w4t_ref, w4b_ref, fr_ref, dl_ref,
                 k_ref, l1_ref):
    st = {}

    def hidden(w_ref, b_ref, src):
        def stage():
            st["h"] = jnp.sin(fr_ref[...] * (_dot3(st[src] if src == "h" else emb_ref[...], w_ref[...])
                                             + b_ref[...]))
        return stage

    def taps_half(part, w_ref):
        def stage():
            tcol = part * FILTER_HIDDEN
            taps = jnp.dot(st["h"].astype(BF16), w_ref[...].astype(BF16),
                           preferred_element_type=F32)
            decay = jnp.exp(-emb_ref[:, tcol:tcol + 1] * dl_ref[...])
            rows = r * FLT_TR + part * FLT_HALF + lax.broadcasted_iota(jnp.int32, taps.shape, 0)
            taps = jnp.where(rows == SEQ, 0.0, taps * decay)
            k_ref[part * FLT_HALF:(part + 1) * FLT_HALF, :] = taps
            total = jnp.sum(jnp.abs(taps), axis=0, keepdims=True)
            if part == 0:
                st["l1"] = jnp.where(r == 0, 0.0, l1_ref[...]) + total
            else:
                l1_ref[...] = st["l1"] + total
        return stage

    return [hidden(w1_ref, b1_ref, "emb"), hidden(w2_ref, b2_ref, "h"), hidden(w3_ref, b3_ref, "h"),
            taps_half(0, w4t_ref), taps_half(1, w4b_ref)]


def _filter_side_call(emb2, w1, b1, w2, b2, w3, b3, w4, freq, deltas, steps):
    n = 2 * emb2.shape[0]
    assert n // FLT_TR == steps[0] * steps[1]
    fh = FILTER_HIDDEN
    eye2 = jnp.eye(2, dtype=F32)
    w1p = jnp.concatenate([w1, jnp.zeros((fh - w1.shape[0], fh), F32)], axis=0)
    wd = [jnp.kron(eye2, w) for w in (w1p, w2, w3)]
    bd = [jnp.tile(b, (1, 2)) for b in (b1, b2, b3, freq)]
    zeros = jnp.zeros_like(w4)
    w4t = jnp.concatenate([w4, zeros], axis=0)
    w4b = jnp.concatenate([zeros, w4], axis=0)
    step = lambda j, i: j * steps[1] + i
    const = lambda j, i: (0, 0)
    small = lambda a: (a, a.shape, const)
    fwd_tiles = SEQ // FLT_TR
    w4_half = lambda a: (a, (2 * fh, HYENA_W), lambda j, i: (0, step(j, i) // fwd_tiles))
    inputs = [(emb2, (FLT_HALF, 2 * fh), lambda j, i: (step(j, i), 0)),
              small(wd[0]), small(bd[0]), small(wd[1]), small(bd[1]), small(wd[2]), small(bd[2]),
              w4_half(w4t), w4_half(w4b), small(bd[3]), small(deltas)]
    outputs = [(jax.ShapeDtypeStruct((n, HYENA_W), F32), (FLT_TR, HYENA_W), lambda j, i: (step(j, i), 0)),
               (jax.ShapeDtypeStruct((1, HYENA_W), F32), (1, HYENA_W), const)]
    return dict(fn=_filter_step, inputs=inputs, outputs=outputs)


AT_TQ = 512
AT_RQ = 256
AT_TK = 512


def _attn_kernel(*refs, n_side):
    q_ref, k_ref, v_ref = refs[:3]
    o_ref = refs[3 + n_side]
    s_refs = refs[4 + 2 * n_side:]
    for side_ref, side_out_ref in zip(refs[3:3 + n_side], refs[4 + n_side:4 + 2 * n_side]):
        side_out_ref[...] = side_ref[...].astype(BF16)
    nblk = k_ref.shape[0] // AT_TK
    units = [(pl.ds(r * AT_RQ, AT_RQ), slice(g * HEAD_DIM, (g + 1) * HEAD_DIM))
             for r in range(q_ref.shape[0] // AT_RQ) for g in range(Q_PER_KV)]
    lane_blocks = lambda a: [a[:, i:i + LANES] for i in range(0, a.shape[1], LANES)]

    def scores(u, j, m_run):
        rows, lanes = units[u]
        ks = pl.ds(j * AT_TK, AT_TK)
        s = lax.dot_general(q_ref[rows, lanes], k_ref[ks, :], (((1,), (1,)), ((), ())),
                            preferred_element_type=F32)
        s_refs[u % 2][:, ks] = s
        blk = functools.reduce(jnp.maximum, lane_blocks(s))
        return blk if m_run is None else jnp.maximum(m_run, blk)

    def weighted(u, j, m, l_run, acc):
        ks = pl.ds(j * AT_TK, AT_TK)
        p = jnp.exp2((s_refs[u % 2][:, ks] - m).astype(BF16))
        l_blk = functools.reduce(jnp.add, lane_blocks(p)).astype(F32)
        o = jnp.dot(p, v_ref[ks, :], preferred_element_type=F32)
        return (l_blk if l_run is None else l_run + l_blk), (o if acc is None else acc + o)

    m_run = None
    for j in range(nblk):
        m_run = scores(0, j, m_run)
    for u, (rows, lanes) in enumerate(units):
        m = jnp.max(m_run, axis=-1, keepdims=True)
        m_run, l_run, acc = None, None, None
        for j in range(nblk):
            if u + 1 < len(units):
                m_run = scores(u + 1, j, m_run)
            l_run, acc = weighted(u, j, m, l_run, acc)
        l = jnp.sum(l_run, axis=-1, keepdims=True)
        o_ref[rows, lanes] = (acc / l).astype(o_ref.dtype)


def _attention(qkv3, side_casts):
    b, s, _ = qkv3.shape
    gw = Q_PER_KV * HEAD_DIM
    k0 = N_Q_HEADS
    v0 = N_Q_HEADS + N_KV_HEADS
    nq = s // AT_TQ
    steps = b * N_KV_HEADS * nq
    step = lambda bi, kv, qi: ((bi * N_KV_HEADS + kv) * nq + qi, 0)
    sides = [a if isinstance(a, tuple) else (a, a.shape[1]) for a in side_casts]
    side_specs = []
    for arr, ncols in sides:
        slab = arr.shape[0] // steps
        assert slab * steps == arr.shape[0] and slab % ROWS_BF16 == 0 and ncols % LANES == 0
        side_specs.append(pl.BlockSpec((slab, ncols), step))
    return pl.pallas_call(
        functools.partial(_attn_kernel, n_side=len(sides)),
        grid=(b, N_KV_HEADS, nq),
        in_specs=[pl.BlockSpec((None, AT_TQ, gw), lambda bi, kv, qi: (bi, qi, kv)),
                  pl.BlockSpec((None, s, HEAD_DIM), lambda bi, kv, qi: (bi, 0, k0 + kv)),
                  pl.BlockSpec((None, s, HEAD_DIM), lambda bi, kv, qi: (bi, 0, v0 + kv))] + side_specs,
        out_specs=[pl.BlockSpec((None, AT_TQ, gw), lambda bi, kv, qi: (bi, qi, kv))] + side_specs,
        out_shape=[jax.ShapeDtypeStruct((b, s, ATTN_W), BF16)]
        + [jax.ShapeDtypeStruct((arr.shape[0], ncols), BF16) for arr, ncols in sides],
        scratch_shapes=[pltpu.VMEM((AT_RQ, s), F32), pltpu.VMEM((AT_RQ, s), F32)],
        compiler_params=_cparams(3, 56),
        name="attention",
    )(qkv3, qkv3, qkv3, *(arr for arr, _ in sides))


def _ep_residual(accs, extras, rows):
    return extras[0][rows, :] + accs[0]


def _ep_swiglu(accs, extras, rows):
    return jax.nn.silu(accs[0]) * accs[1]


def _merge_out_kernel(yh_ref, ya_ref, gt_ref, x_ref, g_ref, wh_ref, wa_ref, wo_ref, x1_ref, h2_ref):
    d = x_ref.shape[1]
    for c in range(x_ref.shape[0] // MM_RC):
        rows = pl.ds(c * MM_RC, MM_RC)
        ph = jnp.dot(yh_ref[rows, :], wh_ref[...], preferred_element_type=F32)
        pa = jnp.dot(ya_ref[rows, :], wa_ref[...], preferred_element_type=F32)
        merged = gt_ref[rows, :d].astype(F32) * ph + gt_ref[rows, d:].astype(F32) * pa
        x1 = x_ref[rows, :] + jnp.dot(merged.astype(BF16), wo_ref[...], preferred_element_type=F32)
        ms = jnp.mean(x1 * x1, axis=-1, keepdims=True)
        x1_ref[rows, :] = x1
        h2_ref[rows, :] = (x1 * lax.rsqrt(ms + EPS) * g_ref[...]).astype(h2_ref.dtype)


def _merge_out(yh, ya, gates, x, norm_g, w_h, w_a, w_o, tm=512):
    m, d = x.shape
    once = pl.Buffered(1)
    rows = lambda a: pl.BlockSpec((tm, a.shape[1]), lambda i: (i, 0))
    whole = lambda a: pl.BlockSpec(a.shape, lambda i: (0, 0), pipeline_mode=once)
    return pl.pallas_call(
        _merge_out_kernel,
        grid=(m // tm,),
        in_specs=[rows(yh), rows(ya), rows(gates), rows(x), pl.BlockSpec((1, d), lambda i: (0, 0)),
                  whole(w_h), whole(w_a), whole(w_o)],
        out_specs=[pl.BlockSpec((tm, d), lambda i: (i, 0))] * 2,
        out_shape=[jax.ShapeDtypeStruct((m, d), F32), jax.ShapeDtypeStruct((m, d), BF16)],
        compiler_params=_cparams(1, 56),
        name="merge_out",
    )(yh, ya, gates, x, norm_g, w_h, w_a, w_o)


def _ffn_down(a, x, w_bf16, tm=512, tn=1024):
    return _wres_matmul("ffn_down", [a], [(w_bf16, 0)], [(x, (tm, tn), lambda j, i: (i, j))], _ep_residual,
                        pairs=[0], n_out=w_bf16.shape[1], out_dtype=F32, tm=tm, tn=tn, vmem_mib=56)


def _swiglu(h, w_g_bf16, w_u_bf16, w_down, tm=4096):
    return _wres_matmul("swiglu", [h], [(w_g_bf16, 0), (w_u_bf16, 0)], [], _ep_swiglu,
                        pairs=[0, 0], n_out=w_g_bf16.shape[1], out_dtype=BF16, tm=tm, vmem_mib=60,
                        side_cast=w_down)


def _layer(x, mix_norm_g, w_in, b_gate, hy_conv_w, hy_conv_b,
           flt_w1, flt_b1, flt_w2, flt_b2, flt_w3, flt_b3, flt_w4, flt_freq, hy_bias,
           q_norm_g, k_norm_g, w_br_hyena, w_br_attn, w_out,
           ffn_norm_g, w_ffn_gate, w_ffn_up, w_ffn_down):
    b, s, d = x.shape
    m = b * s
    row = lambda a: a.reshape(1, -1)
    tabs = _dft_tables()
    cos, sin = (jnp.asarray(t) for t in _rope_tables_np())
    xm = x.reshape(m, d)

    emb2, deltas = (jnp.asarray(t) for t in _filter_tables_np())
    filter_side = functools.partial(_filter_side_call, emb2, flt_w1, row(flt_b1), flt_w2, row(flt_b2),
                                    flt_w3, row(flt_b3), flt_w4, row(flt_freq), deltas)
    qkv, h = _inproj_qkv(xm, row(mix_norm_g), w_in, row(q_norm_g), row(k_norm_g), cos, sin)

    y_a, w_hy_bf16, w_bh_bf16, w_ba_bf16, w_out_bf16, w_g_bf16, w_u_bf16 = _attention(
        qkv.reshape(b, s, ATTN_W + 2 * KV_W),
        [(w_in, COL_Q), w_br_hyena, w_br_attn, w_out, w_ffn_gate, w_ffn_up])
    y_a = y_a.reshape(m, ATTN_W)

    gates, taps, l1 = _inproj_gate(h, w_in, row(b_gate), filter_side)
    filt_outer = _filt_outer(taps, tabs)

    x0c, z = _inproj_conv(h, w_hy_bf16, hy_conv_w, row(hy_conv_b))
    y_h = _hy_conv(z.reshape(b, s, HYENA_W), x0c.reshape(b, s, HYENA_W), row(hy_bias), filt_outer, l1,
                   tabs).reshape(m, HYENA_W)

    x1, h2 = _merge_out(y_h, y_a, gates, xm, row(ffn_norm_g), w_bh_bf16, w_ba_bf16, w_out_bf16)
    act, w_down_bf16 = _swiglu(h2, w_g_bf16, w_u_bf16, w_ffn_down)
    out = _ffn_down(act, x1, w_down_bf16)
    return out.reshape(b, s, d)


def kernel(x, mix_norm_g, w_in, b_gate, hy_conv_w, hy_conv_b, flt_w1, flt_b1, flt_w2, flt_b2, flt_w3, flt_b3, flt_w4, flt_freq, hy_bias, q_norm_g, k_norm_g, w_br_hyena, w_br_attn, w_out, ffn_norm_g, w_ffn_gate, w_ffn_up, w_ffn_down):
    params = (mix_norm_g, w_in, b_gate, hy_conv_w, hy_conv_b, flt_w1, flt_b1, flt_w2, flt_b2,
              flt_w3, flt_b3, flt_w4, flt_freq, hy_bias, q_norm_g, k_norm_g, w_br_hyena, w_br_attn,
              w_out, ffn_norm_g, w_ffn_gate, w_ffn_up, w_ffn_down)
    for l in range(mix_norm_g.shape[0]):
        x = _layer(x, *(p[l] for p in params))
    return x
```

```python
import functools
import math

import numpy as np
import jax
import jax.numpy as jnp
from jax import lax
from jax.experimental import pallas as pl
from jax.experimental.pallas import tpu as pltpu

F32 = jnp.float32
BF16 = jnp.bfloat16

D_MODEL = 2048
SEQ = 4096
GRID_W = 64
HEAD_DIM = 128
N_Q_HEADS = 8
N_KV_HEADS = 2
Q_PER_KV = N_Q_HEADS // N_KV_HEADS
ATTN_W = N_Q_HEADS * HEAD_DIM
KV_W = N_KV_HEADS * HEAD_DIM
ROPE_THETA = 10000.0
HYENA_W = D_MODEL - ATTN_W
SHORT_TAPS = 3
FILTER_EMB = 33
FILTER_HIDDEN = 64
DECAY_TARGET = 1e-2
FAST_DECAY_PCT = 0.3
SLOW_DECAY_PCT = 1.5
EPS = 1e-6

COL_Q = 3 * HYENA_W
COL_G = COL_Q + ATTN_W + 2 * KV_W

LANES = 128
SUBLANES_F32 = 8
ROWS_BF16 = 16

FFT_N = 2 * SEQ
FFT_N1 = 64
FFT_N2 = 128

MIB = 1024 * 1024


def _cparams(n_axes, vmem_mib):
    return pltpu.CompilerParams(
        dimension_semantics=("arbitrary",) * n_axes,
        vmem_limit_bytes=vmem_mib * MIB,
    )


DFT_R = SUBLANES_F32


@functools.lru_cache(maxsize=None)
def _dft_tables_np():
    n, n1, n2, r = FFT_N, FFT_N1, FFT_N2, DFT_R
    eye = np.eye(r)
    f1 = np.arange(n1)
    ang1 = 2.0 * np.pi * ((f1[:, None] * f1[None, :]) % n1) / n1
    c1, s1 = np.cos(ang1), np.sin(ang1)
    h = n1 // 2
    m_fwd = np.block([[c1[:, :h], s1[:, :h]], [-s1[:, :h], c1[:, :h]]])
    m_flt = np.concatenate([c1, -s1], axis=0)
    ct, st = c1[:h, :], s1[:h, :]
    m_inv = np.block([[ct, -st], [st, ct]]) / n
    s2 = np.arange(n2)
    f = f1[:, None, None] + n1 * s2[None, :, None]
    th = 2.0 * np.pi * ((f * s2[None, None, :]) % n) / n
    c, s = np.cos(th), np.sin(th)
    fwd2 = np.concatenate(
        [np.concatenate([c, s], axis=2), np.concatenate([-s, c], axis=2)], axis=1)
    c_t, s_t = np.transpose(c, (0, 2, 1)), np.transpose(s, (0, 2, 1))
    inv2 = np.concatenate(
        [np.concatenate([c_t, -s_t], axis=2), np.concatenate([s_t, c_t], axis=2)], axis=1)
    f32 = lambda a: np.ascontiguousarray(a, dtype=np.float32)
    return dict(k_fwd=f32(np.kron(m_fwd, eye)), k_flt=f32(np.kron(m_flt, eye)),
                k_inv=f32(np.kron(m_inv, eye)), fwd2=f32(fwd2), inv2=f32(inv2))


def _dft_tables():
    return {k: jnp.asarray(v).astype(BF16) for k, v in _dft_tables_np().items()}


@functools.lru_cache(maxsize=None)
def _rope_tables_np():
    half = HEAD_DIM // 2
    inv = ROPE_THETA ** (-np.arange(0, half, 2, dtype=np.float64) / half)
    pos = np.arange(SEQ)
    ang_r = (pos // GRID_W)[:, None] * inv[None, :]
    ang_c = (pos % GRID_W)[:, None] * inv[None, :]
    cos = np.concatenate([np.cos(ang_r)] * 2 + [np.cos(ang_c)] * 2, axis=-1)
    sin = np.concatenate([-np.sin(ang_r), np.sin(ang_r), -np.sin(ang_c), np.sin(ang_c)], axis=-1)
    return cos.astype(np.float32), sin.astype(np.float32)


@functools.lru_cache(maxsize=None)
def _filter_tables_np():
    L = SEQ
    bands = (FILTER_EMB - 1) // 2
    pos = np.concatenate([np.arange(L, dtype=np.float64), L - np.arange(L, dtype=np.float64)])
    t = pos / max(L - 1, 1)
    fb = np.linspace(1e-4, bands - 1, bands)
    ang = (2.0 * math.pi * pos / L)[:, None] * fb[None, :]
    emb = np.concatenate([t[:, None], np.cos(ang), -np.sin(ang),
                          np.zeros((2 * L, FILTER_HIDDEN - FILTER_EMB))], axis=-1)
    max_decay = math.log(DECAY_TARGET) / FAST_DECAY_PCT
    min_decay = math.log(DECAY_TARGET) / SLOW_DECAY_PCT
    deltas = np.abs(np.linspace(min_decay, max_decay, HYENA_W))
    tile, hid = FLT_TR, FILTER_HIDDEN
    emb2 = emb.reshape(2 * L // tile, 2, tile // 2, hid).transpose(0, 2, 1, 3).reshape(L, 2 * hid)
    return np.ascontiguousarray(emb2, dtype=np.float32), deltas[None, :].astype(np.float32)


MM_TN = 512
MM_RC = 256


def _wres_kernel(*refs, pairs, n_act, n_extra, n_out, n_scratch, rc, epilogue, norm_first, side_cast,
                 side_call):
    n_w = len(pairs)
    acts = refs[:n_act]
    ws = refs[n_act:n_act + n_w]
    extras = refs[n_act + n_w:n_act + n_w + n_extra]
    o_refs = refs[n_act + n_w + n_extra:n_act + n_w + n_extra + n_out]
    wbs = refs[n_act + n_w + n_extra + n_out:]
    side_stages = []
    if side_call:
        fn, n_in, n_res = side_call
        step = pl.program_id(0) * pl.num_programs(1) + pl.program_id(1)
        side_stages = fn(step, *extras[-n_in:], *o_refs[-n_res:])
        extras, o_refs = extras[:-n_in], o_refs[:-n_res]
    for _ in range(side_cast):
        o_refs[-1][...] = extras[-1][...].astype(BF16)
        extras, o_refs = extras[:-1], o_refs[:-1]
    if norm_first:
        g_ref, extras = extras[-1], extras[:-1]
        h_ref, o_refs = o_refs[-1], o_refs[:-1]

    if n_scratch:
        @pl.when(pl.program_id(1) == 0)
        def _():
            for w_ref, wb_ref in zip(ws, wbs):
                wb_ref[...] = w_ref[...].astype(BF16)
    else:
        wbs = ws

    for c in range(o_refs[0].shape[0] // rc):
        rows = pl.ds(c * rc, rc)
        lhs = [a[rows, :] for a in acts]
        if norm_first:
            x = lhs[0]
            ms = jnp.mean(x * x, axis=-1, keepdims=True)
            lhs[0] = (x * lax.rsqrt(ms + EPS) * g_ref[...]).astype(BF16)
            h_ref[rows, :] = lhs[0]
        accs = [jnp.dot(lhs[a], wb_ref[...], preferred_element_type=F32)
                for a, wb_ref in zip(pairs, wbs)]
        if c < len(side_stages):
            side_stages[c]()
        outs = epilogue(accs, extras, rows)
        for o_ref, out in zip(o_refs, outs if isinstance(outs, tuple) else (outs,)):
            o_ref[rows, :] = out.astype(o_ref.dtype)
    assert len(side_stages) <= o_refs[0].shape[0] // rc


def _wres_matmul(name, acts, weights, extras, epilogue, *, pairs, n_out, out_dtype, tm,
                 tn=MM_TN, rc=MM_RC, vmem_mib=48, norm_gain=None, side_cast=None, side_call=None):
    m = acts[0].shape[0]
    grid = (n_out // tn, m // tm)
    out_dtypes = out_dtype if isinstance(out_dtype, tuple) else (out_dtype,)
    out_specs = [pl.BlockSpec((tm, tn), lambda j, i: (i, j)) for _ in out_dtypes]
    out_shapes = [jax.ShapeDtypeStruct((m, n_out), dt) for dt in out_dtypes]
    extras = list(extras)
    if norm_gain is not None:
        assert n_out == tn
        k0 = acts[0].shape[1]
        extras.append((norm_gain, (1, k0), lambda j, i: (0, 0)))
        out_specs.append(pl.BlockSpec((tm, k0), lambda j, i: (i, 0)))
        out_shapes.append(jax.ShapeDtypeStruct((m, k0), BF16))
    side_casts = side_cast if isinstance(side_cast, list) else [side_cast] * (side_cast is not None)
    for arr, ncols in [(a, a.shape[1]) if not isinstance(a, tuple) else a for a in side_casts]:
        slab = arr.shape[0] // (grid[0] * grid[1])
        assert slab * grid[0] * grid[1] == arr.shape[0] and slab % ROWS_BF16 == 0 and ncols % LANES == 0
        step = lambda j, i: (j * grid[1] + i, 0)
        extras.append((arr, (slab, ncols), step))
        out_specs.append(pl.BlockSpec((slab, ncols), step))
        out_shapes.append(jax.ShapeDtypeStruct((arr.shape[0], ncols), BF16))
    side = side_call(grid) if side_call is not None else None
    if side is not None:
        extras += side["inputs"]
        out_specs += [pl.BlockSpec(blk, imap) for _, blk, imap in side["outputs"]]
        out_shapes += [shape for shape, _, _ in side["outputs"]]
    in_specs = [pl.BlockSpec((tm, a.shape[1]), lambda j, i: (i, 0)) for a in acts]
    assert all(off % LANES == 0 for _, off in weights) and tn % LANES == 0
    w_mode = dict(pipeline_mode=pl.Buffered(1)) if n_out == tn else {}
    in_specs += [pl.BlockSpec((pl.Element(w.shape[0]), pl.Element(tn)),
                              lambda j, i, off=off: (0, pl.multiple_of(off + j * tn, LANES)), **w_mode)
                 for w, off in weights]
    in_specs += [pl.BlockSpec(blk, imap) for _, blk, imap in extras]
    scratch = [pltpu.VMEM((w.shape[0], tn), BF16) for w, _ in weights if w.dtype != BF16]
    assert len(scratch) in (0, len(weights))
    kern = functools.partial(_wres_kernel, pairs=tuple(pairs), n_act=len(acts), n_extra=len(extras),
                             n_out=len(out_specs), n_scratch=len(scratch), rc=rc, epilogue=epilogue,
                             norm_first=norm_gain is not None, side_cast=len(side_casts),
                             side_call=(side["fn"], len(side["inputs"]), len(side["outputs"])) if side else None)
    outs = pl.pallas_call(
        kern,
        grid=grid,
        in_specs=in_specs,
        out_specs=out_specs,
        out_shape=out_shapes,
        scratch_shapes=scratch,
        compiler_params=_cparams(2, vmem_mib),
        name=name,
    )(*acts, *(w for w, _ in weights), *(e for e, _, _ in extras))
    return outs if len(outs) > 1 else outs[0]


def _head_norm_rope(x, g, cos, sin, scale):
    ms = jnp.mean(x * x, axis=-1, keepdims=True)
    y = x * lax.rsqrt(ms + EPS) * g
    lane = lax.broadcasted_iota(jnp.int32, y.shape, 1)
    swapped = jnp.where((lane % 64) < 32, pltpu.roll(y, 96, 1), pltpu.roll(y, 32, 1))
    out = y * cos + swapped * sin
    return out * scale if scale != 1.0 else out


def _ep_qkv(accs, extras, rows):
    qg_ref, kg_ref, cos_ref, sin_ref = extras
    acc = accs[0]
    cos, sin = cos_ref[rows, :], sin_ref[rows, :]
    head = lambda hh: acc[:, hh * HEAD_DIM:(hh + 1) * HEAD_DIM]
    parts = [_head_norm_rope(head(hh), qg_ref[...], cos, sin, HEAD_DIM ** -0.5 * LOG2E)
             for hh in range(N_Q_HEADS)]
    parts += [_head_norm_rope(head(N_Q_HEADS + hh), kg_ref[...], cos, sin, 1.0) for hh in range(N_KV_HEADS)]
    parts.append(acc[:, (N_Q_HEADS + N_KV_HEADS) * HEAD_DIM:])
    return jnp.concatenate(parts, axis=1)


def _ep_gate(accs, extras, rows):
    return jax.nn.sigmoid(accs[0] + extras[0][...])


IN_TM = 2048
IN_TN = 1024
LOG2E = math.log2(math.e)


def _inproj_qkv(x, norm_g, w_in, q_g, k_g, cos, sin):
    qkv_w = ATTN_W + 2 * KV_W
    tm = IN_TM // 2
    head = lambda g: (g, (1, HEAD_DIM), lambda j, i: (0, 0))
    pos = lambda t: (t, (tm, HEAD_DIM), lambda j, i: (i % (SEQ // tm), 0))
    return _wres_matmul("inproj_qkv", [x], [(w_in, COL_Q)], [head(q_g), head(k_g), pos(cos), pos(sin)],
                        _ep_qkv, pairs=[0], n_out=qkv_w, out_dtype=BF16, tn=qkv_w, tm=tm, norm_gain=norm_g,
                        vmem_mib=56)


def _inproj_gate(h, w_in, b_gate, filter_side):
    tn = IN_TN
    return _wres_matmul("inproj_gate", [h], [(w_in, COL_G)], [(b_gate, (1, tn), lambda j, i: (0, j))],
                        _ep_gate, pairs=[0], n_out=2 * D_MODEL, out_dtype=BF16, tn=tn, tm=IN_TM,
                        side_call=filter_side, vmem_mib=56)


HC_TM = 1024
HC_RC = 1024
HC_HALO = ROWS_BF16
HC_GROUPS = 3


def _inproj_conv_kernel(*refs):
    n_w = HC_GROUPS
    h_ref, top_ref, bot_ref = refs[:3]
    wb_refs = refs[3:3 + n_w]
    cw_refs = refs[3 + n_w:3 + 2 * n_w]
    cb_refs = refs[3 + 2 * n_w:3 + 3 * n_w]
    x0_ref, z_ref = refs[3 + 3 * n_w:]
    i = pl.program_id(0)
    tm, rc, hl = x0_ref.shape[0], HC_RC, HC_HALO
    seq_blocks = SEQ // tm
    top = jnp.where(i % seq_blocks != 0, top_ref[...], jnp.zeros_like(top_ref))
    bot = jnp.where(i % seq_blocks != seq_blocks - 1, bot_ref[...], jnp.zeros_like(bot_ref))
    nchunk = tm // rc
    for c in range(nchunk):
        lo, hi = c * rc - hl, (c + 1) * rc + hl
        parts = ([top] if c == 0 else []) + [h_ref[max(lo, 0):min(hi, tm), :]] + ([bot] if c == nchunk - 1 else [])
        lhs = jnp.concatenate(parts, axis=0) if len(parts) > 1 else parts[0]
        outs = []
        for wb_ref, cw_ref, cb_ref in zip(wb_refs, cw_refs, cb_refs):
            e = jnp.dot(lhs, wb_ref[...], preferred_element_type=F32)
            up = pltpu.roll(e, 1, 0)[hl:hl + rc]
            dn = pltpu.roll(e, rc + 2 * hl - 1, 0)[hl:hl + rc]
            outs.append(cb_ref[...] + up * cw_ref[0:1, :] + e[hl:hl + rc] * cw_ref[1:2, :]
                        + dn * cw_ref[2:3, :])
        x0_ref[c * rc:(c + 1) * rc, :] = outs[0].astype(x0_ref.dtype)
        z_ref[c * rc:(c + 1) * rc, :] = outs[2] * outs[1]


def _inproj_conv(h, w_hy, conv_w, conv_b):
    m, d = h.shape
    tm, tn, hl = HC_TM, HYENA_W, HC_HALO
    once = pl.Buffered(1)
    groups = range(HC_GROUPS)
    in_specs = [pl.BlockSpec((tm, d), lambda i: (i, 0)),
                pl.BlockSpec((hl, d), lambda i: (jnp.maximum(i * (tm // hl) - 1, 0), 0)),
                pl.BlockSpec((hl, d), lambda i: (jnp.minimum((i + 1) * (tm // hl), m // hl - 1), 0))]
    in_specs += [pl.BlockSpec((d, tn), lambda i, g=g: (0, g), pipeline_mode=once) for g in groups]
    in_specs += [pl.BlockSpec((SHORT_TAPS, tn), lambda i, g=g: (0, g)) for g in groups]
    in_specs += [pl.BlockSpec((1, tn), lambda i, g=g: (0, g)) for g in groups]
    row_spec = pl.BlockSpec((tm, tn), lambda i: (i, 0))
    return pl.pallas_call(
        _inproj_conv_kernel,
        grid=(m // tm,),
        in_specs=in_specs,
        out_specs=[row_spec, row_spec],
        out_shape=[jax.ShapeDtypeStruct((m, tn), BF16), jax.ShapeDtypeStruct((m, tn), F32)],
        compiler_params=_cparams(1, 56),
        name="inproj_hyena",
    )(h, h, h, *([w_hy] * HC_GROUPS), *([conv_w] * HC_GROUPS), *([conv_b] * HC_GROUPS))


CV_CT = 256
CV_F1B = 16
CV_NF = FFT_N1 // CV_F1B
CV_SLAB = ROWS_BF16
CV_NSLAB = FFT_N2 // CV_SLAB
CV_HALF = FFT_N1 // 2


def _outer_fwd_slab(src_ref, k_ref, a_ref, j, rows_in):
    ct = a_ref.shape[-1]
    halves = []
    for h in range(2):
        r = src_ref[:, :, 2 * j + h] if src_ref.ndim == 5 else src_ref[:, 2 * j + h]
        r = r.reshape(rows_in, ct).astype(BF16)
        o = jnp.dot(k_ref[...], r, preferred_element_type=F32)
        halves.append(o.reshape(2 * FFT_N1, DFT_R, ct))
    slab = jnp.concatenate(halves, axis=1).astype(BF16)
    a_ref[:, :, pl.ds(pl.multiple_of(j * CV_SLAB, CV_SLAB), CV_SLAB), :] = slab.reshape(
        2, FFT_N1, CV_SLAB, ct)


def _hy_conv_kernel(z_ref, x0_ref, bias_ref, l1_ref, kf_ref, ki_ref, fw_ref, iv_ref, fa_ref, o_ref, a_ref):
    s = pl.program_id(2)
    ct = a_ref.shape[-1]

    @pl.when(s == 0)
    def _():
        def body(j, carry):
            _outer_fwd_slab(z_ref, kf_ref, a_ref, j, 2 * CV_HALF * DFT_R)
            return carry

        lax.fori_loop(0, CV_NSLAB, body, 0, unroll=4)

    @pl.when((s >= 1) & (s <= CV_NF))
    def _():
        f0 = (s - 1) * CV_F1B
        inv_l1 = 1.0 / l1_ref[...]

        def spectra(fl):
            u = jnp.dot(fw_ref[fl], a_ref[:, f0 + fl].reshape(2 * FFT_N2, ct), preferred_element_type=F32)
            k = jnp.dot(fw_ref[fl], fa_ref[:, fl].reshape(2 * FFT_N2, ct), preferred_element_type=F32)
            return u, k * inv_l1

        nxt = spectra(0)
        for fl in range(CV_F1B):
            u, k = nxt
            if fl + 1 < CV_F1B:
                nxt = spectra(fl + 1)
            ure, uim = u[:FFT_N2], u[FFT_N2:]
            kre, kim = k[:FFT_N2], k[FFT_N2:]
            p = jnp.concatenate([ure * kre - uim * kim, ure * kim + uim * kre], axis=0).astype(BF16)
            y = jnp.dot(iv_ref[fl], p, preferred_element_type=F32)
            a_ref[:, f0 + fl] = y.reshape(2, FFT_N2, ct).astype(BF16)

    @pl.when(s == CV_NF + 1)
    def _():
        bias = bias_ref[...]

        def body(j, carry):
            slab = a_ref[:, :, pl.ds(pl.multiple_of(j * CV_SLAB, CV_SLAB), CV_SLAB), :].astype(F32)
            x0 = x0_ref[:, :, j].astype(F32)
            halves = []
            for h in range(2):
                r = slab[:, :, h * DFT_R:(h + 1) * DFT_R, :].reshape(2 * FFT_N1 * DFT_R, ct).astype(BF16)
                y = jnp.dot(ki_ref[...], r, preferred_element_type=F32).reshape(2, CV_HALF, DFT_R, ct)
                z = z_ref[:, :, 2 * j + h]
                halves.append((y + bias * z) * x0[:, :, h * DFT_R:(h + 1) * DFT_R, :])
            o_ref[:, :, j] = jnp.concatenate(halves, axis=2).astype(o_ref.dtype)
            return carry

        lax.fori_loop(0, CV_NSLAB, body, 0, unroll=4)


def _hy_conv(z, x0c, bias, filt_outer, l1, tabs):
    b, L, c = z.shape
    assert b % 2 == 0 and L * 2 == FFT_N
    ct = CV_CT
    z5 = z.reshape(b, CV_HALF, FFT_N2 // DFT_R, DFT_R, c)
    x5 = x0c.reshape(b, CV_HALF, CV_NSLAB, CV_SLAB, c)
    fidx = lambda s: jnp.clip(s - 1, 0, CV_NF - 1)
    once = pl.Buffered(1)
    tab_spec = pl.BlockSpec((CV_F1B, 2 * FFT_N2, 2 * FFT_N2), lambda p, cb, s: (fidx(s), 0, 0))
    io16 = pl.BlockSpec((2, CV_HALF, CV_NSLAB, CV_SLAB, ct), lambda p, cb, s: (p, 0, 0, 0, cb))
    out = pl.pallas_call(
        _hy_conv_kernel,
        grid=(b // 2, c // ct, CV_NF + 2),
        in_specs=[
            pl.BlockSpec((2, CV_HALF, FFT_N2 // DFT_R, DFT_R, ct), lambda p, cb, s: (p, 0, 0, 0, cb)),
            io16,
            pl.BlockSpec((1, ct), lambda p, cb, s: (0, cb)),
            pl.BlockSpec((1, ct), lambda p, cb, s: (0, cb)),
            pl.BlockSpec(tabs["k_fwd"].shape, lambda p, cb, s: (0, 0), pipeline_mode=once),
            pl.BlockSpec(tabs["k_inv"].shape, lambda p, cb, s: (0, 0), pipeline_mode=once),
            tab_spec, tab_spec,
            pl.BlockSpec((2, CV_F1B, FFT_N2, ct), lambda p, cb, s: (0, fidx(s), 0, cb)),
        ],
        out_specs=io16,
        out_shape=jax.ShapeDtypeStruct(x5.shape, BF16),
        scratch_shapes=[pltpu.VMEM((2, FFT_N1, FFT_N2, ct), BF16)],
        compiler_params=_cparams(3, 60),
        name="hyena_conv",
    )(z5, x5, bias, l1, tabs["k_fwd"], tabs["k_inv"], tabs["fwd2"], tabs["inv2"], filt_outer)
    return out.reshape(b, L, c)


def _filt_outer_kernel(t_ref, kf_ref, o_ref):
    def body(j, carry):
        _outer_fwd_slab(t_ref, kf_ref, o_ref, j, FFT_N1 * DFT_R)
        return carry

    lax.fori_loop(0, CV_NSLAB, body, 0, unroll=4)


def _filt_outer(taps, tabs):
    n, c = taps.shape
    ct = CV_CT
    t4 = taps.reshape(FFT_N1, FFT_N2 // DFT_R, DFT_R, c)
    return pl.pallas_call(
        _filt_outer_kernel,
        grid=(c // ct,),
        in_specs=[
            pl.BlockSpec((FFT_N1, FFT_N2 // DFT_R, DFT_R, ct), lambda cb: (0, 0, 0, cb)),
            pl.BlockSpec(tabs["k_flt"].shape, lambda cb: (0, 0), pipeline_mode=pl.Buffered(1)),
        ],
        out_specs=pl.BlockSpec((2, FFT_N1, FFT_N2, ct), lambda cb: (0, 0, 0, cb)),
        out_shape=jax.ShapeDtypeStruct((2, FFT_N1, FFT_N2, c), BF16),
        compiler_params=_cparams(1, 48),
        name="filter_outer",
    )(t4, tabs["k_flt"])


FLT_TR = 512
FLT_HALF = FLT_TR // 2


def _dot3(a, w):
    a_hi = a.astype(BF16)
    a_lo = (a - a_hi.astype(F32)).astype(BF16)
    w_hi = w.astype(BF16)
    w_lo = (w - w_hi.astype(F32)).astype(BF16)
    lhs = jnp.concatenate([a_hi, a_hi, a_lo], axis=1)
    rhs = jnp.concatenate([w_hi, w_lo, w_hi], axis=0)
    return jnp.dot(lhs, rhs, preferred_element_type=F32)


def _filter_step(r, emb_ref, w1_ref, b1_ref, w2_ref, b2_ref, w3_ref, b3_ref, w4t_ref, w4b_ref, fr_ref, dl_ref,
                 k_ref, l1_ref):
    st = {}

    def hidden(w_ref, b_ref, src):
        def stage():
            st["h"] = jnp.sin(fr_ref[...] * (_dot3(st[src] if src == "h" else emb_ref[...], w_ref[...])
                                             + b_ref[...]))
        return stage

    def taps_half(part, w_ref):
        def stage():
            tcol = part * FILTER_HIDDEN
            taps = jnp.dot(st["h"].astype(BF16), w_ref[...].astype(BF16),
                           preferred_element_type=F32)
            decay = jnp.exp(-emb_ref[:, tcol:tcol + 1] * dl_ref[...])
            rows = r * FLT_TR + part * FLT_HALF + lax.broadcasted_iota(jnp.int32, taps.shape, 0)
            taps = jnp.where(rows == SEQ, 0.0, taps * decay)
            k_ref[part * FLT_HALF:(part + 1) * FLT_HALF, :] = taps
            total = jnp.sum(jnp.abs(taps), axis=0, keepdims=True)
            if part == 0:
                st["l1"] = jnp.where(r == 0, 0.0, l1_ref[...]) + total
            else:
                l1_ref[...] = st["l1"] + total
        return stage

    return [hidden(w1_ref, b1_ref, "emb"), hidden(w2_ref, b2_ref, "h"), hidden(w3_ref, b3_ref, "h"),
            taps_half(0, w4t_ref), taps_half(1, w4b_ref)]


def _filter_side_call(emb2, w1, b1, w2, b2, w3, b3, w4, freq, deltas, steps):
    n = 2 * emb2.shape[0]
    assert n // FLT_TR == steps[0] * steps[1]
    fh = FILTER_HIDDEN
    eye2 = jnp.eye(2, dtype=F32)
    w1p = jnp.concatenate([w1, jnp.zeros((fh - w1.shape[0], fh), F32)], axis=0)
    wd = [jnp.kron(eye2, w) for w in (w1p, w2, w3)]
    bd = [jnp.tile(b, (1, 2)) for b in (b1, b2, b3, freq)]
    zeros = jnp.zeros_like(w4)
    w4t = jnp.concatenate([w4, zeros], axis=0)
    w4b = jnp.concatenate([zeros, w4], axis=0)
    step = lambda j, i: j * steps[1] + i
    const = lambda j, i: (0, 0)
    small = lambda a: (a, a.shape, const)
    fwd_tiles = SEQ // FLT_TR
    w4_half = lambda a: (a, (2 * fh, HYENA_W), lambda j, i: (0, step(j, i) // fwd_tiles))
    inputs = [(emb2, (FLT_HALF, 2 * fh), lambda j, i: (step(j, i), 0)),
              small(wd[0]), small(bd[0]), small(wd[1]), small(bd[1]), small(wd[2]), small(bd[2]),
              w4_half(w4t), w4_half(w4b), small(bd[3]), small(deltas)]
    outputs = [(jax.ShapeDtypeStruct((n, HYENA_W), F32), (FLT_TR, HYENA_W), lambda j, i: (step(j, i), 0)),
               (jax.ShapeDtypeStruct((1, HYENA_W), F32), (1, HYENA_W), const)]
    return dict(fn=_filter_step, inputs=inputs, outputs=outputs)


AT_TQ = 512
AT_RQ = 256
AT_TK = 512


def _attn_kernel(*refs, n_side):
    q_ref, k_ref, v_ref = refs[:3]
    o_ref = refs[3 + n_side]
    s_refs = refs[4 + 2 * n_side:]
    for side_ref, side_out_ref in zip(refs[3:3 + n_side], refs[4 + n_side:4 + 2 * n_side]):
        side_out_ref[...] = side_ref[...].astype(BF16)
    nblk = k_ref.shape[0] // AT_TK
    units = [(pl.ds(r * AT_RQ, AT_RQ), slice(g * HEAD_DIM, (g + 1) * HEAD_DIM))
             for r in range(q_ref.shape[0] // AT_RQ) for g in range(Q_PER_KV)]
    lane_blocks = lambda a: [a[:, i:i + LANES] for i in range(0, a.shape[1], LANES)]

    def scores(u, j, m_run):
        rows, lanes = units[u]
        ks = pl.ds(j * AT_TK, AT_TK)
        s = lax.dot_general(q_ref[rows, lanes], k_ref[ks, :], (((1,), (1,)), ((), ())),
                            preferred_element_type=F32)
        s_refs[u % 2][:, ks] = s
        blk = functools.reduce(jnp.maximum, lane_blocks(s))
        return blk if m_run is None else jnp.maximum(m_run, blk)

    def weighted(u, j, m, l_run, acc):
        ks = pl.ds(j * AT_TK, AT_TK)
        p = jnp.exp2((s_refs[u % 2][:, ks] - m).astype(BF16))
        l_blk = functools.reduce(jnp.add, lane_blocks(p)).astype(F32)
        o = jnp.dot(p, v_ref[ks, :], preferred_element_type=F32)
        return (l_blk if l_run is None else l_run + l_blk), (o if acc is None else acc + o)

    m_run = None
    for j in range(nblk):
        m_run = scores(0, j, m_run)
    for u, (rows, lanes) in enumerate(units):
        m = jnp.max(m_run, axis=-1, keepdims=True)
        m_run, l_run, acc = None, None, None
        for j in range(nblk):
            if u + 1 < len(units):
                m_run = scores(u + 1, j, m_run)
            l_run, acc = weighted(u, j, m, l_run, acc)
        l = jnp.sum(l_run, axis=-1, keepdims=True)
        o_ref[rows, lanes] = (acc / l).astype(o_ref.dtype)


def _attention(qkv3, side_casts):
    b, s, _ = qkv3.shape
    gw = Q_PER_KV * HEAD_DIM
    k0 = N_Q_HEADS
    v0 = N_Q_HEADS + N_KV_HEADS
    nq = s // AT_TQ
    steps = b * N_KV_HEADS * nq
    step = lambda bi, kv, qi: ((bi * N_KV_HEADS + kv) * nq + qi, 0)
    sides = [a if isinstance(a, tuple) else (a, a.shape[1]) for a in side_casts]
    side_specs = []
    for arr, ncols in sides:
        slab = arr.shape[0] // steps
        assert slab * steps == arr.shape[0] and slab % ROWS_BF16 == 0 and ncols % LANES == 0
        side_specs.append(pl.BlockSpec((slab, ncols), step))
    return pl.pallas_call(
        functools.partial(_attn_kernel, n_side=len(sides)),
        grid=(b, N_KV_HEADS, nq),
        in_specs=[pl.BlockSpec((None, AT_TQ, gw), lambda bi, kv, qi: (bi, qi, kv)),
                  pl.BlockSpec((None, s, HEAD_DIM), lambda bi, kv, qi: (bi, 0, k0 + kv)),
                  pl.BlockSpec((None, s, HEAD_DIM), lambda bi, kv, qi: (bi, 0, v0 + kv))] + side_specs,
        out_specs=[pl.BlockSpec((None, AT_TQ, gw), lambda bi, kv, qi: (bi, qi, kv))] + side_specs,
        out_shape=[jax.ShapeDtypeStruct((b, s, ATTN_W), BF16)]
        + [jax.ShapeDtypeStruct((arr.shape[0], ncols), BF16) for arr, ncols in sides],
        scratch_shapes=[pltpu.VMEM((AT_RQ, s), F32), pltpu.VMEM((AT_RQ, s), F32)],
        compiler_params=_cparams(3, 56),
        name="attention",
    )(qkv3, qkv3, qkv3, *(arr for arr, _ in sides))


def _ep_swiglu(accs, extras, rows):
    return jax.nn.silu(accs[0]) * accs[1]


def _merge_out_kernel(yh_ref, ya_ref, gt_ref, x_ref, g_ref, wh_ref, wa_ref, wo_ref, x1_ref, h2_ref):
    d = x_ref.shape[1]
    for c in range(x_ref.shape[0] // MM_RC):
        rows = pl.ds(c * MM_RC, MM_RC)
        ph = jnp.dot(yh_ref[rows, :], wh_ref[...], preferred_element_type=F32)
        pa = jnp.dot(ya_ref[rows, :], wa_ref[...], preferred_element_type=F32)
        merged = gt_ref[rows, :d].astype(F32) * ph + gt_ref[rows, d:].astype(F32) * pa
        x1 = x_ref[rows, :] + jnp.dot(merged.astype(BF16), wo_ref[...], preferred_element_type=F32)
        ms = jnp.mean(x1 * x1, axis=-1, keepdims=True)
        x1_ref[rows, :] = x1
        h2_ref[rows, :] = (x1 * lax.rsqrt(ms + EPS) * g_ref[...]).astype(h2_ref.dtype)


def _merge_out(yh, ya, gates, x, norm_g, w_h, w_a, w_o, tm=512):
    m, d = x.shape
    once = pl.Buffered(1)
    rows = lambda a: pl.BlockSpec((tm, a.shape[1]), lambda i: (i, 0))
    whole = lambda a: pl.BlockSpec(a.shape, lambda i: (0, 0), pipeline_mode=once)
    return pl.pallas_call(
        _merge_out_kernel,
        grid=(m // tm,),
        in_specs=[rows(yh), rows(ya), rows(gates), rows(x), pl.BlockSpec((1, d), lambda i: (0, 0)),
                  whole(w_h), whole(w_a), whole(w_o)],
        out_specs=[pl.BlockSpec((tm, d), lambda i: (i, 0))] * 2,
        out_shape=[jax.ShapeDtypeStruct((m, d), F32), jax.ShapeDtypeStruct((m, d), BF16)],
        compiler_params=_cparams(1, 56),
        name="merge_out",
    )(yh, ya, gates, x, norm_g, w_h, w_a, w_o)


def _ffn_down_body(a_ref, x_ref, w_ref, o_ref):
    for c in range(o_ref.shape[0] // MM_RC):
        rows = pl.ds(c * MM_RC, MM_RC)
        o_ref[rows, :] = x_ref[rows, :] + jnp.dot(a_ref[rows, :], w_ref[...], preferred_element_type=F32)


def _ffn_down(a, x, w_bf16, tm=512, tn=1024):
    m, k = a.shape
    n = w_bf16.shape[1]
    pipeline = pltpu.emit_pipeline(
        _ffn_down_body,
        grid=(n // tn, m // tm),
        in_specs=[pl.BlockSpec((tm, k), lambda j, i: (i, 0), pipeline_mode=pl.Buffered(3)),
                  pl.BlockSpec((tm, tn), lambda j, i: (i, j)),
                  pl.BlockSpec((k, tn), lambda j, i: (0, j))],
        out_specs=[pl.BlockSpec((tm, tn), lambda j, i: (i, j))],
    )

    def outer(a_hbm, x_hbm, w_hbm, o_hbm):
        pipeline(a_hbm, x_hbm, w_hbm, o_hbm)

    return pl.pallas_call(
        outer,
        in_specs=[pl.BlockSpec(memory_space=pl.ANY)] * 3,
        out_specs=pl.BlockSpec(memory_space=pl.ANY),
        out_shape=jax.ShapeDtypeStruct((m, n), F32),
        compiler_params=pltpu.CompilerParams(vmem_limit_bytes=56 * MIB),
        name="ffn_down",
    )(a, x, w_bf16)


def _swiglu(h, w_g_bf16, w_u_bf16, w_down, tm=4096):
    return _wres_matmul("swiglu", [h], [(w_g_bf16, 0), (w_u_bf16, 0)], [], _ep_swiglu,
                        pairs=[0, 0], n_out=w_g_bf16.shape[1], out_dtype=BF16, tm=tm, vmem_mib=60,
                        side_cast=w_down)


def _layer(x, mix_norm_g, w_in, b_gate, hy_conv_w, hy_conv_b,
           flt_w1, flt_b1, flt_w2, flt_b2, flt_w3, flt_b3, flt_w4, flt_freq, hy_bias,
           q_norm_g, k_norm_g, w_br_hyena, w_br_attn, w_out,
           ffn_norm_g, w_ffn_gate, w_ffn_up, w_ffn_down):
    b, s, d = x.shape
    m = b * s
    row = lambda a: a.reshape(1, -1)
    tabs = _dft_tables()
    cos, sin = (jnp.asarray(t) for t in _rope_tables_np())
    xm = x.reshape(m, d)

    emb2, deltas = (jnp.asarray(t) for t in _filter_tables_np())
    filter_side = functools.partial(_filter_side_call, emb2, flt_w1, row(flt_b1), flt_w2, row(flt_b2),
                                    flt_w3, row(flt_b3), flt_w4, row(flt_freq), deltas)
    qkv, h = _inproj_qkv(xm, row(mix_norm_g), w_in, row(q_norm_g), row(k_norm_g), cos, sin)

    y_a, w_hy_bf16, w_bh_bf16, w_ba_bf16, w_out_bf16, w_g_bf16, w_u_bf16 = _attention(
        qkv.reshape(b, s, ATTN_W + 2 * KV_W),
        [(w_in, COL_Q), w_br_hyena, w_br_attn, w_out, w_ffn_gate, w_ffn_up])
    y_a = y_a.reshape(m, ATTN_W)

    gates, taps, l1 = _inproj_gate(h, w_in, row(b_gate), filter_side)
    filt_outer = _filt_outer(taps, tabs)

    x0c, z = _inproj_conv(h, w_hy_bf16, hy_conv_w, row(hy_conv_b))
    y_h = _hy_conv(z.reshape(b, s, HYENA_W), x0c.reshape(b, s, HYENA_W), row(hy_bias), filt_outer, l1,
                   tabs).reshape(m, HYENA_W)

    x1, h2 = _merge_out(y_h, y_a, gates, xm, row(ffn_norm_g), w_bh_bf16, w_ba_bf16, w_out_bf16)
    act, w_down_bf16 = _swiglu(h2, w_g_bf16, w_u_bf16, w_ffn_down)
    out = _ffn_down(act, x1, w_down_bf16)
    return out.reshape(b, s, d)


def kernel(x, mix_norm_g, w_in, b_gate, hy_conv_w, hy_conv_b, flt_w1, flt_b1, flt_w2, flt_b2, flt_w3, flt_b3, flt_w4, flt_freq, hy_bias, q_norm_g, k_norm_g, w_br_hyena, w_br_attn, w_out, ffn_norm_g, w_ffn_gate, w_ffn_up, w_ffn_down):
    params = (mix_norm_g, w_in, b_gate, hy_conv_w, hy_conv_b, flt_w1, flt_b1, flt_w2, flt_b2,
              flt_w3, flt_b3, flt_w4, flt_freq, hy_bias, q_norm_g, k_norm_g, w_br_hyena, w_br_attn,
              w_out, ffn_norm_g, w_ffn_gate, w_ffn_up, w_ffn_down)
    for l in range(mix_norm_g.shape[0]):
        x = _layer(x, *(p[l] for p in params))
    return x
```
